```python
import math
import jax, jax.numpy as jnp
from jax import lax
import numpy as np

D_MODEL = 2048
BATCH = 2
SEQ = 4096
DEPTH = 2
DEC_BATCH = 8
DEC_SEQ = 4
PAST_LEN = 16384
PAGE_SIZE = 128

HEAD_DIM = 128
SB_HEADS = 8
NSA_HEADS = 8
NSA_KV_HEADS = 2
CMP_BLOCK = 32
CMP_STRIDE = 16
SLC_BLOCK = 64
N_SELECT = 16
NSA_WINDOW = 512
DIL_HEADS = 16
DIL_PATTERNS = ((128, 1), (512, 4), (2048, 16))
DIL_MAX_WINDOW = 2048
QBLK = 128
N_BUCKETS = 32
BUCKET_MAX_DIST = 2048
BIAS_HEADS = 16
N_GROUPS = 4
EXPERTS_PER_GROUP = 4
N_EXPERTS = N_GROUPS * EXPERTS_PER_GROUP
TOP_K_INNER = 2
D_EXPERT = 1024
N_EVEN_LAYERS = (DEPTH + 1) // 2
N_ODD_LAYERS = DEPTH // 2
RMS_EPS = 1e-6
NEG_INF = -1e30
FORCE_SCORE = 1e9
SB_W = SB_HEADS * HEAD_DIM
NSA_QW = NSA_HEADS * HEAD_DIM
NSA_KVW = NSA_KV_HEADS * HEAD_DIM
SPARSE_SPLITS = (SB_W, SB_W, SB_W, NSA_QW) + (NSA_KVW,) * 6 + (3 * NSA_HEADS,)
SPARSE_IN = sum(SPARSE_SPLITS)
SPARSE_OUT = SB_W + NSA_QW
DIL_W = DIL_HEADS * HEAD_DIM

kernel_name = "hybrid_sb_nsa_dilated_hmoe_step"


def rms_norm(x, g):
    xf = x.astype(jnp.float32)
    y = xf * lax.rsqrt(jnp.mean(xf * xf, axis=-1, keepdims=True) + RMS_EPS)
    return (y * g).astype(x.dtype)


def t5_bucket(dist):
    n = jnp.maximum(dist, 0)
    exact = N_BUCKETS // 2
    nf = jnp.maximum(n, exact).astype(jnp.float32)
    far = exact + (jnp.log(nf / exact) / math.log(BUCKET_MAX_DIST / exact) * (N_BUCKETS - exact)).astype(jnp.int32)
    return jnp.where(n < exact, n, jnp.minimum(far, N_BUCKETS - 1))


def softmax_stats(s, mask):
    s = jnp.where(mask, s, NEG_INF)
    m = jnp.max(s, axis=-1, keepdims=True)
    p = jnp.where(mask, jnp.exp(s - m), 0.0)
    return m, jnp.sum(p, axis=-1, keepdims=True), p


def split_cols(a, sizes):
    return jnp.split(a, np.cumsum(sizes)[:-1].tolist(), axis=-1)


def gather_pages(pool, page_table):
    g = pool[page_table]
    return g.reshape((page_table.shape[0], -1) + pool.shape[2:])


def sb_attend(q, k, v, q_pos, k_pos):
    z = jnp.einsum('bqhd,bkhd->bhqk', q, k).astype(jnp.float32) * (q.shape[-1] ** -0.5)
    mask = k_pos[None, :] < q_pos[:, None]
    log_keep = jnp.where(mask, jax.nn.log_sigmoid(-z), 0.0)
    after = lax.cumsum(log_keep, axis=3, reverse=True)
    after = jnp.pad(after[..., 1:], ((0, 0), (0, 0), (0, 0), (0, 1)))
    w = jnp.where(mask, jnp.exp(jax.nn.log_sigmoid(z) + after), 0.0)
    return jnp.einsum('bhqk,bkhd->bqhd', w.astype(v.dtype), v)


def sb_prompt(q, k, v):
    B, S, H, dh = q.shape
    nb = S // QBLK
    qb = q.reshape(B, nb, QBLK, H, dh).swapaxes(0, 1)
    k_pos = jnp.arange(S)

    def one(args):
        i, qi = args
        return sb_attend(qi, k, v, i * QBLK + jnp.arange(QBLK), k_pos)

    o = lax.map(one, (jnp.arange(nb), qb))
    return o.swapaxes(0, 1).reshape(B, S, H, dh)


def compress_blocks(rows, w, pe):
    B, L, G, dh = rows.shape
    n_chunk = L // CMP_STRIDE
    half = CMP_BLOCK // CMP_STRIDE
    n_c = n_chunk - half + 1
    ch = rows[:, :n_chunk * CMP_STRIDE].reshape(B, n_chunk, CMP_STRIDE, G, dh)
    out = 0.0
    for j in range(half):
        sl = slice(j * CMP_STRIDE, (j + 1) * CMP_STRIDE)
        out = out + jnp.einsum('bnlgd,lde->bnge', ch[:, j:j + n_c] + pe[sl][None, None, :, None, :], w[sl])
    return out.astype(rows.dtype)


def nsa_attend(q, q_pos, kc, vc, c_end, ks_blk, vs_blk, kw, vw, w_pos, gates, table):
    B, Q, Hq, dh = q.shape
    G = kc.shape[2]
    R = Hq // G
    n_c = kc.shape[1]
    n_s = ks_blk.shape[1]
    scale = dh ** -0.5
    qg = q.reshape(B, Q, G, R, dh)
    tb = table.reshape(N_BUCKETS, G, R)
    dist_c = q_pos[:, None] - c_end[None, :]
    s_c = jnp.einsum('bqgrd,bngd->bgrqn', qg, kc).astype(jnp.float32) * scale
    s_c = s_c + tb[t5_bucket(dist_c)].transpose(2, 3, 0, 1)
    _, l_c, p_c = softmax_stats(s_c, dist_c >= 0)
    p_c = p_c / jnp.maximum(l_c, 1.0)
    o_c = jnp.einsum('bgrqn,bngd->bqgrd', p_c.astype(vc.dtype), vc)
    ratio = SLC_BLOCK // CMP_STRIDE
    half = CMP_BLOCK // CMP_STRIDE
    imp = jnp.pad(p_c.sum(axis=2), ((0, 0), (0, 0), (0, 0), (0, ratio * n_s - n_c)))
    score = imp.reshape(B, G, Q, n_s, ratio).sum(-1)
    for j in range(1, half):
        tail = imp[..., ratio - j::ratio]
        score = score + jnp.pad(tail[..., :-1], ((0, 0), (0, 0), (0, 0), (1, 0)))
    blk = jnp.arange(n_s)[None, :]
    q_blk = (q_pos // SLC_BLOCK)[:, None]
    valid = blk <= q_blk
    forced = valid & ((blk == 0) | (blk == q_blk) | (blk == q_blk - 1))
    score = jnp.where(forced, FORCE_SCORE, jnp.where(valid, score, NEG_INF))
    _, idx = lax.top_k(score, min(N_SELECT, n_s))
    n_sel = idx.shape[-1]
    take = jax.vmap(jax.vmap(lambda blocks, ix: blocks[ix]))
    kg = take(ks_blk.transpose(0, 3, 1, 2, 4), idx).reshape(B, G, Q, n_sel * SLC_BLOCK, dh)
    vg = take(vs_blk.transpose(0, 3, 1, 2, 4), idx).reshape(B, G, Q, n_sel * SLC_BLOCK, dh)
    s_pos = (idx[..., None] * SLC_BLOCK + jnp.arange(SLC_BLOCK)).reshape(B, G, Q, n_sel * SLC_BLOCK)
    dist_s = q_pos[:, None] - s_pos
    s_s = jnp.einsum('bqgrd,bgqkd->bgrqk', qg, kg).astype(jnp.float32) * scale
    bias_s = tb[t5_bucket(dist_s), jnp.arange(G)[None, :, None, None]]
    s_s = s_s + bias_s.transpose(0, 1, 4, 2, 3)
    _, l_s, p_s = softmax_stats(s_s, (dist_s >= 0)[:, :, None])
    o_s = jnp.einsum('bgrqk,bgqkd->bqgrd', (p_s / jnp.maximum(l_s, 1.0)).astype(vg.dtype), vg)
    dist_w = q_pos[:, None] - w_pos[None, :]
    mask_w = (dist_w >= 0) & (dist_w < NSA_WINDOW) & (w_pos >= 0)[None, :]
    s_w = jnp.einsum('bqgrd,bwgd->bgrqw', qg, kw).astype(jnp.float32) * scale
    s_w = s_w + tb[t5_bucket(dist_w)].transpose(2, 3, 0, 1)
    _, l_w, p_w = softmax_stats(s_w, mask_w)
    o_w = jnp.einsum('bgrqw,bwgd->bqgrd', (p_w / jnp.maximum(l_w, 1.0)).astype(vw.dtype), vw)
    gt = gates.reshape(B, Q, G, R, 3)
    o = gt[..., 0:1] * o_c + gt[..., 1:2] * o_s + gt[..., 2:3] * o_w
    return o.reshape(B, Q, Hq, dh).astype(q.dtype)


def nsa_prompt(q, cmp_rows, slc, win, gates, cmp_w, cmp_pe, ck_gain, table):
    B, S, Hq, dh = q.shape
    kc = rms_norm(compress_blocks(cmp_rows[0], cmp_w[0], cmp_pe[0]), ck_gain)
    vc = compress_blocks(cmp_rows[1], cmp_w[1], cmp_pe[1])
    c_end = jnp.arange(kc.shape[1]) * CMP_STRIDE + (CMP_BLOCK - 1)
    n_s = S // SLC_BLOCK
    ks_blk = slc[0].reshape(B, n_s, SLC_BLOCK, NSA_KV_HEADS, dh)
    vs_blk = slc[1].reshape(B, n_s, SLC_BLOCK, NSA_KV_HEADS, dh)
    kw = jnp.pad(win[0], ((0, 0), (NSA_WINDOW, 0), (0, 0), (0, 0)))
    vw = jnp.pad(win[1], ((0, 0), (NSA_WINDOW, 0), (0, 0), (0, 0)))
    nb = S // QBLK
    qb = q.reshape(B, nb, QBLK, Hq, dh).swapaxes(0, 1)
    gb = gates.reshape(B, nb, QBLK, Hq, 3).swapaxes(0, 1)

    def one(args):
        i, qi, gi = args
        start = i * QBLK
        kwi = lax.dynamic_slice_in_dim(kw, start, NSA_WINDOW + QBLK, axis=1)
        vwi = lax.dynamic_slice_in_dim(vw, start, NSA_WINDOW + QBLK, axis=1)
        w_pos = start - NSA_WINDOW + jnp.arange(NSA_WINDOW + QBLK)
        q_pos = start + jnp.arange(QBLK)
        return nsa_attend(qi, q_pos, kc, vc, c_end, ks_blk, vs_blk, kwi, vwi, w_pos, gi, table)

    o = lax.map(one, (jnp.arange(nb), qb, gb))
    return o.swapaxes(0, 1).reshape(B, S, Hq, dh)


def nsa_sample(q, cmp_rows, slc, win, gates, past_cmp, past_slc, win_buf, cmp_w, cmp_pe, ck_gain, table):
    DB, Q, Hq, dh = q.shape
    P = past_cmp.shape[1]
    L = P + Q
    rows_k = jnp.concatenate([past_cmp[:, :, 0], cmp_rows[0]], 1)
    rows_v = jnp.concatenate([past_cmp[:, :, 1], cmp_rows[1]], 1)
    kc = rms_norm(compress_blocks(rows_k, cmp_w[0], cmp_pe[0]), ck_gain)
    vc = compress_blocks(rows_v, cmp_w[1], cmp_pe[1])
    c_end = jnp.arange(kc.shape[1]) * CMP_STRIDE + (CMP_BLOCK - 1)
    n_s = -(-L // SLC_BLOCK)

    def blocks(past, new):
        t = jnp.concatenate([past, new], 1)
        t = jnp.pad(t, ((0, 0), (0, n_s * SLC_BLOCK - L), (0, 0), (0, 0)))
        return t.reshape(DB, n_s, SLC_BLOCK, NSA_KV_HEADS, dh)

    ks_blk = blocks(past_slc[:, :, 0], slc[0])
    vs_blk = blocks(past_slc[:, :, 1], slc[1])
    wb = win_buf.shape[1]
    kw = jnp.concatenate([win_buf[:, :, 0], win[0]], 1)
    vw = jnp.concatenate([win_buf[:, :, 1], win[1]], 1)
    w_pos = P - wb + jnp.arange(wb + Q)
    q_pos = P + jnp.arange(Q)
    return nsa_attend(q, q_pos, kc, vc, c_end, ks_blk, vs_blk, kw, vw, w_pos, gates, table)


def sparse_project(h, w_in, qk_gain):
    B, L, _ = h.shape
    sb_q, sb_k, sb_v, q, ck, cv, sk, sv, wk, wv, gate = split_cols(h @ w_in, SPARSE_SPLITS)
    def heads(t, n):
        return t.reshape(B, L, n, HEAD_DIM)
    G = NSA_KV_HEADS
    sb = (heads(sb_q, SB_HEADS), heads(sb_k, SB_HEADS), heads(sb_v, SB_HEADS))
    q = rms_norm(heads(q, NSA_HEADS), qk_gain[0])
    cmp_rows = (heads(ck, G), heads(cv, G))
    slc = (rms_norm(heads(sk, G), qk_gain[2]), heads(sv, G))
    win = (rms_norm(heads(wk, G), qk_gain[3]), heads(wv, G))
    gates = jax.nn.sigmoid(gate.reshape(B, L, NSA_HEADS, 3).astype(jnp.float32))
    return sb, q, cmp_rows, slc, win, gates


def dilated_pattern_prompt(q, k, v, span, dil, table):
    B, S, H, dh = q.shape
    M = S // dil
    n = B * dil
    nb = -(-M // QBLK)
    Mp = nb * QBLK

    def regroup(t):
        return t.reshape(B, M, dil, H, dh).transpose(0, 2, 1, 3, 4).reshape(n, M, H, dh)

    def band(t):
        tr = jnp.pad(regroup(t), ((0, 0), (QBLK, Mp - M), (0, 0), (0, 0)))
        return jnp.concatenate([tr[:, :Mp].reshape(n, nb, QBLK, H, dh),
                                tr[:, QBLK:].reshape(n, nb, QBLK, H, dh)], 2)

    qr = jnp.pad(regroup(q), ((0, 0), (0, Mp - M), (0, 0), (0, 0))).reshape(n, nb, QBLK, H, dh)
    kb, vb = band(k), band(v)
    dm = QBLK + jnp.arange(QBLK)[:, None] - jnp.arange(2 * QBLK)[None, :]
    key_m = jnp.arange(nb)[:, None] * QBLK - QBLK + jnp.arange(2 * QBLK)[None, :]
    mask = ((dm >= 0) & (dm <= span))[None] & (key_m >= 0)[:, None, :]
    s = jnp.einsum('nbqhd,nbkhd->nbhqk', qr, kb).astype(jnp.float32) * (dh ** -0.5)
    s = s + table[t5_bucket(dm * dil)].transpose(2, 0, 1)[None, None]
    m, l, p = softmax_stats(s, mask[None, :, None])
    num = jnp.einsum('nbhqk,nbkhd->nbqhd', p, vb.astype(jnp.float32))

    def ungroup(t):
        X = t.shape[-1]
        t = t.reshape(B, dil, Mp, H, X)[:, :, :M]
        return t.transpose(0, 2, 1, 3, 4).reshape(B, S, H, X)

    return ungroup(m.transpose(0, 1, 3, 2, 4)), ungroup(l.transpose(0, 1, 3, 2, 4)), ungroup(num)


def dilated_pattern_sample(q, kr, vr, wb, span, dil, table):
    DB, Q, H, dh = q.shape
    steps = jnp.arange(span + 1) * dil
    rows = wb + jnp.arange(Q)[:, None] - steps[None, :]
    valid = rows >= 0
    kg = kr[:, jnp.clip(rows, 0)]
    vg = vr[:, jnp.clip(rows, 0)]
    s = jnp.einsum('bqhd,bqkhd->bhqk', q, kg).astype(jnp.float32) * (dh ** -0.5)
    s = s + table[t5_bucket(steps)].T[:, None, :]
    m, l, p = softmax_stats(s, valid)
    num = jnp.einsum('bhqk,bqkhd->bqhd', p, vg.astype(jnp.float32))
    return m.transpose(0, 2, 1, 3), l.transpose(0, 2, 1, 3), num


def merge_dilations(stats):
    m_all = jnp.stack([st[0] for st in stats])
    l_all = jnp.stack([st[1] for st in stats])
    n_all = jnp.stack([st[2] for st in stats])
    w = jnp.exp(m_all - jnp.max(m_all, axis=0))
    return (w * n_all).sum(0) / (w * l_all).sum(0)


def dil_project(h, w_in, qk_gain):
    B, L, _ = h.shape
    q, k, v = jnp.split(h @ w_in, 3, axis=-1)
    sh = (B, L, DIL_HEADS, HEAD_DIM)
    return rms_norm(q.reshape(sh), qk_gain[0]), rms_norm(k.reshape(sh), qk_gain[1]), v.reshape(sh)


def hier_moe(x, w_group, b_group, w_expert, b_expert, w_gate, w_up, w_down):
    T = x.shape[0]
    g_logit = (x @ w_group).astype(jnp.float32) + b_group
    g_idx = jnp.argmax(g_logit, axis=-1)
    g_w = jnp.take_along_axis(jax.nn.softmax(g_logit, axis=-1), g_idx[:, None], axis=-1)
    e_logit = ((x @ w_expert).astype(jnp.float32) + b_expert).reshape(T, N_GROUPS, EXPERTS_PER_GROUP)
    e_logit = jnp.take_along_axis(e_logit, g_idx[:, None, None], axis=1)[:, 0]
    top_v, top_i = lax.top_k(jax.nn.softmax(e_logit, axis=-1), TOP_K_INNER)
    top_v = top_v / jnp.sum(top_v, axis=-1, keepdims=True)
    eid = g_idx[:, None] * EXPERTS_PER_GROUP + top_i
    gate = jnp.sum(jax.nn.one_hot(eid, N_EXPERTS, dtype=jnp.float32) * (g_w * top_v)[..., None], axis=1)
    h = jnp.einsum('td,edf->tef', x, w_gate)
    u = jnp.einsum('td,edf->tef', x, w_up)
    a = jax.nn.silu(h) * u * gate[..., None].astype(x.dtype)
    return jnp.einsum('tef,efd->td', a, w_down)


def setup_inputs(seed: int = 0) -> dict:
    key = jax.random.key(seed)
    keys = iter(jax.random.split(key, 40))

    def nrm(shape, scale):
        return scale * jax.random.normal(next(keys), shape, jnp.float32)

    def gain(shape):
        return 1.0 + nrm(shape, 0.1)

    n_pages = PAST_LEN // PAGE_SIZE
    n_used = DEC_BATCH * n_pages
    n_pool = n_used + max(1, n_used // 4)
    nsa_wb = min(NSA_WINDOW, PAST_LEN)
    dil_wb = min(DIL_MAX_WINDOW, PAST_LEN)
    perm = jax.random.permutation(next(keys), n_pool)
    page_table = perm[:n_used].reshape(DEC_BATCH, n_pages).astype(jnp.int32)
    return {
        "x_prompt": nrm((BATCH, SEQ, D_MODEL), 1.0),
        "x_sample": nrm((DEC_BATCH, DEC_SEQ, D_MODEL), 1.0),
        "cache_sb_kv": nrm((N_EVEN_LAYERS, n_pool, PAGE_SIZE, 2, SB_HEADS, HEAD_DIM), 1.0),
        "cache_nsa_cmp_kv": nrm((N_EVEN_LAYERS, n_pool, PAGE_SIZE, 2, NSA_KV_HEADS, HEAD_DIM), 1.0),
        "cache_nsa_slc_kv": nrm((N_EVEN_LAYERS, n_pool, PAGE_SIZE, 2, NSA_KV_HEADS, HEAD_DIM), 1.0),
        "state_nsa_win_kv": nrm((N_EVEN_LAYERS, DEC_BATCH, nsa_wb, 2, NSA_KV_HEADS, HEAD_DIM), 1.0),
        "state_dil_kv": nrm((N_ODD_LAYERS, DEC_BATCH, dil_wb, 2, DIL_HEADS, HEAD_DIM), 1.0),
        "page_table": page_table,
        "rel_bias_table": nrm((N_BUCKETS, BIAS_HEADS), 0.5),
        "norm_mix": gain((DEPTH, D_MODEL)),
        "norm_ffn": gain((DEPTH, D_MODEL)),
        "sparse_w_in": nrm((N_EVEN_LAYERS, D_MODEL, SPARSE_IN), D_MODEL ** -0.5),
        "sparse_w_out": nrm((N_EVEN_LAYERS, SPARSE_OUT, D_MODEL), SPARSE_OUT ** -0.5),
        "nsa_qk_gain": gain((N_EVEN_LAYERS, 4, HEAD_DIM)),
        "nsa_cmp_w": nrm((N_EVEN_LAYERS, 2, CMP_BLOCK, HEAD_DIM, HEAD_DIM), (CMP_BLOCK * HEAD_DIM) ** -0.5),
        "nsa_cmp_pe": nrm((N_EVEN_LAYERS, 2, CMP_BLOCK, HEAD_DIM), 0.5),
        "dil_w_in": nrm((N_ODD_LAYERS, D_MODEL, 3 * DIL_W), D_MODEL ** -0.5),
        "dil_w_out": nrm((N_ODD_LAYERS, DIL_W, D_MODEL), DIL_W ** -0.5),
        "dil_qk_gain": gain((N_ODD_LAYERS, 2, HEAD_DIM)),
        "moe_w_group": nrm((DEPTH, D_MODEL, N_GROUPS), D_MODEL ** -0.5),
        "moe_b_group": nrm((DEPTH, N_GROUPS), 0.01),
        "moe_w_expert": nrm((DEPTH, D_MODEL, N_EXPERTS), D_MODEL ** -0.5),
        "moe_b_expert": nrm((DEPTH, N_EXPERTS), 0.01),
        "moe_w_gate": nrm((DEPTH, N_EXPERTS, D_MODEL, D_EXPERT), D_MODEL ** -0.5),
        "moe_w_up": nrm((DEPTH, N_EXPERTS, D_MODEL, D_EXPERT), D_MODEL ** -0.5),
        "moe_w_down": nrm((DEPTH, N_EXPERTS, D_EXPERT, D_MODEL), D_EXPERT ** -0.5),
    }


def reference(x_prompt, x_sample, cache_sb_kv, cache_nsa_cmp_kv, cache_nsa_slc_kv, state_nsa_win_kv,
              state_dil_kv, page_table, rel_bias_table, norm_mix, norm_ffn, sparse_w_in, sparse_w_out,
              nsa_qk_gain, nsa_cmp_w, nsa_cmp_pe, dil_w_in, dil_w_out, dil_qk_gain, moe_w_group,
              moe_b_group, moe_w_expert, moe_b_expert, moe_w_gate, moe_w_up, moe_w_down):
    xp, xs = x_prompt, x_sample
    B, S, D = xp.shape
    DB, DS, _ = xs.shape
    nsa_table = rel_bias_table[:, :NSA_HEADS]
    dil_table = rel_bias_table[:, :DIL_HEADS]
    sb_p, sb_s, cmp_p, cmp_s, slc_p, slc_s, win_p, win_s, dil_p, dil_s = ([] for _ in range(10))
    for layer in range(DEPTH):
        hp = rms_norm(xp, norm_mix[layer])
        hs = rms_norm(xs, norm_mix[layer])
        i = layer // 2
        if layer % 2 == 0:
            w_in, w_out, g = sparse_w_in[i], sparse_w_out[i], nsa_qk_gain[i]
            cw, cpe = nsa_cmp_w[i], nsa_cmp_pe[i]
            (q_sb, k_sb, v_sb), q, cmp_rows, slc, win, gates = sparse_project(hp, w_in, g)
            o_sb = sb_prompt(q_sb, k_sb, v_sb)
            o_nsa = nsa_prompt(q, cmp_rows, slc, win, gates, cw, cpe, g[1], nsa_table)
            mp = jnp.concatenate([o_sb, o_nsa], 2).reshape(B, S, SPARSE_OUT) @ w_out
            sb_p.append(jnp.stack([k_sb, v_sb], 2))
            cmp_p.append(jnp.stack(cmp_rows, 2))
            slc_p.append(jnp.stack(slc, 2))
            win_p.append(jnp.stack(win, 2)[:, S - min(NSA_WINDOW, S):])
            past_sb = gather_pages(cache_sb_kv[i], page_table)
            P = past_sb.shape[1]
            (q_sb, k_sb, v_sb), q, cmp_rows, slc, win, gates = sparse_project(hs, w_in, g)
            k_all = jnp.concatenate([past_sb[:, :, 0], k_sb], 1)
            v_all = jnp.concatenate([past_sb[:, :, 1], v_sb], 1)
            o_sb = sb_attend(q_sb, k_all, v_all, P + jnp.arange(DS), jnp.arange(P + DS))
            o_nsa = nsa_sample(q, cmp_rows, slc, win, gates,
                               gather_pages(cache_nsa_cmp_kv[i], page_table),
                               gather_pages(cache_nsa_slc_kv[i], page_table),
                               state_nsa_win_kv[i], cw, cpe, g[1], nsa_table)
            ms = jnp.concatenate([o_sb, o_nsa], 2).reshape(DB, DS, SPARSE_OUT) @ w_out
            sb_s.append(jnp.stack([k_sb, v_sb], 2))
            cmp_s.append(jnp.stack(cmp_rows, 2))
            slc_s.append(jnp.stack(slc, 2))
            win_s.append(jnp.concatenate([state_nsa_win_kv[i], jnp.stack(win, 2)], 1)[:, DS:])
        else:
            q, k, v = dil_project(hp, dil_w_in[i], dil_qk_gain[i])
            stats = [dilated_pattern_prompt(q, k, v, w // d, d, dil_table) for w, d in DIL_PATTERNS]
            o = merge_dilations(stats).astype(xp.dtype)
            mp = o.reshape(B, S, DIL_W) @ dil_w_out[i]
            dil_p.append(jnp.stack([k, v], 2)[:, S - min(DIL_MAX_WINDOW, S):])
            q, k, v = dil_project(hs, dil_w_in[i], dil_qk_gain[i])
            buf = state_dil_kv[i]
            wb = buf.shape[1]
            kr = jnp.concatenate([buf[:, :, 0], k], 1)
            vr = jnp.concatenate([buf[:, :, 1], v], 1)
            stats = [dilated_pattern_sample(q, kr, vr, wb, w // d, d, dil_table) for w, d in DIL_PATTERNS]
            o = merge_dilations(stats).astype(xs.dtype)
            ms = o.reshape(DB, DS, DIL_W) @ dil_w_out[i]
            dil_s.append(jnp.concatenate([buf, jnp.stack([k, v], 2)], 1)[:, DS:])
        xp = xp + mp
        xs = xs + ms
        flat = jnp.concatenate([xp.reshape(-1, D), xs.reshape(-1, D)], 0)
        f = hier_moe(rms_norm(flat, norm_ffn[layer]), moe_w_group[layer], moe_b_group[layer],
                     moe_w_expert[layer], moe_b_expert[layer], moe_w_gate[layer], moe_w_up[layer],
                     moe_w_down[layer])
        xp = xp + f[:B * S].reshape(B, S, D)
        xs = xs + f[B * S:].reshape(DB, DS, D)
    y_prompt, y_sample = xp, xs
    sb_kv_prompt, sb_kv_sample = jnp.stack(sb_p), jnp.stack(sb_s)
    cmp_kv_prompt, cmp_kv_sample = jnp.stack(cmp_p), jnp.stack(cmp_s)
    slc_kv_prompt, slc_kv_sample = jnp.stack(slc_p), jnp.stack(slc_s)
    win_kv_prompt, win_kv_sample = jnp.stack(win_p), jnp.stack(win_s)
    dil_kv_prompt, dil_kv_sample = jnp.stack(dil_p), jnp.stack(dil_s)
    return (y_prompt, y_sample, sb_kv_prompt, sb_kv_sample, cmp_kv_prompt, cmp_kv_sample,
            slc_kv_prompt, slc_kv_sample, win_kv_prompt, win_kv_sample, dil_kv_prompt, dil_kv_sample)
```

```python
import functools
import math

import jax
import jax.numpy as jnp
import numpy as np
from jax import lax
from jax.experimental import pallas as pl
from jax.experimental.pallas import tpu as pltpu

D_MODEL = 2048
HEAD_DIM = 128
SB_HEADS = 8
NSA_HEADS = 8
NSA_KV_HEADS = 2
NSA_REP = NSA_HEADS // NSA_KV_HEADS
CMP_BLOCK = 32
CMP_STRIDE = 16
SLC_BLOCK = 64
N_SELECT = 16
NSA_WINDOW = 512
DIL_HEADS = 16
DIL_PATTERNS = ((128, 1), (512, 4), (2048, 16))
DIL_MAX_WINDOW = 2048
QBLK = 128
N_BUCKETS = 32
BUCKET_MAX_DIST = 2048
N_GROUPS = 4
EXPERTS_PER_GROUP = 4
N_EXPERTS = N_GROUPS * EXPERTS_PER_GROUP
TOP_K_INNER = 2
D_EXPERT = 1024
PAGE_SIZE = 128
RMS_EPS = 1e-6
NEG_INF = -1e30
FORCE_SCORE = 1e9
SB_W = SB_HEADS * HEAD_DIM
NSA_QW = NSA_HEADS * HEAD_DIM
NSA_KVW = NSA_KV_HEADS * HEAD_DIM
SPARSE_IN = 3 * SB_W + NSA_QW + 6 * NSA_KVW + 3 * NSA_HEADS
DIL_W = DIL_HEADS * HEAD_DIM

LANES = 128
SUBLANES = 8
VMEM_LIMIT = 56 * 1024 * 1024

BF16 = jnp.bfloat16
F32 = jnp.float32


def _cparams(sem):
    return pltpu.CompilerParams(dimension_semantics=sem, vmem_limit_bytes=VMEM_LIMIT)


def _dot(a, b):
    return jnp.dot(a, b, preferred_element_type=F32)


def _dot_t(a, b):
    return lax.dot_general(a, b, (((1,), (1,)), ((), ())), preferred_element_type=F32)


def _split3(x):
    h1 = x.astype(BF16)
    r1 = x - h1.astype(F32)
    h2 = r1.astype(BF16)
    h3 = (r1 - h2.astype(F32)).astype(BF16)
    return h1, h2, h3


def _dot01(x, m01):
    h1, h2, h3 = _split3(x)
    return _dot(h1, m01) + _dot(h2, m01) + _dot(h3, m01)


EPI_PLAIN, EPI_NORM, EPI_SIGMOID = 0, 1, 2


def _norm_mm_kernel(kind_ref, x_ref, g_ref, w_ref, cg_ref, cf_ref, o32_ref, o16_ref, h_scr):
    j = pl.program_id(1)

    @pl.when(j == 0)
    def _():
        x = x_ref[...]
        y = x * lax.rsqrt(jnp.mean(x * x, axis=-1, keepdims=True) + RMS_EPS)
        h_scr[...] = (y * g_ref[...]).astype(BF16)

    acc = _dot(h_scr[...], w_ref[...])
    kind = kind_ref[j]

    @pl.when(kind == EPI_PLAIN)
    def _():
        o32_ref[...] = acc
        o16_ref[...] = acc.astype(BF16)

    @pl.when(kind == EPI_NORM)
    def _():
        tn = acc.shape[1]
        parts = []
        for c in range(tn // HEAD_DIM):
            a = acc[:, c * HEAD_DIM:(c + 1) * HEAD_DIM]
            n = a * lax.rsqrt(jnp.mean(a * a, axis=-1, keepdims=True) + RMS_EPS)
            parts.append(n)
        normed = jnp.concatenate(parts, axis=1) * cg_ref[...]
        out = jnp.where(cf_ref[...] > 0.5, normed, acc)
        o32_ref[...] = out
        o16_ref[...] = out.astype(BF16)

    @pl.when(kind == EPI_SIGMOID)
    def _():
        out = jax.nn.sigmoid(acc)
        o32_ref[...] = out
        o16_ref[...] = out.astype(BF16)


def norm_mm(x, g, w16, col_gain, col_flag, tile_kind, *, tm, tn):
    T, D = x.shape
    N = w16.shape[1]
    assert T % tm == 0 and N % tn == 0
    grid = (T // tm, N // tn)
    return pl.pallas_call(
        _norm_mm_kernel,
        grid_spec=pltpu.PrefetchScalarGridSpec(
            num_scalar_prefetch=1,
            grid=grid,
            in_specs=[
                pl.BlockSpec((tm, D), lambda i, j, k: (i, 0)),
                pl.BlockSpec((1, D), lambda i, j, k: (0, 0)),
                pl.BlockSpec((D, tn), lambda i, j, k: (0, j)),
                pl.BlockSpec((1, tn), lambda i, j, k: (0, j)),
                pl.BlockSpec((1, tn), lambda i, j, k: (0, j)),
            ],
            out_specs=[
                pl.BlockSpec((tm, tn), lambda i, j, k: (i, j)),
                pl.BlockSpec((tm, tn), lambda i, j, k: (i, j)),
            ],
            scratch_shapes=[pltpu.VMEM((tm, D), BF16)],
        ),
        out_shape=[jax.ShapeDtypeStruct((T, N), F32), jax.ShapeDtypeStruct((T, N), BF16)],
        compiler_params=_cparams(("parallel", "arbitrary")),
        name="norm_mm",
    )(tile_kind, x, g.reshape(1, D), w16, col_gain.reshape(1, N), col_flag.reshape(1, N))


def _mm_res_kernel(a_ref, w_ref, r_ref, o_ref):
    o_ref[...] = r_ref[...] + _dot(a_ref[...], w_ref[...])


def mm_res(a16, w16, res, *, tm, tn):
    T, K = a16.shape
    N = w16.shape[1]
    assert T % tm == 0 and N % tn == 0
    return pl.pallas_call(
        _mm_res_kernel,
        grid=(T // tm, N // tn),
        in_specs=[
            pl.BlockSpec((tm, K), lambda i, j: (i, 0)),
            pl.BlockSpec((K, tn), lambda i, j: (0, j)),
            pl.BlockSpec((tm, tn), lambda i, j: (i, j)),
        ],
        out_specs=pl.BlockSpec((tm, tn), lambda i, j: (i, j)),
        out_shape=jax.ShapeDtypeStruct((T, N), F32),
        compiler_params=_cparams(("parallel", "arbitrary")),
        name="mm_res",
    )(a16, w16, res)


def _softplus(z):
    return jnp.maximum(z, 0.0) + jnp.log1p(jnp.exp(-jnp.abs(z)))


def _sb_prompt_kernel(q_ref, k_ref, v_ref, o_ref, *, bq, bk, scale):
    i = pl.program_id(2)
    q = q_ref[0]
    nkb = (i * bq + bq) // bk
    qpos = i * bq + lax.broadcasted_iota(jnp.int32, (bq, 1), 0)
    upper = jnp.where(lax.broadcasted_iota(jnp.int32, (bk, bk), 0) > lax.broadcasted_iota(jnp.int32, (bk, bk), 1),
                      1.0, 0.0).astype(BF16)

    def body(t, carry):
        acc, run = carry
        kb = nkb - 1 - t
        ks = pl.multiple_of(kb * bk, bk)
        k = k_ref[0, pl.ds(ks, bk), :]
        v = v_ref[0, pl.ds(ks, bk), :]
        z = _dot_t(q, k) * scale
        kpos = ks + lax.broadcasted_iota(jnp.int32, (1, bk), 1)
        mask = kpos < qpos
        sp = _softplus(z)
        lk = jnp.where(mask, -sp, 0.0)
        h1 = lk.astype(BF16)
        h2 = (lk - h1.astype(F32)).astype(BF16)
        after = _dot(h1, upper) + _dot(h2, upper)
        w = jnp.where(mask, jnp.exp(z - sp + after + run), 0.0)
        acc = acc + _dot(w.astype(BF16), v)
        run = run + jnp.sum(lk, axis=-1, keepdims=True)
        return acc, run

    acc, _ = lax.fori_loop(0, nkb, body, (jnp.zeros((bq, HEAD_DIM), F32), jnp.zeros((bq, 1), F32)))
    o_ref[0] = acc.astype(o_ref.dtype)


def sb_prompt(p16, B, S, *, q_col, k_col, v_col, bq=128, bk=128):
    kern = functools.partial(_sb_prompt_kernel, bq=bq, bk=bk, scale=HEAD_DIM ** -0.5)
    qb, kb_, vb = q_col // HEAD_DIM, k_col // HEAD_DIM, v_col // HEAD_DIM
    return pl.pallas_call(
        kern,
        grid=(B, SB_HEADS, S // bq),
        in_specs=[
            pl.BlockSpec((1, bq, HEAD_DIM), lambda b, h, i: (b, i, qb + h)),
            pl.BlockSpec((1, S, HEAD_DIM), lambda b, h, i: (b, 0, kb_ + h)),
            pl.BlockSpec((1, S, HEAD_DIM), lambda b, h, i: (b, 0, vb + h)),
        ],
        out_specs=pl.BlockSpec((1, bq, HEAD_DIM), lambda b, h, i: (b, i, h)),
        out_shape=jax.ShapeDtypeStruct((B, S, SB_W), BF16),
        compiler_params=_cparams(("parallel", "parallel", "arbitrary")),
        name="sb_prompt",
    )(p16, p16, p16)


def _t5_bucket(dist):
    n = jnp.maximum(dist, 0)
    exact = N_BUCKETS // 2
    nf = jnp.maximum(n, exact).astype(F32)
    far = exact + (jnp.log(nf / exact) / math.log(BUCKET_MAX_DIST / exact) * (N_BUCKETS - exact)).astype(jnp.int32)
    return jnp.where(n < exact, n, jnp.minimum(far, N_BUCKETS - 1))


def _table_lookup(bucket, tab_ref, head):
    out = jnp.zeros(bucket.shape, F32)
    for k in range(N_BUCKETS):
        out = jnp.where(bucket == k, tab_ref[k, head], out)
    return out


N_BIAS_DIAG = -(-(BUCKET_MAX_DIST + QBLK) // QBLK) + 1


def _bias_tiles_kernel(tab_ref, o_ref):
    g = pl.program_id(0)
    d = pl.program_id(1)
    dist = d * QBLK + lax.broadcasted_iota(jnp.int32, (QBLK, QBLK), 0) - lax.broadcasted_iota(jnp.int32, (QBLK, QBLK), 1)
    bucket = _t5_bucket(dist)
    for r in range(NSA_REP):
        o_ref[0, 0, r * QBLK:(r + 1) * QBLK, :] = _table_lookup(bucket, tab_ref, g * NSA_REP + r)


def nsa_bias_tiles(table):
    return pl.pallas_call(
        _bias_tiles_kernel,
        grid=(NSA_KV_HEADS, N_BIAS_DIAG),
        in_specs=[pl.BlockSpec(memory_space=pltpu.SMEM)],
        out_specs=pl.BlockSpec((1, 1, NSA_REP * QBLK, QBLK), lambda g, d: (g, d, 0, 0)),
        out_shape=jax.ShapeDtypeStruct((NSA_KV_HEADS, N_BIAS_DIAG, NSA_REP * QBLK, QBLK), F32),
        compiler_params=_cparams(("parallel", "arbitrary")),
        name="nsa_bias_tiles",
    )(table)


CMP_HALF = CMP_BLOCK // CMP_STRIDE
CHUNK_W = CMP_STRIDE * 2 * NSA_KVW


def _compress_chunks(x, w_ref, pe_ref, gain):
    n = x.shape[0]
    outs = []
    for kv in range(2):
        for g in range(NSA_KV_HEADS):
            off = kv * NSA_KVW + g * HEAD_DIM
            xs = jnp.concatenate(
                [x[:, l * 2 * NSA_KVW + off:l * 2 * NSA_KVW + off + HEAD_DIM] for l in range(CMP_STRIDE)], axis=1)
            halves = []
            for j in range(CMP_HALF):
                w = w_ref[kv, j]
                pe = jnp.broadcast_to(pe_ref[kv, j], (SUBLANES, CMP_STRIDE * HEAD_DIM)).astype(BF16)
                halves.append(_dot(xs, w) + _dot(pe, w)[0:1, :])
            out = halves[0]
            for j in range(1, CMP_HALF):
                out = out + pltpu.roll(halves[j], n - j, axis=0)
            if kv == 0:
                out = out * lax.rsqrt(jnp.mean(out * out, axis=-1, keepdims=True) + RMS_EPS) * gain
            outs.append(out)
    kc = jnp.concatenate(outs[:NSA_KV_HEADS], axis=1)
    vc = jnp.concatenate(outs[NSA_KV_HEADS:], axis=1)
    return kc, vc


def _compress_prompt_kernel(x_ref, w_ref, pe_ref, g_ref, kc_ref, vc_ref):
    kc, vc = _compress_chunks(x_ref[0], w_ref, pe_ref, g_ref[...])
    kc_ref[0] = kc.astype(BF16)
    vc_ref[0] = vc.astype(BF16)


def _prep_cmp_weights(cmp_w, cmp_pe):
    w = cmp_w.reshape(2, CMP_HALF, CMP_STRIDE * HEAD_DIM, HEAD_DIM).astype(BF16)
    pe = cmp_pe.reshape(2, CMP_HALF, 1, CMP_STRIDE * HEAD_DIM)
    return w, pe


def compress_prompt(chunks16, w16, pe, ck_gain):
    B, n, _ = chunks16.shape
    return pl.pallas_call(
        _compress_prompt_kernel,
        grid=(B,),
        in_specs=[
            pl.BlockSpec((1, n, CHUNK_W), lambda b: (b, 0, 0)),
            pl.BlockSpec((2, CMP_HALF, CMP_STRIDE * HEAD_DIM, HEAD_DIM), lambda b: (0, 0, 0, 0)),
            pl.BlockSpec((2, CMP_HALF, 1, CMP_STRIDE * HEAD_DIM), lambda b: (0, 0, 0, 0)),
            pl.BlockSpec((1, HEAD_DIM), lambda b: (0, 0)),
        ],
        out_specs=[pl.BlockSpec((1, n, NSA_KVW), lambda b: (b, 0, 0))] * 2,
        out_shape=[jax.ShapeDtypeStruct((B, n, NSA_KVW), BF16)] * 2,
        compiler_params=_cparams(("parallel",)),
        name="nsa_compress_prompt",
    )(chunks16, w16, pe, ck_gain.reshape(1, HEAD_DIM))


def _flash_step(s, mask, v, m_scr, l_scr, acc_scr):
    s = jnp.where(mask, s, NEG_INF)
    m_prev = m_scr[...]
    m_new = jnp.maximum(m_prev, jnp.max(s, axis=-1, keepdims=True))
    p = jnp.where(mask, jnp.exp(s - m_new), 0.0)
    alpha = jnp.exp(m_prev - m_new)
    l_scr[...] = alpha * l_scr[...] + jnp.sum(p, axis=-1, keepdims=True)
    acc_scr[...] = alpha * acc_scr[...] + _dot(p.astype(BF16), v)
    m_scr[...] = m_new


def _flash_reset(m_scr, l_scr, acc_scr):
    m_scr[...] = jnp.full(m_scr.shape, NEG_INF, F32)
    l_scr[...] = jnp.zeros(l_scr.shape, F32)
    acc_scr[...] = jnp.zeros(acc_scr.shape, F32)


def _select_blocks(score, q_blk, n_sel):
    Q, n_s = score.shape
    blk = lax.broadcasted_iota(jnp.int32, (1, n_s), 1)
    valid = blk <= q_blk
    forced = valid & ((blk == 0) | (blk == q_blk) | (blk == q_blk - 1))
    score = jnp.where(forced, FORCE_SCORE, jnp.where(valid, score, NEG_INF))
    rank = jnp.zeros((Q, n_s), F32)
    for c in range(n_s):
        col = score[:, c:c + 1]
        before = (col > score) | ((col == score) & (blk > c))
        rank = rank + jnp.where(before, 1.0, 0.0)
    return jnp.where(rank < n_sel, 1.0, 0.0)


def _nsa_prompt_kernel(tab_ref, q_ref, kc_ref, vc_ref, ks_ref, vs_ref, kw_ref, vw_ref, gt_ref, tb_ref,
                       o_ref, m_scr, l_scr, acc_scr, sel_scr, *, n_c, scale):
    g = pl.program_id(1)
    i = pl.program_id(2)
    R = NSA_REP
    qs = i * QBLK
    q4 = q_ref[0]
    q = jnp.concatenate([q4[:, r * HEAD_DIM:(r + 1) * HEAD_DIM] for r in range(R)], axis=0)
    qpos1 = qs + lax.broadcasted_iota(jnp.int32, (QBLK, 1), 0)
    qpos = jnp.concatenate([qpos1] * R, axis=0)

    ncp = kc_ref.shape[1]
    s = _dot_t(q, kc_ref[0]) * scale
    c_end = lax.broadcasted_iota(jnp.int32, (1, ncp), 1) * CMP_STRIDE + (CMP_BLOCK - 1)
    dist1 = qpos1 - c_end
    bucket = _t5_bucket(dist1)
    bias = jnp.concatenate([_table_lookup(bucket, tab_ref, g * R + r) for r in range(R)], axis=0)
    n_idx = lax.broadcasted_iota(jnp.int32, (1, ncp), 1)
    mask = jnp.concatenate([(dist1 >= 0) & (n_idx < n_c)] * R, axis=0)
    s = jnp.where(mask, s + bias, NEG_INF)
    m = jnp.max(s, axis=-1, keepdims=True)
    p = jnp.where(mask, jnp.exp(s - m), 0.0)
    l = jnp.sum(p, axis=-1, keepdims=True)
    p = p / jnp.maximum(l, 1.0)
    o_c = _dot(p.astype(BF16), vc_ref[0])
    imp = p[0:QBLK]
    for r in range(1, R):
        imp = imp + p[r * QBLK:(r + 1) * QBLK]
    n_s = ks_ref.shape[1] // SLC_BLOCK
    ratio = SLC_BLOCK // CMP_STRIDE
    ci = lax.broadcasted_iota(jnp.int32, (ncp, n_s), 0)
    ti = lax.broadcasted_iota(jnp.int32, (ncp, n_s), 1)
    gather01 = jnp.where((ci >= ti * ratio - (CMP_HALF - 1)) & (ci < (ti + 1) * ratio), 1.0, 0.0).astype(BF16)
    score = _dot01(imp, gather01)
    sel = _select_blocks(score, qpos1 // SLC_BLOCK, min(N_SELECT, n_s))
    sel_scr[...] = sel.astype(BF16)

    per_tile = QBLK // SLC_BLOCK

    def key_mask(kb):
        kpos = kb * QBLK + lax.broadcasted_iota(jnp.int32, (1, QBLK), 1)
        return kpos, qpos - kpos

    _flash_reset(m_scr, l_scr, acc_scr)

    def slc_body(kb, carry):
        ks_ = pl.multiple_of(kb * QBLK, QBLK)
        k = ks_ref[0, pl.ds(ks_, QBLK), :]
        v = vs_ref[0, pl.ds(ks_, QBLK), :]
        _, dist = key_mask(kb)
        bi = lax.broadcasted_iota(jnp.int32, (n_s, QBLK), 0)
        ki = lax.broadcasted_iota(jnp.int32, (n_s, QBLK), 1)
        expand = jnp.where(bi == kb * per_tile + ki // SLC_BLOCK, 1.0, 0.0).astype(BF16)
        selk = _dot(sel_scr[...], expand)
        selk = jnp.concatenate([selk] * R, axis=0)
        msk = (dist >= 0) & (selk > 0.5)
        sc = _dot_t(q, k) * scale + tb_ref[0, jnp.minimum(i - kb, N_BIAS_DIAG - 1)]
        _flash_step(sc, msk, v, m_scr, l_scr, acc_scr)
        return carry

    lax.fori_loop(0, i + 1, slc_body, 0)
    o_s = acc_scr[...] / jnp.maximum(l_scr[...], 1.0)

    _flash_reset(m_scr, l_scr, acc_scr)
    n_wt = NSA_WINDOW // QBLK

    def win_body(kb, carry):
        ks_ = pl.multiple_of(kb * QBLK, QBLK)
        k = kw_ref[0, pl.ds(ks_, QBLK), :]
        v = vw_ref[0, pl.ds(ks_, QBLK), :]
        _, dist = key_mask(kb)
        msk = (dist >= 0) & (dist < NSA_WINDOW)
        sc = _dot_t(q, k) * scale + tb_ref[0, jnp.minimum(i - kb, N_BIAS_DIAG - 1)]
        _flash_step(sc, msk, v, m_scr, l_scr, acc_scr)
        return carry

    lax.fori_loop(jnp.maximum(i - n_wt, 0), i + 1, win_body, 0)
    o_w = acc_scr[...] / jnp.maximum(l_scr[...], 1.0)

    gt = gt_ref[0]
    outs = []
    for r in range(R):
        c0 = (g * R + r) * 3
        rows = slice(r * QBLK, (r + 1) * QBLK)
        lane = lax.broadcasted_iota(jnp.int32, (1, LANES), 1)
        gates = [jnp.sum(jnp.where(lane == c0 + t, gt, 0.0), axis=-1, keepdims=True) for t in range(3)]
        outs.append(gates[0] * o_c[rows] + gates[1] * o_s[rows] + gates[2] * o_w[rows])
    o_ref[0] = jnp.concatenate(outs, axis=1).astype(o_ref.dtype)


def nsa_prompt(p16, p32, kc, vc, tb, table, B, S, *, q_col, sk_col, sv_col, wk_col, wv_col, gate_col):
    G, R = NSA_KV_HEADS, NSA_REP
    n_c = S // CMP_STRIDE - CMP_HALF + 1
    kern = functools.partial(_nsa_prompt_kernel, n_c=n_c, scale=HEAD_DIM ** -0.5)
    hd = HEAD_DIM
    return pl.pallas_call(
        kern,
        grid_spec=pltpu.PrefetchScalarGridSpec(
            num_scalar_prefetch=0,
            grid=(B, G, S // QBLK),
            in_specs=[
                pl.BlockSpec(memory_space=pltpu.SMEM),
                pl.BlockSpec((1, QBLK, R * hd), lambda b, g, i: (b, i, q_col // (R * hd) + g)),
                pl.BlockSpec((1, kc.shape[1], hd), lambda b, g, i: (b, 0, g)),
                pl.BlockSpec((1, vc.shape[1], hd), lambda b, g, i: (b, 0, g)),
                pl.BlockSpec((1, S, hd), lambda b, g, i: (b, 0, sk_col // hd + g)),
                pl.BlockSpec((1, S, hd), lambda b, g, i: (b, 0, sv_col // hd + g)),
                pl.BlockSpec((1, S, hd), lambda b, g, i: (b, 0, wk_col // hd + g)),
                pl.BlockSpec((1, S, hd), lambda b, g, i: (b, 0, wv_col // hd + g)),
                pl.BlockSpec((1, QBLK, LANES), lambda b, g, i: (b, i, gate_col // LANES)),
                pl.BlockSpec((1, N_BIAS_DIAG, R * QBLK, QBLK), lambda b, g, i: (g, 0, 0, 0)),
            ],
            out_specs=pl.BlockSpec((1, QBLK, R * hd), lambda b, g, i: (b, i, g)),
            scratch_shapes=[
                pltpu.VMEM((R * QBLK, LANES), F32),
                pltpu.VMEM((R * QBLK, LANES), F32),
                pltpu.VMEM((R * QBLK, hd), F32),
                pltpu.VMEM((QBLK, S // SLC_BLOCK), BF16),
            ],
        ),
        out_shape=jax.ShapeDtypeStruct((B, S, NSA_QW), BF16),
        compiler_params=_cparams(("parallel", "parallel", "arbitrary")),
        name="nsa_prompt",
    )(table, p16, kc, vc, p16, p16, p16, p16, p32, tb)


ROUTER_W = LANES
MOE_TM = 256


def _router_kernel(x_ref, g_ref, w1_ref, w2_ref, b_ref, h_ref, info_ref):
    x = x_ref[...]
    hn = x * lax.rsqrt(jnp.mean(x * x, axis=-1, keepdims=True) + RMS_EPS) * g_ref[...]
    a1 = hn.astype(BF16)
    h_ref[...] = a1
    a2 = (hn - a1.astype(F32)).astype(BF16)
    logits = _dot(a1, w1_ref[...]) + _dot(a1, w2_ref[...]) + _dot(a2, w1_ref[...]) + b_ref[...]
    lane = lax.broadcasted_iota(jnp.int32, (1, ROUTER_W), 1)
    big = ROUTER_W

    def first_lane(cond):
        return jnp.min(jnp.where(cond, lane, big), axis=-1, keepdims=True)

    gmask = lane < N_GROUPS
    gl = jnp.where(gmask, logits, NEG_INF)
    gm = jnp.max(gl, axis=-1, keepdims=True)
    g_idx = first_lane(gmask & (gl == gm))
    g_w = 1.0 / jnp.sum(jnp.where(gmask, jnp.exp(gl - gm), 0.0), axis=-1, keepdims=True)
    lo = N_GROUPS + g_idx * EXPERTS_PER_GROUP
    emask = (lane >= lo) & (lane < lo + EXPERTS_PER_GROUP)
    el = jnp.where(emask, logits, NEG_INF)
    em = jnp.max(el, axis=-1, keepdims=True)
    ee = jnp.where(emask, jnp.exp(el - em), 0.0)
    pe = jnp.where(emask, ee / jnp.sum(ee, axis=-1, keepdims=True), -1.0)
    v1 = jnp.max(pe, axis=-1, keepdims=True)
    i1 = first_lane(pe == v1)
    pe2 = jnp.where(lane == i1, -1.0, pe)
    v2 = jnp.max(pe2, axis=-1, keepdims=True)
    i2 = first_lane((pe2 == v2) & emask & (lane != i1))
    tot = v1 + v2
    info = jnp.where(lane == 0, (i1 - N_GROUPS).astype(F32), 0.0)
    info = jnp.where(lane == 1, (i2 - N_GROUPS).astype(F32), info)
    info = jnp.where(lane == 2, g_w * (v1 / tot), info)
    info = jnp.where(lane == 3, g_w * (v2 / tot), info)
    info_ref[...] = info


def moe_router(x, g, w_group, b_group, w_expert, b_expert, *, tm):
    T, D = x.shape
    pad = ROUTER_W - N_GROUPS - N_EXPERTS
    w = jnp.pad(jnp.concatenate([w_group, w_expert], axis=1), ((0, 0), (0, pad)))
    b = jnp.pad(jnp.concatenate([b_group, b_expert]), (0, pad)).reshape(1, ROUTER_W)
    w1 = w.astype(BF16)
    w2 = (w - w1.astype(F32)).astype(BF16)
    return pl.pallas_call(
        _router_kernel,
        grid=(T // tm,),
        in_specs=[
            pl.BlockSpec((tm, D), lambda i: (i, 0)),
            pl.BlockSpec((1, D), lambda i: (0, 0)),
            pl.BlockSpec((D, ROUTER_W), lambda i: (0, 0)),
            pl.BlockSpec((D, ROUTER_W), lambda i: (0, 0)),
            pl.BlockSpec((1, ROUTER_W), lambda i: (0, 0)),
        ],
        out_specs=[pl.BlockSpec((tm, D), lambda i: (i, 0)), pl.BlockSpec((tm, ROUTER_W), lambda i: (i, 0))],
        out_shape=[jax.ShapeDtypeStruct((T, D), BF16), jax.ShapeDtypeStruct((T, ROUTER_W), F32)],
        compiler_params=_cparams(("parallel",)),
        name="moe_router",
    )(x, g.reshape(1, D), w1, w2, b)


def _moe_ffn_kernel(te_ref, nu_ref, x_ref, gw_ref, wg_ref, wu_ref, wd_ref, y_ref):
    i = pl.program_id(0)

    @pl.when(i < nu_ref[0])
    def _():
        x = x_ref[...]
        h = _dot(x, wg_ref[0])
        u = _dot(x, wu_ref[0])
        a = (h * jax.nn.sigmoid(h)) * u * gw_ref[...]
        y_ref[...] = _dot(a.astype(BF16), wd_ref[0])

    @pl.when(i >= nu_ref[0])
    def _():
        y_ref[...] = jnp.zeros(y_ref.shape, F32)


def moe_ffn(xs, gate_rows, tile_expert, n_used, wg16, wu16, wd16):
    M, D = xs.shape
    F = wg16.shape[2]
    n_tiles = M // MOE_TM
    return pl.pallas_call(
        _moe_ffn_kernel,
        grid_spec=pltpu.PrefetchScalarGridSpec(
            num_scalar_prefetch=2,
            grid=(n_tiles,),
            in_specs=[
                pl.BlockSpec((MOE_TM, D), lambda i, te, nu: (i, 0)),
                pl.BlockSpec((MOE_TM, 1), lambda i, te, nu: (i, 0)),
                pl.BlockSpec((1, D, F), lambda i, te, nu: (te[i], 0, 0)),
                pl.BlockSpec((1, D, F), lambda i, te, nu: (te[i], 0, 0)),
                pl.BlockSpec((1, F, D), lambda i, te, nu: (te[i], 0, 0)),
            ],
            out_specs=pl.BlockSpec((MOE_TM, D), lambda i, te, nu: (i, 0)),
        ),
        out_shape=jax.ShapeDtypeStruct((M, D), F32),
        compiler_params=_cparams(("arbitrary",)),
        name="moe_ffn",
    )(tile_expert, n_used, xs, gate_rows, wg16, wu16, wd16)


def moe_dispatch(info):
    T = info.shape[0]
    eid = info[:, :TOP_K_INNER].astype(jnp.int32).reshape(-1)
    wts = info[:, TOP_K_INNER:2 * TOP_K_INNER].reshape(-1)
    M = T * TOP_K_INNER
    n_tiles = -(-(M + N_EXPERTS * (MOE_TM - 1)) // MOE_TM)
    onehot = (eid[:, None] == jnp.arange(N_EXPERTS)[None, :]).astype(jnp.int32)
    rank = jnp.cumsum(onehot, axis=0) - onehot
    counts = jnp.sum(onehot, axis=0)
    tiles_per = (counts + MOE_TM - 1) // MOE_TM
    tile_end = jnp.cumsum(tiles_per)
    offs = (tile_end - tiles_per) * MOE_TM
    dest = jnp.sum(onehot * (offs[None, :] + rank), axis=1)
    n_used = tile_end[-1]
    tile_ids = jnp.arange(n_tiles)
    tile_expert = jnp.minimum(jnp.sum(tile_ids[:, None] >= tile_end[None, :], axis=1), N_EXPERTS - 1)
    last_e = jnp.max(jnp.where(counts > 0, jnp.arange(N_EXPERTS), 0))
    tile_expert = jnp.where(tile_ids < n_used, tile_expert, last_e).astype(jnp.int32)
    src = jnp.zeros((n_tiles * MOE_TM,), jnp.int32).at[dest].set(jnp.arange(M, dtype=jnp.int32) // TOP_K_INNER)
    gate_rows = jnp.zeros((n_tiles * MOE_TM,), F32).at[dest].set(wts)
    return src, gate_rows.reshape(-1, 1), dest.reshape(T, TOP_K_INNER), tile_expert, n_used.reshape(1).astype(jnp.int32)


def moe_layer(xp, xs, g, w_group, b_group, w_expert, b_expert, wg16, wu16, wd16):
    Tp, Ts = xp.shape[0], xs.shape[0]
    hp, ip = moe_router(xp, g, w_group, b_group, w_expert, b_expert, tm=512)
    hs, is_ = moe_router(xs, g, w_group, b_group, w_expert, b_expert, tm=Ts)
    h = jnp.concatenate([hp, hs], axis=0)
    info = jnp.concatenate([ip, is_], axis=0)
    src, gate_rows, dest, tile_expert, n_used = moe_dispatch(info)
    y = moe_ffn(jnp.take(h, src, axis=0), gate_rows, tile_expert, n_used, wg16, wu16, wd16)
    f = jnp.take(y, dest[:, 0], axis=0) + jnp.take(y, dest[:, 1], axis=0)
    return xp + f[:Tp], xs + f[Tp:]


N_DIL = len(DIL_PATTERNS)
DIL_SPANS = tuple(w // d for w, d in DIL_PATTERNS)
DIL_STEPS = tuple(d for _, d in DIL_PATTERNS)
assert all(s <= QBLK for s in DIL_SPANS)
assert all(d & (d - 1) == 0 for d in DIL_STEPS)


def _dil_bias_kernel(dil_ref, tab_ref, o_ref):
    p = pl.program_id(0)
    dm = QBLK + lax.broadcasted_iota(jnp.int32, (QBLK, 2 * QBLK), 0) - lax.broadcasted_iota(jnp.int32, (QBLK, 2 * QBLK), 1)
    bucket = _t5_bucket(dm * dil_ref[p])
    for h in range(DIL_HEADS):
        o_ref[0, h] = _table_lookup(bucket, tab_ref, h)


def dil_bias_tiles(table):
    return pl.pallas_call(
        _dil_bias_kernel,
        grid=(N_DIL,),
        in_specs=[pl.BlockSpec(memory_space=pltpu.SMEM), pl.BlockSpec(memory_space=pltpu.SMEM)],
        out_specs=pl.BlockSpec((1, DIL_HEADS, QBLK, 2 * QBLK), lambda p: (p, 0, 0, 0)),
        out_shape=jax.ShapeDtypeStruct((N_DIL, DIL_HEADS, QBLK, 2 * QBLK), F32),
        compiler_params=_cparams(("arbitrary",)),
        name="dil_bias_tiles",
    )(jnp.array(DIL_STEPS, jnp.int32), table)


def _dil_prompt_kernel(q_ref, kp_ref, kc_ref, vp_ref, vc_ref, b_ref, num_ref, ml_ref, *, span, scale):
    mb = pl.program_id(2)
    qi = lax.broadcasted_iota(jnp.int32, (QBLK, 2 * QBLK), 0)
    kj = lax.broadcasted_iota(jnp.int32, (QBLK, 2 * QBLK), 1)
    dm = QBLK + qi - kj
    mask = (dm >= 0) & (dm <= span) & ((kj >= QBLK) | (mb > 0))
    lane = lax.broadcasted_iota(jnp.int32, (1, LANES), 1)
    m_t = jnp.zeros((QBLK, LANES), F32)
    l_t = jnp.zeros((QBLK, LANES), F32)
    for h in range(DIL_HEADS):
        cs = slice(h * HEAD_DIM, (h + 1) * HEAD_DIM)
        k = jnp.concatenate([kp_ref[0, :, cs], kc_ref[0, :, cs]], axis=0)
        v = jnp.concatenate([vp_ref[0, :, cs], vc_ref[0, :, cs]], axis=0)
        s = _dot_t(q_ref[0, :, cs], k) * scale + b_ref[0, h]
        s = jnp.where(mask, s, NEG_INF)
        m = jnp.max(s, axis=-1, keepdims=True)
        p = jnp.where(mask, jnp.exp(s - m), 0.0)
        l = jnp.sum(p, axis=-1, keepdims=True)
        num_ref[0, :, cs] = _dot(p.astype(BF16), v)
        m_t = jnp.where(lane == h, m, m_t)
        l_t = jnp.where(lane == h, l, l_t)
    ml_ref[0, :, 0:LANES] = m_t
    ml_ref[0, :, LANES:2 * LANES] = l_t


def dil_prompt_pattern(p16, bias, pat, B, S):
    span, dil = DIL_SPANS[pat], DIL_STEPS[pat]
    M = S // dil
    assert M % QBLK == 0
    pv = p16.reshape(B, M, dil * 3 * DIL_W)
    kern = functools.partial(_dil_prompt_kernel, span=span, scale=HEAD_DIM ** -0.5)
    blk = (1, QBLK, DIL_W)
    prev = lambda m: jnp.maximum(m - 1, 0)
    num, ml = pl.pallas_call(
        kern,
        grid=(B, dil, M // QBLK),
        in_specs=[
            pl.BlockSpec(blk, lambda b, r, m: (b, m, 3 * r)),
            pl.BlockSpec(blk, lambda b, r, m: (b, prev(m), 3 * r + 1)),
            pl.BlockSpec(blk, lambda b, r, m: (b, m, 3 * r + 1)),
            pl.BlockSpec(blk, lambda b, r, m: (b, prev(m), 3 * r + 2)),
            pl.BlockSpec(blk, lambda b, r, m: (b, m, 3 * r + 2)),
            pl.BlockSpec((1, DIL_HEADS, QBLK, 2 * QBLK), lambda b, r, m: (pat, 0, 0, 0)),
        ],
        out_specs=[
            pl.BlockSpec(blk, lambda b, r, m: (b, m, r)),
            pl.BlockSpec((1, QBLK, 2 * LANES), lambda b, r, m: (b, m, r)),
        ],
        out_shape=[jax.ShapeDtypeStruct((B, M, dil * DIL_W), F32), jax.ShapeDtypeStruct((B, M, dil * 2 * LANES), F32)],
        compiler_params=_cparams(("parallel", "parallel", "arbitrary")),
        name=f"dil_prompt_{dil}",
    )(pv, pv, pv, pv, pv, bias)
    return num.reshape(B * S, DIL_W), ml.reshape(B * S, 2 * LANES)


def _dil_merge_kernel(*refs):
    nums, mls, o_ref = refs[:N_DIL], refs[N_DIL:2 * N_DIL], refs[2 * N_DIL]
    m_all = [r[:, 0:LANES] for r in mls]
    l_all = [r[:, LANES:2 * LANES] for r in mls]
    m_max = m_all[0]
    for m in m_all[1:]:
        m_max = jnp.maximum(m_max, m)
    w_all = [jnp.exp(m - m_max) for m in m_all]
    den = w_all[0] * l_all[0]
    for w, l in zip(w_all[1:], l_all[1:]):
        den = den + w * l
    for h in range(DIL_HEADS):
        cs = slice(h * HEAD_DIM, (h + 1) * HEAD_DIM)
        acc = w_all[0][:, h:h + 1] * nums[0][:, cs]
        for w, n in zip(w_all[1:], nums[1:]):
            acc = acc + w[:, h:h + 1] * n[:, cs]
        o_ref[:, cs] = (acc / den[:, h:h + 1]).astype(o_ref.dtype)


def dil_merge(nums, mls, *, tm):
    T = nums[0].shape[0]
    return pl.pallas_call(
        _dil_merge_kernel,
        grid=(T // tm,),
        in_specs=[pl.BlockSpec((tm, DIL_W), lambda i: (i, 0))] * N_DIL + [pl.BlockSpec((tm, 2 * LANES), lambda i: (i, 0))] * N_DIL,
        out_specs=pl.BlockSpec((tm, DIL_W), lambda i: (i, 0)),
        out_shape=jax.ShapeDtypeStruct((T, DIL_W), BF16),
        compiler_params=_cparams(("parallel",)),
        name="dil_merge",
    )(*nums, *mls)


def _block_diag_rows(q, n_heads):
    DB, Q, W = q.shape
    dh = W // n_heads
    eye = jnp.repeat(jnp.eye(n_heads, dtype=q.dtype), dh, axis=1)
    return (q[:, None, :, :] * eye[None, :, None, :]).reshape(DB, n_heads * Q, W)


def _own_block(acc, n_heads, rows_per_head):
    rows = acc.shape[0]
    rh = lax.broadcasted_iota(jnp.int32, (rows, 1), 0) // rows_per_head
    out = jnp.zeros((rows, HEAD_DIM), F32)
    for h in range(n_heads):
        out = jnp.where(rh == h, acc[:, h * HEAD_DIM:(h + 1) * HEAD_DIM], out)
    return out


def _row_bias(bucket, tab_ref, head0, n_heads, rows_per_head):
    rows = bucket.shape[0]
    rh = lax.broadcasted_iota(jnp.int32, (rows, 1), 0) // rows_per_head
    out = jnp.zeros(bucket.shape, F32)
    for h in range(n_heads):
        out = jnp.where(rh == h, _table_lookup(bucket, tab_ref, head0 + h), out)
    return out


def _flash_step_cols(s, mask, v, m_scr, l_scr, acc_scr):
    s = jnp.where(mask, s, NEG_INF)
    m_prev = m_scr[:, 0:1]
    m_new = jnp.maximum(m_prev, jnp.max(s, axis=-1, keepdims=True))
    p = jnp.where(mask, jnp.exp(s - m_new), 0.0)
    alpha = jnp.exp(m_prev - m_new)
    l_new = alpha * l_scr[:, 0:1] + jnp.sum(p, axis=-1, keepdims=True)
    acc_scr[...] = alpha * acc_scr[...] + _dot(p.astype(BF16), v)
    m_scr[...] = jnp.broadcast_to(m_new, m_scr.shape)
    l_scr[...] = jnp.broadcast_to(l_new, l_scr.shape)


def _sb_decode_kernel(pt_ref, q_ref, new_ref, page_ref, o_ref, acc_scr, run_scr, *, n_pages, n_new, n_q, past, scale):
    p = pl.program_id(1)
    q = q_ref[0]
    rows = q.shape[0]
    upper = jnp.where(lax.broadcasted_iota(jnp.int32, (PAGE_SIZE, PAGE_SIZE), 0)
                      > lax.broadcasted_iota(jnp.int32, (PAGE_SIZE, PAGE_SIZE), 1), 1.0, 0.0).astype(BF16)

    def absorb(kv, mask):
        k = kv[:, :SB_W].astype(BF16)
        v = kv[:, SB_W:].astype(BF16)
        z = _dot_t(q, k) * scale
        sp = _softplus(z)
        lk = -sp if mask is None else jnp.where(mask, -sp, 0.0)
        h1 = lk.astype(BF16)
        h2 = (lk - h1.astype(F32)).astype(BF16)
        after = _dot(h1, upper) + _dot(h2, upper)
        w = jnp.exp(z - sp + after + run_scr[:, 0:1])
        if mask is not None:
            w = jnp.where(mask, w, 0.0)
        acc_scr[...] = acc_scr[...] + _dot(w.astype(BF16), v)
        run_scr[...] = run_scr[...] + jnp.sum(lk, axis=-1, keepdims=True)

    @pl.when(p == 0)
    def _():
        acc_scr[...] = jnp.zeros(acc_scr.shape, F32)
        run_scr[...] = jnp.zeros(run_scr.shape, F32)
        t = lax.broadcasted_iota(jnp.int32, (rows, PAGE_SIZE), 1)
        qi = lax.broadcasted_iota(jnp.int32, (rows, PAGE_SIZE), 0) % n_q
        absorb(new_ref[0], (t < qi) & (t < n_new))

    absorb(page_ref[0], None)

    @pl.when(p == n_pages - 1)
    def _():
        o_ref[0] = _own_block(acc_scr[...], SB_HEADS, n_q)


def sb_decode(page_table, qbd16, new_kv, cache):
    DB, n_pages = page_table.shape
    rows = qbd16.shape[1]
    n_q = rows // SB_HEADS
    kern = functools.partial(_sb_decode_kernel, n_pages=n_pages, n_new=n_q, n_q=n_q, past=n_pages * PAGE_SIZE,
                             scale=HEAD_DIM ** -0.5)
    return pl.pallas_call(
        kern,
        grid_spec=pltpu.PrefetchScalarGridSpec(
            num_scalar_prefetch=1,
            grid=(DB, n_pages),
            in_specs=[
                pl.BlockSpec((1, rows, SB_W), lambda b, p, pt: (b, 0, 0)),
                pl.BlockSpec((1, PAGE_SIZE, 2 * SB_W), lambda b, p, pt: (b, 0, 0)),
                pl.BlockSpec((1, PAGE_SIZE, 2 * SB_W), lambda b, p, pt: (pt[b, n_pages - 1 - p], 0, 0)),
            ],
            out_specs=pl.BlockSpec((1, rows, HEAD_DIM), lambda b, p, pt: (b, 0, 0)),
            scratch_shapes=[pltpu.VMEM((rows, SB_W), F32), pltpu.VMEM((rows, LANES), F32)],
        ),
        out_shape=jax.ShapeDtypeStruct((DB, rows, HEAD_DIM), F32),
        compiler_params=_cparams(("parallel", "arbitrary")),
        name="sb_decode",
    )(page_table, qbd16, new_kv, cache)


CMP_ROWS_TILE = 256
CHUNKS_PER_PAGE = PAGE_SIZE // CMP_STRIDE


def _nsa_cmp_decode_kernel(pt_ref, tab_ref, q_ref, pg0_ref, pg1_ref, w_ref, pe_ref, g_ref, oc_ref, sel_ref,
                           x_scr, a_scr, *, n_steps, n_q, past, scale):
    s_ = pl.program_id(1)
    rows16 = 2 * CHUNKS_PER_PAGE
    x_scr[pl.ds(pl.multiple_of(s_ * rows16, rows16), rows16), :] = jnp.concatenate(
        [pg0_ref[0], pg1_ref[0]], axis=0).astype(BF16)

    @pl.when(s_ == n_steps - 1)
    def _():
        n_chunk = x_scr.shape[0]
        n_c = n_chunk - CMP_HALF + 1
        G, R = NSA_KV_HEADS, NSA_REP
        for rt in range(n_chunk // CMP_ROWS_TILE):
            rs = slice(rt * CMP_ROWS_TILE, (rt + 1) * CMP_ROWS_TILE)
            for kv in range(2):
                for g in range(G):
                    off = kv * NSA_KVW + g * HEAD_DIM
                    xs = jnp.concatenate(
                        [x_scr[rs, l * 2 * NSA_KVW + off:l * 2 * NSA_KVW + off + HEAD_DIM] for l in range(CMP_STRIDE)], axis=1)
                    for j in range(CMP_HALF):
                        w = w_ref[kv, j]
                        pe = jnp.broadcast_to(pe_ref[kv, j], (SUBLANES, CMP_STRIDE * HEAD_DIM)).astype(BF16)
                        a_scr[j, rs, off:off + HEAD_DIM] = _dot(xs, w) + _dot(pe, w)[0:1, :]
        out = a_scr[0]
        for j in range(1, CMP_HALF):
            out = out + pltpu.roll(a_scr[j], n_chunk - j, axis=0)
        kparts = []
        for g in range(G):
            kg = out[:, g * HEAD_DIM:(g + 1) * HEAD_DIM]
            kparts.append(kg * lax.rsqrt(jnp.mean(kg * kg, axis=-1, keepdims=True) + RMS_EPS) * g_ref[...])
        kc = jnp.concatenate(kparts, axis=1).astype(BF16)
        vc = out[:, NSA_KVW:2 * NSA_KVW].astype(BF16)

        q = q_ref[0]
        nrows = q.shape[0]
        qi = lax.broadcasted_iota(jnp.int32, (nrows, 1), 0) % n_q
        n_idx = lax.broadcasted_iota(jnp.int32, (1, n_chunk), 1)
        dist = past + qi - (n_idx * CMP_STRIDE + (CMP_BLOCK - 1))
        mask = (dist >= 0) & (n_idx < n_c)
        s = _dot_t(q, kc) * scale + _row_bias(_t5_bucket(dist), tab_ref, 0, NSA_HEADS, n_q)
        s = jnp.where(mask, s, NEG_INF)
        m = jnp.max(s, axis=-1, keepdims=True)
        p = jnp.where(mask, jnp.exp(s - m), 0.0)
        p = p / jnp.maximum(jnp.sum(p, axis=-1, keepdims=True), 1.0)
        oc_ref[0] = _own_block(_dot(p.astype(BF16), vc), G, R * n_q)
        gq = G * n_q
        oi = lax.broadcasted_iota(jnp.int32, (gq, nrows), 0)
        ri = lax.broadcasted_iota(jnp.int32, (gq, nrows), 1)
        pick = jnp.where((ri // (R * n_q) == oi // n_q) & (ri % n_q == oi % n_q), 1.0, 0.0).astype(BF16)
        p1, p2, p3 = _split3(p)
        imp = _dot(pick, p1) + _dot(pick, p2) + _dot(pick, p3)
        n_sp = sel_ref.shape[2]
        ratio = SLC_BLOCK // CMP_STRIDE
        ci = lax.broadcasted_iota(jnp.int32, (n_chunk, n_sp), 0)
        ti = lax.broadcasted_iota(jnp.int32, (n_chunk, n_sp), 1)
        gather01 = jnp.where((ci >= ti * ratio - (CMP_HALF - 1)) & (ci < (ti + 1) * ratio), 1.0, 0.0).astype(BF16)
        score = _dot01(imp, gather01)
        q_blk = (past + lax.broadcasted_iota(jnp.int32, (gq, 1), 0) % n_q) // SLC_BLOCK
        sel_ref[0] = _select_blocks(score, q_blk, N_SELECT)


def nsa_cmp_decode(page_table, table, qbd16, cache_chunks, w16, pe, ck_gain, n_q):
    DB, n_pages = page_table.shape
    assert n_pages % 2 == 0
    n_steps = n_pages // 2
    n_chunk = n_pages * CHUNKS_PER_PAGE
    assert n_chunk % CMP_ROWS_TILE == 0
    past = n_pages * PAGE_SIZE
    n_s = -(-(past + n_q) // SLC_BLOCK)
    n_sp = -(-n_s // LANES) * LANES
    rows = qbd16.shape[1]
    kern = functools.partial(_nsa_cmp_decode_kernel, n_steps=n_steps, n_q=n_q, past=past, scale=HEAD_DIM ** -0.5)
    pg = (1, CHUNKS_PER_PAGE, CHUNK_W)
    return pl.pallas_call(
        kern,
        grid_spec=pltpu.PrefetchScalarGridSpec(
            num_scalar_prefetch=1,
            grid=(DB, n_steps),
            in_specs=[
                pl.BlockSpec(memory_space=pltpu.SMEM),
                pl.BlockSpec((1, rows, NSA_KVW), lambda b, s, pt: (b, 0, 0)),
                pl.BlockSpec(pg, lambda b, s, pt: (pt[b, 2 * s], 0, 0)),
                pl.BlockSpec(pg, lambda b, s, pt: (pt[b, 2 * s + 1], 0, 0)),
                pl.BlockSpec((2, CMP_HALF, CMP_STRIDE * HEAD_DIM, HEAD_DIM), lambda b, s, pt: (0, 0, 0, 0)),
                pl.BlockSpec((2, CMP_HALF, 1, CMP_STRIDE * HEAD_DIM), lambda b, s, pt: (0, 0, 0, 0)),
                pl.BlockSpec((1, HEAD_DIM), lambda b, s, pt: (0, 0)),
            ],
            out_specs=[
                pl.BlockSpec((1, rows, HEAD_DIM), lambda b, s, pt: (b, 0, 0)),
                pl.BlockSpec((1, NSA_KV_HEADS * n_q, n_sp), lambda b, s, pt: (b, 0, 0)),
            ],
            scratch_shapes=[
                pltpu.VMEM((n_chunk, CHUNK_W), BF16),
                pltpu.VMEM((CMP_HALF, n_chunk, 2 * NSA_KVW), F32),
            ],
        ),
        out_shape=[jax.ShapeDtypeStruct((DB, rows, HEAD_DIM), F32),
                   jax.ShapeDtypeStruct((DB, NSA_KV_HEADS * n_q, n_sp), F32)],
        compiler_params=_cparams(("parallel", "arbitrary")),
        name="nsa_cmp_decode",
    )(page_table, table, qbd16, cache_chunks, cache_chunks, w16, pe, ck_gain.reshape(1, HEAD_DIM))


def _nsa_slc_decode_kernel(pt_ref, tab_ref, q_ref, sel_ref, page_ref, new_ref, wbuf_ref, wnew_ref, oc_ref, gt_ref,
                           o_ref, m_scr, l_scr, acc_scr, *, n_pages, n_q, past, scale):
    p = pl.program_id(1)
    q = q_ref[0]
    rows = q.shape[0]
    G, R = NSA_KV_HEADS, NSA_REP
    qi = lax.broadcasted_iota(jnp.int32, (rows, 1), 0) % n_q
    per_page = PAGE_SIZE // SLC_BLOCK
    n_sp = sel_ref.shape[2]

    def sel_cols(first_block):
        bi = lax.broadcasted_iota(jnp.int32, (n_sp, PAGE_SIZE), 0)
        ki = lax.broadcasted_iota(jnp.int32, (n_sp, PAGE_SIZE), 1)
        expand = jnp.where(bi == first_block + ki // SLC_BLOCK, 1.0, 0.0).astype(BF16)
        return _dot(sel_ref[0], expand) > 0.5

    def bias_for(dist):
        return _row_bias(_t5_bucket(dist), tab_ref, 0, NSA_HEADS, n_q)

    def attend(kv, dist, mask):
        k = kv[:, :NSA_KVW].astype(BF16)
        v = kv[:, NSA_KVW:].astype(BF16)
        s = _dot_t(q, k) * scale + bias_for(dist)
        _flash_step_cols(s, mask, v, m_scr, l_scr, acc_scr)

    def result():
        return _own_block(acc_scr[...] / jnp.maximum(l_scr[:, 0:1], 1.0), G, R * n_q)

    @pl.when(p == 0)
    def _():
        _flash_reset(m_scr, l_scr, acc_scr)

    col = lax.broadcasted_iota(jnp.int32, (1, PAGE_SIZE), 1)
    dist = past + qi - (p * PAGE_SIZE + col)
    attend(page_ref[0], dist, sel_cols(p * per_page) & (dist >= 0))

    @pl.when(p == n_pages - 1)
    def _():
        dist_n = qi - col
        attend(new_ref[0], dist_n, sel_cols(past // SLC_BLOCK) & (dist_n >= 0) & (col < n_q))
        o_s = result()
        _flash_reset(m_scr, l_scr, acc_scr)
        wb = wbuf_ref.shape[1]
        for t in range(wb // PAGE_SIZE):
            dist_w = wb + qi - (t * PAGE_SIZE + col)
            attend(wbuf_ref[0, t * PAGE_SIZE:(t + 1) * PAGE_SIZE, :], dist_w, (dist_w >= 0) & (dist_w < NSA_WINDOW))
        attend(wnew_ref[0], dist_n, (dist_n >= 0) & (col < n_q))
        o_w = result()
        gt = gt_ref[0]
        o_ref[0] = gt[:, 0:1] * oc_ref[0] + gt[:, 1:2] * o_s + gt[:, 2:3] * o_w


def nsa_slc_decode(page_table, table, qbd16, sel16, cache_pages, new_kv, win_buf, win_new, o_c, gates, n_q):
    DB, n_pages = page_table.shape
    rows = qbd16.shape[1]
    past = n_pages * PAGE_SIZE
    wb = win_buf.shape[1]
    assert wb % PAGE_SIZE == 0
    kern = functools.partial(_nsa_slc_decode_kernel, n_pages=n_pages, n_q=n_q, past=past, scale=HEAD_DIM ** -0.5)
    full = lambda shape: pl.BlockSpec((1,) + shape, lambda b, p, pt: (b, 0, 0))
    return pl.pallas_call(
        kern,
        grid_spec=pltpu.PrefetchScalarGridSpec(
            num_scalar_prefetch=1,
            grid=(DB, n_pages),
            in_specs=[
                pl.BlockSpec(memory_space=pltpu.SMEM),
                full((rows, NSA_KVW)),
                full((rows, sel16.shape[2])),
                pl.BlockSpec((1, PAGE_SIZE, 2 * NSA_KVW), lambda b, p, pt: (pt[b, p], 0, 0)),
                full((PAGE_SIZE, 2 * NSA_KVW)),
                full((wb, 2 * NSA_KVW)),
                full((PAGE_SIZE, 2 * NSA_KVW)),
                full((rows, HEAD_DIM)),
                full((rows, LANES)),
            ],
            out_specs=full((rows, HEAD_DIM)),
            scratch_shapes=[pltpu.VMEM((rows, LANES), F32), pltpu.VMEM((rows, LANES), F32),
                            pltpu.VMEM((rows, NSA_KVW), F32)],
        ),
        out_shape=jax.ShapeDtypeStruct((DB, rows, HEAD_DIM), F32),
        compiler_params=_cparams(("parallel", "arbitrary")),
        name="nsa_slc_decode",
    )(page_table, table, qbd16, sel16, cache_pages, new_kv, win_buf, win_new, o_c, gates)


DIL_DEC_TILE = 256


def _dil_decode_kernel(tab_ref, q_ref, buf_ref, new_ref, o_ref, m_scr, l_scr, acc_scr, *, n_tiles, n_q, wb, scale):
    kt = pl.program_id(1)
    q = q_ref[0]
    rows = q.shape[0]
    qi = lax.broadcasted_iota(jnp.int32, (rows, 1), 0) % n_q

    @pl.when(kt == 0)
    def _():
        _flash_reset(m_scr, l_scr, acc_scr)

    def attend(kv, dist, valid):
        k = kv[:, :DIL_W].astype(BF16)
        v = kv[:, DIL_W:].astype(BF16)
        s = _dot_t(q, k) * scale + _row_bias(_t5_bucket(dist), tab_ref, 0, DIL_HEADS, n_q)
        ps = []
        for i, (span, dil) in enumerate(zip(DIL_SPANS, DIL_STEPS)):
            mask = valid & (dist >= 0) & (jnp.bitwise_and(dist, dil - 1) == 0) & (dist <= span * dil)
            rs = slice(i * rows, (i + 1) * rows)
            si = jnp.where(mask, s, NEG_INF)
            m_prev = m_scr[rs, 0:1]
            m_new = jnp.maximum(m_prev, jnp.max(si, axis=-1, keepdims=True))
            p = jnp.where(mask, jnp.exp(si - m_new), 0.0)
            alpha = jnp.exp(m_prev - m_new)
            l_scr[rs, :] = jnp.broadcast_to(alpha * l_scr[rs, 0:1] + jnp.sum(p, axis=-1, keepdims=True), (rows, LANES))
            m_scr[rs, :] = jnp.broadcast_to(m_new, (rows, LANES))
            acc_scr[rs, :] = alpha * acc_scr[rs, :]
            ps.append(p.astype(BF16))
        acc_scr[...] = acc_scr[...] + _dot(jnp.concatenate(ps, axis=0), v)

    n = buf_ref.shape[1]
    col = lax.broadcasted_iota(jnp.int32, (1, n), 1)
    attend(buf_ref[0], wb + qi - (kt * n + col), col >= 0)

    @pl.when(kt == n_tiles - 1)
    def _():
        cn = lax.broadcasted_iota(jnp.int32, (1, new_ref.shape[1]), 1)
        attend(new_ref[0], qi - cn, cn < n_q)
        nums = [_own_block(acc_scr[i * rows:(i + 1) * rows, :], DIL_HEADS, n_q) for i in range(N_DIL)]
        ms = [m_scr[i * rows:(i + 1) * rows, 0:1] for i in range(N_DIL)]
        ls = [l_scr[i * rows:(i + 1) * rows, 0:1] for i in range(N_DIL)]
        m_max = ms[0]
        for m in ms[1:]:
            m_max = jnp.maximum(m_max, m)
        ws = [jnp.exp(m - m_max) for m in ms]
        num = ws[0] * nums[0]
        den = ws[0] * ls[0]
        for w, n_, l in zip(ws[1:], nums[1:], ls[1:]):
            num = num + w * n_
            den = den + w * l
        o_ref[0] = num / den


def dil_decode(table, qbd16, buf, new_kv, n_q):
    DB, wb, _ = buf.shape
    rows = qbd16.shape[1]
    assert wb % DIL_DEC_TILE == 0
    n_tiles = wb // DIL_DEC_TILE
    kern = functools.partial(_dil_decode_kernel, n_tiles=n_tiles, n_q=n_q, wb=wb, scale=HEAD_DIM ** -0.5)
    return pl.pallas_call(
        kern,
        grid=(DB, n_tiles),
        in_specs=[
            pl.BlockSpec(memory_space=pltpu.SMEM),
            pl.BlockSpec((1, rows, DIL_W), lambda b, t: (b, 0, 0)),
            pl.BlockSpec((1, DIL_DEC_TILE, 2 * DIL_W), lambda b, t: (b, t, 0)),
            pl.BlockSpec((1, new_kv.shape[1], 2 * DIL_W), lambda b, t: (b, 0, 0)),
        ],
        out_specs=pl.BlockSpec((1, rows, HEAD_DIM), lambda b, t: (b, 0, 0)),
        out_shape=jax.ShapeDtypeStruct((DB, rows, HEAD_DIM), F32),
        scratch_shapes=[pltpu.VMEM((N_DIL * rows, LANES), F32), pltpu.VMEM((N_DIL * rows, LANES), F32),
                        pltpu.VMEM((N_DIL * rows, DIL_W), F32)],
        compiler_params=_cparams(("parallel", "arbitrary")),
        name="dil_decode",
    )(table, qbd16, buf, new_kv)


PROJ_TN = 256
_C_SBQ, _C_SBK, _C_SBV = 0, SB_W, 2 * SB_W
_C_NQ = 3 * SB_W
_C_CMP = _C_NQ + NSA_QW
_C_SLC = _C_CMP + 2 * NSA_KVW
_C_WIN = _C_SLC + 2 * NSA_KVW
_C_GATE = _C_WIN + 2 * NSA_KVW
SPARSE_NP = -(-SPARSE_IN // PROJ_TN) * PROJ_TN


def _sparse_proj_plan(w_in, qk_gain):
    w16 = jnp.pad(w_in, ((0, 0), (0, SPARSE_NP - SPARSE_IN))).astype(BF16)
    gain = jnp.ones((SPARSE_NP,), F32)
    flag = np.zeros((SPARSE_NP,), np.float32)
    kinds = np.full((SPARSE_NP // PROJ_TN,), EPI_PLAIN, np.int32)
    for col, width, gi in ((_C_NQ, NSA_QW, 0), (_C_SLC, NSA_KVW, 2), (_C_WIN, NSA_KVW, 3)):
        gain = gain.at[col:col + width].set(jnp.tile(qk_gain[gi], width // HEAD_DIM))
        flag[col:col + width] = 1.0
        assert col % PROJ_TN == 0 and width % PROJ_TN == 0
        kinds[col // PROJ_TN:(col + width) // PROJ_TN] = EPI_NORM
    assert _C_GATE % PROJ_TN == 0 and SPARSE_NP - _C_GATE == PROJ_TN
    kinds[_C_GATE // PROJ_TN] = EPI_SIGMOID
    return w16, gain, jnp.asarray(flag), jnp.asarray(kinds)


def _dil_proj_plan(w_in, qk_gain):
    N = 3 * DIL_W
    gain = jnp.concatenate([jnp.tile(qk_gain[0], DIL_HEADS), jnp.tile(qk_gain[1], DIL_HEADS), jnp.ones((DIL_W,), F32)])
    flag = np.concatenate([np.ones((2 * DIL_W,), np.float32), np.zeros((DIL_W,), np.float32)])
    kinds = np.full((N // PROJ_TN,), EPI_PLAIN, np.int32)
    kinds[:2 * DIL_W // PROJ_TN] = EPI_NORM
    return w_in.astype(BF16), gain, jnp.asarray(flag), jnp.asarray(kinds)


def _pad_rows(a, n):
    return jnp.pad(a, ((0, 0), (0, n - a.shape[1]), (0, 0)))


def _rows_to_tokens(o, n_heads, n_q):
    DB = o.shape[0]
    return o.reshape(DB, n_heads, n_q, HEAD_DIM).transpose(0, 2, 1, 3).reshape(DB * n_q, n_heads * HEAD_DIM)


def kernel(x_prompt, x_sample, cache_sb_kv, cache_nsa_cmp_kv, cache_nsa_slc_kv, state_nsa_win_kv, state_dil_kv,
           page_table, rel_bias_table, norm_mix, norm_ffn, sparse_w_in, sparse_w_out, nsa_qk_gain, nsa_cmp_w,
           nsa_cmp_pe, dil_w_in, dil_w_out, dil_qk_gain, moe_w_group, moe_b_group, moe_w_expert, moe_b_expert,
           moe_w_gate, moe_w_up, moe_w_down):
    B, S, D = x_prompt.shape
    DB, DS, _ = x_sample.shape
    n_pool = cache_sb_kv.shape[1]
    xp = x_prompt.reshape(B * S, D)
    xs = x_sample.reshape(DB * DS, D)
    nsa_table = rel_bias_table[:, :NSA_HEADS]
    dil_table = rel_bias_table[:, :DIL_HEADS]
    G, R = NSA_KV_HEADS, NSA_REP

    def moe(layer, xp, xs):
        return moe_layer(xp, xs, norm_ffn[layer], moe_w_group[layer], moe_b_group[layer], moe_w_expert[layer],
                         moe_b_expert[layer], moe_w_gate[layer].astype(BF16), moe_w_up[layer].astype(BF16),
                         moe_w_down[layer].astype(BF16))

    w16, gain, flag, kinds = _sparse_proj_plan(sparse_w_in[0], nsa_qk_gain[0])
    w_out16 = sparse_w_out[0].astype(BF16)
    cw16, cpe = _prep_cmp_weights(nsa_cmp_w[0], nsa_cmp_pe[0])
    ck_gain = nsa_qk_gain[0, 1]
    p32, p16 = norm_mm(xp, norm_mix[0], w16, gain, flag, kinds, tm=1024, tn=PROJ_TN)
    p32b = p32.reshape(B, S, SPARSE_NP)
    p16b = p16.reshape(B, S, SPARSE_NP)
    o_sb = sb_prompt(p16b, B, S, q_col=_C_SBQ, k_col=_C_SBK, v_col=_C_SBV)
    chunks = p16b[:, :, _C_CMP:_C_CMP + 2 * NSA_KVW].reshape(B, S // CMP_STRIDE, CHUNK_W)
    kc, vc = compress_prompt(chunks, cw16, cpe, ck_gain)
    tb = nsa_bias_tiles(nsa_table)
    o_nsa = nsa_prompt(p16b, p32b, kc, vc, tb, nsa_table, B, S, q_col=_C_NQ, sk_col=_C_SLC, sv_col=_C_SLC + NSA_KVW,
                       wk_col=_C_WIN, wv_col=_C_WIN + NSA_KVW, gate_col=_C_GATE)
    a = jnp.concatenate([o_sb, o_nsa], axis=-1).reshape(B * S, SB_W + NSA_QW)
    xp = mm_res(a, w_out16, xp, tm=1024, tn=PROJ_TN)
    sb_kv_prompt = p32b[:, :, _C_SBK:_C_SBK + 2 * SB_W].reshape(1, B, S, 2, SB_HEADS, HEAD_DIM)
    cmp_kv_prompt = p32b[:, :, _C_CMP:_C_CMP + 2 * NSA_KVW].reshape(1, B, S, 2, G, HEAD_DIM)
    slc_kv_prompt = p32b[:, :, _C_SLC:_C_SLC + 2 * NSA_KVW].reshape(1, B, S, 2, G, HEAD_DIM)
    wn = min(NSA_WINDOW, S)
    win_kv_prompt = p32b[:, S - wn:, _C_WIN:_C_WIN + 2 * NSA_KVW].reshape(1, B, wn, 2, G, HEAD_DIM)
    q32, _ = norm_mm(xs, norm_mix[0], w16, gain, flag, kinds, tm=DB * DS, tn=PROJ_TN)
    q32b = q32.reshape(DB, DS, SPARSE_NP)
    sb_new = q32b[:, :, _C_SBK:_C_SBK + 2 * SB_W]
    cmp_new = q32b[:, :, _C_CMP:_C_CMP + 2 * NSA_KVW]
    slc_new = q32b[:, :, _C_SLC:_C_SLC + 2 * NSA_KVW]
    win_new = q32b[:, :, _C_WIN:_C_WIN + 2 * NSA_KVW]
    o_sb_s = sb_decode(page_table, _block_diag_rows(q32b[:, :, _C_SBQ:_C_SBQ + SB_W], SB_HEADS).astype(BF16),
                       _pad_rows(sb_new, PAGE_SIZE), cache_sb_kv[0].reshape(n_pool, PAGE_SIZE, 2 * SB_W))
    qn = q32b[:, :, _C_NQ:_C_NQ + NSA_QW].reshape(DB, DS, G, R, HEAD_DIM).transpose(0, 2, 3, 1, 4)
    qn = (qn[:, :, :, :, None, :] * jnp.eye(G, dtype=F32)[None, :, None, None, :, None]).reshape(DB, G * R * DS, NSA_KVW)
    qn16 = qn.astype(BF16)
    o_c, sel = nsa_cmp_decode(page_table, nsa_table, qn16,
                              cache_nsa_cmp_kv[0].reshape(n_pool, CHUNKS_PER_PAGE, CHUNK_W), cw16, cpe, ck_gain, DS)
    sel16 = jnp.broadcast_to(sel.reshape(DB, G, 1, DS, -1), (DB, G, R, DS, sel.shape[-1])).reshape(DB, G * R * DS, -1).astype(BF16)
    gates = q32b[:, :, _C_GATE:_C_GATE + 3 * NSA_HEADS].reshape(DB, DS, G, R, 3).transpose(0, 2, 3, 1, 4).reshape(DB, G * R * DS, 3)
    gates = jnp.pad(gates, ((0, 0), (0, 0), (0, LANES - 3)))
    wbuf = state_nsa_win_kv[0]
    o_nsa_s = nsa_slc_decode(page_table, nsa_table, qn16, sel16,
                             cache_nsa_slc_kv[0].reshape(n_pool, PAGE_SIZE, 2 * NSA_KVW), _pad_rows(slc_new, PAGE_SIZE),
                             wbuf.reshape(DB, wbuf.shape[1], 2 * NSA_KVW), _pad_rows(win_new, PAGE_SIZE), o_c, gates, DS)
    a_s = jnp.concatenate([_rows_to_tokens(o_sb_s, SB_HEADS, DS), _rows_to_tokens(o_nsa_s, NSA_HEADS, DS)], axis=-1)
    xs = mm_res(a_s.astype(BF16), w_out16, xs, tm=DB * DS, tn=PROJ_TN)
    sb_kv_sample = sb_new.reshape(1, DB, DS, 2, SB_HEADS, HEAD_DIM)
    cmp_kv_sample = cmp_new.reshape(1, DB, DS, 2, G, HEAD_DIM)
    slc_kv_sample = slc_new.reshape(1, DB, DS, 2, G, HEAD_DIM)
    win_kv_sample = jnp.concatenate([wbuf, win_new.reshape(DB, DS, 2, G, HEAD_DIM)], axis=1)[None, :, DS:]
    xp, xs = moe(0, xp, xs)

    w16, gain, flag, kinds = _dil_proj_plan(dil_w_in[0], dil_qk_gain[0])
    w_out16 = dil_w_out[0].astype(BF16)
    p32, p16 = norm_mm(xp, norm_mix[1], w16, gain, flag, kinds, tm=1024, tn=PROJ_TN)
    dbias = dil_bias_tiles(dil_table)
    stats = [dil_prompt_pattern(p16.reshape(B, S, 3 * DIL_W), dbias, pat, B, S) for pat in range(N_DIL)]
    o = dil_merge([n for n, _ in stats], [ml for _, ml in stats], tm=512)
    xp = mm_res(o, w_out16, xp, tm=1024, tn=PROJ_TN)
    dn = min(DIL_MAX_WINDOW, S)
    dil_kv_prompt = p32.reshape(B, S, 3 * DIL_W)[:, S - dn:, DIL_W:].reshape(1, B, dn, 2, DIL_HEADS, HEAD_DIM)
    q32, _ = norm_mm(xs, norm_mix[1], w16, gain, flag, kinds, tm=DB * DS, tn=PROJ_TN)
    q32b = q32.reshape(DB, DS, 3 * DIL_W)
    dil_new = q32b[:, :, DIL_W:]
    dbuf = state_dil_kv[0]
    o_s = dil_decode(dil_table, _block_diag_rows(q32b[:, :, :DIL_W], DIL_HEADS).astype(BF16),
                     dbuf.reshape(DB, dbuf.shape[1], 2 * DIL_W), _pad_rows(dil_new, PAGE_SIZE), DS)
    xs = mm_res(_rows_to_tokens(o_s, DIL_HEADS, DS).astype(BF16), w_out16, xs, tm=DB * DS, tn=PROJ_TN)
    dil_kv_sample = jnp.concatenate([dbuf, dil_new.reshape(DB, DS, 2, DIL_HEADS, HEAD_DIM)], axis=1)[None, :, DS:]
    xp, xs = moe(1, xp, xs)

    return (xp.reshape(B, S, D), xs.reshape(DB, DS, D), sb_kv_prompt, sb_kv_sample, cmp_kv_prompt, cmp_kv_sample,
            slc_kv_prompt, slc_kv_sample, win_kv_prompt, win_kv_sample, dil_kv_prompt, dil_kv_sample)
```

```python
import functools
import math

import jax
import jax.numpy as jnp
import numpy as np
from jax import lax
from jax.experimental import pallas as pl
from jax.experimental.pallas import tpu as pltpu

D_MODEL = 2048
HEAD_DIM = 128
SB_HEADS = 8
NSA_HEADS = 8
NSA_KV_HEADS = 2
NSA_REP = NSA_HEADS // NSA_KV_HEADS
CMP_BLOCK = 32
CMP_STRIDE = 16
SLC_BLOCK = 64
N_SELECT = 16
NSA_WINDOW = 512
DIL_HEADS = 16
DIL_PATTERNS = ((128, 1), (512, 4), (2048, 16))
DIL_MAX_WINDOW = 2048
QBLK = 128
N_BUCKETS = 32
BUCKET_MAX_DIST = 2048
N_GROUPS = 4
EXPERTS_PER_GROUP = 4
N_EXPERTS = N_GROUPS * EXPERTS_PER_GROUP
TOP_K_INNER = 2
D_EXPERT = 1024
PAGE_SIZE = 128
RMS_EPS = 1e-6
NEG_INF = -1e30
FORCE_SCORE = 1e9
SB_W = SB_HEADS * HEAD_DIM
NSA_QW = NSA_HEADS * HEAD_DIM
NSA_KVW = NSA_KV_HEADS * HEAD_DIM
SPARSE_IN = 3 * SB_W + NSA_QW + 6 * NSA_KVW + 3 * NSA_HEADS
DIL_W = DIL_HEADS * HEAD_DIM

LANES = 128
SUBLANES = 8
VMEM_LIMIT = 56 * 1024 * 1024

BF16 = jnp.bfloat16
F32 = jnp.float32


def _cparams(sem):
    return pltpu.CompilerParams(dimension_semantics=sem, vmem_limit_bytes=VMEM_LIMIT)


def _dot(a, b):
    return jnp.dot(a, b, preferred_element_type=F32)


def _dot_t(a, b):
    return lax.dot_general(a, b, (((1,), (1,)), ((), ())), preferred_element_type=F32)


def _split3(x):
    h1 = x.astype(BF16)
    r1 = x - h1.astype(F32)
    h2 = r1.astype(BF16)
    h3 = (r1 - h2.astype(F32)).astype(BF16)
    return h1, h2, h3


def _dot01(x, m01):
    h1, h2, h3 = _split3(x)
    return _dot(h1, m01) + _dot(h2, m01) + _dot(h3, m01)


EPI_PLAIN, EPI_NORM, EPI_SIGMOID = 0, 1, 2


def _norm_mm_kernel(kind_ref, x_ref, g_ref, w_ref, cg_ref, cf_ref, o32_ref, o16_ref, h_scr):
    j = pl.program_id(1)

    @pl.when(j == 0)
    def _():
        x = x_ref[...]
        y = x * lax.rsqrt(jnp.mean(x * x, axis=-1, keepdims=True) + RMS_EPS)
        h_scr[...] = (y * g_ref[...]).astype(BF16)

    acc = _dot(h_scr[...], w_ref[...])
    kind = kind_ref[j]

    @pl.when(kind == EPI_PLAIN)
    def _():
        o32_ref[...] = acc
        o16_ref[...] = acc.astype(BF16)

    @pl.when(kind == EPI_NORM)
    def _():
        tn = acc.shape[1]
        parts = []
        for c in range(tn // HEAD_DIM):
            a = acc[:, c * HEAD_DIM:(c + 1) * HEAD_DIM]
            n = a * lax.rsqrt(jnp.mean(a * a, axis=-1, keepdims=True) + RMS_EPS)
            parts.append(n)
        normed = jnp.concatenate(parts, axis=1) * cg_ref[...]
        out = jnp.where(cf_ref[...] > 0.5, normed, acc)
        o32_ref[...] = out
        o16_ref[...] = out.astype(BF16)

    @pl.when(kind == EPI_SIGMOID)
    def _():
        out = jax.nn.sigmoid(acc)
        o32_ref[...] = out
        o16_ref[...] = out.astype(BF16)


def norm_mm(x, g, w16, col_gain, col_flag, tile_kind, *, tm, tn):
    T, D = x.shape
    N = w16.shape[1]
    assert T % tm == 0 and N % tn == 0
    grid = (T // tm, N // tn)
    return pl.pallas_call(
        _norm_mm_kernel,
        grid_spec=pltpu.PrefetchScalarGridSpec(
            num_scalar_prefetch=1,
            grid=grid,
            in_specs=[
                pl.BlockSpec((tm, D), lambda i, j, k: (i, 0)),
                pl.BlockSpec((1, D), lambda i, j, k: (0, 0)),
                pl.BlockSpec((D, tn), lambda i, j, k: (0, j)),
                pl.BlockSpec((1, tn), lambda i, j, k: (0, j)),
                pl.BlockSpec((1, tn), lambda i, j, k: (0, j)),
            ],
            out_specs=[
                pl.BlockSpec((tm, tn), lambda i, j, k: (i, j)),
                pl.BlockSpec((tm, tn), lambda i, j, k: (i, j)),
            ],
            scratch_shapes=[pltpu.VMEM((tm, D), BF16)],
        ),
        out_shape=[jax.ShapeDtypeStruct((T, N), F32), jax.ShapeDtypeStruct((T, N), BF16)],
        compiler_params=_cparams(("parallel", "arbitrary")),
        name="norm_mm",
    )(tile_kind, x, g.reshape(1, D), w16, col_gain.reshape(1, N), col_flag.reshape(1, N))


def _mm_res_kernel(a_ref, w_ref, r_ref, o_ref):
    o_ref[...] = r_ref[...] + _dot(a_ref[...], w_ref[...])


def mm_res(a16, w16, res, *, tm, tn):
    T, K = a16.shape
    N = w16.shape[1]
    assert T % tm == 0 and N % tn == 0
    return pl.pallas_call(
        _mm_res_kernel,
        grid=(T // tm, N // tn),
        in_specs=[
            pl.BlockSpec((tm, K), lambda i, j: (i, 0)),
            pl.BlockSpec((K, tn), lambda i, j: (0, j)),
            pl.BlockSpec((tm, tn), lambda i, j: (i, j)),
        ],
        out_specs=pl.BlockSpec((tm, tn), lambda i, j: (i, j)),
        out_shape=jax.ShapeDtypeStruct((T, N), F32),
        compiler_params=_cparams(("parallel", "arbitrary")),
        name="mm_res",
    )(a16, w16, res)


LOG2_E = math.log2(math.e)


def _log2_keep(z2):
    nz = -z2
    return jnp.minimum(nz, 0.0) - jnp.log2(1.0 + jnp.exp2(jnp.minimum(z2, nz)))


def _upper01(n):
    r = lax.broadcasted_iota(jnp.int32, (2 * n, n), 0)
    c = lax.broadcasted_iota(jnp.int32, (2 * n, n), 1)
    return jnp.where(jnp.where(r >= n, r - n, r) > c, 1.0, 0.0).astype(BF16)


def _after_in_tile(lk, upper2):
    h1 = lk.astype(BF16)
    h2 = (lk - h1.astype(F32)).astype(BF16)
    return _dot(jnp.concatenate([h1, h2], axis=1), upper2)


def _sb_prompt_kernel(q_ref, k_ref, v_ref, o_ref, acc_scr, run_scr, *, bq, scale):
    bk = LANES
    i = pl.program_id(2)
    nd = bq // bk
    acc_scr[...] = jnp.zeros(acc_scr.shape, F32)
    run_scr[...] = jnp.zeros(run_scr.shape, F32)
    upper2 = _upper01(bk)

    def absorb(kb, r0, masked):
        ks = pl.multiple_of(kb * bk, bk)
        k = k_ref[0, pl.ds(ks, bk), :]
        v = v_ref[0, pl.ds(ks, bk), :]
        z2 = _dot_t(q_ref[0, r0:, :], k) * (scale * LOG2_E)
        lk = _log2_keep(z2)
        lb = z2 + lk
        if masked:
            qpos = i * bq + r0 + lax.broadcasted_iota(jnp.int32, (bq - r0, 1), 0)
            mask = ks + lax.broadcasted_iota(jnp.int32, (1, bk), 1) < qpos
            lk = jnp.where(mask, lk, 0.0)
        w = jnp.exp2(lb + _after_in_tile(lk, upper2) + run_scr[r0:, :])
        if masked:
            w = jnp.where(mask, w, 0.0)
        acc_scr[r0:, :] = acc_scr[r0:, :] + _dot(w.astype(BF16), v)
        run_scr[r0:, :] = run_scr[r0:, :] + jnp.sum(lk, axis=-1, keepdims=True)

    for d in range(nd - 1, -1, -1):
        absorb(i * nd + d, d * bk, True)

    assert nd % 2 == 0

    def body(t, carry):
        absorb(i * nd - 1 - 2 * t, 0, False)
        absorb(i * nd - 2 - 2 * t, 0, False)
        return carry

    lax.fori_loop(0, i * (nd // 2), body, 0)
    o_ref[0] = acc_scr[...].astype(o_ref.dtype)


def sb_prompt(p16, B, S, *, q_col, k_col, v_col, bq=512):
    assert S % bq == 0 and bq % LANES == 0
    kern = functools.partial(_sb_prompt_kernel, bq=bq, scale=HEAD_DIM ** -0.5)
    qb, kb_, vb = q_col // HEAD_DIM, k_col // HEAD_DIM, v_col // HEAD_DIM
    return pl.pallas_call(
        kern,
        grid=(B, SB_HEADS, S // bq),
        in_specs=[
            pl.BlockSpec((1, bq, HEAD_DIM), lambda b, h, i: (b, i, qb + h)),
            pl.BlockSpec((1, S, HEAD_DIM), lambda b, h, i: (b, 0, kb_ + h)),
            pl.BlockSpec((1, S, HEAD_DIM), lambda b, h, i: (b, 0, vb + h)),
        ],
        out_specs=pl.BlockSpec((1, bq, HEAD_DIM), lambda b, h, i: (b, i, h)),
        out_shape=jax.ShapeDtypeStruct((B, S, SB_W), BF16),
        scratch_shapes=[pltpu.VMEM((bq, HEAD_DIM), F32), pltpu.VMEM((bq, LANES), F32)],
        compiler_params=_cparams(("parallel", "parallel", "arbitrary")),
        name="sb_prompt",
    )(p16, p16, p16)


def _t5_bucket(dist):
    n = jnp.maximum(dist, 0)
    exact = N_BUCKETS // 2
    nf = jnp.maximum(n, exact).astype(F32)
    far = exact + (jnp.log(nf / exact) / math.log(BUCKET_MAX_DIST / exact) * (N_BUCKETS - exact)).astype(jnp.int32)
    return jnp.where(n < exact, n, jnp.minimum(far, N_BUCKETS - 1))


def _table_lookup(bucket, tab_ref, head):
    out = jnp.zeros(bucket.shape, F32)
    for k in range(N_BUCKETS):
        out = jnp.where(bucket == k, tab_ref[k, head], out)
    return out


N_BIAS_DIAG = -(-(BUCKET_MAX_DIST + QBLK) // QBLK) + 1


def _bias_tiles_kernel(tab_ref, o_ref):
    g = pl.program_id(0)
    d = pl.program_id(1)
    dist = d * QBLK + lax.broadcasted_iota(jnp.int32, (QBLK, QBLK), 0) - lax.broadcasted_iota(jnp.int32, (QBLK, QBLK), 1)
    bucket = _t5_bucket(dist)
    for r in range(NSA_REP):
        o_ref[0, 0, r * QBLK:(r + 1) * QBLK, :] = _table_lookup(bucket, tab_ref, g * NSA_REP + r)


def nsa_bias_tiles(table):
    return pl.pallas_call(
        _bias_tiles_kernel,
        grid=(NSA_KV_HEADS, N_BIAS_DIAG),
        in_specs=[pl.BlockSpec(memory_space=pltpu.SMEM)],
        out_specs=pl.BlockSpec((1, 1, NSA_REP * QBLK, QBLK), lambda g, d: (g, d, 0, 0)),
        out_shape=jax.ShapeDtypeStruct((NSA_KV_HEADS, N_BIAS_DIAG, NSA_REP * QBLK, QBLK), F32),
        compiler_params=_cparams(("parallel", "arbitrary")),
        name="nsa_bias_tiles",
    )(table)


CMP_HALF = CMP_BLOCK // CMP_STRIDE
CHUNK_W = CMP_STRIDE * 2 * NSA_KVW


def _compress_chunks(x, w_ref, pe_ref, gain):
    n = x.shape[0]
    outs = []
    for kv in range(2):
        for g in range(NSA_KV_HEADS):
            off = kv * NSA_KVW + g * HEAD_DIM
            xs = jnp.concatenate(
                [x[:, l * 2 * NSA_KVW + off:l * 2 * NSA_KVW + off + HEAD_DIM] for l in range(CMP_STRIDE)], axis=1)
            halves = []
            for j in range(CMP_HALF):
                w = w_ref[kv, j]
                pe = jnp.broadcast_to(pe_ref[kv, j], (SUBLANES, CMP_STRIDE * HEAD_DIM)).astype(BF16)
                halves.append(_dot(xs, w) + _dot(pe, w)[0:1, :])
            out = halves[0]
            for j in range(1, CMP_HALF):
                out = out + pltpu.roll(halves[j], n - j, axis=0)
            if kv == 0:
                out = out * lax.rsqrt(jnp.mean(out * out, axis=-1, keepdims=True) + RMS_EPS) * gain
            outs.append(out)
    kc = jnp.concatenate(outs[:NSA_KV_HEADS], axis=1)
    vc = jnp.concatenate(outs[NSA_KV_HEADS:], axis=1)
    return kc, vc


def _compress_prompt_kernel(x_ref, w_ref, pe_ref, g_ref, kc_ref, vc_ref):
    kc, vc = _compress_chunks(x_ref[0], w_ref, pe_ref, g_ref[...])
    kc_ref[0] = kc.astype(BF16)
    vc_ref[0] = vc.astype(BF16)


def _prep_cmp_weights(cmp_w, cmp_pe):
    w = cmp_w.reshape(2, CMP_HALF, CMP_STRIDE * HEAD_DIM, HEAD_DIM).astype(BF16)
    pe = cmp_pe.reshape(2, CMP_HALF, 1, CMP_STRIDE * HEAD_DIM)
    return w, pe


def compress_prompt(chunks16, w16, pe, ck_gain):
    B, n, _ = chunks16.shape
    return pl.pallas_call(
        _compress_prompt_kernel,
        grid=(B,),
        in_specs=[
            pl.BlockSpec((1, n, CHUNK_W), lambda b: (b, 0, 0)),
            pl.BlockSpec((2, CMP_HALF, CMP_STRIDE * HEAD_DIM, HEAD_DIM), lambda b: (0, 0, 0, 0)),
            pl.BlockSpec((2, CMP_HALF, 1, CMP_STRIDE * HEAD_DIM), lambda b: (0, 0, 0, 0)),
            pl.BlockSpec((1, HEAD_DIM), lambda b: (0, 0)),
        ],
        out_specs=[pl.BlockSpec((1, n, NSA_KVW), lambda b: (b, 0, 0))] * 2,
        out_shape=[jax.ShapeDtypeStruct((B, n, NSA_KVW), BF16)] * 2,
        compiler_params=_cparams(("parallel",)),
        name="nsa_compress_prompt",
    )(chunks16, w16, pe, ck_gain.reshape(1, HEAD_DIM))


def _flash_step(s, mask, v, m_scr, l_scr, acc_scr):
    n = s.shape[1] // LANES
    cols = [slice(c * LANES, (c + 1) * LANES) for c in range(n)]
    sm = [jnp.where(mask[:, c], s[:, c], NEG_INF) for c in cols]
    top = sm[0]
    for x in sm[1:]:
        top = jnp.maximum(top, x)
    m_prev = m_scr[...]
    m_new = jnp.maximum(m_prev, jnp.max(top, axis=-1, keepdims=True))
    ps = [jnp.where(mask[:, c], jnp.exp(x - m_new), 0.0) for c, x in zip(cols, sm)]
    tot = ps[0]
    for x in ps[1:]:
        tot = tot + x
    alpha = jnp.exp(m_prev - m_new)
    l_scr[...] = alpha * l_scr[...] + jnp.sum(tot, axis=-1, keepdims=True)
    acc_scr[...] = alpha * acc_scr[...] + _dot(jnp.concatenate([x.astype(BF16) for x in ps], axis=1), v)
    m_scr[...] = m_new


def _flash_reset(m_scr, l_scr, acc_scr):
    m_scr[...] = jnp.full(m_scr.shape, NEG_INF, F32)
    l_scr[...] = jnp.zeros(l_scr.shape, F32)
    acc_scr[...] = jnp.zeros(acc_scr.shape, F32)


def _select_blocks(score, q_blk, n_sel):
    Q, n_s = score.shape
    blk = lax.broadcasted_iota(jnp.int32, (1, n_s), 1)
    valid = blk <= q_blk
    forced = valid & ((blk == 0) | (blk == q_blk) | (blk == q_blk - 1))
    score = jnp.where(forced, FORCE_SCORE, jnp.where(valid, score, NEG_INF))
    rank = jnp.zeros((Q, n_s), F32)
    for c in range(n_s):
        col = score[:, c:c + 1]
        before = (col > score) | ((col == score) & (blk > c))
        rank = rank + jnp.where(before, 1.0, 0.0)
    return jnp.where(rank < n_sel, 1.0, 0.0)


def _select_blocks_t(score_t, q_blk, n_sel):
    n_s, Q = score_t.shape
    blk = lax.broadcasted_iota(jnp.int32, (n_s, 1), 0)
    valid = blk <= q_blk
    forced = valid & ((blk == 0) | (blk == q_blk) | (blk == q_blk - 1))
    score_t = jnp.where(forced, FORCE_SCORE, jnp.where(valid, score_t, NEG_INF))
    rank = jnp.zeros((n_s, Q), F32)
    for c in range(n_s):
        row = score_t[c:c + 1, :]
        before = (row > score_t) | ((row == score_t) & (blk > c))
        rank = rank + jnp.where(before, 1.0, 0.0)
    return jnp.where(rank < n_sel, 1.0, 0.0)


def _nsa_cmp_bias_kernel(tab_ref, o_ref):
    g = pl.program_id(0)
    i = pl.program_id(1)
    ncp = o_ref.shape[3]
    qpos1 = i * QBLK + lax.broadcasted_iota(jnp.int32, (QBLK, 1), 0)
    c_end = lax.broadcasted_iota(jnp.int32, (1, ncp), 1) * CMP_STRIDE + (CMP_BLOCK - 1)
    bucket = _t5_bucket(qpos1 - c_end)
    for r in range(NSA_REP):
        o_ref[0, 0, r * QBLK:(r + 1) * QBLK, :] = _table_lookup(bucket, tab_ref, g * NSA_REP + r)


def nsa_cmp_bias(table, S, ncp):
    return pl.pallas_call(
        _nsa_cmp_bias_kernel,
        grid=(NSA_KV_HEADS, S // QBLK),
        in_specs=[pl.BlockSpec(memory_space=pltpu.SMEM)],
        out_specs=pl.BlockSpec((1, 1, NSA_REP * QBLK, ncp), lambda g, i: (g, i, 0, 0)),
        out_shape=jax.ShapeDtypeStruct((NSA_KV_HEADS, S // QBLK, NSA_REP * QBLK, ncp), F32),
        compiler_params=_cparams(("parallel", "arbitrary")),
        name="nsa_cmp_bias",
    )(table)


NSA_KT = 2 * QBLK


def _nsa_prompt_kernel(q_ref, kc_ref, vc_ref, ks_ref, vs_ref, kw_ref, vw_ref, gt_ref, tb_ref, cb_ref,
                       o_ref, m_scr, l_scr, acc_scr, sel_scr, *, n_c, scale):
    g = pl.program_id(1)
    i = pl.program_id(2)
    R = NSA_REP
    qs = i * QBLK
    q4 = q_ref[0]
    q = jnp.concatenate([q4[:, r * HEAD_DIM:(r + 1) * HEAD_DIM] for r in range(R)], axis=0)
    qpos1 = qs + lax.broadcasted_iota(jnp.int32, (QBLK, 1), 0)
    qpos = jnp.concatenate([qpos1] * R, axis=0)

    ncp = kc_ref.shape[1]
    s = _dot_t(q, kc_ref[0]) * scale
    n_idx = lax.broadcasted_iota(jnp.int32, (1, ncp), 1)
    mask = (qpos - (n_idx * CMP_STRIDE + (CMP_BLOCK - 1)) >= 0) & (n_idx < n_c)
    s = jnp.where(mask, s + cb_ref[0, 0], NEG_INF)
    m = jnp.max(s, axis=-1, keepdims=True)
    p = jnp.where(mask, jnp.exp(s - m), 0.0)
    l = jnp.sum(p, axis=-1, keepdims=True)
    p = p / jnp.maximum(l, 1.0)
    o_c = _dot(p.astype(BF16), vc_ref[0])
    imp = p[0:QBLK]
    for r in range(1, R):
        imp = imp + p[r * QBLK:(r + 1) * QBLK]
    n_s = ks_ref.shape[1] // SLC_BLOCK
    ratio = SLC_BLOCK // CMP_STRIDE
    ti = lax.broadcasted_iota(jnp.int32, (n_s, ncp), 0)
    ci = lax.broadcasted_iota(jnp.int32, (n_s, ncp), 1)
    gather01 = jnp.where((ci >= ti * ratio - (CMP_HALF - 1)) & (ci < (ti + 1) * ratio), 1.0, 0.0).astype(BF16)
    i1, i2, i3 = _split3(imp)
    score_t = _dot_t(gather01, i1) + _dot_t(gather01, i2) + _dot_t(gather01, i3)
    q_blk = (qs + lax.broadcasted_iota(jnp.int32, (1, QBLK), 1)) // SLC_BLOCK
    sel_scr[...] = _select_blocks_t(score_t, q_blk, min(N_SELECT, n_s)).T.astype(BF16)

    kcol = lax.broadcasted_iota(jnp.int32, (1, NSA_KT), 1)

    def tile_bias(kt):
        parts = [tb_ref[0, jnp.clip(i - (kt * (NSA_KT // QBLK) + t), 0, N_BIAS_DIAG - 1)] for t in range(NSA_KT // QBLK)]
        return jnp.concatenate(parts, axis=1)

    _flash_reset(m_scr, l_scr, acc_scr)

    def slc_body(kt, carry):
        ks_ = pl.multiple_of(kt * NSA_KT, NSA_KT)
        k = ks_ref[0, pl.ds(ks_, NSA_KT), :]
        v = vs_ref[0, pl.ds(ks_, NSA_KT), :]
        dist = qpos - (ks_ + kcol)
        bi = lax.broadcasted_iota(jnp.int32, (n_s, NSA_KT), 0)
        ki = lax.broadcasted_iota(jnp.int32, (n_s, NSA_KT), 1)
        expand = jnp.where(bi == kt * (NSA_KT // SLC_BLOCK) + ki // SLC_BLOCK, 1.0, 0.0).astype(BF16)
        selk = _dot(sel_scr[...], expand)
        selk = jnp.concatenate([selk] * R, axis=0)
        msk = (dist >= 0) & (selk > 0.5)
        sc = _dot_t(q, k) * scale + tile_bias(kt)
        _flash_step(sc, msk, v, m_scr, l_scr, acc_scr)
        return carry

    lax.fori_loop(0, (qs + QBLK - 1) // NSA_KT + 1, slc_body, 0)
    o_s = acc_scr[...] / jnp.maximum(l_scr[...], 1.0)

    _flash_reset(m_scr, l_scr, acc_scr)

    def win_body(kt, carry):
        ks_ = pl.multiple_of(kt * NSA_KT, NSA_KT)
        k = kw_ref[0, pl.ds(ks_, NSA_KT), :]
        v = vw_ref[0, pl.ds(ks_, NSA_KT), :]
        dist = qpos - (ks_ + kcol)
        msk = (dist >= 0) & (dist < NSA_WINDOW)
        sc = _dot_t(q, k) * scale + tile_bias(kt)
        _flash_step(sc, msk, v, m_scr, l_scr, acc_scr)
        return carry

    lax.fori_loop(jnp.maximum(qs - (NSA_WINDOW - 1), 0) // NSA_KT, (qs + QBLK - 1) // NSA_KT + 1, win_body, 0)
    o_w = acc_scr[...] / jnp.maximum(l_scr[...], 1.0)

    gt = gt_ref[0]
    outs = []
    for r in range(R):
        c0 = (g * R + r) * 3
        rows = slice(r * QBLK, (r + 1) * QBLK)
        lane = lax.broadcasted_iota(jnp.int32, (1, LANES), 1)
        gates = [jnp.sum(jnp.where(lane == c0 + t, gt, 0.0), axis=-1, keepdims=True) for t in range(3)]
        outs.append(gates[0] * o_c[rows] + gates[1] * o_s[rows] + gates[2] * o_w[rows])
    o_ref[0] = jnp.concatenate(outs, axis=1).astype(o_ref.dtype)


def nsa_prompt(p16, p32, kc, vc, tb, cb, B, S, *, q_col, sk_col, sv_col, wk_col, wv_col, gate_col):
    G, R = NSA_KV_HEADS, NSA_REP
    assert S % NSA_KT == 0
    n_c = S // CMP_STRIDE - CMP_HALF + 1
    kern = functools.partial(_nsa_prompt_kernel, n_c=n_c, scale=HEAD_DIM ** -0.5)
    hd = HEAD_DIM
    return pl.pallas_call(
        kern,
        grid_spec=pltpu.PrefetchScalarGridSpec(
            num_scalar_prefetch=0,
            grid=(B, G, S // QBLK),
            in_specs=[
                pl.BlockSpec((1, QBLK, R * hd), lambda b, g, i: (b, i, q_col // (R * hd) + g)),
                pl.BlockSpec((1, kc.shape[1], hd), lambda b, g, i: (b, 0, g)),
                pl.BlockSpec((1, vc.shape[1], hd), lambda b, g, i: (b, 0, g)),
                pl.BlockSpec((1, S, hd), lambda b, g, i: (b, 0, sk_col // hd + g)),
                pl.BlockSpec((1, S, hd), lambda b, g, i: (b, 0, sv_col // hd + g)),
                pl.BlockSpec((1, S, hd), lambda b, g, i: (b, 0, wk_col // hd + g)),
                pl.BlockSpec((1, S, hd), lambda b, g, i: (b, 0, wv_col // hd + g)),
                pl.BlockSpec((1, QBLK, LANES), lambda b, g, i: (b, i, gate_col // LANES)),
                pl.BlockSpec((1, N_BIAS_DIAG, R * QBLK, QBLK), lambda b, g, i: (g, 0, 0, 0)),
                pl.BlockSpec((1, 1, R * QBLK, cb.shape[3]), lambda b, g, i: (g, i, 0, 0)),
            ],
            out_specs=pl.BlockSpec((1, QBLK, R * hd), lambda b, g, i: (b, i, g)),
            scratch_shapes=[
                pltpu.VMEM((R * QBLK, LANES), F32),
                pltpu.VMEM((R * QBLK, LANES), F32),
                pltpu.VMEM((R * QBLK, hd), F32),
                pltpu.VMEM((QBLK, S // SLC_BLOCK), BF16),
            ],
        ),
        out_shape=jax.ShapeDtypeStruct((B, S, NSA_QW), BF16),
        compiler_params=_cparams(("parallel", "parallel", "arbitrary")),
        name="nsa_prompt",
    )(p16, kc, vc, p16, p16, p16, p16, p32, tb, cb)


ROUTER_W = LANES
MOE_TM = 256


def _router_kernel(x_ref, g_ref, w1_ref, w2_ref, b_ref, h_ref, info_ref):
    x = x_ref[...]
    hn = x * lax.rsqrt(jnp.mean(x * x, axis=-1, keepdims=True) + RMS_EPS) * g_ref[...]
    a1 = hn.astype(BF16)
    h_ref[...] = a1
    a2 = (hn - a1.astype(F32)).astype(BF16)
    logits = _dot(a1, w1_ref[...]) + _dot(a1, w2_ref[...]) + _dot(a2, w1_ref[...]) + b_ref[...]
    lane = lax.broadcasted_iota(jnp.int32, (1, ROUTER_W), 1)
    big = ROUTER_W

    def first_lane(cond):
        return jnp.min(jnp.where(cond, lane, big), axis=-1, keepdims=True)

    gmask = lane < N_GROUPS
    gl = jnp.where(gmask, logits, NEG_INF)
    gm = jnp.max(gl, axis=-1, keepdims=True)
    g_idx = first_lane(gmask & (gl == gm))
    g_w = 1.0 / jnp.sum(jnp.where(gmask, jnp.exp(gl - gm), 0.0), axis=-1, keepdims=True)
    lo = N_GROUPS + g_idx * EXPERTS_PER_GROUP
    emask = (lane >= lo) & (lane < lo + EXPERTS_PER_GROUP)
    el = jnp.where(emask, logits, NEG_INF)
    em = jnp.max(el, axis=-1, keepdims=True)
    ee = jnp.where(emask, jnp.exp(el - em), 0.0)
    pe = jnp.where(emask, ee / jnp.sum(ee, axis=-1, keepdims=True), -1.0)
    v1 = jnp.max(pe, axis=-1, keepdims=True)
    i1 = first_lane(pe == v1)
    pe2 = jnp.where(lane == i1, -1.0, pe)
    v2 = jnp.max(pe2, axis=-1, keepdims=True)
    i2 = first_lane((pe2 == v2) & emask & (lane != i1))
    tot = v1 + v2
    info = jnp.where(lane == 0, (i1 - N_GROUPS).astype(F32), 0.0)
    info = jnp.where(lane == 1, (i2 - N_GROUPS).astype(F32), info)
    info = jnp.where(lane == 2, g_w * (v1 / tot), info)
    info = jnp.where(lane == 3, g_w * (v2 / tot), info)
    info_ref[...] = info


def moe_router(x, g, w_group, b_group, w_expert, b_expert, *, tm):
    T, D = x.shape
    pad = ROUTER_W - N_GROUPS - N_EXPERTS
    w = jnp.pad(jnp.concatenate([w_group, w_expert], axis=1), ((0, 0), (0, pad)))
    b = jnp.pad(jnp.concatenate([b_group, b_expert]), (0, pad)).reshape(1, ROUTER_W)
    w1 = w.astype(BF16)
    w2 = (w - w1.astype(F32)).astype(BF16)
    return pl.pallas_call(
        _router_kernel,
        grid=(T // tm,),
        in_specs=[
            pl.BlockSpec((tm, D), lambda i: (i, 0)),
            pl.BlockSpec((1, D), lambda i: (0, 0)),
            pl.BlockSpec((D, ROUTER_W), lambda i: (0, 0)),
            pl.BlockSpec((D, ROUTER_W), lambda i: (0, 0)),
            pl.BlockSpec((1, ROUTER_W), lambda i: (0, 0)),
        ],
        out_specs=[pl.BlockSpec((tm, D), lambda i: (i, 0)), pl.BlockSpec((tm, ROUTER_W), lambda i: (i, 0))],
        out_shape=[jax.ShapeDtypeStruct((T, D), BF16), jax.ShapeDtypeStruct((T, ROUTER_W), F32)],
        compiler_params=_cparams(("parallel",)),
        name="moe_router",
    )(x, g.reshape(1, D), w1, w2, b)


def _moe_ffn_kernel(te_ref, nu_ref, x_ref, gw_ref, wg_ref, wu_ref, wd_ref, y_ref):
    i = pl.program_id(0)

    @pl.when(i < nu_ref[0])
    def _():
        x = x_ref[...]
        h = _dot(x, wg_ref[0])
        u = _dot(x, wu_ref[0])
        a = (h * jax.nn.sigmoid(h)) * u * gw_ref[...]
        y_ref[...] = _dot(a.astype(BF16), wd_ref[0])

    @pl.when(i >= nu_ref[0])
    def _():
        y_ref[...] = jnp.zeros(y_ref.shape, F32)


def moe_ffn(xs, gate_rows, tile_expert, n_used, wg16, wu16, wd16):
    M, D = xs.shape
    F = wg16.shape[2]
    n_tiles = M // MOE_TM
    return pl.pallas_call(
        _moe_ffn_kernel,
        grid_spec=pltpu.PrefetchScalarGridSpec(
            num_scalar_prefetch=2,
            grid=(n_tiles,),
            in_specs=[
                pl.BlockSpec((MOE_TM, D), lambda i, te, nu: (i, 0)),
                pl.BlockSpec((MOE_TM, 1), lambda i, te, nu: (i, 0)),
                pl.BlockSpec((1, D, F), lambda i, te, nu: (te[i], 0, 0)),
                pl.BlockSpec((1, D, F), lambda i, te, nu: (te[i], 0, 0)),
                pl.BlockSpec((1, F, D), lambda i, te, nu: (te[i], 0, 0)),
            ],
            out_specs=pl.BlockSpec((MOE_TM, D), lambda i, te, nu: (i, 0)),
        ),
        out_shape=jax.ShapeDtypeStruct((M, D), F32),
        compiler_params=_cparams(("arbitrary",)),
        name="moe_ffn",
    )(tile_expert, n_used, xs, gate_rows, wg16, wu16, wd16)


def moe_dispatch(info):
    T = info.shape[0]
    eid = info[:, :TOP_K_INNER].astype(jnp.int32).reshape(-1)
    wts = info[:, TOP_K_INNER:2 * TOP_K_INNER].reshape(-1)
    M = T * TOP_K_INNER
    n_tiles = -(-(M + N_EXPERTS * (MOE_TM - 1)) // MOE_TM)
    onehot = (eid[:, None] == jnp.arange(N_EXPERTS)[None, :]).astype(jnp.int32)
    rank = jnp.cumsum(onehot, axis=0) - onehot
    counts = jnp.sum(onehot, axis=0)
    tiles_per = (counts + MOE_TM - 1) // MOE_TM
    tile_end = jnp.cumsum(tiles_per)
    offs = (tile_end - tiles_per) * MOE_TM
    dest = jnp.sum(onehot * (offs[None, :] + rank), axis=1)
    n_used = tile_end[-1]
    tile_ids = jnp.arange(n_tiles)
    tile_expert = jnp.minimum(jnp.sum(tile_ids[:, None] >= tile_end[None, :], axis=1), N_EXPERTS - 1)
    last_e = jnp.max(jnp.where(counts > 0, jnp.arange(N_EXPERTS), 0))
    tile_expert = jnp.where(tile_ids < n_used, tile_expert, last_e).astype(jnp.int32)
    src = jnp.zeros((n_tiles * MOE_TM,), jnp.int32).at[dest].set(jnp.arange(M, dtype=jnp.int32) // TOP_K_INNER)
    gate_rows = jnp.zeros((n_tiles * MOE_TM,), F32).at[dest].set(wts)
    return src, gate_rows.reshape(-1, 1), dest.reshape(T, TOP_K_INNER), tile_expert, n_used.reshape(1).astype(jnp.int32)


def moe_layer(xp, xs, g, w_group, b_group, w_expert, b_expert, wg16, wu16, wd16):
    Tp, Ts = xp.shape[0], xs.shape[0]
    hp, ip = moe_router(xp, g, w_group, b_group, w_expert, b_expert, tm=512)
    hs, is_ = moe_router(xs, g, w_group, b_group, w_expert, b_expert, tm=Ts)
    h = jnp.concatenate([hp, hs], axis=0)
    info = jnp.concatenate([ip, is_], axis=0)
    src, gate_rows, dest, tile_expert, n_used = moe_dispatch(info)
    y = moe_ffn(jnp.take(h, src, axis=0), gate_rows, tile_expert, n_used, wg16, wu16, wd16)
    f = jnp.take(y, dest[:, 0], axis=0) + jnp.take(y, dest[:, 1], axis=0)
    return xp + f[:Tp], xs + f[Tp:]


N_DIL = len(DIL_PATTERNS)
DIL_SPANS = tuple(w // d for w, d in DIL_PATTERNS)
DIL_STEPS = tuple(d for _, d in DIL_PATTERNS)
assert all(s <= QBLK for s in DIL_SPANS)
assert all(d & (d - 1) == 0 for d in DIL_STEPS)


def _dil_bias_kernel(dil_ref, tab_ref, o_ref):
    p = pl.program_id(0)
    dm = QBLK + lax.broadcasted_iota(jnp.int32, (QBLK, 2 * QBLK), 0) - lax.broadcasted_iota(jnp.int32, (QBLK, 2 * QBLK), 1)
    bucket = _t5_bucket(dm * dil_ref[p])
    for h in range(DIL_HEADS):
        o_ref[0, h] = _table_lookup(bucket, tab_ref, h)


def dil_bias_tiles(table):
    return pl.pallas_call(
        _dil_bias_kernel,
        grid=(N_DIL,),
        in_specs=[pl.BlockSpec(memory_space=pltpu.SMEM), pl.BlockSpec(memory_space=pltpu.SMEM)],
        out_specs=pl.BlockSpec((1, DIL_HEADS, QBLK, 2 * QBLK), lambda p: (p, 0, 0, 0)),
        out_shape=jax.ShapeDtypeStruct((N_DIL, DIL_HEADS, QBLK, 2 * QBLK), F32),
        compiler_params=_cparams(("arbitrary",)),
        name="dil_bias_tiles",
    )(jnp.array(DIL_STEPS, jnp.int32), table)


def _dil_prompt_kernel(q_ref, kp_ref, kc_ref, vp_ref, vc_ref, b_ref, num_ref, ml_ref, *, span, scale):
    mb = pl.program_id(2)
    qi = lax.broadcasted_iota(jnp.int32, (QBLK, 2 * QBLK), 0)
    kj = lax.broadcasted_iota(jnp.int32, (QBLK, 2 * QBLK), 1)
    dm = QBLK + qi - kj
    mask = (dm >= 0) & (dm <= span) & ((kj >= QBLK) | (mb > 0))
    lane = lax.broadcasted_iota(jnp.int32, (1, LANES), 1)
    m_t = jnp.zeros((QBLK, LANES), F32)
    l_t = jnp.zeros((QBLK, LANES), F32)
    for h in range(DIL_HEADS):
        cs = slice(h * HEAD_DIM, (h + 1) * HEAD_DIM)
        k = jnp.concatenate([kp_ref[0, :, cs], kc_ref[0, :, cs]], axis=0)
        v = jnp.concatenate([vp_ref[0, :, cs], vc_ref[0, :, cs]], axis=0)
        s = _dot_t(q_ref[0, :, cs], k) * scale + b_ref[0, h]
        s = jnp.where(mask, s, NEG_INF)
        m = jnp.max(s, axis=-1, keepdims=True)
        p = jnp.where(mask, jnp.exp(s - m), 0.0)
        l = jnp.sum(p, axis=-1, keepdims=True)
        num_ref[0, :, cs] = _dot(p.astype(BF16), v)
        m_t = jnp.where(lane == h, m, m_t)
        l_t = jnp.where(lane == h, l, l_t)
    ml_ref[0, :, 0:LANES] = m_t
    ml_ref[0, :, LANES:2 * LANES] = l_t


def dil_prompt_pattern(p16, bias, pat, B, S):
    span, dil = DIL_SPANS[pat], DIL_STEPS[pat]
    M = S // dil
    assert M % QBLK == 0
    pv = p16.reshape(B, M, dil * 3 * DIL_W)
    kern = functools.partial(_dil_prompt_kernel, span=span, scale=HEAD_DIM ** -0.5)
    blk = (1, QBLK, DIL_W)
    prev = lambda m: jnp.maximum(m - 1, 0)
    num, ml = pl.pallas_call(
        kern,
        grid=(B, dil, M // QBLK),
        in_specs=[
            pl.BlockSpec(blk, lambda b, r, m: (b, m, 3 * r)),
            pl.BlockSpec(blk, lambda b, r, m: (b, prev(m), 3 * r + 1)),
            pl.BlockSpec(blk, lambda b, r, m: (b, m, 3 * r + 1)),
            pl.BlockSpec(blk, lambda b, r, m: (b, prev(m), 3 * r + 2)),
            pl.BlockSpec(blk, lambda b, r, m: (b, m, 3 * r + 2)),
            pl.BlockSpec((1, DIL_HEADS, QBLK, 2 * QBLK), lambda b, r, m: (pat, 0, 0, 0)),
        ],
        out_specs=[
            pl.BlockSpec(blk, lambda b, r, m: (b, m, r)),
            pl.BlockSpec((1, QBLK, 2 * LANES), lambda b, r, m: (b, m, r)),
        ],
        out_shape=[jax.ShapeDtypeStruct((B, M, dil * DIL_W), F32), jax.ShapeDtypeStruct((B, M, dil * 2 * LANES), F32)],
        compiler_params=_cparams(("parallel", "parallel", "arbitrary")),
        name=f"dil_prompt_{dil}",
    )(pv, pv, pv, pv, pv, bias)
    return num.reshape(B * S, DIL_W), ml.reshape(B * S, 2 * LANES)


def _dil_merge_kernel(*refs):
    nums, mls, o_ref = refs[:N_DIL], refs[N_DIL:2 * N_DIL], refs[2 * N_DIL]
    m_all = [r[:, 0:LANES] for r in mls]
    l_all = [r[:, LANES:2 * LANES] for r in mls]
    m_max = m_all[0]
    for m in m_all[1:]:
        m_max = jnp.maximum(m_max, m)
    w_all = [jnp.exp(m - m_max) for m in m_all]
    den = w_all[0] * l_all[0]
    for w, l in zip(w_all[1:], l_all[1:]):
        den = den + w * l
    for h in range(DIL_HEADS):
        cs = slice(h * HEAD_DIM, (h + 1) * HEAD_DIM)
        acc = w_all[0][:, h:h + 1] * nums[0][:, cs]
        for w, n in zip(w_all[1:], nums[1:]):
            acc = acc + w[:, h:h + 1] * n[:, cs]
        o_ref[:, cs] = (acc / den[:, h:h + 1]).astype(o_ref.dtype)


def dil_merge(nums, mls, *, tm):
    T = nums[0].shape[0]
    return pl.pallas_call(
        _dil_merge_kernel,
        grid=(T // tm,),
        in_specs=[pl.BlockSpec((tm, DIL_W), lambda i: (i, 0))] * N_DIL + [pl.BlockSpec((tm, 2 * LANES), lambda i: (i, 0))] * N_DIL,
        out_specs=pl.BlockSpec((tm, DIL_W), lambda i: (i, 0)),
        out_shape=jax.ShapeDtypeStruct((T, DIL_W), BF16),
        compiler_params=_cparams(("parallel",)),
        name="dil_merge",
    )(*nums, *mls)


def _block_diag_rows(q, n_heads):
    DB, Q, W = q.shape
    dh = W // n_heads
    eye = jnp.repeat(jnp.eye(n_heads, dtype=q.dtype), dh, axis=1)
    return (q[:, None, :, :] * eye[None, :, None, :]).reshape(DB, n_heads * Q, W)


def _own_block(acc, n_heads, rows_per_head):
    rows = acc.shape[0]
    rh = lax.broadcasted_iota(jnp.int32, (rows, 1), 0) // rows_per_head
    out = jnp.zeros((rows, HEAD_DIM), F32)
    for h in range(n_heads):
        out = jnp.where(rh == h, acc[:, h * HEAD_DIM:(h + 1) * HEAD_DIM], out)
    return out


def _row_bias(bucket, tab_ref, head0, n_heads, rows_per_head):
    rows = bucket.shape[0]
    rh = lax.broadcasted_iota(jnp.int32, (rows, 1), 0) // rows_per_head
    out = jnp.zeros(bucket.shape, F32)
    for h in range(n_heads):
        out = jnp.where(rh == h, _table_lookup(bucket, tab_ref, head0 + h), out)
    return out


def _flash_step_cols(s, mask, v, m_scr, l_scr, acc_scr):
    s = jnp.where(mask, s, NEG_INF)
    m_prev = m_scr[:, 0:1]
    m_new = jnp.maximum(m_prev, jnp.max(s, axis=-1, keepdims=True))
    p = jnp.where(mask, jnp.exp(s - m_new), 0.0)
    alpha = jnp.exp(m_prev - m_new)
    l_new = alpha * l_scr[:, 0:1] + jnp.sum(p, axis=-1, keepdims=True)
    acc_scr[...] = alpha * acc_scr[...] + _dot(p.astype(BF16), v)
    m_scr[...] = jnp.broadcast_to(m_new, m_scr.shape)
    l_scr[...] = jnp.broadcast_to(l_new, l_scr.shape)


def _head_cols(ref, first, n_heads, n_rows, stride, row0=0):
    return jnp.concatenate(
        [ref[pl.ds(row0 * stride + first + h, n_rows, stride=stride), :] for h in range(n_heads)], axis=1)


SB_DEC_PAGES = 4


def _sb_decode_kernel(pt_ref, q_ref, new_ref, *rest, n_steps, n_q, scale):
    pages, (o_ref, acc_scr, run_scr) = rest[:SB_DEC_PAGES], rest[SB_DEC_PAGES:]
    s_ = pl.program_id(1)
    q = q_ref[0]
    rows = q.shape[0]
    upper2 = _upper01(PAGE_SIZE)

    def absorb(k, v, mask):
        z2 = _dot_t(q, k.astype(BF16)) * (scale * LOG2_E)
        lk = _log2_keep(z2)
        lb = z2 + lk
        if mask is not None:
            lk = jnp.where(mask, lk, 0.0)
        w = jnp.exp2(lb + _after_in_tile(lk, upper2) + run_scr[...])
        if mask is not None:
            w = jnp.where(mask, w, 0.0)
        acc_scr[...] = acc_scr[...] + _dot(w.astype(BF16), v.astype(BF16))
        run_scr[...] = run_scr[...] + jnp.sum(lk, axis=-1, keepdims=True)

    @pl.when(s_ == 0)
    def _():
        acc_scr[...] = jnp.zeros(acc_scr.shape, F32)
        run_scr[...] = jnp.zeros(run_scr.shape, F32)
        t = lax.broadcasted_iota(jnp.int32, (rows, PAGE_SIZE), 1)
        qi = lax.broadcasted_iota(jnp.int32, (rows, PAGE_SIZE), 0) % n_q
        absorb(new_ref[0, :, :SB_W], new_ref[0, :, SB_W:], (t < qi) & (t < n_q))

    for page in pages:
        absorb(_head_cols(page, 0, SB_HEADS, PAGE_SIZE, 2 * SB_HEADS),
               _head_cols(page, SB_HEADS, SB_HEADS, PAGE_SIZE, 2 * SB_HEADS), None)

    @pl.when(s_ == n_steps - 1)
    def _():
        o_ref[0] = _own_block(acc_scr[...], SB_HEADS, n_q)


def sb_decode(page_table, qbd16, new_kv, cache_lines):
    DB, n_pages = page_table.shape
    assert n_pages % SB_DEC_PAGES == 0
    n_steps = n_pages // SB_DEC_PAGES
    rows = qbd16.shape[1]
    n_q = rows // SB_HEADS
    lines = PAGE_SIZE * 2 * SB_HEADS
    kern = functools.partial(_sb_decode_kernel, n_steps=n_steps, n_q=n_q, scale=HEAD_DIM ** -0.5)
    page_spec = lambda j: pl.BlockSpec(
        (lines, HEAD_DIM), lambda b, s, pt: (pt[b, n_pages - 1 - (s * SB_DEC_PAGES + j)], 0))
    return pl.pallas_call(
        kern,
        grid_spec=pltpu.PrefetchScalarGridSpec(
            num_scalar_prefetch=1,
            grid=(DB, n_steps),
            in_specs=[
                pl.BlockSpec((1, rows, SB_W), lambda b, s, pt: (b, 0, 0)),
                pl.BlockSpec((1, PAGE_SIZE, 2 * SB_W), lambda b, s, pt: (b, 0, 0)),
            ] + [page_spec(j) for j in range(SB_DEC_PAGES)],
            out_specs=pl.BlockSpec((1, rows, HEAD_DIM), lambda b, s, pt: (b, 0, 0)),
            scratch_shapes=[pltpu.VMEM((rows, SB_W), F32), pltpu.VMEM((rows, LANES), F32)],
        ),
        out_shape=jax.ShapeDtypeStruct((DB, rows, HEAD_DIM), F32),
        compiler_params=_cparams(("parallel", "arbitrary")),
        name="sb_decode",
    )(page_table, qbd16, new_kv, *([cache_lines] * SB_DEC_PAGES))


CMP_ROWS_TILE = 256
CHUNKS_PER_PAGE = PAGE_SIZE // CMP_STRIDE


def _nsa_cmp_decode_kernel(pt_ref, tab_ref, q_ref, pg0_ref, pg1_ref, w_ref, pe_ref, g_ref, oc_ref, sel_ref,
                           x_scr, a_scr, *, n_steps, n_q, past, scale):
    s_ = pl.program_id(1)
    G, R = NSA_KV_HEADS, NSA_REP
    slots = 2 * G
    rows16 = 2 * CHUNKS_PER_PAGE
    for kvg in range(slots):
        xs = jnp.concatenate(
            [jnp.concatenate([pg[pl.ds(l * slots + kvg, CHUNKS_PER_PAGE, stride=CMP_STRIDE * slots), :]
                              for pg in (pg0_ref, pg1_ref)], axis=0) for l in range(CMP_STRIDE)], axis=1)
        x_scr[kvg, pl.ds(pl.multiple_of(s_ * rows16, rows16), rows16), :] = xs.astype(BF16)

    @pl.when(s_ == n_steps - 1)
    def _():
        n_chunk = x_scr.shape[1]
        n_c = n_chunk - CMP_HALF + 1
        for rt in range(n_chunk // CMP_ROWS_TILE):
            rs = slice(rt * CMP_ROWS_TILE, (rt + 1) * CMP_ROWS_TILE)
            for kv in range(2):
                for g in range(G):
                    off = kv * NSA_KVW + g * HEAD_DIM
                    xs = x_scr[kv * G + g, rs, :]
                    for j in range(CMP_HALF):
                        w = w_ref[kv, j]
                        pe = jnp.broadcast_to(pe_ref[kv, j], (SUBLANES, CMP_STRIDE * HEAD_DIM)).astype(BF16)
                        a_scr[j, rs, off:off + HEAD_DIM] = _dot(xs, w) + _dot(pe, w)[0:1, :]
        out = a_scr[0]
        for j in range(1, CMP_HALF):
            out = out + pltpu.roll(a_scr[j], n_chunk - j, axis=0)
        kparts = []
        for g in range(G):
            kg = out[:, g * HEAD_DIM:(g + 1) * HEAD_DIM]
            kparts.append(kg * lax.rsqrt(jnp.mean(kg * kg, axis=-1, keepdims=True) + RMS_EPS) * g_ref[...])
        kc = jnp.concatenate(kparts, axis=1).astype(BF16)
        vc = out[:, NSA_KVW:2 * NSA_KVW].astype(BF16)

        q = q_ref[0]
        nrows = q.shape[0]
        qi = lax.broadcasted_iota(jnp.int32, (nrows, 1), 0) % n_q
        n_idx = lax.broadcasted_iota(jnp.int32, (1, n_chunk), 1)
        dist = past + qi - (n_idx * CMP_STRIDE + (CMP_BLOCK - 1))
        mask = (dist >= 0) & (n_idx < n_c)
        s = _dot_t(q, kc) * scale + _row_bias(_t5_bucket(dist), tab_ref, 0, NSA_HEADS, n_q)
        s = jnp.where(mask, s, NEG_INF)
        m = jnp.max(s, axis=-1, keepdims=True)
        p = jnp.where(mask, jnp.exp(s - m), 0.0)
        p = p / jnp.maximum(jnp.sum(p, axis=-1, keepdims=True), 1.0)
        oc_ref[0] = _own_block(_dot(p.astype(BF16), vc), G, R * n_q)
        gq = G * n_q
        oi = lax.broadcasted_iota(jnp.int32, (gq, nrows), 0)
        ri = lax.broadcasted_iota(jnp.int32, (gq, nrows), 1)
        pick = jnp.where((ri // (R * n_q) == oi // n_q) & (ri % n_q == oi % n_q), 1.0, 0.0).astype(BF16)
        p1, p2, p3 = _split3(p)
        imp = _dot(pick, p1) + _dot(pick, p2) + _dot(pick, p3)
        n_sp = sel_ref.shape[2]
        ratio = SLC_BLOCK // CMP_STRIDE
        ci = lax.broadcasted_iota(jnp.int32, (n_chunk, n_sp), 0)
        ti = lax.broadcasted_iota(jnp.int32, (n_chunk, n_sp), 1)
        gather01 = jnp.where((ci >= ti * ratio - (CMP_HALF - 1)) & (ci < (ti + 1) * ratio), 1.0, 0.0).astype(BF16)
        score = _dot01(imp, gather01)
        q_blk = (past + lax.broadcasted_iota(jnp.int32, (gq, 1), 0) % n_q) // SLC_BLOCK
        sel_ref[0] = _select_blocks(score, q_blk, N_SELECT)


def nsa_cmp_decode(page_table, table, qbd16, cache_lines, w16, pe, ck_gain, n_q):
    DB, n_pages = page_table.shape
    assert n_pages % 2 == 0
    n_steps = n_pages // 2
    n_chunk = n_pages * CHUNKS_PER_PAGE
    assert n_chunk % CMP_ROWS_TILE == 0
    past = n_pages * PAGE_SIZE
    n_s = -(-(past + n_q) // SLC_BLOCK)
    n_sp = -(-n_s // LANES) * LANES
    rows = qbd16.shape[1]
    kern = functools.partial(_nsa_cmp_decode_kernel, n_steps=n_steps, n_q=n_q, past=past, scale=HEAD_DIM ** -0.5)
    pg = (PAGE_SIZE * 2 * NSA_KV_HEADS, HEAD_DIM)
    return pl.pallas_call(
        kern,
        grid_spec=pltpu.PrefetchScalarGridSpec(
            num_scalar_prefetch=1,
            grid=(DB, n_steps),
            in_specs=[
                pl.BlockSpec(memory_space=pltpu.SMEM),
                pl.BlockSpec((1, rows, NSA_KVW), lambda b, s, pt: (b, 0, 0)),
                pl.BlockSpec(pg, lambda b, s, pt: (pt[b, 2 * s], 0)),
                pl.BlockSpec(pg, lambda b, s, pt: (pt[b, 2 * s + 1], 0)),
                pl.BlockSpec((2, CMP_HALF, CMP_STRIDE * HEAD_DIM, HEAD_DIM), lambda b, s, pt: (0, 0, 0, 0)),
                pl.BlockSpec((2, CMP_HALF, 1, CMP_STRIDE * HEAD_DIM), lambda b, s, pt: (0, 0, 0, 0)),
                pl.BlockSpec((1, HEAD_DIM), lambda b, s, pt: (0, 0)),
            ],
            out_specs=[
                pl.BlockSpec((1, rows, HEAD_DIM), lambda b, s, pt: (b, 0, 0)),
                pl.BlockSpec((1, NSA_KV_HEADS * n_q, n_sp), lambda b, s, pt: (b, 0, 0)),
            ],
            scratch_shapes=[
                pltpu.VMEM((2 * NSA_KV_HEADS, n_chunk, CMP_STRIDE * HEAD_DIM), BF16),
                pltpu.VMEM((CMP_HALF, n_chunk, 2 * NSA_KVW), F32),
            ],
        ),
        out_shape=[jax.ShapeDtypeStruct((DB, rows, HEAD_DIM), F32),
                   jax.ShapeDtypeStruct((DB, NSA_KV_HEADS * n_q, n_sp), F32)],
        compiler_params=_cparams(("parallel", "arbitrary")),
        name="nsa_cmp_decode",
    )(page_table, table, qbd16, cache_lines, cache_lines, w16, pe, ck_gain.reshape(1, HEAD_DIM))


SLC_DEC_PAGES = 4


def _nsa_slc_decode_kernel(pt_ref, plist_ref, pcnt_ref, tab_ref, q_ref, sel_ref, new_ref, wbuf_ref, wnew_ref, oc_ref,
                           gt_ref, *rest, n_steps, n_q, past, scale):
    pages, (o_ref, m_scr, l_scr, acc_scr) = rest[:SLC_DEC_PAGES], rest[SLC_DEC_PAGES:]
    b_ = pl.program_id(0)
    s_ = pl.program_id(1)
    q = q_ref[0]
    rows = q.shape[0]
    G, R = NSA_KV_HEADS, NSA_REP
    slots = 2 * G
    qi = lax.broadcasted_iota(jnp.int32, (rows, 1), 0) % n_q
    rh = lax.broadcasted_iota(jnp.int32, (rows, 1), 0) // n_q
    per_page = PAGE_SIZE // SLC_BLOCK
    n_sp = sel_ref.shape[2]
    col = lax.broadcasted_iota(jnp.int32, (1, PAGE_SIZE), 1)

    def sel_cols(first_block):
        bi = lax.broadcasted_iota(jnp.int32, (n_sp, PAGE_SIZE), 0)
        ki = lax.broadcasted_iota(jnp.int32, (n_sp, PAGE_SIZE), 1)
        expand = jnp.where(bi == first_block + ki // SLC_BLOCK, 1.0, 0.0).astype(BF16)
        return _dot(sel_ref[0], expand) > 0.5

    def near_bias(dist):
        return _row_bias(_t5_bucket(dist), tab_ref, 0, NSA_HEADS, n_q)

    def far_bias():
        out = jnp.zeros((rows, 1), F32)
        for h in range(NSA_HEADS):
            out = jnp.where(rh == h, tab_ref[N_BUCKETS - 1, h], out)
        return jnp.broadcast_to(out, (rows, PAGE_SIZE))

    def attend(k, v, bias, mask):
        s = _dot_t(q, k.astype(BF16)) * scale + bias
        _flash_step_cols(s, mask, v.astype(BF16), m_scr, l_scr, acc_scr)

    def result():
        return _own_block(acc_scr[...] / jnp.maximum(l_scr[:, 0:1], 1.0), G, R * n_q)

    @pl.when(s_ == 0)
    def _():
        _flash_reset(m_scr, l_scr, acc_scr)

    for j, page in enumerate(pages):
        idx = s_ * SLC_DEC_PAGES + j

        @pl.when(idx < pcnt_ref[b_])
        def _(page=page, idx=idx):
            p = plist_ref[b_, idx]
            dist = past + qi - (p * PAGE_SIZE + col)
            bias = lax.cond(past - (p + 1) * PAGE_SIZE + 1 >= BUCKET_MAX_DIST, far_bias, lambda: near_bias(dist))
            attend(_head_cols(page, 0, G, PAGE_SIZE, slots), _head_cols(page, G, G, PAGE_SIZE, slots), bias,
                   sel_cols(p * per_page) & (dist >= 0))

    @pl.when(s_ == n_steps - 1)
    def _():
        dist_n = qi - col
        attend(new_ref[0, :, :NSA_KVW], new_ref[0, :, NSA_KVW:], near_bias(dist_n),
               sel_cols(past // SLC_BLOCK) & (dist_n >= 0) & (col < n_q))
        o_s = result()
        _flash_reset(m_scr, l_scr, acc_scr)
        wb = wbuf_ref.shape[0] // slots
        for t in range(wb // PAGE_SIZE):
            dist_w = wb + qi - (t * PAGE_SIZE + col)
            attend(_head_cols(wbuf_ref, 0, G, PAGE_SIZE, slots, row0=t * PAGE_SIZE),
                   _head_cols(wbuf_ref, G, G, PAGE_SIZE, slots, row0=t * PAGE_SIZE), near_bias(dist_w),
                   (dist_w >= 0) & (dist_w < NSA_WINDOW))
        attend(wnew_ref[0, :, :NSA_KVW], wnew_ref[0, :, NSA_KVW:], near_bias(dist_n), (dist_n >= 0) & (col < n_q))
        o_w = result()
        gt = gt_ref[0]
        o_ref[0] = gt[:, 0:1] * oc_ref[0] + gt[:, 1:2] * o_s + gt[:, 2:3] * o_w


def nsa_slc_decode(page_table, table, qbd16, sel16, cache_lines, new_kv, win_lines, win_new, o_c, gates, n_q):
    DB, n_pages = page_table.shape
    assert n_pages % SLC_DEC_PAGES == 0
    n_steps = n_pages // SLC_DEC_PAGES
    rows = qbd16.shape[1]
    past = n_pages * PAGE_SIZE
    slots = 2 * NSA_KV_HEADS
    wb_lines = win_lines.shape[0] // DB
    assert (wb_lines // slots) % PAGE_SIZE == 0
    kern = functools.partial(_nsa_slc_decode_kernel, n_steps=n_steps, n_q=n_q, past=past, scale=HEAD_DIM ** -0.5)
    per_page = PAGE_SIZE // SLC_BLOCK
    need = jnp.any(sel16[:, :, :n_pages * per_page].reshape(DB, rows, n_pages, per_page) > 0, axis=(1, 3))
    plist = jnp.argsort(jnp.logical_not(need), axis=1, stable=True).astype(jnp.int32)
    pcnt = jnp.maximum(jnp.sum(need, axis=1), 1).astype(jnp.int32)
    full = lambda shape: pl.BlockSpec((1,) + shape, lambda b, s, pt, pls, pc: (b, 0, 0))

    def page_spec(j):
        def index(b, s, pt, pls, pc):
            return (pt[b, pls[b, jnp.minimum(s * SLC_DEC_PAGES + j, pc[b] - 1)]], 0)
        return pl.BlockSpec((PAGE_SIZE * slots, HEAD_DIM), index)

    return pl.pallas_call(
        kern,
        grid_spec=pltpu.PrefetchScalarGridSpec(
            num_scalar_prefetch=3,
            grid=(DB, n_steps),
            in_specs=[
                pl.BlockSpec(memory_space=pltpu.SMEM),
                full((rows, NSA_KVW)),
                full((rows, sel16.shape[2])),
                full((PAGE_SIZE, 2 * NSA_KVW)),
                pl.BlockSpec((wb_lines, HEAD_DIM), lambda b, s, pt, pls, pc: (b, 0)),
                full((PAGE_SIZE, 2 * NSA_KVW)),
                full((rows, HEAD_DIM)),
                full((rows, LANES)),
            ] + [page_spec(j) for j in range(SLC_DEC_PAGES)],
            out_specs=full((rows, HEAD_DIM)),
            scratch_shapes=[pltpu.VMEM((rows, LANES), F32), pltpu.VMEM((rows, LANES), F32),
                            pltpu.VMEM((rows, NSA_KVW), F32)],
        ),
        out_shape=jax.ShapeDtypeStruct((DB, rows, HEAD_DIM), F32),
        compiler_params=_cparams(("parallel", "arbitrary")),
        name="nsa_slc_decode",
    )(page_table, plist, pcnt, table, qbd16, sel16, new_kv, win_lines, win_new, o_c, gates,
      *([cache_lines] * SLC_DEC_PAGES))


DIL_DEC_TILE = 256


def _dil_decode_kernel(tab_ref, q_ref, buf_ref, new_ref, o_ref, m_scr, l_scr, acc_scr, *, n_tiles, n_q, wb, scale):
    kt = pl.program_id(1)
    q = q_ref[0]
    rows = q.shape[0]
    qi = lax.broadcasted_iota(jnp.int32, (rows, 1), 0) % n_q

    @pl.when(kt == 0)
    def _():
        _flash_reset(m_scr, l_scr, acc_scr)

    def attend(k, v, dist, valid):
        v = v.astype(BF16)
        s = _dot_t(q, k.astype(BF16)) * scale + _row_bias(_t5_bucket(dist), tab_ref, 0, DIL_HEADS, n_q)
        ps = []
        for i, (span, dil) in enumerate(zip(DIL_SPANS, DIL_STEPS)):
            mask = valid & (dist >= 0) & (jnp.bitwise_and(dist, dil - 1) == 0) & (dist <= span * dil)
            rs = slice(i * rows, (i + 1) * rows)
            si = jnp.where(mask, s, NEG_INF)
            m_prev = m_scr[rs, 0:1]
            m_new = jnp.maximum(m_prev, jnp.max(si, axis=-1, keepdims=True))
            p = jnp.where(mask, jnp.exp(si - m_new), 0.0)
            alpha = jnp.exp(m_prev - m_new)
            l_scr[rs, :] = jnp.broadcast_to(alpha * l_scr[rs, 0:1] + jnp.sum(p, axis=-1, keepdims=True), (rows, LANES))
            m_scr[rs, :] = jnp.broadcast_to(m_new, (rows, LANES))
            acc_scr[rs, :] = alpha * acc_scr[rs, :]
            ps.append(p.astype(BF16))
        acc_scr[...] = acc_scr[...] + _dot(jnp.concatenate(ps, axis=0), v)

    n = DIL_DEC_TILE
    slots = 2 * DIL_HEADS
    col = lax.broadcasted_iota(jnp.int32, (1, n), 1)
    attend(_head_cols(buf_ref, 0, DIL_HEADS, n, slots), _head_cols(buf_ref, DIL_HEADS, DIL_HEADS, n, slots),
           wb + qi - (kt * n + col), col >= 0)

    @pl.when(kt == n_tiles - 1)
    def _():
        cn = lax.broadcasted_iota(jnp.int32, (1, new_ref.shape[1]), 1)
        attend(new_ref[0, :, :DIL_W], new_ref[0, :, DIL_W:], qi - cn, cn < n_q)
        nums = [_own_block(acc_scr[i * rows:(i + 1) * rows, :], DIL_HEADS, n_q) for i in range(N_DIL)]
        ms = [m_scr[i * rows:(i + 1) * rows, 0:1] for i in range(N_DIL)]
        ls = [l_scr[i * rows:(i + 1) * rows, 0:1] for i in range(N_DIL)]
        m_max = ms[0]
        for m in ms[1:]:
            m_max = jnp.maximum(m_max, m)
        ws = [jnp.exp(m - m_max) for m in ms]
        num = ws[0] * nums[0]
        den = ws[0] * ls[0]
        for w, n_, l in zip(ws[1:], nums[1:], ls[1:]):
            num = num + w * n_
            den = den + w * l
        o_ref[0] = num / den


def dil_decode(table, qbd16, buf_lines, new_kv, n_q):
    DB = qbd16.shape[0]
    slots = 2 * DIL_HEADS
    wb = buf_lines.shape[0] // (DB * slots)
    rows = qbd16.shape[1]
    assert wb % DIL_DEC_TILE == 0
    n_tiles = wb // DIL_DEC_TILE
    kern = functools.partial(_dil_decode_kernel, n_tiles=n_tiles, n_q=n_q, wb=wb, scale=HEAD_DIM ** -0.5)
    return pl.pallas_call(
        kern,
        grid=(DB, n_tiles),
        in_specs=[
            pl.BlockSpec(memory_space=pltpu.SMEM),
            pl.BlockSpec((1, rows, DIL_W), lambda b, t: (b, 0, 0)),
            pl.BlockSpec((DIL_DEC_TILE * slots, HEAD_DIM), lambda b, t: (b * n_tiles + t, 0)),
            pl.BlockSpec((1, new_kv.shape[1], 2 * DIL_W), lambda b, t: (b, 0, 0)),
        ],
        out_specs=pl.BlockSpec((1, rows, HEAD_DIM), lambda b, t: (b, 0, 0)),
        out_shape=jax.ShapeDtypeStruct((DB, rows, HEAD_DIM), F32),
        scratch_shapes=[pltpu.VMEM((N_DIL * rows, LANES), F32), pltpu.VMEM((N_DIL * rows, LANES), F32),
                        pltpu.VMEM((N_DIL * rows, DIL_W), F32)],
        compiler_params=_cparams(("parallel", "arbitrary")),
        name="dil_decode",
    )(table, qbd16, buf_lines, new_kv)


PROJ_TN = 256
_C_SBQ, _C_SBK, _C_SBV = 0, SB_W, 2 * SB_W
_C_NQ = 3 * SB_W
_C_CMP = _C_NQ + NSA_QW
_C_SLC = _C_CMP + 2 * NSA_KVW
_C_WIN = _C_SLC + 2 * NSA_KVW
_C_GATE = _C_WIN + 2 * NSA_KVW
SPARSE_NP = -(-SPARSE_IN // PROJ_TN) * PROJ_TN


def _sparse_proj_plan(w_in, qk_gain):
    w16 = jnp.pad(w_in, ((0, 0), (0, SPARSE_NP - SPARSE_IN))).astype(BF16)
    gain = jnp.ones((SPARSE_NP,), F32)
    flag = np.zeros((SPARSE_NP,), np.float32)
    kinds = np.full((SPARSE_NP // PROJ_TN,), EPI_PLAIN, np.int32)
    for col, width, gi in ((_C_NQ, NSA_QW, 0), (_C_SLC, NSA_KVW, 2), (_C_WIN, NSA_KVW, 3)):
        gain = gain.at[col:col + width].set(jnp.tile(qk_gain[gi], width // HEAD_DIM))
        flag[col:col + width] = 1.0
        assert col % PROJ_TN == 0 and width % PROJ_TN == 0
        kinds[col // PROJ_TN:(col + width) // PROJ_TN] = EPI_NORM
    assert _C_GATE % PROJ_TN == 0 and SPARSE_NP - _C_GATE == PROJ_TN
    kinds[_C_GATE // PROJ_TN] = EPI_SIGMOID
    return w16, gain, jnp.asarray(flag), jnp.asarray(kinds)


def _dil_proj_plan(w_in, qk_gain):
    N = 3 * DIL_W
    gain = jnp.concatenate([jnp.tile(qk_gain[0], DIL_HEADS), jnp.tile(qk_gain[1], DIL_HEADS), jnp.ones((DIL_W,), F32)])
    flag = np.concatenate([np.ones((2 * DIL_W,), np.float32), np.zeros((DIL_W,), np.float32)])
    kinds = np.full((N // PROJ_TN,), EPI_PLAIN, np.int32)
    kinds[:2 * DIL_W // PROJ_TN] = EPI_NORM
    return w_in.astype(BF16), gain, jnp.asarray(flag), jnp.asarray(kinds)


def _pad_rows(a, n):
    return jnp.pad(a, ((0, 0), (0, n - a.shape[1]), (0, 0)))


def _rows_to_tokens(o, n_heads, n_q):
    DB = o.shape[0]
    return o.reshape(DB, n_heads, n_q, HEAD_DIM).transpose(0, 2, 1, 3).reshape(DB * n_q, n_heads * HEAD_DIM)


def kernel(x_prompt, x_sample, cache_sb_kv, cache_nsa_cmp_kv, cache_nsa_slc_kv, state_nsa_win_kv, state_dil_kv,
           page_table, rel_bias_table, norm_mix, norm_ffn, sparse_w_in, sparse_w_out, nsa_qk_gain, nsa_cmp_w,
           nsa_cmp_pe, dil_w_in, dil_w_out, dil_qk_gain, moe_w_group, moe_b_group, moe_w_expert, moe_b_expert,
           moe_w_gate, moe_w_up, moe_w_down):
    B, S, D = x_prompt.shape
    DB, DS, _ = x_sample.shape
    n_pool = cache_sb_kv.shape[1]
    xp = x_prompt.reshape(B * S, D)
    xs = x_sample.reshape(DB * DS, D)
    nsa_table = rel_bias_table[:, :NSA_HEADS]
    dil_table = rel_bias_table[:, :DIL_HEADS]
    G, R = NSA_KV_HEADS, NSA_REP

    def moe(layer, xp, xs):
        return moe_layer(xp, xs, norm_ffn[layer], moe_w_group[layer], moe_b_group[layer], moe_w_expert[layer],
                         moe_b_expert[layer], moe_w_gate[layer].astype(BF16), moe_w_up[layer].astype(BF16),
                         moe_w_down[layer].astype(BF16))

    w16, gain, flag, kinds = _sparse_proj_plan(sparse_w_in[0], nsa_qk_gain[0])
    w_out16 = sparse_w_out[0].astype(BF16)
    cw16, cpe = _prep_cmp_weights(nsa_cmp_w[0], nsa_cmp_pe[0])
    ck_gain = nsa_qk_gain[0, 1]
    p32, p16 = norm_mm(xp, norm_mix[0], w16, gain, flag, kinds, tm=1024, tn=PROJ_TN)
    p32b = p32.reshape(B, S, SPARSE_NP)
    p16b = p16.reshape(B, S, SPARSE_NP)
    o_sb = sb_prompt(p16b, B, S, q_col=_C_SBQ, k_col=_C_SBK, v_col=_C_SBV)
    chunks = p16b[:, :, _C_CMP:_C_CMP + 2 * NSA_KVW].reshape(B, S // CMP_STRIDE, CHUNK_W)
    kc, vc = compress_prompt(chunks, cw16, cpe, ck_gain)
    tb = nsa_bias_tiles(nsa_table)
    cb = nsa_cmp_bias(nsa_table, S, kc.shape[1])
    o_nsa = nsa_prompt(p16b, p32b, kc, vc, tb, cb, B, S, q_col=_C_NQ, sk_col=_C_SLC, sv_col=_C_SLC + NSA_KVW,
                       wk_col=_C_WIN, wv_col=_C_WIN + NSA_KVW, gate_col=_C_GATE)
    a = jnp.concatenate([o_sb, o_nsa], axis=-1).reshape(B * S, SB_W + NSA_QW)
    xp = mm_res(a, w_out16, xp, tm=1024, tn=PROJ_TN)
    sb_kv_prompt = p32b[:, :, _C_SBK:_C_SBK + 2 * SB_W].reshape(1, B, S, 2, SB_HEADS, HEAD_DIM)
    cmp_kv_prompt = p32b[:, :, _C_CMP:_C_CMP + 2 * NSA_KVW].reshape(1, B, S, 2, G, HEAD_DIM)
    slc_kv_prompt = p32b[:, :, _C_SLC:_C_SLC + 2 * NSA_KVW].reshape(1, B, S, 2, G, HEAD_DIM)
    wn = min(NSA_WINDOW, S)
    win_kv_prompt = p32b[:, S - wn:, _C_WIN:_C_WIN + 2 * NSA_KVW].reshape(1, B, wn, 2, G, HEAD_DIM)
    q32, _ = norm_mm(xs, norm_mix[0], w16, gain, flag, kinds, tm=DB * DS, tn=PROJ_TN)
    q32b = q32.reshape(DB, DS, SPARSE_NP)
    sb_new = q32b[:, :, _C_SBK:_C_SBK + 2 * SB_W]
    cmp_new = q32b[:, :, _C_CMP:_C_CMP + 2 * NSA_KVW]
    slc_new = q32b[:, :, _C_SLC:_C_SLC + 2 * NSA_KVW]
    win_new = q32b[:, :, _C_WIN:_C_WIN + 2 * NSA_KVW]
    o_sb_s = sb_decode(page_table, _block_diag_rows(q32b[:, :, _C_SBQ:_C_SBQ + SB_W], SB_HEADS).astype(BF16),
                       _pad_rows(sb_new, PAGE_SIZE), cache_sb_kv[0].reshape(-1, HEAD_DIM))
    qn = q32b[:, :, _C_NQ:_C_NQ + NSA_QW].reshape(DB, DS, G, R, HEAD_DIM).transpose(0, 2, 3, 1, 4)
    qn = (qn[:, :, :, :, None, :] * jnp.eye(G, dtype=F32)[None, :, None, None, :, None]).reshape(DB, G * R * DS, NSA_KVW)
    qn16 = qn.astype(BF16)
    o_c, sel = nsa_cmp_decode(page_table, nsa_table, qn16, cache_nsa_cmp_kv[0].reshape(-1, HEAD_DIM), cw16, cpe, ck_gain, DS)
    sel16 = jnp.broadcast_to(sel.reshape(DB, G, 1, DS, -1), (DB, G, R, DS, sel.shape[-1])).reshape(DB, G * R * DS, -1).astype(BF16)
    gates = q32b[:, :, _C_GATE:_C_GATE + 3 * NSA_HEADS].reshape(DB, DS, G, R, 3).transpose(0, 2, 3, 1, 4).reshape(DB, G * R * DS, 3)
    gates = jnp.pad(gates, ((0, 0), (0, 0), (0, LANES - 3)))
    wbuf = state_nsa_win_kv[0]
    o_nsa_s = nsa_slc_decode(page_table, nsa_table, qn16, sel16, cache_nsa_slc_kv[0].reshape(-1, HEAD_DIM),
                             _pad_rows(slc_new, PAGE_SIZE), wbuf.reshape(-1, HEAD_DIM), _pad_rows(win_new, PAGE_SIZE),
                             o_c, gates, DS)
    a_s = jnp.concatenate([_rows_to_tokens(o_sb_s, SB_HEADS, DS), _rows_to_tokens(o_nsa_s, NSA_HEADS, DS)], axis=-1)
    xs = mm_res(a_s.astype(BF16), w_out16, xs, tm=DB * DS, tn=PROJ_TN)
    sb_kv_sample = sb_new.reshape(1, DB, DS, 2, SB_HEADS, HEAD_DIM)
    cmp_kv_sample = cmp_new.reshape(1, DB, DS, 2, G, HEAD_DIM)
    slc_kv_sample = slc_new.reshape(1, DB, DS, 2, G, HEAD_DIM)
    win_kv_sample = jnp.concatenate([wbuf, win_new.reshape(DB, DS, 2, G, HEAD_DIM)], axis=1)[None, :, DS:]
    xp, xs = moe(0, xp, xs)

    w16, gain, flag, kinds = _dil_proj_plan(dil_w_in[0], dil_qk_gain[0])
    w_out16 = dil_w_out[0].astype(BF16)
    p32, p16 = norm_mm(xp, norm_mix[1], w16, gain, flag, kinds, tm=1024, tn=PROJ_TN)
    dbias = dil_bias_tiles(dil_table)
    stats = [dil_prompt_pattern(p16.reshape(B, S, 3 * DIL_W), dbias, pat, B, S) for pat in range(N_DIL)]
    o = dil_merge([n for n, _ in stats], [ml for _, ml in stats], tm=512)
    xp = mm_res(o, w_out16, xp, tm=1024, tn=PROJ_TN)
    dn = min(DIL_MAX_WINDOW, S)
    dil_kv_prompt = p32.reshape(B, S, 3 * DIL_W)[:, S - dn:, DIL_W:].reshape(1, B, dn, 2, DIL_HEADS, HEAD_DIM)
    q32, _ = norm_mm(xs, norm_mix[1], w16, gain, flag, kinds, tm=DB * DS, tn=PROJ_TN)
    q32b = q32.reshape(DB, DS, 3 * DIL_W)
    dil_new = q32b[:, :, DIL_W:]
    dbuf = state_dil_kv[0]
    o_s = dil_decode(dil_table, _block_diag_rows(q32b[:, :, :DIL_W], DIL_HEADS).astype(BF16),
                     dbuf.reshape(-1, HEAD_DIM), _pad_rows(dil_new, PAGE_SIZE), DS)
    xs = mm_res(_rows_to_tokens(o_s, DIL_HEADS, DS).astype(BF16), w_out16, xs, tm=DB * DS, tn=PROJ_TN)
    dil_kv_sample = jnp.concatenate([dbuf, dil_new.reshape(DB, DS, 2, DIL_HEADS, HEAD_DIM)], axis=1)[None, :, DS:]
    xp, xs = moe(1, xp, xs)

    return (xp.reshape(B, S, D), xs.reshape(DB, DS, D), sb_kv_prompt, sb_kv_sample, cmp_kv_prompt, cmp_kv_sample,
            slc_kv_prompt, slc_kv_sample, win_kv_prompt, win_kv_sample, dil_kv_prompt, dil_kv_sample)
```

```python
import functools
import math

import jax
import jax.numpy as jnp
import numpy as np
from jax import lax
from jax.experimental import pallas as pl
from jax.experimental.pallas import tpu as pltpu

D_MODEL = 2048
HEAD_DIM = 128
SB_HEADS = 8
NSA_HEADS = 8
NSA_KV_HEADS = 2
NSA_REP = NSA_HEADS // NSA_KV_HEADS
CMP_BLOCK = 32
CMP_STRIDE = 16
SLC_BLOCK = 64
N_SELECT = 16
NSA_WINDOW = 512
DIL_HEADS = 16
DIL_PATTERNS = ((128, 1), (512, 4), (2048, 16))
DIL_MAX_WINDOW = 2048
QBLK = 128
N_BUCKETS = 32
BUCKET_MAX_DIST = 2048
N_GROUPS = 4
EXPERTS_PER_GROUP = 4
N_EXPERTS = N_GROUPS * EXPERTS_PER_GROUP
TOP_K_INNER = 2
D_EXPERT = 1024
PAGE_SIZE = 128
RMS_EPS = 1e-6
NEG_INF = -1e30
FORCE_SCORE = 1e9
SB_W = SB_HEADS * HEAD_DIM
NSA_QW = NSA_HEADS * HEAD_DIM
NSA_KVW = NSA_KV_HEADS * HEAD_DIM
SPARSE_IN = 3 * SB_W + NSA_QW + 6 * NSA_KVW + 3 * NSA_HEADS
DIL_W = DIL_HEADS * HEAD_DIM

LANES = 128
SUBLANES = 8
VMEM_LIMIT = 56 * 1024 * 1024

BF16 = jnp.bfloat16
F32 = jnp.float32


def _cparams(sem):
    return pltpu.CompilerParams(dimension_semantics=sem, vmem_limit_bytes=VMEM_LIMIT)


def _dot(a, b):
    return jnp.dot(a, b, preferred_element_type=F32)


def _dot_t(a, b):
    return lax.dot_general(a, b, (((1,), (1,)), ((), ())), preferred_element_type=F32)


def _split3(x):
    h1 = x.astype(BF16)
    r1 = x - h1.astype(F32)
    h2 = r1.astype(BF16)
    h3 = (r1 - h2.astype(F32)).astype(BF16)
    return h1, h2, h3


def _dot01(x, m01):
    h1, h2, h3 = _split3(x)
    return _dot(h1, m01) + _dot(h2, m01) + _dot(h3, m01)


EPI_PLAIN, EPI_NORM, EPI_SIGMOID = 0, 1, 2


def _norm_mm_kernel(kind_ref, x_ref, g_ref, w_ref, cg_ref, cf_ref, o16_ref, *rest, f32_tiles, line_tiles):
    h_scr = rest[-1]
    o32_ref = rest[0] if f32_tiles is not None else None
    line_refs = rest[(1 if f32_tiles is not None else 0):-1]
    j = pl.program_id(1)
    tm, tn = o16_ref.shape

    @pl.when(j == 0)
    def _():
        x = x_ref[...]
        y = x * lax.rsqrt(jnp.mean(x * x, axis=-1, keepdims=True) + RMS_EPS)
        h_scr[...] = (y * g_ref[...]).astype(BF16)

    acc = _dot(h_scr[...], w_ref[...])
    kind = kind_ref[j]

    def emit(out):
        o16_ref[...] = out.astype(BF16)
        if f32_tiles is not None:
            @pl.when((j >= f32_tiles[0]) & (j < f32_tiles[1]))
            def _():
                o32_ref[...] = out
        for ref, (j0, j1, slots) in zip(line_refs, line_tiles):
            @pl.when((j >= j0) & (j < j1))
            def _(ref=ref, j0=j0, slots=slots):
                for c in range(tn // LANES):
                    ref[pl.ds((j - j0) * (tn // LANES) + c, tm, stride=slots), :] = out[:, c * LANES:(c + 1) * LANES]

    @pl.when(kind == EPI_PLAIN)
    def _():
        emit(acc)

    @pl.when(kind == EPI_NORM)
    def _():
        parts = []
        for c in range(tn // HEAD_DIM):
            a = acc[:, c * HEAD_DIM:(c + 1) * HEAD_DIM]
            n = a * lax.rsqrt(jnp.mean(a * a, axis=-1, keepdims=True) + RMS_EPS)
            parts.append(n)
        normed = jnp.concatenate(parts, axis=1) * cg_ref[...]
        emit(jnp.where(cf_ref[...] > 0.5, normed, acc))

    @pl.when(kind == EPI_SIGMOID)
    def _():
        emit(jax.nn.sigmoid(acc))


def norm_mm(x, g, w16, col_gain, col_flag, tile_kind, *, tm, tn, f32_cols=None, lines=()):
    T, D = x.shape
    N = w16.shape[1]
    assert T % tm == 0 and N % tn == 0
    grid = (T // tm, N // tn)
    out_specs = [pl.BlockSpec((tm, tn), lambda i, j, k: (i, j))]
    out_shape = [jax.ShapeDtypeStruct((T, N), BF16)]
    f32_tiles = None
    if f32_cols is not None:
        c0, width = f32_cols
        assert c0 % tn == 0 and width % tn == 0
        j0, nj = c0 // tn, width // tn
        f32_tiles = (j0, j0 + nj)
        out_specs.append(pl.BlockSpec((tm, tn), lambda i, j, k, j0=j0, nj=nj: (i, jnp.clip(j - j0, 0, nj - 1))))
        out_shape.append(jax.ShapeDtypeStruct((T, width), F32))
    line_tiles = []
    for c0, width in lines:
        assert c0 % tn == 0 and width % tn == 0
        slots = width // LANES
        line_tiles.append((c0 // tn, (c0 + width) // tn, slots))
        out_specs.append(pl.BlockSpec((tm * slots, LANES), lambda i, j, k: (i, 0)))
        out_shape.append(jax.ShapeDtypeStruct((T * slots, LANES), F32))
    kern = functools.partial(_norm_mm_kernel, f32_tiles=f32_tiles, line_tiles=tuple(line_tiles))
    return pl.pallas_call(
        kern,
        grid_spec=pltpu.PrefetchScalarGridSpec(
            num_scalar_prefetch=1,
            grid=grid,
            in_specs=[
                pl.BlockSpec((tm, D), lambda i, j, k: (i, 0)),
                pl.BlockSpec((1, D), lambda i, j, k: (0, 0)),
                pl.BlockSpec((D, tn), lambda i, j, k: (0, j)),
                pl.BlockSpec((1, tn), lambda i, j, k: (0, j)),
                pl.BlockSpec((1, tn), lambda i, j, k: (0, j)),
            ],
            out_specs=out_specs,
            scratch_shapes=[pltpu.VMEM((tm, D), BF16)],
        ),
        out_shape=out_shape,
        compiler_params=_cparams(("parallel", "arbitrary")),
        name="norm_mm",
    )(tile_kind, x, g.reshape(1, D), w16, col_gain.reshape(1, N), col_flag.reshape(1, N))


def _mm_res_kernel(a_ref, w_ref, r_ref, o_ref):
    o_ref[...] = r_ref[...] + _dot(a_ref[...], w_ref[...])


def mm_res(a16, w16, res, *, tm, tn):
    T, K = a16.shape
    N = w16.shape[1]
    assert T % tm == 0 and N % tn == 0
    return pl.pallas_call(
        _mm_res_kernel,
        grid=(T // tm, N // tn),
        in_specs=[
            pl.BlockSpec((tm, K), lambda i, j: (i, 0)),
            pl.BlockSpec((K, tn), lambda i, j: (0, j)),
            pl.BlockSpec((tm, tn), lambda i, j: (i, j)),
        ],
        out_specs=pl.BlockSpec((tm, tn), lambda i, j: (i, j)),
        out_shape=jax.ShapeDtypeStruct((T, N), F32),
        compiler_params=_cparams(("parallel", "arbitrary")),
        name="mm_res",
    )(a16, w16, res)


LOG2_E = math.log2(math.e)


def _log2_keep(z2):
    nz = -z2
    return jnp.minimum(nz, 0.0) - jnp.log2(1.0 + jnp.exp2(jnp.minimum(z2, nz)))


def _upper01(n):
    r = lax.broadcasted_iota(jnp.int32, (2 * n, n), 0)
    c = lax.broadcasted_iota(jnp.int32, (2 * n, n), 1)
    return jnp.where(jnp.where(r >= n, r - n, r) > c, 1.0, 0.0).astype(BF16)


def _after_in_tile(lk, upper2):
    h1 = lk.astype(BF16)
    h2 = (lk - h1.astype(F32)).astype(BF16)
    return _dot(jnp.concatenate([h1, h2], axis=1), upper2)


def _sb_prompt_kernel(q_ref, k_ref, v_ref, o_ref, acc_scr, run_scr, *, bq, scale):
    bk = LANES
    i = pl.program_id(2)
    nd = bq // bk
    acc_scr[...] = jnp.zeros(acc_scr.shape, F32)
    run_scr[...] = jnp.zeros(run_scr.shape, F32)
    upper2 = _upper01(bk)

    def absorb(kb, r0, masked):
        ks = pl.multiple_of(kb * bk, bk)
        k = k_ref[0, pl.ds(ks, bk), :]
        v = v_ref[0, pl.ds(ks, bk), :]
        z2 = _dot_t(q_ref[0, r0:, :], k) * (scale * LOG2_E)
        lk = _log2_keep(z2)
        lb = z2 + lk
        if masked:
            qpos = i * bq + r0 + lax.broadcasted_iota(jnp.int32, (bq - r0, 1), 0)
            mask = ks + lax.broadcasted_iota(jnp.int32, (1, bk), 1) < qpos
            lk = jnp.where(mask, lk, 0.0)
        w = jnp.exp2(lb + _after_in_tile(lk, upper2) + run_scr[r0:, :])
        if masked:
            w = jnp.where(mask, w, 0.0)
        acc_scr[r0:, :] = acc_scr[r0:, :] + _dot(w.astype(BF16), v)
        run_scr[r0:, :] = run_scr[r0:, :] + jnp.sum(lk, axis=-1, keepdims=True)

    for d in range(nd - 1, -1, -1):
        absorb(i * nd + d, d * bk, True)

    assert nd % 2 == 0

    def body(t, carry):
        absorb(i * nd - 1 - 2 * t, 0, False)
        absorb(i * nd - 2 - 2 * t, 0, False)
        return carry

    lax.fori_loop(0, i * (nd // 2), body, 0)
    o_ref[0] = acc_scr[...].astype(o_ref.dtype)


def sb_prompt(p16, B, S, *, q_col, k_col, v_col, bq=512):
    assert S % bq == 0 and bq % LANES == 0
    kern = functools.partial(_sb_prompt_kernel, bq=bq, scale=HEAD_DIM ** -0.5)
    qb, kb_, vb = q_col // HEAD_DIM, k_col // HEAD_DIM, v_col // HEAD_DIM
    return pl.pallas_call(
        kern,
        grid=(B, SB_HEADS, S // bq),
        in_specs=[
            pl.BlockSpec((1, bq, HEAD_DIM), lambda b, h, i: (b, i, qb + h)),
            pl.BlockSpec((1, S, HEAD_DIM), lambda b, h, i: (b, 0, kb_ + h)),
            pl.BlockSpec((1, S, HEAD_DIM), lambda b, h, i: (b, 0, vb + h)),
        ],
        out_specs=pl.BlockSpec((1, bq, HEAD_DIM), lambda b, h, i: (b, i, h)),
        out_shape=jax.ShapeDtypeStruct((B, S, SB_W), BF16),
        scratch_shapes=[pltpu.VMEM((bq, HEAD_DIM), F32), pltpu.VMEM((bq, LANES), F32)],
        compiler_params=_cparams(("parallel", "parallel", "arbitrary")),
        name="sb_prompt",
    )(p16, p16, p16)


def _t5_bucket(dist):
    n = jnp.maximum(dist, 0)
    exact = N_BUCKETS // 2
    nf = jnp.maximum(n, exact).astype(F32)
    far = exact + (jnp.log(nf / exact) / math.log(BUCKET_MAX_DIST / exact) * (N_BUCKETS - exact)).astype(jnp.int32)
    return jnp.where(n < exact, n, jnp.minimum(far, N_BUCKETS - 1))


def _table_lookup(bucket, tab_ref, head):
    out = jnp.zeros(bucket.shape, F32)
    for k in range(N_BUCKETS):
        out = jnp.where(bucket == k, tab_ref[k, head], out)
    return out


N_BIAS_DIAG = -(-(BUCKET_MAX_DIST + QBLK) // QBLK) + 1


def _bias_tiles_kernel(tab_ref, o_ref):
    g = pl.program_id(0)
    d = pl.program_id(1)
    dist = d * QBLK + lax.broadcasted_iota(jnp.int32, (QBLK, QBLK), 0) - lax.broadcasted_iota(jnp.int32, (QBLK, QBLK), 1)
    bucket = _t5_bucket(dist)
    for r in range(NSA_REP):
        o_ref[0, 0, r * QBLK:(r + 1) * QBLK, :] = _table_lookup(bucket, tab_ref, g * NSA_REP + r)


def nsa_bias_tiles(table):
    return pl.pallas_call(
        _bias_tiles_kernel,
        grid=(NSA_KV_HEADS, N_BIAS_DIAG),
        in_specs=[pl.BlockSpec(memory_space=pltpu.SMEM)],
        out_specs=pl.BlockSpec((1, 1, NSA_REP * QBLK, QBLK), lambda g, d: (g, d, 0, 0)),
        out_shape=jax.ShapeDtypeStruct((NSA_KV_HEADS, N_BIAS_DIAG, NSA_REP * QBLK, QBLK), F32),
        compiler_params=_cparams(("parallel", "arbitrary")),
        name="nsa_bias_tiles",
    )(table)


CMP_HALF = CMP_BLOCK // CMP_STRIDE
CHUNK_W = CMP_STRIDE * 2 * NSA_KVW


def _compress_chunks(x, w_ref, pe_ref, gain):
    n = x.shape[0]
    outs = []
    for kv in range(2):
        for g in range(NSA_KV_HEADS):
            off = kv * NSA_KVW + g * HEAD_DIM
            xs = jnp.concatenate(
                [x[:, l * 2 * NSA_KVW + off:l * 2 * NSA_KVW + off + HEAD_DIM] for l in range(CMP_STRIDE)], axis=1)
            halves = []
            for j in range(CMP_HALF):
                w = w_ref[kv, j]
                pe = jnp.broadcast_to(pe_ref[kv, j], (SUBLANES, CMP_STRIDE * HEAD_DIM)).astype(BF16)
                halves.append(_dot(xs, w) + _dot(pe, w)[0:1, :])
            out = halves[0]
            for j in range(1, CMP_HALF):
                out = out + pltpu.roll(halves[j], n - j, axis=0)
            if kv == 0:
                out = out * lax.rsqrt(jnp.mean(out * out, axis=-1, keepdims=True) + RMS_EPS) * gain
            outs.append(out)
    kc = jnp.concatenate(outs[:NSA_KV_HEADS], axis=1)
    vc = jnp.concatenate(outs[NSA_KV_HEADS:], axis=1)
    return kc, vc


def _compress_prompt_kernel(x_ref, w_ref, pe_ref, g_ref, kc_ref, vc_ref):
    kc, vc = _compress_chunks(x_ref[0], w_ref, pe_ref, g_ref[...])
    kc_ref[0] = kc.astype(BF16)
    vc_ref[0] = vc.astype(BF16)


def _prep_cmp_weights(cmp_w, cmp_pe):
    w = cmp_w.reshape(2, CMP_HALF, CMP_STRIDE * HEAD_DIM, HEAD_DIM).astype(BF16)
    pe = cmp_pe.reshape(2, CMP_HALF, 1, CMP_STRIDE * HEAD_DIM)
    return w, pe


def compress_prompt(chunks16, w16, pe, ck_gain):
    B, n, _ = chunks16.shape
    return pl.pallas_call(
        _compress_prompt_kernel,
        grid=(B,),
        in_specs=[
            pl.BlockSpec((1, n, CHUNK_W), lambda b: (b, 0, 0)),
            pl.BlockSpec((2, CMP_HALF, CMP_STRIDE * HEAD_DIM, HEAD_DIM), lambda b: (0, 0, 0, 0)),
            pl.BlockSpec((2, CMP_HALF, 1, CMP_STRIDE * HEAD_DIM), lambda b: (0, 0, 0, 0)),
            pl.BlockSpec((1, HEAD_DIM), lambda b: (0, 0)),
        ],
        out_specs=[pl.BlockSpec((1, n, NSA_KVW), lambda b: (b, 0, 0))] * 2,
        out_shape=[jax.ShapeDtypeStruct((B, n, NSA_KVW), BF16)] * 2,
        compiler_params=_cparams(("parallel",)),
        name="nsa_compress_prompt",
    )(chunks16, w16, pe, ck_gain.reshape(1, HEAD_DIM))


def _flash_step(s, mask, v, m_scr, l_scr, acc_scr):
    n = s.shape[1] // LANES
    cols = [slice(c * LANES, (c + 1) * LANES) for c in range(n)]
    sm = [jnp.where(mask[:, c], s[:, c], NEG_INF) for c in cols]
    top = sm[0]
    for x in sm[1:]:
        top = jnp.maximum(top, x)
    m_prev = m_scr[...]
    m_new = jnp.maximum(m_prev, jnp.max(top, axis=-1, keepdims=True))
    ps = [jnp.where(mask[:, c], jnp.exp(x - m_new), 0.0) for c, x in zip(cols, sm)]
    tot = ps[0]
    for x in ps[1:]:
        tot = tot + x
    alpha = jnp.exp(m_prev - m_new)
    l_scr[...] = alpha * l_scr[...] + jnp.sum(tot, axis=-1, keepdims=True)
    acc_scr[...] = alpha * acc_scr[...] + _dot(jnp.concatenate([x.astype(BF16) for x in ps], axis=1), v)
    m_scr[...] = m_new


def _flash_reset(m_scr, l_scr, acc_scr):
    m_scr[...] = jnp.full(m_scr.shape, NEG_INF, F32)
    l_scr[...] = jnp.zeros(l_scr.shape, F32)
    acc_scr[...] = jnp.zeros(acc_scr.shape, F32)


def _select_blocks(score, q_blk, n_sel):
    Q, n_s = score.shape
    blk = lax.broadcasted_iota(jnp.int32, (1, n_s), 1)
    valid = blk <= q_blk
    forced = valid & ((blk == 0) | (blk == q_blk) | (blk == q_blk - 1))
    score = jnp.where(forced, FORCE_SCORE, jnp.where(valid, score, NEG_INF))
    rank = jnp.zeros((Q, n_s), F32)
    for c in range(n_s):
        col = score[:, c:c + 1]
        before = (col > score) | ((col == score) & (blk > c))
        rank = rank + jnp.where(before, 1.0, 0.0)
    return jnp.where(rank < n_sel, 1.0, 0.0)


def _select_blocks_t(score_t, q_blk, n_sel):
    n_s, Q = score_t.shape
    blk = lax.broadcasted_iota(jnp.int32, (n_s, 1), 0)
    valid = blk <= q_blk
    forced = valid & ((blk == 0) | (blk == q_blk) | (blk == q_blk - 1))
    score_t = jnp.where(forced, FORCE_SCORE, jnp.where(valid, score_t, NEG_INF))
    rank = jnp.zeros((n_s, Q), F32)
    for c in range(n_s):
        row = score_t[c:c + 1, :]
        before = (row > score_t) | ((row == score_t) & (blk > c))
        rank = rank + jnp.where(before, 1.0, 0.0)
    return jnp.where(rank < n_sel, 1.0, 0.0)


def _nsa_cmp_bias_kernel(tab_ref, o_ref):
    g = pl.program_id(0)
    i = pl.program_id(1)
    ncp = o_ref.shape[3]
    qpos1 = i * QBLK + lax.broadcasted_iota(jnp.int32, (QBLK, 1), 0)
    c_end = lax.broadcasted_iota(jnp.int32, (1, ncp), 1) * CMP_STRIDE + (CMP_BLOCK - 1)
    bucket = _t5_bucket(qpos1 - c_end)
    for r in range(NSA_REP):
        o_ref[0, 0, r * QBLK:(r + 1) * QBLK, :] = _table_lookup(bucket, tab_ref, g * NSA_REP + r)


def nsa_cmp_bias(table, S, ncp):
    return pl.pallas_call(
        _nsa_cmp_bias_kernel,
        grid=(NSA_KV_HEADS, S // QBLK),
        in_specs=[pl.BlockSpec(memory_space=pltpu.SMEM)],
        out_specs=pl.BlockSpec((1, 1, NSA_REP * QBLK, ncp), lambda g, i: (g, i, 0, 0)),
        out_shape=jax.ShapeDtypeStruct((NSA_KV_HEADS, S // QBLK, NSA_REP * QBLK, ncp), F32),
        compiler_params=_cparams(("parallel", "arbitrary")),
        name="nsa_cmp_bias",
    )(table)


NSA_KT = 2 * QBLK


def _nsa_prompt_kernel(q_ref, kc_ref, vc_ref, ks_ref, vs_ref, kw_ref, vw_ref, gt_ref, tb_ref, cb_ref,
                       o_ref, m_scr, l_scr, acc_scr, sel_scr, *, n_c, scale):
    g = pl.program_id(1)
    i = pl.program_id(2)
    R = NSA_REP
    qs = i * QBLK
    q4 = q_ref[0]
    q = jnp.concatenate([q4[:, r * HEAD_DIM:(r + 1) * HEAD_DIM] for r in range(R)], axis=0)
    qpos1 = qs + lax.broadcasted_iota(jnp.int32, (QBLK, 1), 0)
    qpos = jnp.concatenate([qpos1] * R, axis=0)

    ncp = kc_ref.shape[1]
    s = _dot_t(q, kc_ref[0]) * scale
    n_idx = lax.broadcasted_iota(jnp.int32, (1, ncp), 1)
    mask = (qpos - (n_idx * CMP_STRIDE + (CMP_BLOCK - 1)) >= 0) & (n_idx < n_c)
    s = jnp.where(mask, s + cb_ref[0, 0], NEG_INF)
    m = jnp.max(s, axis=-1, keepdims=True)
    p = jnp.where(mask, jnp.exp(s - m), 0.0)
    l = jnp.sum(p, axis=-1, keepdims=True)
    p = p / jnp.maximum(l, 1.0)
    o_c = _dot(p.astype(BF16), vc_ref[0])
    imp = p[0:QBLK]
    for r in range(1, R):
        imp = imp + p[r * QBLK:(r + 1) * QBLK]
    n_s = ks_ref.shape[1] // SLC_BLOCK
    ratio = SLC_BLOCK // CMP_STRIDE
    ti = lax.broadcasted_iota(jnp.int32, (n_s, ncp), 0)
    ci = lax.broadcasted_iota(jnp.int32, (n_s, ncp), 1)
    gather01 = jnp.where((ci >= ti * ratio - (CMP_HALF - 1)) & (ci < (ti + 1) * ratio), 1.0, 0.0).astype(BF16)
    i1, i2, i3 = _split3(imp)
    score_t = _dot_t(gather01, i1) + _dot_t(gather01, i2) + _dot_t(gather01, i3)
    q_blk = (qs + lax.broadcasted_iota(jnp.int32, (1, QBLK), 1)) // SLC_BLOCK
    sel_scr[...] = _select_blocks_t(score_t, q_blk, min(N_SELECT, n_s)).T.astype(BF16)

    kcol = lax.broadcasted_iota(jnp.int32, (1, NSA_KT), 1)

    def tile_bias(kt):
        parts = [tb_ref[0, jnp.clip(i - (kt * (NSA_KT // QBLK) + t), 0, N_BIAS_DIAG - 1)] for t in range(NSA_KT // QBLK)]
        return jnp.concatenate(parts, axis=1)

    _flash_reset(m_scr, l_scr, acc_scr)

    def slc_body(kt, carry):
        ks_ = pl.multiple_of(kt * NSA_KT, NSA_KT)
        k = ks_ref[0, pl.ds(ks_, NSA_KT), :]
        v = vs_ref[0, pl.ds(ks_, NSA_KT), :]
        dist = qpos - (ks_ + kcol)
        bi = lax.broadcasted_iota(jnp.int32, (n_s, NSA_KT), 0)
        ki = lax.broadcasted_iota(jnp.int32, (n_s, NSA_KT), 1)
        expand = jnp.where(bi == kt * (NSA_KT // SLC_BLOCK) + ki // SLC_BLOCK, 1.0, 0.0).astype(BF16)
        selk = _dot(sel_scr[...], expand)
        selk = jnp.concatenate([selk] * R, axis=0)
        msk = (dist >= 0) & (selk > 0.5)
        sc = _dot_t(q, k) * scale + tile_bias(kt)
        _flash_step(sc, msk, v, m_scr, l_scr, acc_scr)
        return carry

    lax.fori_loop(0, (qs + QBLK - 1) // NSA_KT + 1, slc_body, 0)
    o_s = acc_scr[...] / jnp.maximum(l_scr[...], 1.0)

    _flash_reset(m_scr, l_scr, acc_scr)

    def win_body(kt, carry):
        ks_ = pl.multiple_of(kt * NSA_KT, NSA_KT)
        k = kw_ref[0, pl.ds(ks_, NSA_KT), :]
        v = vw_ref[0, pl.ds(ks_, NSA_KT), :]
        dist = qpos - (ks_ + kcol)
        msk = (dist >= 0) & (dist < NSA_WINDOW)
        sc = _dot_t(q, k) * scale + tile_bias(kt)
        _flash_step(sc, msk, v, m_scr, l_scr, acc_scr)
        return carry

    lax.fori_loop(jnp.maximum(qs - (NSA_WINDOW - 1), 0) // NSA_KT, (qs + QBLK - 1) // NSA_KT + 1, win_body, 0)
    o_w = acc_scr[...] / jnp.maximum(l_scr[...], 1.0)

    gt = gt_ref[0]
    outs = []
    for r in range(R):
        c0 = (g * R + r) * 3
        rows = slice(r * QBLK, (r + 1) * QBLK)
        lane = lax.broadcasted_iota(jnp.int32, (1, LANES), 1)
        gates = [jnp.sum(jnp.where(lane == c0 + t, gt, 0.0), axis=-1, keepdims=True) for t in range(3)]
        outs.append(gates[0] * o_c[rows] + gates[1] * o_s[rows] + gates[2] * o_w[rows])
    o_ref[0] = jnp.concatenate(outs, axis=1).astype(o_ref.dtype)


def nsa_prompt(p16, p32, kc, vc, tb, cb, B, S, *, q_col, sk_col, sv_col, wk_col, wv_col, gate_col):
    G, R = NSA_KV_HEADS, NSA_REP
    assert S % NSA_KT == 0
    n_c = S // CMP_STRIDE - CMP_HALF + 1
    kern = functools.partial(_nsa_prompt_kernel, n_c=n_c, scale=HEAD_DIM ** -0.5)
    hd = HEAD_DIM
    return pl.pallas_call(
        kern,
        grid_spec=pltpu.PrefetchScalarGridSpec(
            num_scalar_prefetch=0,
            grid=(B, G, S // QBLK),
            in_specs=[
                pl.BlockSpec((1, QBLK, R * hd), lambda b, g, i: (b, i, q_col // (R * hd) + g)),
                pl.BlockSpec((1, kc.shape[1], hd), lambda b, g, i: (b, 0, g)),
                pl.BlockSpec((1, vc.shape[1], hd), lambda b, g, i: (b, 0, g)),
                pl.BlockSpec((1, S, hd), lambda b, g, i: (b, 0, sk_col // hd + g)),
                pl.BlockSpec((1, S, hd), lambda b, g, i: (b, 0, sv_col // hd + g)),
                pl.BlockSpec((1, S, hd), lambda b, g, i: (b, 0, wk_col // hd + g)),
                pl.BlockSpec((1, S, hd), lambda b, g, i: (b, 0, wv_col // hd + g)),
                pl.BlockSpec((1, QBLK, LANES), lambda b, g, i: (b, i, gate_col // LANES)),
                pl.BlockSpec((1, N_BIAS_DIAG, R * QBLK, QBLK), lambda b, g, i: (g, 0, 0, 0)),
                pl.BlockSpec((1, 1, R * QBLK, cb.shape[3]), lambda b, g, i: (g, i, 0, 0)),
            ],
            out_specs=pl.BlockSpec((1, QBLK, R * hd), lambda b, g, i: (b, i, g)),
            scratch_shapes=[
                pltpu.VMEM((R * QBLK, LANES), F32),
                pltpu.VMEM((R * QBLK, LANES), F32),
                pltpu.VMEM((R * QBLK, hd), F32),
                pltpu.VMEM((QBLK, S // SLC_BLOCK), BF16),
            ],
        ),
        out_shape=jax.ShapeDtypeStruct((B, S, NSA_QW), BF16),
        compiler_params=_cparams(("parallel", "parallel", "arbitrary")),
        name="nsa_prompt",
    )(p16, kc, vc, p16, p16, p16, p16, p32, tb, cb)


ROUTER_W = LANES
MOE_TM = 256


def _router_kernel(x_ref, g_ref, w1_ref, w2_ref, b_ref, h_ref, info_ref):
    x = x_ref[...]
    hn = x * lax.rsqrt(jnp.mean(x * x, axis=-1, keepdims=True) + RMS_EPS) * g_ref[...]
    h_ref[...] = hn
    a1 = hn.astype(BF16)
    a2 = (hn - a1.astype(F32)).astype(BF16)
    logits = _dot(a1, w1_ref[...]) + _dot(a1, w2_ref[...]) + _dot(a2, w1_ref[...]) + b_ref[...]
    lane = lax.broadcasted_iota(jnp.int32, (1, ROUTER_W), 1)
    big = ROUTER_W

    def first_lane(cond):
        return jnp.min(jnp.where(cond, lane, big), axis=-1, keepdims=True)

    gmask = lane < N_GROUPS
    gl = jnp.where(gmask, logits, NEG_INF)
    gm = jnp.max(gl, axis=-1, keepdims=True)
    g_idx = first_lane(gmask & (gl == gm))
    g_w = 1.0 / jnp.sum(jnp.where(gmask, jnp.exp(gl - gm), 0.0), axis=-1, keepdims=True)
    lo = N_GROUPS + g_idx * EXPERTS_PER_GROUP
    emask = (lane >= lo) & (lane < lo + EXPERTS_PER_GROUP)
    el = jnp.where(emask, logits, NEG_INF)
    em = jnp.max(el, axis=-1, keepdims=True)
    ee = jnp.where(emask, jnp.exp(el - em), 0.0)
    pe = jnp.where(emask, ee / jnp.sum(ee, axis=-1, keepdims=True), -1.0)
    v1 = jnp.max(pe, axis=-1, keepdims=True)
    i1 = first_lane(pe == v1)
    pe2 = jnp.where(lane == i1, -1.0, pe)
    v2 = jnp.max(pe2, axis=-1, keepdims=True)
    i2 = first_lane((pe2 == v2) & emask & (lane != i1))
    tot = v1 + v2
    info = jnp.where(lane == 0, (i1 - N_GROUPS).astype(F32), 0.0)
    info = jnp.where(lane == 1, (i2 - N_GROUPS).astype(F32), info)
    info = jnp.where(lane == 2, g_w * (v1 / tot), info)
    info = jnp.where(lane == 3, g_w * (v2 / tot), info)
    info_ref[...] = info


def moe_router(x, g, w_group, b_group, w_expert, b_expert, *, tm):
    T, D = x.shape
    pad = ROUTER_W - N_GROUPS - N_EXPERTS
    w = jnp.pad(jnp.concatenate([w_group, w_expert], axis=1), ((0, 0), (0, pad)))
    b = jnp.pad(jnp.concatenate([b_group, b_expert]), (0, pad)).reshape(1, ROUTER_W)
    w1 = w.astype(BF16)
    w2 = (w - w1.astype(F32)).astype(BF16)
    return pl.pallas_call(
        _router_kernel,
        grid=(T // tm,),
        in_specs=[
            pl.BlockSpec((tm, D), lambda i: (i, 0)),
            pl.BlockSpec((1, D), lambda i: (0, 0)),
            pl.BlockSpec((D, ROUTER_W), lambda i: (0, 0)),
            pl.BlockSpec((D, ROUTER_W), lambda i: (0, 0)),
            pl.BlockSpec((1, ROUTER_W), lambda i: (0, 0)),
        ],
        out_specs=[pl.BlockSpec((tm, D), lambda i: (i, 0)), pl.BlockSpec((tm, ROUTER_W), lambda i: (i, 0))],
        out_shape=[jax.ShapeDtypeStruct((T, D), F32), jax.ShapeDtypeStruct((T, ROUTER_W), F32)],
        compiler_params=_cparams(("parallel",)),
        name="moe_router",
    )(x, g.reshape(1, D), w1, w2, b)


MOE_FSPLIT = 2


def _moe_ffn_kernel(te_ref, nu_ref, x_ref, wg_ref, wu_ref, wd_ref, *rest, accumulate):
    if accumulate:
        yp_ref, y_ref, wg16, wu16, wd16 = rest
    else:
        (y_ref, wg16, wu16, wd16), yp_ref = rest, None
    i = pl.program_id(0)

    @pl.when((i == 0) | (te_ref[i] != te_ref[jnp.maximum(i - 1, 0)]))
    def _():
        wg16[...] = wg_ref[0].astype(BF16)
        wu16[...] = wu_ref[0].astype(BF16)
        wd16[...] = wd_ref[0].astype(BF16)

    @pl.when(i < nu_ref[0])
    def _():
        x = x_ref[...].astype(BF16)
        h = _dot(x, wg16[...])
        u = _dot(x, wu16[...])
        y = _dot(((h * jax.nn.sigmoid(h)) * u).astype(BF16), wd16[...])
        y_ref[...] = y + yp_ref[...] if accumulate else y

    @pl.when(i >= nu_ref[0])
    def _():
        y_ref[...] = jnp.zeros(y_ref.shape, F32)


def moe_ffn(xs, tile_expert, n_used, w_gate, w_up, w_down):
    M, D = xs.shape
    F = w_gate.shape[2]
    assert F % MOE_FSPLIT == 0
    Fh = F // MOE_FSPLIT
    n_tiles = M // MOE_TM
    y = None
    for f in range(MOE_FSPLIT):
        acc = f > 0
        in_specs = [
            pl.BlockSpec((MOE_TM, D), lambda i, te, nu: (i, 0)),
            pl.BlockSpec((1, D, Fh), lambda i, te, nu, f=f: (te[i], 0, f)),
            pl.BlockSpec((1, D, Fh), lambda i, te, nu, f=f: (te[i], 0, f)),
            pl.BlockSpec((1, Fh, D), lambda i, te, nu, f=f: (te[i], f, 0)),
        ]
        args = [xs, w_gate, w_up, w_down]
        if acc:
            in_specs.append(pl.BlockSpec((MOE_TM, D), lambda i, te, nu: (i, 0)))
            args.append(y)
        y = pl.pallas_call(
            functools.partial(_moe_ffn_kernel, accumulate=acc),
            grid_spec=pltpu.PrefetchScalarGridSpec(
                num_scalar_prefetch=2,
                grid=(n_tiles,),
                in_specs=in_specs,
                out_specs=pl.BlockSpec((MOE_TM, D), lambda i, te, nu: (i, 0)),
                scratch_shapes=[pltpu.VMEM((D, Fh), BF16), pltpu.VMEM((D, Fh), BF16), pltpu.VMEM((Fh, D), BF16)],
            ),
            out_shape=jax.ShapeDtypeStruct((M, D), F32),
            input_output_aliases={6: 0} if acc else {},
            compiler_params=_cparams(("arbitrary",)),
            name=f"moe_ffn_{f}",
        )(tile_expert, n_used, *args)
    return y


def moe_plan(info):
    T = info.shape[0]
    eid = info[:, :TOP_K_INNER].astype(jnp.int32).reshape(-1)
    M = T * TOP_K_INNER
    n_tiles = -(-(M + N_EXPERTS * (MOE_TM - 1)) // MOE_TM)
    onehot = (eid[:, None] == jnp.arange(N_EXPERTS)[None, :]).astype(jnp.int32)
    rank = jnp.cumsum(onehot, axis=0) - onehot
    counts = jnp.sum(onehot, axis=0)
    tiles_per = (counts + MOE_TM - 1) // MOE_TM
    tile_end = jnp.cumsum(tiles_per)
    offs = (tile_end - tiles_per) * MOE_TM
    dest = jnp.sum(onehot * (offs[None, :] + rank), axis=1).astype(jnp.int32)
    n_used = tile_end[-1]
    tile_ids = jnp.arange(n_tiles)
    tile_expert = jnp.minimum(jnp.sum(tile_ids[:, None] >= tile_end[None, :], axis=1), N_EXPERTS - 1)
    last_e = jnp.max(jnp.where(counts > 0, jnp.arange(N_EXPERTS), 0))
    tile_expert = jnp.where(tile_ids < n_used, tile_expert, last_e).astype(jnp.int32)
    return dest.reshape(T, TOP_K_INNER), tile_expert, n_used.reshape(1).astype(jnp.int32), n_tiles * MOE_TM


def _row_copy(src_ref, s, dst_ref, d, sem):
    return pltpu.make_async_copy(src_ref.at[pl.ds(s, 1), :], dst_ref.at[pl.ds(d, 1), :], sem)


def _moe_scatter_kernel(dest_ref, h_ref, xs_in_ref, xs_ref, sem):
    del xs_in_ref
    tm = h_ref.shape[0]

    def issue(r, c):
        for slot in range(TOP_K_INNER):
            _row_copy(h_ref, r, xs_ref, dest_ref[0, 0, TOP_K_INNER * r + slot], sem).start()
        return c

    lax.fori_loop(0, tm, issue, 0)

    def drain(r, c):
        for slot in range(TOP_K_INNER):
            _row_copy(h_ref, 0, xs_ref, 0, sem).wait()
        return c

    lax.fori_loop(0, tm, drain, 0)


def moe_scatter(h, dest, xs, *, tm):
    T, D = h.shape
    assert T % tm == 0
    return pl.pallas_call(
        _moe_scatter_kernel,
        grid=(T // tm,),
        in_specs=[
            pl.BlockSpec((1, 1, TOP_K_INNER * tm), lambda i: (i, 0, 0), memory_space=pltpu.SMEM),
            pl.BlockSpec((tm, D), lambda i: (i, 0)),
            pl.BlockSpec(memory_space=pl.ANY),
        ],
        out_specs=pl.BlockSpec(memory_space=pl.ANY),
        out_shape=jax.ShapeDtypeStruct(xs.shape, xs.dtype),
        scratch_shapes=[pltpu.SemaphoreType.DMA(())],
        input_output_aliases={2: 0},
        compiler_params=_cparams(("arbitrary",)),
        name="moe_scatter",
    )(dest.reshape(T // tm, 1, TOP_K_INNER * tm), h, xs)


def _moe_combine_kernel(dest_ref, x_ref, info_ref, y_ref, o_ref, buf, sem):
    tm = x_ref.shape[0]

    def issue(r, c):
        for slot in range(TOP_K_INNER):
            _row_copy(y_ref, dest_ref[0, 0, TOP_K_INNER * r + slot], buf.at[slot], r, sem).start()
        return c

    lax.fori_loop(0, tm, issue, 0)

    def drain(r, c):
        for slot in range(TOP_K_INNER):
            _row_copy(y_ref, 0, buf.at[slot], 0, sem).wait()
        return c

    lax.fori_loop(0, tm, drain, 0)
    out = x_ref[...]
    for slot in range(TOP_K_INNER):
        out = out + info_ref[:, TOP_K_INNER + slot:TOP_K_INNER + slot + 1] * buf[slot]
    o_ref[...] = out


def moe_combine(x, info, dest, y, *, tm):
    T, D = x.shape
    assert T % tm == 0
    return pl.pallas_call(
        _moe_combine_kernel,
        grid=(T // tm,),
        in_specs=[
            pl.BlockSpec((1, 1, TOP_K_INNER * tm), lambda i: (i, 0, 0), memory_space=pltpu.SMEM),
            pl.BlockSpec((tm, D), lambda i: (i, 0)),
            pl.BlockSpec((tm, ROUTER_W), lambda i: (i, 0)),
            pl.BlockSpec(memory_space=pl.ANY),
        ],
        out_specs=pl.BlockSpec((tm, D), lambda i: (i, 0)),
        out_shape=jax.ShapeDtypeStruct((T, D), F32),
        scratch_shapes=[pltpu.VMEM((TOP_K_INNER, tm, D), F32), pltpu.SemaphoreType.DMA(())],
        compiler_params=_cparams(("arbitrary",)),
        name="moe_combine",
    )(dest.reshape(T // tm, 1, TOP_K_INNER * tm), x, info, y)


MOE_ROW_TILE = 256


def moe_layer(xp, xs, g, w_group, b_group, w_expert, b_expert, w_gate, w_up, w_down):
    Tp, Ts = xp.shape[0], xs.shape[0]
    hp, ip = moe_router(xp, g, w_group, b_group, w_expert, b_expert, tm=512)
    hs, is_ = moe_router(xs, g, w_group, b_group, w_expert, b_expert, tm=Ts)
    dest, tile_expert, n_used, m_rows = moe_plan(jnp.concatenate([ip, is_], axis=0))
    rows = jnp.zeros((m_rows, xp.shape[1]), F32)
    rows = moe_scatter(hp, dest[:Tp], rows, tm=MOE_ROW_TILE)
    rows = moe_scatter(hs, dest[Tp:], rows, tm=Ts)
    y = moe_ffn(rows, tile_expert, n_used, w_gate, w_up, w_down)
    return (moe_combine(xp, ip, dest[:Tp], y, tm=MOE_ROW_TILE), moe_combine(xs, is_, dest[Tp:], y, tm=Ts))


N_DIL = len(DIL_PATTERNS)
DIL_SPANS = tuple(w // d for w, d in DIL_PATTERNS)
DIL_STEPS = tuple(d for _, d in DIL_PATTERNS)
assert all(s <= QBLK for s in DIL_SPANS)
assert all(d & (d - 1) == 0 for d in DIL_STEPS)


def _dil_bias_kernel(dil_ref, tab_ref, o_ref):
    p = pl.program_id(0)
    dm = QBLK + lax.broadcasted_iota(jnp.int32, (QBLK, 2 * QBLK), 0) - lax.broadcasted_iota(jnp.int32, (QBLK, 2 * QBLK), 1)
    bucket = _t5_bucket(dm * dil_ref[p])
    for h in range(DIL_HEADS):
        o_ref[0, h] = _table_lookup(bucket, tab_ref, h)


def dil_bias_tiles(table):
    return pl.pallas_call(
        _dil_bias_kernel,
        grid=(N_DIL,),
        in_specs=[pl.BlockSpec(memory_space=pltpu.SMEM), pl.BlockSpec(memory_space=pltpu.SMEM)],
        out_specs=pl.BlockSpec((1, DIL_HEADS, QBLK, 2 * QBLK), lambda p: (p, 0, 0, 0)),
        out_shape=jax.ShapeDtypeStruct((N_DIL, DIL_HEADS, QBLK, 2 * QBLK), F32),
        compiler_params=_cparams(("arbitrary",)),
        name="dil_bias_tiles",
    )(jnp.array(DIL_STEPS, jnp.int32), table)


def _dil_prompt_kernel(q_ref, kp_ref, kc_ref, vp_ref, vc_ref, b_ref, num_ref, ml_ref, *, span, scale):
    mb = pl.program_id(2)
    qi = lax.broadcasted_iota(jnp.int32, (QBLK, 2 * QBLK), 0)
    kj = lax.broadcasted_iota(jnp.int32, (QBLK, 2 * QBLK), 1)
    dm = QBLK + qi - kj
    mask = (dm >= 0) & (dm <= span) & ((kj >= QBLK) | (mb > 0))
    lane = lax.broadcasted_iota(jnp.int32, (1, LANES), 1)
    m_t = jnp.zeros((QBLK, LANES), F32)
    l_t = jnp.zeros((QBLK, LANES), F32)
    for h in range(DIL_HEADS):
        cs = slice(h * HEAD_DIM, (h + 1) * HEAD_DIM)
        k = jnp.concatenate([kp_ref[0, :, cs], kc_ref[0, :, cs]], axis=0)
        v = jnp.concatenate([vp_ref[0, :, cs], vc_ref[0, :, cs]], axis=0)
        s = _dot_t(q_ref[0, :, cs], k) * scale + b_ref[0, h]
        s = jnp.where(mask, s, NEG_INF)
        m = jnp.max(s, axis=-1, keepdims=True)
        p = jnp.where(mask, jnp.exp(s - m), 0.0)
        l = jnp.sum(p, axis=-1, keepdims=True)
        num_ref[0, :, cs] = _dot(p.astype(BF16), v)
        m_t = jnp.where(lane == h, m, m_t)
        l_t = jnp.where(lane == h, l, l_t)
    ml_ref[0, :, 0:LANES] = m_t
    ml_ref[0, :, LANES:2 * LANES] = l_t


def dil_prompt_pattern(p16, bias, pat, B, S):
    span, dil = DIL_SPANS[pat], DIL_STEPS[pat]
    M = S // dil
    assert M % QBLK == 0
    pv = p16.reshape(B, M, dil * 3 * DIL_W)
    kern = functools.partial(_dil_prompt_kernel, span=span, scale=HEAD_DIM ** -0.5)
    blk = (1, QBLK, DIL_W)
    prev = lambda m: jnp.maximum(m - 1, 0)
    num, ml = pl.pallas_call(
        kern,
        grid=(B, dil, M // QBLK),
        in_specs=[
            pl.BlockSpec(blk, lambda b, r, m: (b, m, 3 * r)),
            pl.BlockSpec(blk, lambda b, r, m: (b, prev(m), 3 * r + 1)),
            pl.BlockSpec(blk, lambda b, r, m: (b, m, 3 * r + 1)),
            pl.BlockSpec(blk, lambda b, r, m: (b, prev(m), 3 * r + 2)),
            pl.BlockSpec(blk, lambda b, r, m: (b, m, 3 * r + 2)),
            pl.BlockSpec((1, DIL_HEADS, QBLK, 2 * QBLK), lambda b, r, m: (pat, 0, 0, 0)),
        ],
        out_specs=[
            pl.BlockSpec(blk, lambda b, r, m: (b, m, r)),
            pl.BlockSpec((1, QBLK, 2 * LANES), lambda b, r, m: (b, m, r)),
        ],
        out_shape=[jax.ShapeDtypeStruct((B, M, dil * DIL_W), F32), jax.ShapeDtypeStruct((B, M, dil * 2 * LANES), F32)],
        compiler_params=_cparams(("parallel", "parallel", "arbitrary")),
        name=f"dil_prompt_{dil}",
    )(pv, pv, pv, pv, pv, bias)
    return num.reshape(B * S, DIL_W), ml.reshape(B * S, 2 * LANES)


def _dil_merge_kernel(*refs):
    nums, mls, o_ref = refs[:N_DIL], refs[N_DIL:2 * N_DIL], refs[2 * N_DIL]
    m_all = [r[:, 0:LANES] for r in mls]
    l_all = [r[:, LANES:2 * LANES] for r in mls]
    m_max = m_all[0]
    for m in m_all[1:]:
        m_max = jnp.maximum(m_max, m)
    w_all = [jnp.exp(m - m_max) for m in m_all]
    den = w_all[0] * l_all[0]
    for w, l in zip(w_all[1:], l_all[1:]):
        den = den + w * l
    for h in range(DIL_HEADS):
        cs = slice(h * HEAD_DIM, (h + 1) * HEAD_DIM)
        acc = w_all[0][:, h:h + 1] * nums[0][:, cs]
        for w, n in zip(w_all[1:], nums[1:]):
            acc = acc + w[:, h:h + 1] * n[:, cs]
        o_ref[:, cs] = (acc / den[:, h:h + 1]).astype(o_ref.dtype)


def dil_merge(nums, mls, *, tm):
    T = nums[0].shape[0]
    return pl.pallas_call(
        _dil_merge_kernel,
        grid=(T // tm,),
        in_specs=[pl.BlockSpec((tm, DIL_W), lambda i: (i, 0))] * N_DIL + [pl.BlockSpec((tm, 2 * LANES), lambda i: (i, 0))] * N_DIL,
        out_specs=pl.BlockSpec((tm, DIL_W), lambda i: (i, 0)),
        out_shape=jax.ShapeDtypeStruct((T, DIL_W), BF16),
        compiler_params=_cparams(("parallel",)),
        name="dil_merge",
    )(*nums, *mls)


def _block_diag_rows(q, n_heads):
    DB, Q, W = q.shape
    dh = W // n_heads
    eye = jnp.repeat(jnp.eye(n_heads, dtype=q.dtype), dh, axis=1)
    return (q[:, None, :, :] * eye[None, :, None, :]).reshape(DB, n_heads * Q, W)


def _own_block(acc, n_heads, rows_per_head):
    rows = acc.shape[0]
    rh = lax.broadcasted_iota(jnp.int32, (rows, 1), 0) // rows_per_head
    out = jnp.zeros((rows, HEAD_DIM), F32)
    for h in range(n_heads):
        out = jnp.where(rh == h, acc[:, h * HEAD_DIM:(h + 1) * HEAD_DIM], out)
    return out


def _row_bias(bucket, tab_ref, head0, n_heads, rows_per_head):
    rows = bucket.shape[0]
    rh = lax.broadcasted_iota(jnp.int32, (rows, 1), 0) // rows_per_head
    out = jnp.zeros(bucket.shape, F32)
    for h in range(n_heads):
        out = jnp.where(rh == h, _table_lookup(bucket, tab_ref, head0 + h), out)
    return out


def _flash_step_cols(s, mask, v, m_scr, l_scr, acc_scr):
    s = jnp.where(mask, s, NEG_INF)
    m_prev = m_scr[:, 0:1]
    m_new = jnp.maximum(m_prev, jnp.max(s, axis=-1, keepdims=True))
    p = jnp.where(mask, jnp.exp(s - m_new), 0.0)
    alpha = jnp.exp(m_prev - m_new)
    l_new = alpha * l_scr[:, 0:1] + jnp.sum(p, axis=-1, keepdims=True)
    acc_scr[...] = alpha * acc_scr[...] + _dot(p.astype(BF16), v)
    m_scr[...] = jnp.broadcast_to(m_new, m_scr.shape)
    l_scr[...] = jnp.broadcast_to(l_new, l_scr.shape)


def _head_cols(ref, first, n_heads, n_rows, stride, row0=0):
    return jnp.concatenate(
        [ref[pl.ds(row0 * stride + first + h, n_rows, stride=stride), :] for h in range(n_heads)], axis=1)


SB_DEC_PAGES = 4


def _sb_decode_kernel(pt_ref, q_ref, new_ref, *rest, n_steps, n_q, scale):
    pages, (o_ref, acc_scr, run_scr) = rest[:SB_DEC_PAGES], rest[SB_DEC_PAGES:]
    s_ = pl.program_id(1)
    q = q_ref[0]
    rows = q.shape[0]
    upper2 = _upper01(PAGE_SIZE)

    def absorb(k, v, mask):
        z2 = _dot_t(q, k.astype(BF16)) * (scale * LOG2_E)
        lk = _log2_keep(z2)
        lb = z2 + lk
        if mask is not None:
            lk = jnp.where(mask, lk, 0.0)
        w = jnp.exp2(lb + _after_in_tile(lk, upper2) + run_scr[...])
        if mask is not None:
            w = jnp.where(mask, w, 0.0)
        acc_scr[...] = acc_scr[...] + _dot(w.astype(BF16), v.astype(BF16))
        run_scr[...] = run_scr[...] + jnp.sum(lk, axis=-1, keepdims=True)

    @pl.when(s_ == 0)
    def _():
        acc_scr[...] = jnp.zeros(acc_scr.shape, F32)
        run_scr[...] = jnp.zeros(run_scr.shape, F32)
        t = lax.broadcasted_iota(jnp.int32, (rows, PAGE_SIZE), 1)
        qi = lax.broadcasted_iota(jnp.int32, (rows, PAGE_SIZE), 0) % n_q
        absorb(new_ref[0, :, :SB_W], new_ref[0, :, SB_W:], (t < qi) & (t < n_q))

    for page in pages:
        absorb(_head_cols(page, 0, SB_HEADS, PAGE_SIZE, 2 * SB_HEADS),
               _head_cols(page, SB_HEADS, SB_HEADS, PAGE_SIZE, 2 * SB_HEADS), None)

    @pl.when(s_ == n_steps - 1)
    def _():
        o_ref[0] = _own_block(acc_scr[...], SB_HEADS, n_q)


def sb_decode(page_table, qbd16, new_kv, cache_lines):
    DB, n_pages = page_table.shape
    assert n_pages % SB_DEC_PAGES == 0
    n_steps = n_pages // SB_DEC_PAGES
    rows = qbd16.shape[1]
    n_q = rows // SB_HEADS
    lines = PAGE_SIZE * 2 * SB_HEADS
    kern = functools.partial(_sb_decode_kernel, n_steps=n_steps, n_q=n_q, scale=HEAD_DIM ** -0.5)
    page_spec = lambda j: pl.BlockSpec(
        (lines, HEAD_DIM), lambda b, s, pt: (pt[b, n_pages - 1 - (s * SB_DEC_PAGES + j)], 0))
    return pl.pallas_call(
        kern,
        grid_spec=pltpu.PrefetchScalarGridSpec(
            num_scalar_prefetch=1,
            grid=(DB, n_steps),
            in_specs=[
                pl.BlockSpec((1, rows, SB_W), lambda b, s, pt: (b, 0, 0)),
                pl.BlockSpec((1, PAGE_SIZE, 2 * SB_W), lambda b, s, pt: (b, 0, 0)),
            ] + [page_spec(j) for j in range(SB_DEC_PAGES)],
            out_specs=pl.BlockSpec((1, rows, HEAD_DIM), lambda b, s, pt: (b, 0, 0)),
            scratch_shapes=[pltpu.VMEM((rows, SB_W), F32), pltpu.VMEM((rows, LANES), F32)],
        ),
        out_shape=jax.ShapeDtypeStruct((DB, rows, HEAD_DIM), F32),
        compiler_params=_cparams(("parallel", "arbitrary")),
        name="sb_decode",
    )(page_table, qbd16, new_kv, *([cache_lines] * SB_DEC_PAGES))


CMP_ROWS_TILE = 256
CHUNKS_PER_PAGE = PAGE_SIZE // CMP_STRIDE


CMP_DEC_PAGES = 8


def _nsa_cmp_decode_kernel(pt_ref, tab_ref, q_ref, w_ref, pe_ref, g_ref, *rest, n_steps, n_q, past, scale):
    pages, (oc_ref, sel_ref, x_scr, a_scr) = rest[:CMP_DEC_PAGES], rest[CMP_DEC_PAGES:]
    s_ = pl.program_id(1)
    G, R = NSA_KV_HEADS, NSA_REP
    slots = 2 * G
    n_stage = CMP_DEC_PAGES * CHUNKS_PER_PAGE
    for kvg in range(slots):
        xs = jnp.concatenate(
            [jnp.concatenate([pg[pl.ds(l * slots + kvg, CHUNKS_PER_PAGE, stride=CMP_STRIDE * slots), :]
                              for pg in pages], axis=0) for l in range(CMP_STRIDE)], axis=1)
        x_scr[kvg, pl.ds(pl.multiple_of(s_ * n_stage, n_stage), n_stage), :] = xs.astype(BF16)

    @pl.when(s_ == n_steps - 1)
    def _():
        n_chunk = x_scr.shape[1]
        n_c = n_chunk - CMP_HALF + 1
        for rt in range(n_chunk // CMP_ROWS_TILE):
            rs = slice(rt * CMP_ROWS_TILE, (rt + 1) * CMP_ROWS_TILE)
            for kv in range(2):
                for g in range(G):
                    off = kv * NSA_KVW + g * HEAD_DIM
                    xs = x_scr[kv * G + g, rs, :]
                    for j in range(CMP_HALF):
                        w = w_ref[kv, j]
                        pe = jnp.broadcast_to(pe_ref[kv, j], (SUBLANES, CMP_STRIDE * HEAD_DIM)).astype(BF16)
                        a_scr[j, rs, off:off + HEAD_DIM] = _dot(xs, w) + _dot(pe, w)[0:1, :]
        out = a_scr[0]
        for j in range(1, CMP_HALF):
            out = out + pltpu.roll(a_scr[j], n_chunk - j, axis=0)
        kparts = []
        for g in range(G):
            kg = out[:, g * HEAD_DIM:(g + 1) * HEAD_DIM]
            kparts.append(kg * lax.rsqrt(jnp.mean(kg * kg, axis=-1, keepdims=True) + RMS_EPS) * g_ref[...])
        kc = jnp.concatenate(kparts, axis=1).astype(BF16)
        vc = out[:, NSA_KVW:2 * NSA_KVW].astype(BF16)

        q = q_ref[0]
        nrows = q.shape[0]
        qi = lax.broadcasted_iota(jnp.int32, (nrows, 1), 0) % n_q
        n_idx = lax.broadcasted_iota(jnp.int32, (1, n_chunk), 1)
        dist = past + qi - (n_idx * CMP_STRIDE + (CMP_BLOCK - 1))
        mask = (dist >= 0) & (n_idx < n_c)
        s = _dot_t(q, kc) * scale + _row_bias(_t5_bucket(dist), tab_ref, 0, NSA_HEADS, n_q)
        s = jnp.where(mask, s, NEG_INF)
        m = jnp.max(s, axis=-1, keepdims=True)
        p = jnp.where(mask, jnp.exp(s - m), 0.0)
        p = p / jnp.maximum(jnp.sum(p, axis=-1, keepdims=True), 1.0)
        oc_ref[0] = _own_block(_dot(p.astype(BF16), vc), G, R * n_q)
        gq = G * n_q
        oi = lax.broadcasted_iota(jnp.int32, (gq, nrows), 0)
        ri = lax.broadcasted_iota(jnp.int32, (gq, nrows), 1)
        pick = jnp.where((ri // (R * n_q) == oi // n_q) & (ri % n_q == oi % n_q), 1.0, 0.0).astype(BF16)
        p1, p2, p3 = _split3(p)
        imp = _dot(pick, p1) + _dot(pick, p2) + _dot(pick, p3)
        n_sp = sel_ref.shape[2]
        ratio = SLC_BLOCK // CMP_STRIDE
        ci = lax.broadcasted_iota(jnp.int32, (n_chunk, n_sp), 0)
        ti = lax.broadcasted_iota(jnp.int32, (n_chunk, n_sp), 1)
        gather01 = jnp.where((ci >= ti * ratio - (CMP_HALF - 1)) & (ci < (ti + 1) * ratio), 1.0, 0.0).astype(BF16)
        score = _dot01(imp, gather01)
        q_blk = (past + lax.broadcasted_iota(jnp.int32, (gq, 1), 0) % n_q) // SLC_BLOCK
        sel_ref[0] = _select_blocks(score, q_blk, N_SELECT)


def nsa_cmp_decode(page_table, table, qbd16, cache_lines, w16, pe, ck_gain, n_q):
    DB, n_pages = page_table.shape
    assert CMP_DEC_PAGES % 2 == 0 and n_pages % CMP_DEC_PAGES == 0
    n_steps = n_pages // CMP_DEC_PAGES
    n_chunk = n_pages * CHUNKS_PER_PAGE
    assert n_chunk % CMP_ROWS_TILE == 0
    past = n_pages * PAGE_SIZE
    n_s = -(-(past + n_q) // SLC_BLOCK)
    n_sp = -(-n_s // LANES) * LANES
    rows = qbd16.shape[1]
    kern = functools.partial(_nsa_cmp_decode_kernel, n_steps=n_steps, n_q=n_q, past=past, scale=HEAD_DIM ** -0.5)
    pg = (PAGE_SIZE * 2 * NSA_KV_HEADS, HEAD_DIM)
    return pl.pallas_call(
        kern,
        grid_spec=pltpu.PrefetchScalarGridSpec(
            num_scalar_prefetch=1,
            grid=(DB, n_steps),
            in_specs=[
                pl.BlockSpec(memory_space=pltpu.SMEM),
                pl.BlockSpec((1, rows, NSA_KVW), lambda b, s, pt: (b, 0, 0)),
                pl.BlockSpec((2, CMP_HALF, CMP_STRIDE * HEAD_DIM, HEAD_DIM), lambda b, s, pt: (0, 0, 0, 0)),
                pl.BlockSpec((2, CMP_HALF, 1, CMP_STRIDE * HEAD_DIM), lambda b, s, pt: (0, 0, 0, 0)),
                pl.BlockSpec((1, HEAD_DIM), lambda b, s, pt: (0, 0)),
            ] + [pl.BlockSpec(pg, lambda b, s, pt, j=j: (pt[b, CMP_DEC_PAGES * s + j], 0)) for j in range(CMP_DEC_PAGES)],
            out_specs=[
                pl.BlockSpec((1, rows, HEAD_DIM), lambda b, s, pt: (b, 0, 0)),
                pl.BlockSpec((1, NSA_KV_HEADS * n_q, n_sp), lambda b, s, pt: (b, 0, 0)),
            ],
            scratch_shapes=[
                pltpu.VMEM((2 * NSA_KV_HEADS, n_chunk, CMP_STRIDE * HEAD_DIM), BF16),
                pltpu.VMEM((CMP_HALF, n_chunk, 2 * NSA_KVW), F32),
            ],
        ),
        out_shape=[jax.ShapeDtypeStruct((DB, rows, HEAD_DIM), F32),
                   jax.ShapeDtypeStruct((DB, NSA_KV_HEADS * n_q, n_sp), F32)],
        compiler_params=_cparams(("parallel", "arbitrary")),
        name="nsa_cmp_decode",
    )(page_table, table, qbd16, w16, pe, ck_gain.reshape(1, HEAD_DIM), *([cache_lines] * CMP_DEC_PAGES))


SLC_DEC_PAGES = 4


def _nsa_slc_decode_kernel(pt_ref, plist_ref, pcnt_ref, tab_ref, q_ref, sel_ref, new_ref, wbuf_ref, wnew_ref, oc_ref,
                           gt_ref, *rest, n_steps, n_q, past, scale):
    pages, (o_ref, m_scr, l_scr, acc_scr) = rest[:SLC_DEC_PAGES], rest[SLC_DEC_PAGES:]
    b_ = pl.program_id(0)
    s_ = pl.program_id(1)
    q = q_ref[0]
    rows = q.shape[0]
    G, R = NSA_KV_HEADS, NSA_REP
    slots = 2 * G
    qi = lax.broadcasted_iota(jnp.int32, (rows, 1), 0) % n_q
    rh = lax.broadcasted_iota(jnp.int32, (rows, 1), 0) // n_q
    per_page = PAGE_SIZE // SLC_BLOCK
    n_sp = sel_ref.shape[2]
    col = lax.broadcasted_iota(jnp.int32, (1, PAGE_SIZE), 1)

    def sel_cols(first_block):
        bi = lax.broadcasted_iota(jnp.int32, (n_sp, PAGE_SIZE), 0)
        ki = lax.broadcasted_iota(jnp.int32, (n_sp, PAGE_SIZE), 1)
        expand = jnp.where(bi == first_block + ki // SLC_BLOCK, 1.0, 0.0).astype(BF16)
        return _dot(sel_ref[0], expand) > 0.5

    def near_bias(dist):
        return _row_bias(_t5_bucket(dist), tab_ref, 0, NSA_HEADS, n_q)

    def far_bias():
        out = jnp.zeros((rows, 1), F32)
        for h in range(NSA_HEADS):
            out = jnp.where(rh == h, tab_ref[N_BUCKETS - 1, h], out)
        return jnp.broadcast_to(out, (rows, PAGE_SIZE))

    def attend(k, v, bias, mask):
        s = _dot_t(q, k.astype(BF16)) * scale + bias
        _flash_step_cols(s, mask, v.astype(BF16), m_scr, l_scr, acc_scr)

    def result():
        return _own_block(acc_scr[...] / jnp.maximum(l_scr[:, 0:1], 1.0), G, R * n_q)

    @pl.when(s_ == 0)
    def _():
        _flash_reset(m_scr, l_scr, acc_scr)

    for j, page in enumerate(pages):
        idx = s_ * SLC_DEC_PAGES + j

        @pl.when(idx < pcnt_ref[b_])
        def _(page=page, idx=idx):
            p = plist_ref[b_, idx]
            dist = past + qi - (p * PAGE_SIZE + col)
            bias = lax.cond(past - (p + 1) * PAGE_SIZE + 1 >= BUCKET_MAX_DIST, far_bias, lambda: near_bias(dist))
            attend(_head_cols(page, 0, G, PAGE_SIZE, slots), _head_cols(page, G, G, PAGE_SIZE, slots), bias,
                   sel_cols(p * per_page) & (dist >= 0))

    @pl.when(s_ == n_steps - 1)
    def _():
        dist_n = qi - col
        attend(new_ref[0, :, :NSA_KVW], new_ref[0, :, NSA_KVW:], near_bias(dist_n),
               sel_cols(past // SLC_BLOCK) & (dist_n >= 0) & (col < n_q))
        o_s = result()
        _flash_reset(m_scr, l_scr, acc_scr)
        wb = wbuf_ref.shape[0] // slots
        for t in range(wb // PAGE_SIZE):
            dist_w = wb + qi - (t * PAGE_SIZE + col)
            attend(_head_cols(wbuf_ref, 0, G, PAGE_SIZE, slots, row0=t * PAGE_SIZE),
                   _head_cols(wbuf_ref, G, G, PAGE_SIZE, slots, row0=t * PAGE_SIZE), near_bias(dist_w),
                   (dist_w >= 0) & (dist_w < NSA_WINDOW))
        attend(wnew_ref[0, :, :NSA_KVW], wnew_ref[0, :, NSA_KVW:], near_bias(dist_n), (dist_n >= 0) & (col < n_q))
        o_w = result()
        gt = gt_ref[0]
        o_ref[0] = gt[:, 0:1] * oc_ref[0] + gt[:, 1:2] * o_s + gt[:, 2:3] * o_w


def nsa_slc_decode(page_table, table, qbd16, sel16, cache_lines, new_kv, win_lines, win_new, o_c, gates, n_q):
    DB, n_pages = page_table.shape
    assert n_pages % SLC_DEC_PAGES == 0
    n_steps = n_pages // SLC_DEC_PAGES
    rows = qbd16.shape[1]
    past = n_pages * PAGE_SIZE
    slots = 2 * NSA_KV_HEADS
    wb_lines = win_lines.shape[0] // DB
    assert (wb_lines // slots) % PAGE_SIZE == 0
    kern = functools.partial(_nsa_slc_decode_kernel, n_steps=n_steps, n_q=n_q, past=past, scale=HEAD_DIM ** -0.5)
    per_page = PAGE_SIZE // SLC_BLOCK
    need = jnp.any(sel16[:, :, :n_pages * per_page].reshape(DB, rows, n_pages, per_page) > 0, axis=(1, 3))
    plist = jnp.argsort(jnp.logical_not(need), axis=1, stable=True).astype(jnp.int32)
    pcnt = jnp.maximum(jnp.sum(need, axis=1), 1).astype(jnp.int32)
    full = lambda shape: pl.BlockSpec((1,) + shape, lambda b, s, pt, pls, pc: (b, 0, 0))

    def page_spec(j):
        def index(b, s, pt, pls, pc):
            return (pt[b, pls[b, jnp.minimum(s * SLC_DEC_PAGES + j, pc[b] - 1)]], 0)
        return pl.BlockSpec((PAGE_SIZE * slots, HEAD_DIM), index)

    return pl.pallas_call(
        kern,
        grid_spec=pltpu.PrefetchScalarGridSpec(
            num_scalar_prefetch=3,
            grid=(DB, n_steps),
            in_specs=[
                pl.BlockSpec(memory_space=pltpu.SMEM),
                full((rows, NSA_KVW)),
                full((rows, sel16.shape[2])),
                full((PAGE_SIZE, 2 * NSA_KVW)),
                pl.BlockSpec((wb_lines, HEAD_DIM), lambda b, s, pt, pls, pc: (b, 0)),
                full((PAGE_SIZE, 2 * NSA_KVW)),
                full((rows, HEAD_DIM)),
                full((rows, LANES)),
            ] + [page_spec(j) for j in range(SLC_DEC_PAGES)],
            out_specs=full((rows, HEAD_DIM)),
            scratch_shapes=[pltpu.VMEM((rows, LANES), F32), pltpu.VMEM((rows, LANES), F32),
                            pltpu.VMEM((rows, NSA_KVW), F32)],
        ),
        out_shape=jax.ShapeDtypeStruct((DB, rows, HEAD_DIM), F32),
        compiler_params=_cparams(("parallel", "arbitrary")),
        name="nsa_slc_decode",
    )(page_table, plist, pcnt, table, qbd16, sel16, new_kv, win_lines, win_new, o_c, gates,
      *([cache_lines] * SLC_DEC_PAGES))


DIL_DEC_TILE = 256


def _dil_decode_kernel(tab_ref, q_ref, buf_ref, new_ref, o_ref, m_scr, l_scr, acc_scr, *, n_tiles, n_q, wb, scale):
    kt = pl.program_id(1)
    q = q_ref[0]
    rows = q.shape[0]
    qi = lax.broadcasted_iota(jnp.int32, (rows, 1), 0) % n_q

    @pl.when(kt == 0)
    def _():
        _flash_reset(m_scr, l_scr, acc_scr)

    def attend(k, v, dist, valid):
        v = v.astype(BF16)
        s = _dot_t(q, k.astype(BF16)) * scale + _row_bias(_t5_bucket(dist), tab_ref, 0, DIL_HEADS, n_q)
        ps = []
        for i, (span, dil) in enumerate(zip(DIL_SPANS, DIL_STEPS)):
            mask = valid & (dist >= 0) & (jnp.bitwise_and(dist, dil - 1) == 0) & (dist <= span * dil)
            rs = slice(i * rows, (i + 1) * rows)
            si = jnp.where(mask, s, NEG_INF)
            m_prev = m_scr[rs, 0:1]
            m_new = jnp.maximum(m_prev, jnp.max(si, axis=-1, keepdims=True))
            p = jnp.where(mask, jnp.exp(si - m_new), 0.0)
            alpha = jnp.exp(m_prev - m_new)
            l_scr[rs, :] = jnp.broadcast_to(alpha * l_scr[rs, 0:1] + jnp.sum(p, axis=-1, keepdims=True), (rows, LANES))
            m_scr[rs, :] = jnp.broadcast_to(m_new, (rows, LANES))
            acc_scr[rs, :] = alpha * acc_scr[rs, :]
            ps.append(p.astype(BF16))
        acc_scr[...] = acc_scr[...] + _dot(jnp.concatenate(ps, axis=0), v)

    n = DIL_DEC_TILE
    slots = 2 * DIL_HEADS
    col = lax.broadcasted_iota(jnp.int32, (1, n), 1)
    attend(_head_cols(buf_ref, 0, DIL_HEADS, n, slots), _head_cols(buf_ref, DIL_HEADS, DIL_HEADS, n, slots),
           wb + qi - (kt * n + col), col >= 0)

    @pl.when(kt == n_tiles - 1)
    def _():
        cn = lax.broadcasted_iota(jnp.int32, (1, new_ref.shape[1]), 1)
        attend(new_ref[0, :, :DIL_W], new_ref[0, :, DIL_W:], qi - cn, cn < n_q)
        nums = [_own_block(acc_scr[i * rows:(i + 1) * rows, :], DIL_HEADS, n_q) for i in range(N_DIL)]
        ms = [m_scr[i * rows:(i + 1) * rows, 0:1] for i in range(N_DIL)]
        ls = [l_scr[i * rows:(i + 1) * rows, 0:1] for i in range(N_DIL)]
        m_max = ms[0]
        for m in ms[1:]:
            m_max = jnp.maximum(m_max, m)
        ws = [jnp.exp(m - m_max) for m in ms]
        num = ws[0] * nums[0]
        den = ws[0] * ls[0]
        for w, n_, l in zip(ws[1:], nums[1:], ls[1:]):
            num = num + w * n_
            den = den + w * l
        o_ref[0] = num / den


def dil_decode(table, qbd16, buf_lines, new_kv, n_q):
    DB = qbd16.shape[0]
    slots = 2 * DIL_HEADS
    wb = buf_lines.shape[0] // (DB * slots)
    rows = qbd16.shape[1]
    assert wb % DIL_DEC_TILE == 0
    n_tiles = wb // DIL_DEC_TILE
    kern = functools.partial(_dil_decode_kernel, n_tiles=n_tiles, n_q=n_q, wb=wb, scale=HEAD_DIM ** -0.5)
    return pl.pallas_call(
        kern,
        grid=(DB, n_tiles),
        in_specs=[
            pl.BlockSpec(memory_space=pltpu.SMEM),
            pl.BlockSpec((1, rows, DIL_W), lambda b, t: (b, 0, 0)),
            pl.BlockSpec((DIL_DEC_TILE * slots, HEAD_DIM), lambda b, t: (b * n_tiles + t, 0)),
            pl.BlockSpec((1, new_kv.shape[1], 2 * DIL_W), lambda b, t: (b, 0, 0)),
        ],
        out_specs=pl.BlockSpec((1, rows, HEAD_DIM), lambda b, t: (b, 0, 0)),
        out_shape=jax.ShapeDtypeStruct((DB, rows, HEAD_DIM), F32),
        scratch_shapes=[pltpu.VMEM((N_DIL * rows, LANES), F32), pltpu.VMEM((N_DIL * rows, LANES), F32),
                        pltpu.VMEM((N_DIL * rows, DIL_W), F32)],
        compiler_params=_cparams(("parallel", "arbitrary")),
        name="dil_decode",
    )(table, qbd16, buf_lines, new_kv)


PROJ_TN = 256
PROJ_TM = 512
_C_SBQ, _C_SBK, _C_SBV = 0, SB_W, 2 * SB_W
_C_NQ = 3 * SB_W
_C_CMP = _C_NQ + NSA_QW
_C_SLC = _C_CMP + 2 * NSA_KVW
_C_WIN = _C_SLC + 2 * NSA_KVW
_C_GATE = _C_WIN + 2 * NSA_KVW
SPARSE_NP = -(-SPARSE_IN // PROJ_TN) * PROJ_TN


def _sparse_proj_plan(w_in, qk_gain):
    w16 = jnp.pad(w_in, ((0, 0), (0, SPARSE_NP - SPARSE_IN))).astype(BF16)
    gain = jnp.ones((SPARSE_NP,), F32)
    flag = np.zeros((SPARSE_NP,), np.float32)
    kinds = np.full((SPARSE_NP // PROJ_TN,), EPI_PLAIN, np.int32)
    for col, width, gi in ((_C_NQ, NSA_QW, 0), (_C_SLC, NSA_KVW, 2), (_C_WIN, NSA_KVW, 3)):
        gain = gain.at[col:col + width].set(jnp.tile(qk_gain[gi], width // HEAD_DIM))
        flag[col:col + width] = 1.0
        assert col % PROJ_TN == 0 and width % PROJ_TN == 0
        kinds[col // PROJ_TN:(col + width) // PROJ_TN] = EPI_NORM
    assert _C_GATE % PROJ_TN == 0 and SPARSE_NP - _C_GATE == PROJ_TN
    kinds[_C_GATE // PROJ_TN] = EPI_SIGMOID
    return w16, gain, jnp.asarray(flag), jnp.asarray(kinds)


def _dil_proj_plan(w_in, qk_gain):
    N = 3 * DIL_W
    gain = jnp.concatenate([jnp.tile(qk_gain[0], DIL_HEADS), jnp.tile(qk_gain[1], DIL_HEADS), jnp.ones((DIL_W,), F32)])
    flag = np.concatenate([np.ones((2 * DIL_W,), np.float32), np.zeros((DIL_W,), np.float32)])
    kinds = np.full((N // PROJ_TN,), EPI_PLAIN, np.int32)
    kinds[:2 * DIL_W // PROJ_TN] = EPI_NORM
    return w_in.astype(BF16), gain, jnp.asarray(flag), jnp.asarray(kinds)


def _pad_rows(a, n):
    return jnp.pad(a, ((0, 0), (0, n - a.shape[1]), (0, 0)))


def _rows_to_tokens(o, n_heads, n_q):
    DB = o.shape[0]
    return o.reshape(DB, n_heads, n_q, HEAD_DIM).transpose(0, 2, 1, 3).reshape(DB * n_q, n_heads * HEAD_DIM)


def kernel(x_prompt, x_sample, cache_sb_kv, cache_nsa_cmp_kv, cache_nsa_slc_kv, state_nsa_win_kv, state_dil_kv,
           page_table, rel_bias_table, norm_mix, norm_ffn, sparse_w_in, sparse_w_out, nsa_qk_gain, nsa_cmp_w,
           nsa_cmp_pe, dil_w_in, dil_w_out, dil_qk_gain, moe_w_group, moe_b_group, moe_w_expert, moe_b_expert,
           moe_w_gate, moe_w_up, moe_w_down):
    B, S, D = x_prompt.shape
    DB, DS, _ = x_sample.shape
    n_pool = cache_sb_kv.shape[1]
    xp = x_prompt.reshape(B * S, D)
    xs = x_sample.reshape(DB * DS, D)
    nsa_table = rel_bias_table[:, :NSA_HEADS]
    dil_table = rel_bias_table[:, :DIL_HEADS]
    G, R = NSA_KV_HEADS, NSA_REP

    def moe(layer, xp, xs):
        return moe_layer(xp, xs, norm_ffn[layer], moe_w_group[layer], moe_b_group[layer], moe_w_expert[layer],
                         moe_b_expert[layer], moe_w_gate[layer], moe_w_up[layer], moe_w_down[layer])

    w16, gain, flag, kinds = _sparse_proj_plan(sparse_w_in[0], nsa_qk_gain[0])
    w_out16 = sparse_w_out[0].astype(BF16)
    cw16, cpe = _prep_cmp_weights(nsa_cmp_w[0], nsa_cmp_pe[0])
    ck_gain = nsa_qk_gain[0, 1]
    kv_cols = ((_C_SBK, 2 * SB_W), (_C_CMP, 2 * NSA_KVW), (_C_SLC, 2 * NSA_KVW), (_C_WIN, 2 * NSA_KVW))
    p16, gate32, sb_lines, cmp_lines, slc_lines, win_lines = norm_mm(
        xp, norm_mix[0], w16, gain, flag, kinds, tm=PROJ_TM, tn=PROJ_TN, f32_cols=(_C_GATE, PROJ_TN), lines=kv_cols)
    p16b = p16.reshape(B, S, SPARSE_NP)
    o_sb = sb_prompt(p16b, B, S, q_col=_C_SBQ, k_col=_C_SBK, v_col=_C_SBV)
    chunks = p16b[:, :, _C_CMP:_C_CMP + 2 * NSA_KVW].reshape(B, S // CMP_STRIDE, CHUNK_W)
    kc, vc = compress_prompt(chunks, cw16, cpe, ck_gain)
    tb = nsa_bias_tiles(nsa_table)
    cb = nsa_cmp_bias(nsa_table, S, kc.shape[1])
    o_nsa = nsa_prompt(p16b, gate32.reshape(B, S, PROJ_TN), kc, vc, tb, cb, B, S, q_col=_C_NQ, sk_col=_C_SLC,
                       sv_col=_C_SLC + NSA_KVW, wk_col=_C_WIN, wv_col=_C_WIN + NSA_KVW, gate_col=0)
    a = jnp.concatenate([o_sb, o_nsa], axis=-1).reshape(B * S, SB_W + NSA_QW)
    xp = mm_res(a, w_out16, xp, tm=1024, tn=PROJ_TN)
    sb_kv_prompt = sb_lines.reshape(1, B, S, 2, SB_HEADS, HEAD_DIM)
    cmp_kv_prompt = cmp_lines.reshape(1, B, S, 2, G, HEAD_DIM)
    slc_kv_prompt = slc_lines.reshape(1, B, S, 2, G, HEAD_DIM)
    wn = min(NSA_WINDOW, S)
    win_kv_prompt = win_lines.reshape(1, B, S, 2, G, HEAD_DIM)[:, :, S - wn:]
    _, q32 = norm_mm(xs, norm_mix[0], w16, gain, flag, kinds, tm=DB * DS, tn=PROJ_TN, f32_cols=(0, SPARSE_NP))
    q32b = q32.reshape(DB, DS, SPARSE_NP)
    sb_new = q32b[:, :, _C_SBK:_C_SBK + 2 * SB_W]
    cmp_new = q32b[:, :, _C_CMP:_C_CMP + 2 * NSA_KVW]
    slc_new = q32b[:, :, _C_SLC:_C_SLC + 2 * NSA_KVW]
    win_new = q32b[:, :, _C_WIN:_C_WIN + 2 * NSA_KVW]
    o_sb_s = sb_decode(page_table, _block_diag_rows(q32b[:, :, _C_SBQ:_C_SBQ + SB_W], SB_HEADS).astype(BF16),
                       _pad_rows(sb_new, PAGE_SIZE), cache_sb_kv[0].reshape(-1, HEAD_DIM))
    qn = q32b[:, :, _C_NQ:_C_NQ + NSA_QW].reshape(DB, DS, G, R, HEAD_DIM).transpose(0, 2, 3, 1, 4)
    qn = (qn[:, :, :, :, None, :] * jnp.eye(G, dtype=F32)[None, :, None, None, :, None]).reshape(DB, G * R * DS, NSA_KVW)
    qn16 = qn.astype(BF16)
    o_c, sel = nsa_cmp_decode(page_table, nsa_table, qn16, cache_nsa_cmp_kv[0].reshape(-1, HEAD_DIM), cw16, cpe, ck_gain, DS)
    sel16 = jnp.broadcast_to(sel.reshape(DB, G, 1, DS, -1), (DB, G, R, DS, sel.shape[-1])).reshape(DB, G * R * DS, -1).astype(BF16)
    gates = q32b[:, :, _C_GATE:_C_GATE + 3 * NSA_HEADS].reshape(DB, DS, G, R, 3).transpose(0, 2, 3, 1, 4).reshape(DB, G * R * DS, 3)
    gates = jnp.pad(gates, ((0, 0), (0, 0), (0, LANES - 3)))
    wbuf = state_nsa_win_kv[0]
    o_nsa_s = nsa_slc_decode(page_table, nsa_table, qn16, sel16, cache_nsa_slc_kv[0].reshape(-1, HEAD_DIM),
                             _pad_rows(slc_new, PAGE_SIZE), wbuf.reshape(-1, HEAD_DIM), _pad_rows(win_new, PAGE_SIZE),
                             o_c, gates, DS)
    a_s = jnp.concatenate([_rows_to_tokens(o_sb_s, SB_HEADS, DS), _rows_to_tokens(o_nsa_s, NSA_HEADS, DS)], axis=-1)
    xs = mm_res(a_s.astype(BF16), w_out16, xs, tm=DB * DS, tn=PROJ_TN)
    sb_kv_sample = sb_new.reshape(1, DB, DS, 2, SB_HEADS, HEAD_DIM)
    cmp_kv_sample = cmp_new.reshape(1, DB, DS, 2, G, HEAD_DIM)
    slc_kv_sample = slc_new.reshape(1, DB, DS, 2, G, HEAD_DIM)
    win_kv_sample = jnp.concatenate([wbuf, win_new.reshape(DB, DS, 2, G, HEAD_DIM)], axis=1)[None, :, DS:]
    xp, xs = moe(0, xp, xs)

    w16, gain, flag, kinds = _dil_proj_plan(dil_w_in[0], dil_qk_gain[0])
    w_out16 = dil_w_out[0].astype(BF16)
    p16, dil_lines = norm_mm(xp, norm_mix[1], w16, gain, flag, kinds, tm=PROJ_TM, tn=PROJ_TN,
                             lines=((DIL_W, 2 * DIL_W),))
    dbias = dil_bias_tiles(dil_table)
    stats = [dil_prompt_pattern(p16.reshape(B, S, 3 * DIL_W), dbias, pat, B, S) for pat in range(N_DIL)]
    o = dil_merge([n for n, _ in stats], [ml for _, ml in stats], tm=512)
    xp = mm_res(o, w_out16, xp, tm=1024, tn=PROJ_TN)
    dn = min(DIL_MAX_WINDOW, S)
    dil_kv_prompt = dil_lines.reshape(1, B, S, 2, DIL_HEADS, HEAD_DIM)[:, :, S - dn:]
    _, q32 = norm_mm(xs, norm_mix[1], w16, gain, flag, kinds, tm=DB * DS, tn=PROJ_TN, f32_cols=(0, 3 * DIL_W))
    q32b = q32.reshape(DB, DS, 3 * DIL_W)
    dil_new = q32b[:, :, DIL_W:]
    dbuf = state_dil_kv[0]
    o_s = dil_decode(dil_table, _block_diag_rows(q32b[:, :, :DIL_W], DIL_HEADS).astype(BF16),
                     dbuf.reshape(-1, HEAD_DIM), _pad_rows(dil_new, PAGE_SIZE), DS)
    xs = mm_res(_rows_to_tokens(o_s, DIL_HEADS, DS).astype(BF16), w_out16, xs, tm=DB * DS, tn=PROJ_TN)
    dil_kv_sample = jnp.concatenate([dbuf, dil_new.reshape(DB, DS, 2, DIL_HEADS, HEAD_DIM)], axis=1)[None, :, DS:]
    xp, xs = moe(1, xp, xs)

    return (xp.reshape(B, S, D), xs.reshape(DB, DS, D), sb_kv_prompt, sb_kv_sample, cmp_kv_prompt, cmp_kv_sample,
            slc_kv_prompt, slc_kv_sample, win_kv_prompt, win_kv_sample, dil_kv_prompt, dil_kv_sample)
```

```python
import functools
import math

import jax
import jax.numpy as jnp
import numpy as np
from jax import lax
from jax.experimental import pallas as pl
from jax.experimental.pallas import tpu as pltpu

D_MODEL = 2048
HEAD_DIM = 128
SB_HEADS = 8
NSA_HEADS = 8
NSA_KV_HEADS = 2
NSA_REP = NSA_HEADS // NSA_KV_HEADS
CMP_BLOCK = 32
CMP_STRIDE = 16
SLC_BLOCK = 64
N_SELECT = 16
NSA_WINDOW = 512
DIL_HEADS = 16
DIL_PATTERNS = ((128, 1), (512, 4), (2048, 16))
DIL_MAX_WINDOW = 2048
QBLK = 128
N_BUCKETS = 32
BUCKET_MAX_DIST = 2048
N_GROUPS = 4
EXPERTS_PER_GROUP = 4
N_EXPERTS = N_GROUPS * EXPERTS_PER_GROUP
TOP_K_INNER = 2
D_EXPERT = 1024
PAGE_SIZE = 128
RMS_EPS = 1e-6
NEG_INF = -1e30
FORCE_SCORE = 1e9
SB_W = SB_HEADS * HEAD_DIM
NSA_QW = NSA_HEADS * HEAD_DIM
NSA_KVW = NSA_KV_HEADS * HEAD_DIM
SPARSE_IN = 3 * SB_W + NSA_QW + 6 * NSA_KVW + 3 * NSA_HEADS
DIL_W = DIL_HEADS * HEAD_DIM

LANES = 128
SUBLANES = 8
VMEM_LIMIT = 56 * 1024 * 1024

BF16 = jnp.bfloat16
F32 = jnp.float32


def _cparams(sem):
    return pltpu.CompilerParams(dimension_semantics=sem, vmem_limit_bytes=VMEM_LIMIT)


def _dot(a, b):
    return jnp.dot(a, b, preferred_element_type=F32)


def _dot_t(a, b):
    return lax.dot_general(a, b, (((1,), (1,)), ((), ())), preferred_element_type=F32)


def _split3(x):
    h1 = x.astype(BF16)
    r1 = x - h1.astype(F32)
    h2 = r1.astype(BF16)
    h3 = (r1 - h2.astype(F32)).astype(BF16)
    return h1, h2, h3


def _dot01(x, m01):
    h1, h2, h3 = _split3(x)
    return _dot(h1, m01) + _dot(h2, m01) + _dot(h3, m01)


EPI_PLAIN, EPI_NORM, EPI_SIGMOID = 0, 1, 2


MM_ROW_CHUNK = 256


def _norm_mm_kernel(cls_ref, x_ref, g_ref, w_ref, cg_ref, cf_ref, o16_ref, *rest, classes, has_f32, line_cfg):
    h_scr = rest[-1]
    o32_ref = rest[0] if has_f32 else None
    line_refs = rest[(1 if has_f32 else 0):-1]
    j = pl.program_id(1)
    tm, tn = o16_ref.shape
    rc = min(MM_ROW_CHUNK, tm)

    @pl.when(j == 0)
    def _():
        x = x_ref[...]
        y = x * lax.rsqrt(jnp.mean(x * x, axis=-1, keepdims=True) + RMS_EPS)
        h_scr[...] = (y * g_ref[...]).astype(BF16)

    def epilogue(kind, acc):
        if kind == EPI_PLAIN:
            return acc
        if kind == EPI_SIGMOID:
            return jax.nn.sigmoid(acc)
        parts = []
        for c in range(tn // HEAD_DIM):
            a = acc[:, c * HEAD_DIM:(c + 1) * HEAD_DIM]
            parts.append(a * lax.rsqrt(jnp.mean(a * a, axis=-1, keepdims=True) + RMS_EPS))
        return jnp.where(cf_ref[...] > 0.5, jnp.concatenate(parts, axis=1) * cg_ref[...], acc)

    for cid, (kind, in_f32, li) in enumerate(classes):
        @pl.when(cls_ref[j] == cid)
        def _(kind=kind, in_f32=in_f32, li=li):
            for r in range(tm // rc):
                rows = slice(r * rc, (r + 1) * rc)
                out = epilogue(kind, _dot(h_scr[rows, :], w_ref[...]))
                o16_ref[rows, :] = out.astype(BF16)
                if in_f32:
                    o32_ref[rows, :] = out
                if li is not None:
                    j0, slots = line_cfg[li]
                    for c in range(tn // LANES):
                        line_refs[li][pl.ds(r * rc * slots + (j - j0) * (tn // LANES) + c, rc, stride=slots), :] = (
                            out[:, c * LANES:(c + 1) * LANES])


def norm_mm(x, g, w16, col_gain, col_flag, tile_kind, *, tm, tn, f32_cols=None, lines=()):
    T, D = x.shape
    N = w16.shape[1]
    assert T % tm == 0 and N % tn == 0
    n_tiles = N // tn
    grid = (T // tm, n_tiles)
    tile_kind = np.asarray(tile_kind)
    out_specs = [pl.BlockSpec((tm, tn), lambda i, j, k: (i, j))]
    out_shape = [jax.ShapeDtypeStruct((T, N), BF16)]
    in_f32 = np.zeros((n_tiles,), bool)
    if f32_cols is not None:
        c0, width = f32_cols
        assert c0 % tn == 0 and width % tn == 0
        j0, nj = c0 // tn, width // tn
        in_f32[j0:j0 + nj] = True
        out_specs.append(pl.BlockSpec((tm, tn), lambda i, j, k, j0=j0, nj=nj: (i, jnp.clip(j - j0, 0, nj - 1))))
        out_shape.append(jax.ShapeDtypeStruct((T, width), F32))
    line_of = [None] * n_tiles
    line_cfg = []
    for li, (c0, width) in enumerate(lines):
        assert c0 % tn == 0 and width % tn == 0
        slots = width // LANES
        line_cfg.append((c0 // tn, slots))
        for j in range(c0 // tn, (c0 + width) // tn):
            line_of[j] = li
        out_specs.append(pl.BlockSpec((tm * slots, LANES), lambda i, j, k: (i, 0)))
        out_shape.append(jax.ShapeDtypeStruct((T * slots, LANES), F32))
    per_tile = [(int(tile_kind[j]), bool(in_f32[j]), line_of[j]) for j in range(n_tiles)]
    classes = tuple(dict.fromkeys(per_tile))
    tile_cls = jnp.asarray([classes.index(c) for c in per_tile], jnp.int32)
    kern = functools.partial(_norm_mm_kernel, classes=classes, has_f32=f32_cols is not None, line_cfg=tuple(line_cfg))
    return pl.pallas_call(
        kern,
        grid_spec=pltpu.PrefetchScalarGridSpec(
            num_scalar_prefetch=1,
            grid=grid,
            in_specs=[
                pl.BlockSpec((tm, D), lambda i, j, k: (i, 0)),
                pl.BlockSpec((1, D), lambda i, j, k: (0, 0)),
                pl.BlockSpec((D, tn), lambda i, j, k: (0, j)),
                pl.BlockSpec((1, tn), lambda i, j, k: (0, j)),
                pl.BlockSpec((1, tn), lambda i, j, k: (0, j)),
            ],
            out_specs=out_specs,
            scratch_shapes=[pltpu.VMEM((tm, D), BF16)],
        ),
        out_shape=out_shape,
        compiler_params=_cparams(("parallel", "arbitrary")),
        name="norm_mm",
    )(tile_cls, x, g.reshape(1, D), w16, col_gain.reshape(1, N), col_flag.reshape(1, N))


def _mm_res_kernel(a_ref, w_ref, r_ref, o_ref):
    tm = o_ref.shape[0]
    rc = min(MM_ROW_CHUNK, tm)
    for r in range(tm // rc):
        rows = slice(r * rc, (r + 1) * rc)
        o_ref[rows, :] = r_ref[rows, :] + _dot(a_ref[rows, :], w_ref[...])


def mm_res(a16, w16, res, *, tm, tn):
    T, K = a16.shape
    N = w16.shape[1]
    assert T % tm == 0 and N % tn == 0
    return pl.pallas_call(
        _mm_res_kernel,
        grid=(T // tm, N // tn),
        in_specs=[
            pl.BlockSpec((tm, K), lambda i, j: (i, 0)),
            pl.BlockSpec((K, tn), lambda i, j: (0, j)),
            pl.BlockSpec((tm, tn), lambda i, j: (i, j)),
        ],
        out_specs=pl.BlockSpec((tm, tn), lambda i, j: (i, j)),
        out_shape=jax.ShapeDtypeStruct((T, N), F32),
        compiler_params=_cparams(("parallel", "arbitrary")),
        name="mm_res",
    )(a16, w16, res)


LOG2_E = math.log2(math.e)


def _log2_keep(z2):
    nz = -z2
    return jnp.minimum(nz, 0.0) - jnp.log2(1.0 + jnp.exp2(jnp.minimum(z2, nz)))


def _upper01(n):
    r = lax.broadcasted_iota(jnp.int32, (2 * n, n), 0)
    c = lax.broadcasted_iota(jnp.int32, (2 * n, n), 1)
    return jnp.where(jnp.where(r >= n, r - n, r) > c, 1.0, 0.0).astype(BF16)


def _after_in_tile(lk, upper2):
    h1 = lk.astype(BF16)
    h2 = (lk - h1.astype(F32)).astype(BF16)
    return _dot(jnp.concatenate([h1, h2], axis=1), upper2)


def _sb_prompt_kernel(q_ref, k_ref, v_ref, o_ref, acc_scr, run_scr, *, bq, scale):
    bk = LANES
    i = pl.program_id(2)
    nd = bq // bk
    acc_scr[...] = jnp.zeros(acc_scr.shape, F32)
    run_scr[...] = jnp.zeros(run_scr.shape, F32)
    upper2 = _upper01(bk)

    def absorb(kb, r0, masked):
        ks = pl.multiple_of(kb * bk, bk)
        k = k_ref[0, pl.ds(ks, bk), :]
        v = v_ref[0, pl.ds(ks, bk), :]
        z2 = _dot_t(q_ref[0, r0:, :], k) * (scale * LOG2_E)
        lk = _log2_keep(z2)
        lb = z2 + lk
        if masked:
            qpos = i * bq + r0 + lax.broadcasted_iota(jnp.int32, (bq - r0, 1), 0)
            mask = ks + lax.broadcasted_iota(jnp.int32, (1, bk), 1) < qpos
            lk = jnp.where(mask, lk, 0.0)
        w = jnp.exp2(lb + _after_in_tile(lk, upper2) + run_scr[r0:, :])
        if masked:
            w = jnp.where(mask, w, 0.0)
        acc_scr[r0:, :] = acc_scr[r0:, :] + _dot(w.astype(BF16), v)
        run_scr[r0:, :] = run_scr[r0:, :] + jnp.sum(lk, axis=-1, keepdims=True)

    for d in range(nd - 1, -1, -1):
        absorb(i * nd + d, d * bk, True)

    assert nd % 2 == 0

    def body(t, carry):
        absorb(i * nd - 1 - 2 * t, 0, False)
        absorb(i * nd - 2 - 2 * t, 0, False)
        return carry

    lax.fori_loop(0, i * (nd // 2), body, 0)
    o_ref[0] = acc_scr[...].astype(o_ref.dtype)


def sb_prompt(p16, B, S, *, q_col, k_col, v_col, bq=512):
    assert S % bq == 0 and bq % LANES == 0
    kern = functools.partial(_sb_prompt_kernel, bq=bq, scale=HEAD_DIM ** -0.5)
    qb, kb_, vb = q_col // HEAD_DIM, k_col // HEAD_DIM, v_col // HEAD_DIM
    return pl.pallas_call(
        kern,
        grid=(B, SB_HEADS, S // bq),
        in_specs=[
            pl.BlockSpec((1, bq, HEAD_DIM), lambda b, h, i: (b, i, qb + h)),
            pl.BlockSpec((1, S, HEAD_DIM), lambda b, h, i: (b, 0, kb_ + h)),
            pl.BlockSpec((1, S, HEAD_DIM), lambda b, h, i: (b, 0, vb + h)),
        ],
        out_specs=pl.BlockSpec((1, bq, HEAD_DIM), lambda b, h, i: (b, i, h)),
        out_shape=jax.ShapeDtypeStruct((B, S, SB_W), BF16),
        scratch_shapes=[pltpu.VMEM((bq, HEAD_DIM), F32), pltpu.VMEM((bq, LANES), F32)],
        compiler_params=_cparams(("parallel", "parallel", "arbitrary")),
        name="sb_prompt",
    )(p16, p16, p16)


def _t5_bucket(dist):
    n = jnp.maximum(dist, 0)
    exact = N_BUCKETS // 2
    nf = jnp.maximum(n, exact).astype(F32)
    far = exact + (jnp.log(nf / exact) / math.log(BUCKET_MAX_DIST / exact) * (N_BUCKETS - exact)).astype(jnp.int32)
    return jnp.where(n < exact, n, jnp.minimum(far, N_BUCKETS - 1))


def _table_lookup(bucket, tab_ref, head):
    out = jnp.zeros(bucket.shape, F32)
    for k in range(N_BUCKETS):
        out = jnp.where(bucket == k, tab_ref[k, head], out)
    return out


N_BIAS_DIAG = -(-(BUCKET_MAX_DIST + QBLK) // QBLK) + 1


def _bias_tiles_kernel(tab_ref, o_ref):
    g = pl.program_id(0)
    d = pl.program_id(1)
    dist = d * QBLK + lax.broadcasted_iota(jnp.int32, (QBLK, QBLK), 0) - lax.broadcasted_iota(jnp.int32, (QBLK, QBLK), 1)
    bucket = _t5_bucket(dist)
    for r in range(NSA_REP):
        o_ref[0, 0, r * QBLK:(r + 1) * QBLK, :] = _table_lookup(bucket, tab_ref, g * NSA_REP + r)


def nsa_bias_tiles(table):
    return pl.pallas_call(
        _bias_tiles_kernel,
        grid=(NSA_KV_HEADS, N_BIAS_DIAG),
        in_specs=[pl.BlockSpec(memory_space=pltpu.SMEM)],
        out_specs=pl.BlockSpec((1, 1, NSA_REP * QBLK, QBLK), lambda g, d: (g, d, 0, 0)),
        out_shape=jax.ShapeDtypeStruct((NSA_KV_HEADS, N_BIAS_DIAG, NSA_REP * QBLK, QBLK), F32),
        compiler_params=_cparams(("parallel", "arbitrary")),
        name="nsa_bias_tiles",
    )(table)


CMP_HALF = CMP_BLOCK // CMP_STRIDE
CHUNK_W = CMP_STRIDE * 2 * NSA_KVW


def _compress_chunks(x, w_ref, pe_ref, gain):
    n = x.shape[0]
    outs = []
    for kv in range(2):
        for g in range(NSA_KV_HEADS):
            off = kv * NSA_KVW + g * HEAD_DIM
            xs = jnp.concatenate(
                [x[:, l * 2 * NSA_KVW + off:l * 2 * NSA_KVW + off + HEAD_DIM] for l in range(CMP_STRIDE)], axis=1)
            halves = []
            for j in range(CMP_HALF):
                w = w_ref[kv, j]
                pe = jnp.broadcast_to(pe_ref[kv, j], (SUBLANES, CMP_STRIDE * HEAD_DIM)).astype(BF16)
                halves.append(_dot(xs, w) + _dot(pe, w)[0:1, :])
            out = halves[0]
            for j in range(1, CMP_HALF):
                out = out + pltpu.roll(halves[j], n - j, axis=0)
            if kv == 0:
                out = out * lax.rsqrt(jnp.mean(out * out, axis=-1, keepdims=True) + RMS_EPS) * gain
            outs.append(out)
    kc = jnp.concatenate(outs[:NSA_KV_HEADS], axis=1)
    vc = jnp.concatenate(outs[NSA_KV_HEADS:], axis=1)
    return kc, vc


def _compress_prompt_kernel(x_ref, w_ref, pe_ref, g_ref, kc_ref, vc_ref):
    kc, vc = _compress_chunks(x_ref[0], w_ref, pe_ref, g_ref[...])
    kc_ref[0] = kc.astype(BF16)
    vc_ref[0] = vc.astype(BF16)


def _prep_cmp_weights(cmp_w, cmp_pe):
    w = cmp_w.reshape(2, CMP_HALF, CMP_STRIDE * HEAD_DIM, HEAD_DIM).astype(BF16)
    pe = cmp_pe.reshape(2, CMP_HALF, 1, CMP_STRIDE * HEAD_DIM)
    return w, pe


def compress_prompt(chunks16, w16, pe, ck_gain):
    B, n, _ = chunks16.shape
    return pl.pallas_call(
        _compress_prompt_kernel,
        grid=(B,),
        in_specs=[
            pl.BlockSpec((1, n, CHUNK_W), lambda b: (b, 0, 0)),
            pl.BlockSpec((2, CMP_HALF, CMP_STRIDE * HEAD_DIM, HEAD_DIM), lambda b: (0, 0, 0, 0)),
            pl.BlockSpec((2, CMP_HALF, 1, CMP_STRIDE * HEAD_DIM), lambda b: (0, 0, 0, 0)),
            pl.BlockSpec((1, HEAD_DIM), lambda b: (0, 0)),
        ],
        out_specs=[pl.BlockSpec((1, n, NSA_KVW), lambda b: (b, 0, 0))] * 2,
        out_shape=[jax.ShapeDtypeStruct((B, n, NSA_KVW), BF16)] * 2,
        compiler_params=_cparams(("parallel",)),
        name="nsa_compress_prompt",
    )(chunks16, w16, pe, ck_gain.reshape(1, HEAD_DIM))


def _flash_step(s, mask, v, m_scr, l_scr, acc_scr):
    n = s.shape[1] // LANES
    cols = [slice(c * LANES, (c + 1) * LANES) for c in range(n)]
    sm = [jnp.where(mask[:, c], s[:, c], NEG_INF) for c in cols]
    top = sm[0]
    for x in sm[1:]:
        top = jnp.maximum(top, x)
    m_prev = m_scr[...]
    m_new = jnp.maximum(m_prev, jnp.max(top, axis=-1, keepdims=True))
    ps = [jnp.where(mask[:, c], jnp.exp(x - m_new), 0.0) for c, x in zip(cols, sm)]
    tot = ps[0]
    for x in ps[1:]:
        tot = tot + x
    alpha = jnp.exp(m_prev - m_new)
    l_scr[...] = alpha * l_scr[...] + jnp.sum(tot, axis=-1, keepdims=True)
    acc_scr[...] = alpha * acc_scr[...] + _dot(jnp.concatenate([x.astype(BF16) for x in ps], axis=1), v)
    m_scr[...] = m_new


def _flash_reset(m_scr, l_scr, acc_scr):
    m_scr[...] = jnp.full(m_scr.shape, NEG_INF, F32)
    l_scr[...] = jnp.zeros(l_scr.shape, F32)
    acc_scr[...] = jnp.zeros(acc_scr.shape, F32)


def _select_blocks(score, q_blk, n_sel):
    Q, n_s = score.shape
    blk = lax.broadcasted_iota(jnp.int32, (1, n_s), 1)
    valid = blk <= q_blk
    forced = valid & ((blk == 0) | (blk == q_blk) | (blk == q_blk - 1))
    score = jnp.where(forced, FORCE_SCORE, jnp.where(valid, score, NEG_INF))
    rank = jnp.zeros((Q, n_s), F32)
    for c in range(n_s):
        col = score[:, c:c + 1]
        before = (col > score) | ((col == score) & (blk > c))
        rank = rank + jnp.where(before, 1.0, 0.0)
    return jnp.where(rank < n_sel, 1.0, 0.0)


def _select_blocks_t(score_t, q_blk, n_sel):
    n_s, Q = score_t.shape
    blk = lax.broadcasted_iota(jnp.int32, (n_s, 1), 0)
    valid = blk <= q_blk
    forced = valid & ((blk == 0) | (blk == q_blk) | (blk == q_blk - 1))
    score_t = jnp.where(forced, FORCE_SCORE, jnp.where(valid, score_t, NEG_INF))
    rank = jnp.zeros((n_s, Q), F32)
    for c in range(n_s):
        row = score_t[c:c + 1, :]
        before = (row > score_t) | ((row == score_t) & (blk > c))
        rank = rank + jnp.where(before, 1.0, 0.0)
    return jnp.where(rank < n_sel, 1.0, 0.0)


def _nsa_cmp_bias_kernel(tab_ref, o_ref):
    g = pl.program_id(0)
    i = pl.program_id(1)
    ncp = o_ref.shape[3]
    qpos1 = i * QBLK + lax.broadcasted_iota(jnp.int32, (QBLK, 1), 0)
    c_end = lax.broadcasted_iota(jnp.int32, (1, ncp), 1) * CMP_STRIDE + (CMP_BLOCK - 1)
    bucket = _t5_bucket(qpos1 - c_end)
    for r in range(NSA_REP):
        o_ref[0, 0, r * QBLK:(r + 1) * QBLK, :] = _table_lookup(bucket, tab_ref, g * NSA_REP + r)


def nsa_cmp_bias(table, S, ncp):
    return pl.pallas_call(
        _nsa_cmp_bias_kernel,
        grid=(NSA_KV_HEADS, S // QBLK),
        in_specs=[pl.BlockSpec(memory_space=pltpu.SMEM)],
        out_specs=pl.BlockSpec((1, 1, NSA_REP * QBLK, ncp), lambda g, i: (g, i, 0, 0)),
        out_shape=jax.ShapeDtypeStruct((NSA_KV_HEADS, S // QBLK, NSA_REP * QBLK, ncp), F32),
        compiler_params=_cparams(("parallel", "arbitrary")),
        name="nsa_cmp_bias",
    )(table)


NSA_KT = 2 * QBLK


def _nsa_prompt_kernel(q_ref, kc_ref, vc_ref, ks_ref, vs_ref, kw_ref, vw_ref, gt_ref, tb_ref, cb_ref,
                       o_ref, m_scr, l_scr, acc_scr, sel_scr, *, n_c, scale):
    g = pl.program_id(1)
    i = pl.program_id(2)
    R = NSA_REP
    qs = i * QBLK
    q4 = q_ref[0]
    q = jnp.concatenate([q4[:, r * HEAD_DIM:(r + 1) * HEAD_DIM] for r in range(R)], axis=0)
    qpos1 = qs + lax.broadcasted_iota(jnp.int32, (QBLK, 1), 0)
    qpos = jnp.concatenate([qpos1] * R, axis=0)

    ncp = kc_ref.shape[1]
    s = _dot_t(q, kc_ref[0]) * scale
    n_idx = lax.broadcasted_iota(jnp.int32, (1, ncp), 1)
    mask = (qpos - (n_idx * CMP_STRIDE + (CMP_BLOCK - 1)) >= 0) & (n_idx < n_c)
    s = jnp.where(mask, s + cb_ref[0, 0], NEG_INF)
    m = jnp.max(s, axis=-1, keepdims=True)
    p = jnp.where(mask, jnp.exp(s - m), 0.0)
    l = jnp.sum(p, axis=-1, keepdims=True)
    p = p / jnp.maximum(l, 1.0)
    o_c = _dot(p.astype(BF16), vc_ref[0])
    imp = p[0:QBLK]
    for r in range(1, R):
        imp = imp + p[r * QBLK:(r + 1) * QBLK]
    n_s = ks_ref.shape[1] // SLC_BLOCK
    ratio = SLC_BLOCK // CMP_STRIDE
    ti = lax.broadcasted_iota(jnp.int32, (n_s, ncp), 0)
    ci = lax.broadcasted_iota(jnp.int32, (n_s, ncp), 1)
    gather01 = jnp.where((ci >= ti * ratio - (CMP_HALF - 1)) & (ci < (ti + 1) * ratio), 1.0, 0.0).astype(BF16)
    i1, i2, i3 = _split3(imp)
    score_t = _dot_t(gather01, i1) + _dot_t(gather01, i2) + _dot_t(gather01, i3)
    q_blk = (qs + lax.broadcasted_iota(jnp.int32, (1, QBLK), 1)) // SLC_BLOCK
    sel_scr[...] = _select_blocks_t(score_t, q_blk, min(N_SELECT, n_s)).T.astype(BF16)

    kcol = lax.broadcasted_iota(jnp.int32, (1, NSA_KT), 1)

    def tile_bias(kt):
        parts = [tb_ref[0, jnp.clip(i - (kt * (NSA_KT // QBLK) + t), 0, N_BIAS_DIAG - 1)] for t in range(NSA_KT // QBLK)]
        return jnp.concatenate(parts, axis=1)

    _flash_reset(m_scr, l_scr, acc_scr)

    def slc_body(kt, carry):
        ks_ = pl.multiple_of(kt * NSA_KT, NSA_KT)
        k = ks_ref[0, pl.ds(ks_, NSA_KT), :]
        v = vs_ref[0, pl.ds(ks_, NSA_KT), :]
        dist = qpos - (ks_ + kcol)
        bi = lax.broadcasted_iota(jnp.int32, (n_s, NSA_KT), 0)
        ki = lax.broadcasted_iota(jnp.int32, (n_s, NSA_KT), 1)
        expand = jnp.where(bi == kt * (NSA_KT // SLC_BLOCK) + ki // SLC_BLOCK, 1.0, 0.0).astype(BF16)
        selk = _dot(sel_scr[...], expand)
        selk = jnp.concatenate([selk] * R, axis=0)
        msk = (dist >= 0) & (selk > 0.5)
        sc = _dot_t(q, k) * scale + tile_bias(kt)
        _flash_step(sc, msk, v, m_scr, l_scr, acc_scr)
        return carry

    lax.fori_loop(0, (qs + QBLK - 1) // NSA_KT + 1, slc_body, 0)
    o_s = acc_scr[...] / jnp.maximum(l_scr[...], 1.0)

    _flash_reset(m_scr, l_scr, acc_scr)

    def win_body(kt, carry):
        ks_ = pl.multiple_of(kt * NSA_KT, NSA_KT)
        k = kw_ref[0, pl.ds(ks_, NSA_KT), :]
        v = vw_ref[0, pl.ds(ks_, NSA_KT), :]
        dist = qpos - (ks_ + kcol)
        msk = (dist >= 0) & (dist < NSA_WINDOW)
        sc = _dot_t(q, k) * scale + tile_bias(kt)
        _flash_step(sc, msk, v, m_scr, l_scr, acc_scr)
        return carry

    lax.fori_loop(jnp.maximum(qs - (NSA_WINDOW - 1), 0) // NSA_KT, (qs + QBLK - 1) // NSA_KT + 1, win_body, 0)
    o_w = acc_scr[...] / jnp.maximum(l_scr[...], 1.0)

    gt = gt_ref[0]
    outs = []
    for r in range(R):
        c0 = (g * R + r) * 3
        rows = slice(r * QBLK, (r + 1) * QBLK)
        lane = lax.broadcasted_iota(jnp.int32, (1, LANES), 1)
        gates = [jnp.sum(jnp.where(lane == c0 + t, gt, 0.0), axis=-1, keepdims=True) for t in range(3)]
        outs.append(gates[0] * o_c[rows] + gates[1] * o_s[rows] + gates[2] * o_w[rows])
    o_ref[0] = jnp.concatenate(outs, axis=1).astype(o_ref.dtype)


def nsa_prompt(p16, p32, kc, vc, tb, cb, B, S, *, q_col, sk_col, sv_col, wk_col, wv_col, gate_col):
    G, R = NSA_KV_HEADS, NSA_REP
    assert S % NSA_KT == 0
    n_c = S // CMP_STRIDE - CMP_HALF + 1
    kern = functools.partial(_nsa_prompt_kernel, n_c=n_c, scale=HEAD_DIM ** -0.5)
    hd = HEAD_DIM
    return pl.pallas_call(
        kern,
        grid_spec=pltpu.PrefetchScalarGridSpec(
            num_scalar_prefetch=0,
            grid=(B, G, S // QBLK),
            in_specs=[
                pl.BlockSpec((1, QBLK, R * hd), lambda b, g, i: (b, i, q_col // (R * hd) + g)),
                pl.BlockSpec((1, kc.shape[1], hd), lambda b, g, i: (b, 0, g)),
                pl.BlockSpec((1, vc.shape[1], hd), lambda b, g, i: (b, 0, g)),
                pl.BlockSpec((1, S, hd), lambda b, g, i: (b, 0, sk_col // hd + g)),
                pl.BlockSpec((1, S, hd), lambda b, g, i: (b, 0, sv_col // hd + g)),
                pl.BlockSpec((1, S, hd), lambda b, g, i: (b, 0, wk_col // hd + g)),
                pl.BlockSpec((1, S, hd), lambda b, g, i: (b, 0, wv_col // hd + g)),
                pl.BlockSpec((1, QBLK, LANES), lambda b, g, i: (b, i, gate_col // LANES)),
                pl.BlockSpec((1, N_BIAS_DIAG, R * QBLK, QBLK), lambda b, g, i: (g, 0, 0, 0)),
                pl.BlockSpec((1, 1, R * QBLK, cb.shape[3]), lambda b, g, i: (g, i, 0, 0)),
            ],
            out_specs=pl.BlockSpec((1, QBLK, R * hd), lambda b, g, i: (b, i, g)),
            scratch_shapes=[
                pltpu.VMEM((R * QBLK, LANES), F32),
                pltpu.VMEM((R * QBLK, LANES), F32),
                pltpu.VMEM((R * QBLK, hd), F32),
                pltpu.VMEM((QBLK, S // SLC_BLOCK), BF16),
            ],
        ),
        out_shape=jax.ShapeDtypeStruct((B, S, NSA_QW), BF16),
        compiler_params=_cparams(("parallel", "parallel", "arbitrary")),
        name="nsa_prompt",
    )(p16, kc, vc, p16, p16, p16, p16, p32, tb, cb)


ROUTER_W = LANES
MOE_TM = 256


def _router_kernel(x_ref, g_ref, w1_ref, w2_ref, b_ref, h_ref, info_ref):
    x = x_ref[...]
    hn = x * lax.rsqrt(jnp.mean(x * x, axis=-1, keepdims=True) + RMS_EPS) * g_ref[...]
    h_ref[...] = hn
    a1 = hn.astype(BF16)
    a2 = (hn - a1.astype(F32)).astype(BF16)
    logits = _dot(a1, w1_ref[...]) + _dot(a1, w2_ref[...]) + _dot(a2, w1_ref[...]) + b_ref[...]
    lane = lax.broadcasted_iota(jnp.int32, (1, ROUTER_W), 1)
    big = ROUTER_W

    def first_lane(cond):
        return jnp.min(jnp.where(cond, lane, big), axis=-1, keepdims=True)

    gmask = lane < N_GROUPS
    gl = jnp.where(gmask, logits, NEG_INF)
    gm = jnp.max(gl, axis=-1, keepdims=True)
    g_idx = first_lane(gmask & (gl == gm))
    g_w = 1.0 / jnp.sum(jnp.where(gmask, jnp.exp(gl - gm), 0.0), axis=-1, keepdims=True)
    lo = N_GROUPS + g_idx * EXPERTS_PER_GROUP
    emask = (lane >= lo) & (lane < lo + EXPERTS_PER_GROUP)
    el = jnp.where(emask, logits, NEG_INF)
    em = jnp.max(el, axis=-1, keepdims=True)
    ee = jnp.where(emask, jnp.exp(el - em), 0.0)
    pe = jnp.where(emask, ee / jnp.sum(ee, axis=-1, keepdims=True), -1.0)
    v1 = jnp.max(pe, axis=-1, keepdims=True)
    i1 = first_lane(pe == v1)
    pe2 = jnp.where(lane == i1, -1.0, pe)
    v2 = jnp.max(pe2, axis=-1, keepdims=True)
    i2 = first_lane((pe2 == v2) & emask & (lane != i1))
    tot = v1 + v2
    info = jnp.where(lane == 0, (i1 - N_GROUPS).astype(F32), 0.0)
    info = jnp.where(lane == 1, (i2 - N_GROUPS).astype(F32), info)
    info = jnp.where(lane == 2, g_w * (v1 / tot), info)
    info = jnp.where(lane == 3, g_w * (v2 / tot), info)
    info_ref[...] = info


def moe_router(x, g, w_group, b_group, w_expert, b_expert, *, tm):
    T, D = x.shape
    pad = ROUTER_W - N_GROUPS - N_EXPERTS
    w = jnp.pad(jnp.concatenate([w_group, w_expert], axis=1), ((0, 0), (0, pad)))
    b = jnp.pad(jnp.concatenate([b_group, b_expert]), (0, pad)).reshape(1, ROUTER_W)
    w1 = w.astype(BF16)
    w2 = (w - w1.astype(F32)).astype(BF16)
    return pl.pallas_call(
        _router_kernel,
        grid=(T // tm,),
        in_specs=[
            pl.BlockSpec((tm, D), lambda i: (i, 0)),
            pl.BlockSpec((1, D), lambda i: (0, 0)),
            pl.BlockSpec((D, ROUTER_W), lambda i: (0, 0)),
            pl.BlockSpec((D, ROUTER_W), lambda i: (0, 0)),
            pl.BlockSpec((1, ROUTER_W), lambda i: (0, 0)),
        ],
        out_specs=[pl.BlockSpec((tm, D), lambda i: (i, 0)), pl.BlockSpec((tm, ROUTER_W), lambda i: (i, 0))],
        out_shape=[jax.ShapeDtypeStruct((T, D), F32), jax.ShapeDtypeStruct((T, ROUTER_W), F32)],
        compiler_params=_cparams(("parallel",)),
        name="moe_router",
    )(x, g.reshape(1, D), w1, w2, b)


MOE_FSPLIT = 2


def _moe_ffn_kernel(te_ref, nu_ref, x_ref, wg_ref, wu_ref, wd_ref, *rest, accumulate):
    if accumulate:
        yp_ref, y_ref, wg16, wu16, wd16 = rest
    else:
        (y_ref, wg16, wu16, wd16), yp_ref = rest, None
    i = pl.program_id(0)

    @pl.when((i == 0) | (te_ref[i] != te_ref[jnp.maximum(i - 1, 0)]))
    def _():
        wg16[...] = wg_ref[...].astype(BF16)
        wu16[...] = wu_ref[...].astype(BF16)
        wd16[...] = wd_ref[...].astype(BF16)

    @pl.when(i < nu_ref[0])
    def _():
        for r in range(MOE_TM // MOE_ROW_CHUNK):
            rows = slice(r * MOE_ROW_CHUNK, (r + 1) * MOE_ROW_CHUNK)
            x = x_ref[rows, :].astype(BF16)
            h = _dot(x, wg16[...])
            u = _dot(x, wu16[...])
            y = _dot(((h * jax.nn.sigmoid(h)) * u).astype(BF16), wd16[...])
            y_ref[rows, :] = y + yp_ref[rows, :] if accumulate else y

    @pl.when(i >= nu_ref[0])
    def _():
        y_ref[...] = jnp.zeros(y_ref.shape, F32)


MOE_ROW_CHUNK = 128


def moe_ffn(xs, tile_expert, n_used, w_gate, w_up, w_down, layer):
    M, D = xs.shape
    F = w_gate.shape[3]
    assert F % MOE_FSPLIT == 0
    Fh = F // MOE_FSPLIT
    n_tiles = M // MOE_TM
    y = None
    for f in range(MOE_FSPLIT):
        acc = f > 0
        in_specs = [
            pl.BlockSpec((MOE_TM, D), lambda i, te, nu: (i, 0)),
            pl.BlockSpec((None, None, D, Fh), lambda i, te, nu, f=f: (layer, te[i], 0, f)),
            pl.BlockSpec((None, None, D, Fh), lambda i, te, nu, f=f: (layer, te[i], 0, f)),
            pl.BlockSpec((None, None, Fh, D), lambda i, te, nu, f=f: (layer, te[i], f, 0)),
        ]
        args = [xs, w_gate, w_up, w_down]
        if acc:
            in_specs.append(pl.BlockSpec((MOE_TM, D), lambda i, te, nu: (i, 0)))
            args.append(y)
        y = pl.pallas_call(
            functools.partial(_moe_ffn_kernel, accumulate=acc),
            grid_spec=pltpu.PrefetchScalarGridSpec(
                num_scalar_prefetch=2,
                grid=(n_tiles,),
                in_specs=in_specs,
                out_specs=pl.BlockSpec((MOE_TM, D), lambda i, te, nu: (i, 0)),
                scratch_shapes=[pltpu.VMEM((D, Fh), BF16), pltpu.VMEM((D, Fh), BF16), pltpu.VMEM((Fh, D), BF16)],
            ),
            out_shape=jax.ShapeDtypeStruct((M, D), F32),
            input_output_aliases={6: 0} if acc else {},
            compiler_params=_cparams(("arbitrary",)),
            name=f"moe_ffn_{f}",
        )(tile_expert, n_used, *args)
    return y


def moe_plan(info):
    T = info.shape[0]
    eid = info[:, :TOP_K_INNER].astype(jnp.int32).reshape(-1)
    M = T * TOP_K_INNER
    n_tiles = -(-(M + N_EXPERTS * (MOE_TM - 1)) // MOE_TM)
    onehot = (eid[:, None] == jnp.arange(N_EXPERTS)[None, :]).astype(jnp.int32)
    rank = jnp.cumsum(onehot, axis=0) - onehot
    counts = jnp.sum(onehot, axis=0)
    tiles_per = (counts + MOE_TM - 1) // MOE_TM
    tile_end = jnp.cumsum(tiles_per)
    offs = (tile_end - tiles_per) * MOE_TM
    dest = jnp.sum(onehot * (offs[None, :] + rank), axis=1).astype(jnp.int32)
    n_used = tile_end[-1]
    tile_ids = jnp.arange(n_tiles)
    tile_expert = jnp.minimum(jnp.sum(tile_ids[:, None] >= tile_end[None, :], axis=1), N_EXPERTS - 1)
    last_e = jnp.max(jnp.where(counts > 0, jnp.arange(N_EXPERTS), 0))
    tile_expert = jnp.where(tile_ids < n_used, tile_expert, last_e).astype(jnp.int32)
    return dest.reshape(T, TOP_K_INNER), tile_expert, n_used.reshape(1).astype(jnp.int32), n_tiles * MOE_TM


def _row_copy(src_ref, s, dst_ref, d, sem):
    return pltpu.make_async_copy(src_ref.at[pl.ds(s, 1), :], dst_ref.at[pl.ds(d, 1), :], sem)


def _moe_scatter_kernel(dest_ref, h_ref, xs_in_ref, xs_ref, sem):
    del xs_in_ref
    tm = h_ref.shape[0]

    def issue(r, c):
        for slot in range(TOP_K_INNER):
            _row_copy(h_ref, r, xs_ref, dest_ref[0, 0, TOP_K_INNER * r + slot], sem).start()
        return c

    lax.fori_loop(0, tm, issue, 0)

    def drain(r, c):
        for slot in range(TOP_K_INNER):
            _row_copy(h_ref, 0, xs_ref, 0, sem).wait()
        return c

    lax.fori_loop(0, tm, drain, 0)


def moe_scatter(h, dest, xs, *, tm):
    T, D = h.shape
    assert T % tm == 0
    return pl.pallas_call(
        _moe_scatter_kernel,
        grid=(T // tm,),
        in_specs=[
            pl.BlockSpec((1, 1, TOP_K_INNER * tm), lambda i: (i, 0, 0), memory_space=pltpu.SMEM),
            pl.BlockSpec((tm, D), lambda i: (i, 0)),
            pl.BlockSpec(memory_space=pl.ANY),
        ],
        out_specs=pl.BlockSpec(memory_space=pl.ANY),
        out_shape=jax.ShapeDtypeStruct(xs.shape, xs.dtype),
        scratch_shapes=[pltpu.SemaphoreType.DMA(())],
        input_output_aliases={2: 0},
        compiler_params=_cparams(("arbitrary",)),
        name="moe_scatter",
    )(dest.reshape(T // tm, 1, TOP_K_INNER * tm), h, xs)


def _moe_combine_kernel(dest_ref, x_ref, info_ref, y_ref, o_ref, buf, sem):
    tm = x_ref.shape[0]

    def issue(r, c):
        for slot in range(TOP_K_INNER):
            _row_copy(y_ref, dest_ref[0, 0, TOP_K_INNER * r + slot], buf.at[slot], r, sem).start()
        return c

    lax.fori_loop(0, tm, issue, 0)

    def drain(r, c):
        for slot in range(TOP_K_INNER):
            _row_copy(y_ref, 0, buf.at[slot], 0, sem).wait()
        return c

    lax.fori_loop(0, tm, drain, 0)
    out = x_ref[...]
    for slot in range(TOP_K_INNER):
        out = out + info_ref[:, TOP_K_INNER + slot:TOP_K_INNER + slot + 1] * buf[slot]
    o_ref[...] = out


def moe_combine(x, info, dest, y, *, tm):
    T, D = x.shape
    assert T % tm == 0
    return pl.pallas_call(
        _moe_combine_kernel,
        grid=(T // tm,),
        in_specs=[
            pl.BlockSpec((1, 1, TOP_K_INNER * tm), lambda i: (i, 0, 0), memory_space=pltpu.SMEM),
            pl.BlockSpec((tm, D), lambda i: (i, 0)),
            pl.BlockSpec((tm, ROUTER_W), lambda i: (i, 0)),
            pl.BlockSpec(memory_space=pl.ANY),
        ],
        out_specs=pl.BlockSpec((tm, D), lambda i: (i, 0)),
        out_shape=jax.ShapeDtypeStruct((T, D), F32),
        scratch_shapes=[pltpu.VMEM((TOP_K_INNER, tm, D), F32), pltpu.SemaphoreType.DMA(())],
        compiler_params=_cparams(("arbitrary",)),
        name="moe_combine",
    )(dest.reshape(T // tm, 1, TOP_K_INNER * tm), x, info, y)


MOE_ROW_TILE = 256


def moe_layer(xp, xs, g, w_group, b_group, w_expert, b_expert, w_gate, w_up, w_down, layer):
    Tp, Ts = xp.shape[0], xs.shape[0]
    hp, ip = moe_router(xp, g, w_group, b_group, w_expert, b_expert, tm=512)
    hs, is_ = moe_router(xs, g, w_group, b_group, w_expert, b_expert, tm=Ts)
    dest, tile_expert, n_used, m_rows = moe_plan(jnp.concatenate([ip, is_], axis=0))
    rows = jnp.zeros((m_rows, xp.shape[1]), F32)
    rows = moe_scatter(hp, dest[:Tp], rows, tm=MOE_ROW_TILE)
    rows = moe_scatter(hs, dest[Tp:], rows, tm=Ts)
    y = moe_ffn(rows, tile_expert, n_used, w_gate, w_up, w_down, layer)
    return (moe_combine(xp, ip, dest[:Tp], y, tm=MOE_ROW_TILE), moe_combine(xs, is_, dest[Tp:], y, tm=Ts))


N_DIL = len(DIL_PATTERNS)
DIL_SPANS = tuple(w // d for w, d in DIL_PATTERNS)
DIL_STEPS = tuple(d for _, d in DIL_PATTERNS)
assert all(s <= QBLK for s in DIL_SPANS)
assert all(d & (d - 1) == 0 for d in DIL_STEPS)


def _dil_bias_kernel(dil_ref, tab_ref, o_ref):
    p = pl.program_id(0)
    dm = QBLK + lax.broadcasted_iota(jnp.int32, (QBLK, 2 * QBLK), 0) - lax.broadcasted_iota(jnp.int32, (QBLK, 2 * QBLK), 1)
    bucket = _t5_bucket(dm * dil_ref[p])
    for h in range(DIL_HEADS):
        o_ref[0, h] = _table_lookup(bucket, tab_ref, h)


def dil_bias_tiles(table):
    return pl.pallas_call(
        _dil_bias_kernel,
        grid=(N_DIL,),
        in_specs=[pl.BlockSpec(memory_space=pltpu.SMEM), pl.BlockSpec(memory_space=pltpu.SMEM)],
        out_specs=pl.BlockSpec((1, DIL_HEADS, QBLK, 2 * QBLK), lambda p: (p, 0, 0, 0)),
        out_shape=jax.ShapeDtypeStruct((N_DIL, DIL_HEADS, QBLK, 2 * QBLK), F32),
        compiler_params=_cparams(("arbitrary",)),
        name="dil_bias_tiles",
    )(jnp.array(DIL_STEPS, jnp.int32), table)


def _dil_prompt_kernel(q_ref, kp_ref, kc_ref, vp_ref, vc_ref, b_ref, num_ref, ml_ref, *, span, scale):
    mb = pl.program_id(2)
    qi = lax.broadcasted_iota(jnp.int32, (QBLK, 2 * QBLK), 0)
    kj = lax.broadcasted_iota(jnp.int32, (QBLK, 2 * QBLK), 1)
    dm = QBLK + qi - kj
    mask = (dm >= 0) & (dm <= span) & ((kj >= QBLK) | (mb > 0))
    lane = lax.broadcasted_iota(jnp.int32, (1, LANES), 1)
    m_t = jnp.zeros((QBLK, LANES), F32)
    l_t = jnp.zeros((QBLK, LANES), F32)
    for h in range(DIL_HEADS):
        cs = slice(h * HEAD_DIM, (h + 1) * HEAD_DIM)
        k = jnp.concatenate([kp_ref[0, :, cs], kc_ref[0, :, cs]], axis=0)
        v = jnp.concatenate([vp_ref[0, :, cs], vc_ref[0, :, cs]], axis=0)
        s = _dot_t(q_ref[0, :, cs], k) * scale + b_ref[0, h]
        s = jnp.where(mask, s, NEG_INF)
        m = jnp.max(s, axis=-1, keepdims=True)
        p = jnp.where(mask, jnp.exp(s - m), 0.0)
        l = jnp.sum(p, axis=-1, keepdims=True)
        num_ref[0, :, cs] = _dot(p.astype(BF16), v)
        m_t = jnp.where(lane == h, m, m_t)
        l_t = jnp.where(lane == h, l, l_t)
    ml_ref[0, :, 0:LANES] = m_t
    ml_ref[0, :, LANES:2 * LANES] = l_t


def dil_prompt_pattern(p16, bias, pat, B, S):
    span, dil = DIL_SPANS[pat], DIL_STEPS[pat]
    M = S // dil
    assert M % QBLK == 0
    pv = p16.reshape(B, M, dil * 3 * DIL_W)
    kern = functools.partial(_dil_prompt_kernel, span=span, scale=HEAD_DIM ** -0.5)
    blk = (1, QBLK, DIL_W)
    prev = lambda m: jnp.maximum(m - 1, 0)
    num, ml = pl.pallas_call(
        kern,
        grid=(B, dil, M // QBLK),
        in_specs=[
            pl.BlockSpec(blk, lambda b, r, m: (b, m, 3 * r)),
            pl.BlockSpec(blk, lambda b, r, m: (b, prev(m), 3 * r + 1)),
            pl.BlockSpec(blk, lambda b, r, m: (b, m, 3 * r + 1)),
            pl.BlockSpec(blk, lambda b, r, m: (b, prev(m), 3 * r + 2)),
            pl.BlockSpec(blk, lambda b, r, m: (b, m, 3 * r + 2)),
            pl.BlockSpec((1, DIL_HEADS, QBLK, 2 * QBLK), lambda b, r, m: (pat, 0, 0, 0)),
        ],
        out_specs=[
            pl.BlockSpec(blk, lambda b, r, m: (b, m, r)),
            pl.BlockSpec((1, QBLK, 2 * LANES), lambda b, r, m: (b, m, r)),
        ],
        out_shape=[jax.ShapeDtypeStruct((B, M, dil * DIL_W), F32), jax.ShapeDtypeStruct((B, M, dil * 2 * LANES), F32)],
        compiler_params=_cparams(("parallel", "parallel", "arbitrary")),
        name=f"dil_prompt_{dil}",
    )(pv, pv, pv, pv, pv, bias)
    return num.reshape(B * S, DIL_W), ml.reshape(B * S, 2 * LANES)


def _dil_merge_kernel(*refs):
    nums, mls, o_ref = refs[:N_DIL], refs[N_DIL:2 * N_DIL], refs[2 * N_DIL]
    m_all = [r[:, 0:LANES] for r in mls]
    l_all = [r[:, LANES:2 * LANES] for r in mls]
    m_max = m_all[0]
    for m in m_all[1:]:
        m_max = jnp.maximum(m_max, m)
    w_all = [jnp.exp(m - m_max) for m in m_all]
    den = w_all[0] * l_all[0]
    for w, l in zip(w_all[1:], l_all[1:]):
        den = den + w * l
    for h in range(DIL_HEADS):
        cs = slice(h * HEAD_DIM, (h + 1) * HEAD_DIM)
        acc = w_all[0][:, h:h + 1] * nums[0][:, cs]
        for w, n in zip(w_all[1:], nums[1:]):
            acc = acc + w[:, h:h + 1] * n[:, cs]
        o_ref[:, cs] = (acc / den[:, h:h + 1]).astype(o_ref.dtype)


def dil_merge(nums, mls, *, tm):
    T = nums[0].shape[0]
    return pl.pallas_call(
        _dil_merge_kernel,
        grid=(T // tm,),
        in_specs=[pl.BlockSpec((tm, DIL_W), lambda i: (i, 0))] * N_DIL + [pl.BlockSpec((tm, 2 * LANES), lambda i: (i, 0))] * N_DIL,
        out_specs=pl.BlockSpec((tm, DIL_W), lambda i: (i, 0)),
        out_shape=jax.ShapeDtypeStruct((T, DIL_W), BF16),
        compiler_params=_cparams(("parallel",)),
        name="dil_merge",
    )(*nums, *mls)


def _block_diag_rows(q, n_heads):
    DB, Q, W = q.shape
    dh = W // n_heads
    eye = jnp.repeat(jnp.eye(n_heads, dtype=q.dtype), dh, axis=1)
    return (q[:, None, :, :] * eye[None, :, None, :]).reshape(DB, n_heads * Q, W)


def _own_block(acc, n_heads, rows_per_head):
    rows = acc.shape[0]
    rh = lax.broadcasted_iota(jnp.int32, (rows, 1), 0) // rows_per_head
    out = jnp.zeros((rows, HEAD_DIM), F32)
    for h in range(n_heads):
        out = jnp.where(rh == h, acc[:, h * HEAD_DIM:(h + 1) * HEAD_DIM], out)
    return out


def _row_bias(bucket, tab_ref, head0, n_heads, rows_per_head):
    rows = bucket.shape[0]
    rh = lax.broadcasted_iota(jnp.int32, (rows, 1), 0) // rows_per_head
    out = jnp.zeros(bucket.shape, F32)
    for h in range(n_heads):
        out = jnp.where(rh == h, _table_lookup(bucket, tab_ref, head0 + h), out)
    return out


def _flash_step_cols(s, mask, v, m_scr, l_scr, acc_scr):
    s = jnp.where(mask, s, NEG_INF)
    m_prev = m_scr[:, 0:1]
    m_new = jnp.maximum(m_prev, jnp.max(s, axis=-1, keepdims=True))
    p = jnp.where(mask, jnp.exp(s - m_new), 0.0)
    alpha = jnp.exp(m_prev - m_new)
    l_new = alpha * l_scr[:, 0:1] + jnp.sum(p, axis=-1, keepdims=True)
    acc_scr[...] = alpha * acc_scr[...] + _dot(p.astype(BF16), v)
    m_scr[...] = jnp.broadcast_to(m_new, m_scr.shape)
    l_scr[...] = jnp.broadcast_to(l_new, l_scr.shape)


def _head_cols(ref, first, n_heads, n_rows, stride, row0=0):
    return jnp.concatenate(
        [ref[pl.ds(row0 * stride + first + h, n_rows, stride=stride), :] for h in range(n_heads)], axis=1)


SB_DEC_PAGES = 4


def _sb_decode_kernel(pt_ref, q_ref, new_ref, *rest, n_steps, n_q, scale):
    pages, (o_ref, acc_scr, run_scr) = rest[:SB_DEC_PAGES], rest[SB_DEC_PAGES:]
    s_ = pl.program_id(1)
    q = q_ref[0]
    rows = q.shape[0]
    upper2 = _upper01(PAGE_SIZE)

    def absorb(k, v, mask):
        z2 = _dot_t(q, k.astype(BF16)) * (scale * LOG2_E)
        lk = _log2_keep(z2)
        lb = z2 + lk
        if mask is not None:
            lk = jnp.where(mask, lk, 0.0)
        w = jnp.exp2(lb + _after_in_tile(lk, upper2) + run_scr[...])
        if mask is not None:
            w = jnp.where(mask, w, 0.0)
        acc_scr[...] = acc_scr[...] + _dot(w.astype(BF16), v.astype(BF16))
        run_scr[...] = run_scr[...] + jnp.sum(lk, axis=-1, keepdims=True)

    @pl.when(s_ == 0)
    def _():
        acc_scr[...] = jnp.zeros(acc_scr.shape, F32)
        run_scr[...] = jnp.zeros(run_scr.shape, F32)
        t = lax.broadcasted_iota(jnp.int32, (rows, PAGE_SIZE), 1)
        qi = lax.broadcasted_iota(jnp.int32, (rows, PAGE_SIZE), 0) % n_q
        absorb(new_ref[0, :, :SB_W], new_ref[0, :, SB_W:], (t < qi) & (t < n_q))

    for page in pages:
        absorb(_head_cols(page, 0, SB_HEADS, PAGE_SIZE, 2 * SB_HEADS),
               _head_cols(page, SB_HEADS, SB_HEADS, PAGE_SIZE, 2 * SB_HEADS), None)

    @pl.when(s_ == n_steps - 1)
    def _():
        o_ref[0] = _own_block(acc_scr[...], SB_HEADS, n_q)


def sb_decode(page_table, qbd16, new_kv, cache_lines):
    DB, n_pages = page_table.shape
    assert n_pages % SB_DEC_PAGES == 0
    n_steps = n_pages // SB_DEC_PAGES
    rows = qbd16.shape[1]
    n_q = rows // SB_HEADS
    lines = PAGE_SIZE * 2 * SB_HEADS
    kern = functools.partial(_sb_decode_kernel, n_steps=n_steps, n_q=n_q, scale=HEAD_DIM ** -0.5)
    page_spec = lambda j: pl.BlockSpec(
        (lines, HEAD_DIM), lambda b, s, pt: (pt[b, n_pages - 1 - (s * SB_DEC_PAGES + j)], 0))
    return pl.pallas_call(
        kern,
        grid_spec=pltpu.PrefetchScalarGridSpec(
            num_scalar_prefetch=1,
            grid=(DB, n_steps),
            in_specs=[
                pl.BlockSpec((1, rows, SB_W), lambda b, s, pt: (b, 0, 0)),
                pl.BlockSpec((1, PAGE_SIZE, 2 * SB_W), lambda b, s, pt: (b, 0, 0)),
            ] + [page_spec(j) for j in range(SB_DEC_PAGES)],
            out_specs=pl.BlockSpec((1, rows, HEAD_DIM), lambda b, s, pt: (b, 0, 0)),
            scratch_shapes=[pltpu.VMEM((rows, SB_W), F32), pltpu.VMEM((rows, LANES), F32)],
        ),
        out_shape=jax.ShapeDtypeStruct((DB, rows, HEAD_DIM), F32),
        compiler_params=_cparams(("parallel", "arbitrary")),
        name="sb_decode",
    )(page_table, qbd16, new_kv, *([cache_lines] * SB_DEC_PAGES))


CMP_ROWS_TILE = 256
CHUNKS_PER_PAGE = PAGE_SIZE // CMP_STRIDE


CMP_DEC_PAGES = 8


def _nsa_cmp_decode_kernel(pt_ref, tab_ref, q_ref, w_ref, pe_ref, g_ref, *rest, n_steps, n_q, past, scale):
    pages, (oc_ref, sel_ref, x_scr, a_scr) = rest[:CMP_DEC_PAGES], rest[CMP_DEC_PAGES:]
    s_ = pl.program_id(1)
    G, R = NSA_KV_HEADS, NSA_REP
    slots = 2 * G
    n_stage = CMP_DEC_PAGES * CHUNKS_PER_PAGE
    for kvg in range(slots):
        xs = jnp.concatenate(
            [jnp.concatenate([pg[pl.ds(l * slots + kvg, CHUNKS_PER_PAGE, stride=CMP_STRIDE * slots), :]
                              for pg in pages], axis=0) for l in range(CMP_STRIDE)], axis=1)
        x_scr[kvg, pl.ds(pl.multiple_of(s_ * n_stage, n_stage), n_stage), :] = xs.astype(BF16)

    @pl.when(s_ == n_steps - 1)
    def _():
        n_chunk = x_scr.shape[1]
        n_c = n_chunk - CMP_HALF + 1
        for rt in range(n_chunk // CMP_ROWS_TILE):
            rs = slice(rt * CMP_ROWS_TILE, (rt + 1) * CMP_ROWS_TILE)
            for kv in range(2):
                for g in range(G):
                    off = kv * NSA_KVW + g * HEAD_DIM
                    xs = x_scr[kv * G + g, rs, :]
                    for j in range(CMP_HALF):
                        w = w_ref[kv, j]
                        pe = jnp.broadcast_to(pe_ref[kv, j], (SUBLANES, CMP_STRIDE * HEAD_DIM)).astype(BF16)
                        a_scr[j, rs, off:off + HEAD_DIM] = _dot(xs, w) + _dot(pe, w)[0:1, :]
        out = a_scr[0]
        for j in range(1, CMP_HALF):
            out = out + pltpu.roll(a_scr[j], n_chunk - j, axis=0)
        kparts = []
        for g in range(G):
            kg = out[:, g * HEAD_DIM:(g + 1) * HEAD_DIM]
            kparts.append(kg * lax.rsqrt(jnp.mean(kg * kg, axis=-1, keepdims=True) + RMS_EPS) * g_ref[...])
        kc = jnp.concatenate(kparts, axis=1).astype(BF16)
        vc = out[:, NSA_KVW:2 * NSA_KVW].astype(BF16)

        q = q_ref[0]
        nrows = q.shape[0]
        qi = lax.broadcasted_iota(jnp.int32, (nrows, 1), 0) % n_q
        n_idx = lax.broadcasted_iota(jnp.int32, (1, n_chunk), 1)
        dist = past + qi - (n_idx * CMP_STRIDE + (CMP_BLOCK - 1))
        mask = (dist >= 0) & (n_idx < n_c)
        s = _dot_t(q, kc) * scale + _row_bias(_t5_bucket(dist), tab_ref, 0, NSA_HEADS, n_q)
        s = jnp.where(mask, s, NEG_INF)
        m = jnp.max(s, axis=-1, keepdims=True)
        p = jnp.where(mask, jnp.exp(s - m), 0.0)
        p = p / jnp.maximum(jnp.sum(p, axis=-1, keepdims=True), 1.0)
        oc_ref[0] = _own_block(_dot(p.astype(BF16), vc), G, R * n_q)
        gq = G * n_q
        oi = lax.broadcasted_iota(jnp.int32, (gq, nrows), 0)
        ri = lax.broadcasted_iota(jnp.int32, (gq, nrows), 1)
        pick = jnp.where((ri // (R * n_q) == oi // n_q) & (ri % n_q == oi % n_q), 1.0, 0.0).astype(BF16)
        p1, p2, p3 = _split3(p)
        imp = _dot(pick, p1) + _dot(pick, p2) + _dot(pick, p3)
        n_sp = sel_ref.shape[2]
        ratio = SLC_BLOCK // CMP_STRIDE
        ci = lax.broadcasted_iota(jnp.int32, (n_chunk, n_sp), 0)
        ti = lax.broadcasted_iota(jnp.int32, (n_chunk, n_sp), 1)
        gather01 = jnp.where((ci >= ti * ratio - (CMP_HALF - 1)) & (ci < (ti + 1) * ratio), 1.0, 0.0).astype(BF16)
        score = _dot01(imp, gather01)
        q_blk = (past + lax.broadcasted_iota(jnp.int32, (gq, 1), 0) % n_q) // SLC_BLOCK
        sel_ref[0] = _select_blocks(score, q_blk, N_SELECT)


def nsa_cmp_decode(page_table, table, qbd16, cache_lines, w16, pe, ck_gain, n_q):
    DB, n_pages = page_table.shape
    assert CMP_DEC_PAGES % 2 == 0 and n_pages % CMP_DEC_PAGES == 0
    n_steps = n_pages // CMP_DEC_PAGES
    n_chunk = n_pages * CHUNKS_PER_PAGE
    assert n_chunk % CMP_ROWS_TILE == 0
    past = n_pages * PAGE_SIZE
    n_s = -(-(past + n_q) // SLC_BLOCK)
    n_sp = -(-n_s // LANES) * LANES
    rows = qbd16.shape[1]
    kern = functools.partial(_nsa_cmp_decode_kernel, n_steps=n_steps, n_q=n_q, past=past, scale=HEAD_DIM ** -0.5)
    pg = (PAGE_SIZE * 2 * NSA_KV_HEADS, HEAD_DIM)
    return pl.pallas_call(
        kern,
        grid_spec=pltpu.PrefetchScalarGridSpec(
            num_scalar_prefetch=1,
            grid=(DB, n_steps),
            in_specs=[
                pl.BlockSpec(memory_space=pltpu.SMEM),
                pl.BlockSpec((1, rows, NSA_KVW), lambda b, s, pt: (b, 0, 0)),
                pl.BlockSpec((2, CMP_HALF, CMP_STRIDE * HEAD_DIM, HEAD_DIM), lambda b, s, pt: (0, 0, 0, 0)),
                pl.BlockSpec((2, CMP_HALF, 1, CMP_STRIDE * HEAD_DIM), lambda b, s, pt: (0, 0, 0, 0)),
                pl.BlockSpec((1, HEAD_DIM), lambda b, s, pt: (0, 0)),
            ] + [pl.BlockSpec(pg, lambda b, s, pt, j=j: (pt[b, CMP_DEC_PAGES * s + j], 0)) for j in range(CMP_DEC_PAGES)],
            out_specs=[
                pl.BlockSpec((1, rows, HEAD_DIM), lambda b, s, pt: (b, 0, 0)),
                pl.BlockSpec((1, NSA_KV_HEADS * n_q, n_sp), lambda b, s, pt: (b, 0, 0)),
            ],
            scratch_shapes=[
                pltpu.VMEM((2 * NSA_KV_HEADS, n_chunk, CMP_STRIDE * HEAD_DIM), BF16),
                pltpu.VMEM((CMP_HALF, n_chunk, 2 * NSA_KVW), F32),
            ],
        ),
        out_shape=[jax.ShapeDtypeStruct((DB, rows, HEAD_DIM), F32),
                   jax.ShapeDtypeStruct((DB, NSA_KV_HEADS * n_q, n_sp), F32)],
        compiler_params=_cparams(("parallel", "arbitrary")),
        name="nsa_cmp_decode",
    )(page_table, table, qbd16, w16, pe, ck_gain.reshape(1, HEAD_DIM), *([cache_lines] * CMP_DEC_PAGES))


SLC_DEC_PAGES = 4


def _nsa_slc_decode_kernel(pt_ref, plist_ref, pcnt_ref, tab_ref, q_ref, sel_ref, new_ref, wbuf_ref, wnew_ref, oc_ref,
                           gt_ref, *rest, n_steps, n_q, past, scale):
    pages, (o_ref, m_scr, l_scr, acc_scr) = rest[:SLC_DEC_PAGES], rest[SLC_DEC_PAGES:]
    b_ = pl.program_id(0)
    s_ = pl.program_id(1)
    q = q_ref[0]
    rows = q.shape[0]
    G, R = NSA_KV_HEADS, NSA_REP
    slots = 2 * G
    qi = lax.broadcasted_iota(jnp.int32, (rows, 1), 0) % n_q
    rh = lax.broadcasted_iota(jnp.int32, (rows, 1), 0) // n_q
    per_page = PAGE_SIZE // SLC_BLOCK
    n_sp = sel_ref.shape[2]
    col = lax.broadcasted_iota(jnp.int32, (1, PAGE_SIZE), 1)

    def sel_cols(first_block):
        bi = lax.broadcasted_iota(jnp.int32, (n_sp, PAGE_SIZE), 0)
        ki = lax.broadcasted_iota(jnp.int32, (n_sp, PAGE_SIZE), 1)
        expand = jnp.where(bi == first_block + ki // SLC_BLOCK, 1.0, 0.0).astype(BF16)
        return _dot(sel_ref[0], expand) > 0.5

    def near_bias(dist):
        return _row_bias(_t5_bucket(dist), tab_ref, 0, NSA_HEADS, n_q)

    def far_bias():
        out = jnp.zeros((rows, 1), F32)
        for h in range(NSA_HEADS):
            out = jnp.where(rh == h, tab_ref[N_BUCKETS - 1, h], out)
        return jnp.broadcast_to(out, (rows, PAGE_SIZE))

    def attend(k, v, bias, mask):
        s = _dot_t(q, k.astype(BF16)) * scale + bias
        _flash_step_cols(s, mask, v.astype(BF16), m_scr, l_scr, acc_scr)

    def result():
        return _own_block(acc_scr[...] / jnp.maximum(l_scr[:, 0:1], 1.0), G, R * n_q)

    @pl.when(s_ == 0)
    def _():
        _flash_reset(m_scr, l_scr, acc_scr)

    for j, page in enumerate(pages):
        idx = s_ * SLC_DEC_PAGES + j

        @pl.when(idx < pcnt_ref[b_])
        def _(page=page, idx=idx):
            p = plist_ref[b_, idx]
            dist = past + qi - (p * PAGE_SIZE + col)
            bias = lax.cond(past - (p + 1) * PAGE_SIZE + 1 >= BUCKET_MAX_DIST, far_bias, lambda: near_bias(dist))
            attend(_head_cols(page, 0, G, PAGE_SIZE, slots), _head_cols(page, G, G, PAGE_SIZE, slots), bias,
                   sel_cols(p * per_page) & (dist >= 0))

    @pl.when(s_ == n_steps - 1)
    def _():
        dist_n = qi - col
        attend(new_ref[0, :, :NSA_KVW], new_ref[0, :, NSA_KVW:], near_bias(dist_n),
               sel_cols(past // SLC_BLOCK) & (dist_n >= 0) & (col < n_q))
        o_s = result()
        _flash_reset(m_scr, l_scr, acc_scr)
        wb = wbuf_ref.shape[0] // slots
        for t in range(wb // PAGE_SIZE):
            dist_w = wb + qi - (t * PAGE_SIZE + col)
            attend(_head_cols(wbuf_ref, 0, G, PAGE_SIZE, slots, row0=t * PAGE_SIZE),
                   _head_cols(wbuf_ref, G, G, PAGE_SIZE, slots, row0=t * PAGE_SIZE), near_bias(dist_w),
                   (dist_w >= 0) & (dist_w < NSA_WINDOW))
        attend(wnew_ref[0, :, :NSA_KVW], wnew_ref[0, :, NSA_KVW:], near_bias(dist_n), (dist_n >= 0) & (col < n_q))
        o_w = result()
        gt = gt_ref[0]
        o_ref[0] = gt[:, 0:1] * oc_ref[0] + gt[:, 1:2] * o_s + gt[:, 2:3] * o_w


def nsa_slc_decode(page_table, table, qbd16, sel16, cache_lines, new_kv, win_lines, win_new, o_c, gates, n_q):
    DB, n_pages = page_table.shape
    assert n_pages % SLC_DEC_PAGES == 0
    n_steps = n_pages // SLC_DEC_PAGES
    rows = qbd16.shape[1]
    past = n_pages * PAGE_SIZE
    slots = 2 * NSA_KV_HEADS
    wb_lines = win_lines.shape[0] // DB
    assert (wb_lines // slots) % PAGE_SIZE == 0
    kern = functools.partial(_nsa_slc_decode_kernel, n_steps=n_steps, n_q=n_q, past=past, scale=HEAD_DIM ** -0.5)
    per_page = PAGE_SIZE // SLC_BLOCK
    need = jnp.any(sel16[:, :, :n_pages * per_page].reshape(DB, rows, n_pages, per_page) > 0, axis=(1, 3))
    plist = jnp.argsort(jnp.logical_not(need), axis=1, stable=True).astype(jnp.int32)
    pcnt = jnp.maximum(jnp.sum(need, axis=1), 1).astype(jnp.int32)
    full = lambda shape: pl.BlockSpec((1,) + shape, lambda b, s, pt, pls, pc: (b, 0, 0))

    def page_spec(j):
        def index(b, s, pt, pls, pc):
            return (pt[b, pls[b, jnp.minimum(s * SLC_DEC_PAGES + j, pc[b] - 1)]], 0)
        return pl.BlockSpec((PAGE_SIZE * slots, HEAD_DIM), index)

    return pl.pallas_call(
        kern,
        grid_spec=pltpu.PrefetchScalarGridSpec(
            num_scalar_prefetch=3,
            grid=(DB, n_steps),
            in_specs=[
                pl.BlockSpec(memory_space=pltpu.SMEM),
                full((rows, NSA_KVW)),
                full((rows, sel16.shape[2])),
                full((PAGE_SIZE, 2 * NSA_KVW)),
                pl.BlockSpec((wb_lines, HEAD_DIM), lambda b, s, pt, pls, pc: (b, 0)),
                full((PAGE_SIZE, 2 * NSA_KVW)),
                full((rows, HEAD_DIM)),
                full((rows, LANES)),
            ] + [page_spec(j) for j in range(SLC_DEC_PAGES)],
            out_specs=full((rows, HEAD_DIM)),
            scratch_shapes=[pltpu.VMEM((rows, LANES), F32), pltpu.VMEM((rows, LANES), F32),
                            pltpu.VMEM((rows, NSA_KVW), F32)],
        ),
        out_shape=jax.ShapeDtypeStruct((DB, rows, HEAD_DIM), F32),
        compiler_params=_cparams(("parallel", "arbitrary")),
        name="nsa_slc_decode",
    )(page_table, plist, pcnt, table, qbd16, sel16, new_kv, win_lines, win_new, o_c, gates,
      *([cache_lines] * SLC_DEC_PAGES))


DIL_DEC_TILE = 256


def _dil_decode_kernel(tab_ref, q_ref, buf_ref, new_ref, o_ref, m_scr, l_scr, acc_scr, *, n_tiles, n_q, wb, scale):
    kt = pl.program_id(1)
    q = q_ref[0]
    rows = q.shape[0]
    qi = lax.broadcasted_iota(jnp.int32, (rows, 1), 0) % n_q

    @pl.when(kt == 0)
    def _():
        _flash_reset(m_scr, l_scr, acc_scr)

    def attend(k, v, dist, valid):
        v = v.astype(BF16)
        s = _dot_t(q, k.astype(BF16)) * scale + _row_bias(_t5_bucket(dist), tab_ref, 0, DIL_HEADS, n_q)
        ps = []
        for i, (span, dil) in enumerate(zip(DIL_SPANS, DIL_STEPS)):
            mask = valid & (dist >= 0) & (jnp.bitwise_and(dist, dil - 1) == 0) & (dist <= span * dil)
            rs = slice(i * rows, (i + 1) * rows)
            si = jnp.where(mask, s, NEG_INF)
            m_prev = m_scr[rs, 0:1]
            m_new = jnp.maximum(m_prev, jnp.max(si, axis=-1, keepdims=True))
            p = jnp.where(mask, jnp.exp(si - m_new), 0.0)
            alpha = jnp.exp(m_prev - m_new)
            l_scr[rs, :] = jnp.broadcast_to(alpha * l_scr[rs, 0:1] + jnp.sum(p, axis=-1, keepdims=True), (rows, LANES))
            m_scr[rs, :] = jnp.broadcast_to(m_new, (rows, LANES))
            acc_scr[rs, :] = alpha * acc_scr[rs, :]
            ps.append(p.astype(BF16))
        acc_scr[...] = acc_scr[...] + _dot(jnp.concatenate(ps, axis=0), v)

    n = DIL_DEC_TILE
    slots = 2 * DIL_HEADS
    col = lax.broadcasted_iota(jnp.int32, (1, n), 1)
    attend(_head_cols(buf_ref, 0, DIL_HEADS, n, slots), _head_cols(buf_ref, DIL_HEADS, DIL_HEADS, n, slots),
           wb + qi - (kt * n + col), col >= 0)

    @pl.when(kt == n_tiles - 1)
    def _():
        cn = lax.broadcasted_iota(jnp.int32, (1, new_ref.shape[1]), 1)
        attend(new_ref[0, :, :DIL_W], new_ref[0, :, DIL_W:], qi - cn, cn < n_q)
        nums = [_own_block(acc_scr[i * rows:(i + 1) * rows, :], DIL_HEADS, n_q) for i in range(N_DIL)]
        ms = [m_scr[i * rows:(i + 1) * rows, 0:1] for i in range(N_DIL)]
        ls = [l_scr[i * rows:(i + 1) * rows, 0:1] for i in range(N_DIL)]
        m_max = ms[0]
        for m in ms[1:]:
            m_max = jnp.maximum(m_max, m)
        ws = [jnp.exp(m - m_max) for m in ms]
        num = ws[0] * nums[0]
        den = ws[0] * ls[0]
        for w, n_, l in zip(ws[1:], nums[1:], ls[1:]):
            num = num + w * n_
            den = den + w * l
        o_ref[0] = num / den


def dil_decode(table, qbd16, buf_lines, new_kv, n_q):
    DB = qbd16.shape[0]
    slots = 2 * DIL_HEADS
    wb = buf_lines.shape[0] // (DB * slots)
    rows = qbd16.shape[1]
    assert wb % DIL_DEC_TILE == 0
    n_tiles = wb // DIL_DEC_TILE
    kern = functools.partial(_dil_decode_kernel, n_tiles=n_tiles, n_q=n_q, wb=wb, scale=HEAD_DIM ** -0.5)
    return pl.pallas_call(
        kern,
        grid=(DB, n_tiles),
        in_specs=[
            pl.BlockSpec(memory_space=pltpu.SMEM),
            pl.BlockSpec((1, rows, DIL_W), lambda b, t: (b, 0, 0)),
            pl.BlockSpec((DIL_DEC_TILE * slots, HEAD_DIM), lambda b, t: (b * n_tiles + t, 0)),
            pl.BlockSpec((1, new_kv.shape[1], 2 * DIL_W), lambda b, t: (b, 0, 0)),
        ],
        out_specs=pl.BlockSpec((1, rows, HEAD_DIM), lambda b, t: (b, 0, 0)),
        out_shape=jax.ShapeDtypeStruct((DB, rows, HEAD_DIM), F32),
        scratch_shapes=[pltpu.VMEM((N_DIL * rows, LANES), F32), pltpu.VMEM((N_DIL * rows, LANES), F32),
                        pltpu.VMEM((N_DIL * rows, DIL_W), F32)],
        compiler_params=_cparams(("parallel", "arbitrary")),
        name="dil_decode",
    )(table, qbd16, buf_lines, new_kv)


PROJ_TN = 256
PROJ_TM = 512
_C_SBQ, _C_SBK, _C_SBV = 0, SB_W, 2 * SB_W
_C_NQ = 3 * SB_W
_C_CMP = _C_NQ + NSA_QW
_C_SLC = _C_CMP + 2 * NSA_KVW
_C_WIN = _C_SLC + 2 * NSA_KVW
_C_GATE = _C_WIN + 2 * NSA_KVW
SPARSE_NP = -(-SPARSE_IN // PROJ_TN) * PROJ_TN


def _sparse_proj_plan(w_in, qk_gain):
    w16 = jnp.pad(w_in, ((0, 0), (0, SPARSE_NP - SPARSE_IN))).astype(BF16)
    gain = jnp.ones((SPARSE_NP,), F32)
    flag = np.zeros((SPARSE_NP,), np.float32)
    kinds = np.full((SPARSE_NP // PROJ_TN,), EPI_PLAIN, np.int32)
    for col, width, gi in ((_C_NQ, NSA_QW, 0), (_C_SLC, NSA_KVW, 2), (_C_WIN, NSA_KVW, 3)):
        gain = gain.at[col:col + width].set(jnp.tile(qk_gain[gi], width // HEAD_DIM))
        flag[col:col + width] = 1.0
        assert col % PROJ_TN == 0 and width % PROJ_TN == 0
        kinds[col // PROJ_TN:(col + width) // PROJ_TN] = EPI_NORM
    assert _C_GATE % PROJ_TN == 0 and SPARSE_NP - _C_GATE == PROJ_TN
    kinds[_C_GATE // PROJ_TN] = EPI_SIGMOID
    return w16, gain, jnp.asarray(flag), kinds


def _dil_proj_plan(w_in, qk_gain):
    N = 3 * DIL_W
    gain = jnp.concatenate([jnp.tile(qk_gain[0], DIL_HEADS), jnp.tile(qk_gain[1], DIL_HEADS), jnp.ones((DIL_W,), F32)])
    flag = np.concatenate([np.ones((2 * DIL_W,), np.float32), np.zeros((DIL_W,), np.float32)])
    kinds = np.full((N // PROJ_TN,), EPI_PLAIN, np.int32)
    kinds[:2 * DIL_W // PROJ_TN] = EPI_NORM
    return w_in.astype(BF16), gain, jnp.asarray(flag), kinds


def _pad_rows(a, n):
    return jnp.pad(a, ((0, 0), (0, n - a.shape[1]), (0, 0)))


def _rows_to_tokens(o, n_heads, n_q):
    DB = o.shape[0]
    return o.reshape(DB, n_heads, n_q, HEAD_DIM).transpose(0, 2, 1, 3).reshape(DB * n_q, n_heads * HEAD_DIM)


def kernel(x_prompt, x_sample, cache_sb_kv, cache_nsa_cmp_kv, cache_nsa_slc_kv, state_nsa_win_kv, state_dil_kv,
           page_table, rel_bias_table, norm_mix, norm_ffn, sparse_w_in, sparse_w_out, nsa_qk_gain, nsa_cmp_w,
           nsa_cmp_pe, dil_w_in, dil_w_out, dil_qk_gain, moe_w_group, moe_b_group, moe_w_expert, moe_b_expert,
           moe_w_gate, moe_w_up, moe_w_down):
    B, S, D = x_prompt.shape
    DB, DS, _ = x_sample.shape
    n_pool = cache_sb_kv.shape[1]
    xp = x_prompt.reshape(B * S, D)
    xs = x_sample.reshape(DB * DS, D)
    nsa_table = rel_bias_table[:, :NSA_HEADS]
    dil_table = rel_bias_table[:, :DIL_HEADS]
    G, R = NSA_KV_HEADS, NSA_REP

    def moe(layer, xp, xs):
        return moe_layer(xp, xs, norm_ffn[layer], moe_w_group[layer], moe_b_group[layer], moe_w_expert[layer],
                         moe_b_expert[layer], moe_w_gate, moe_w_up, moe_w_down, layer)

    w16, gain, flag, kinds = _sparse_proj_plan(sparse_w_in[0], nsa_qk_gain[0])
    w_out16 = sparse_w_out[0].astype(BF16)
    cw16, cpe = _prep_cmp_weights(nsa_cmp_w[0], nsa_cmp_pe[0])
    ck_gain = nsa_qk_gain[0, 1]
    kv_cols = ((_C_SBK, 2 * SB_W), (_C_CMP, 2 * NSA_KVW), (_C_SLC, 2 * NSA_KVW), (_C_WIN, 2 * NSA_KVW))
    p16, gate32, sb_lines, cmp_lines, slc_lines, win_lines = norm_mm(
        xp, norm_mix[0], w16, gain, flag, kinds, tm=PROJ_TM, tn=PROJ_TN, f32_cols=(_C_GATE, PROJ_TN), lines=kv_cols)
    p16b = p16.reshape(B, S, SPARSE_NP)
    o_sb = sb_prompt(p16b, B, S, q_col=_C_SBQ, k_col=_C_SBK, v_col=_C_SBV)
    chunks = p16b[:, :, _C_CMP:_C_CMP + 2 * NSA_KVW].reshape(B, S // CMP_STRIDE, CHUNK_W)
    kc, vc = compress_prompt(chunks, cw16, cpe, ck_gain)
    tb = nsa_bias_tiles(nsa_table)
    cb = nsa_cmp_bias(nsa_table, S, kc.shape[1])
    o_nsa = nsa_prompt(p16b, gate32.reshape(B, S, PROJ_TN), kc, vc, tb, cb, B, S, q_col=_C_NQ, sk_col=_C_SLC,
                       sv_col=_C_SLC + NSA_KVW, wk_col=_C_WIN, wv_col=_C_WIN + NSA_KVW, gate_col=0)
    a = jnp.concatenate([o_sb, o_nsa], axis=-1).reshape(B * S, SB_W + NSA_QW)
    xp = mm_res(a, w_out16, xp, tm=1024, tn=PROJ_TN)
    sb_kv_prompt = sb_lines.reshape(1, B, S, 2, SB_HEADS, HEAD_DIM)
    cmp_kv_prompt = cmp_lines.reshape(1, B, S, 2, G, HEAD_DIM)
    slc_kv_prompt = slc_lines.reshape(1, B, S, 2, G, HEAD_DIM)
    wn = min(NSA_WINDOW, S)
    win_kv_prompt = win_lines.reshape(1, B, S, 2, G, HEAD_DIM)[:, :, S - wn:]
    _, q32 = norm_mm(xs, norm_mix[0], w16, gain, flag, kinds, tm=DB * DS, tn=PROJ_TN, f32_cols=(0, SPARSE_NP))
    q32b = q32.reshape(DB, DS, SPARSE_NP)
    sb_new = q32b[:, :, _C_SBK:_C_SBK + 2 * SB_W]
    cmp_new = q32b[:, :, _C_CMP:_C_CMP + 2 * NSA_KVW]
    slc_new = q32b[:, :, _C_SLC:_C_SLC + 2 * NSA_KVW]
    win_new = q32b[:, :, _C_WIN:_C_WIN + 2 * NSA_KVW]
    o_sb_s = sb_decode(page_table, _block_diag_rows(q32b[:, :, _C_SBQ:_C_SBQ + SB_W], SB_HEADS).astype(BF16),
                       _pad_rows(sb_new, PAGE_SIZE), cache_sb_kv[0].reshape(-1, HEAD_DIM))
    qn = q32b[:, :, _C_NQ:_C_NQ + NSA_QW].reshape(DB, DS, G, R, HEAD_DIM).transpose(0, 2, 3, 1, 4)
    qn = (qn[:, :, :, :, None, :] * jnp.eye(G, dtype=F32)[None, :, None, None, :, None]).reshape(DB, G * R * DS, NSA_KVW)
    qn16 = qn.astype(BF16)
    o_c, sel = nsa_cmp_decode(page_table, nsa_table, qn16, cache_nsa_cmp_kv[0].reshape(-1, HEAD_DIM), cw16, cpe, ck_gain, DS)
    sel16 = jnp.broadcast_to(sel.reshape(DB, G, 1, DS, -1), (DB, G, R, DS, sel.shape[-1])).reshape(DB, G * R * DS, -1).astype(BF16)
    gates = q32b[:, :, _C_GATE:_C_GATE + 3 * NSA_HEADS].reshape(DB, DS, G, R, 3).transpose(0, 2, 3, 1, 4).reshape(DB, G * R * DS, 3)
    gates = jnp.pad(gates, ((0, 0), (0, 0), (0, LANES - 3)))
    wbuf = state_nsa_win_kv[0]
    o_nsa_s = nsa_slc_decode(page_table, nsa_table, qn16, sel16, cache_nsa_slc_kv[0].reshape(-1, HEAD_DIM),
                             _pad_rows(slc_new, PAGE_SIZE), wbuf.reshape(-1, HEAD_DIM), _pad_rows(win_new, PAGE_SIZE),
                             o_c, gates, DS)
    a_s = jnp.concatenate([_rows_to_tokens(o_sb_s, SB_HEADS, DS), _rows_to_tokens(o_nsa_s, NSA_HEADS, DS)], axis=-1)
    xs = mm_res(a_s.astype(BF16), w_out16, xs, tm=DB * DS, tn=PROJ_TN)
    sb_kv_sample = sb_new.reshape(1, DB, DS, 2, SB_HEADS, HEAD_DIM)
    cmp_kv_sample = cmp_new.reshape(1, DB, DS, 2, G, HEAD_DIM)
    slc_kv_sample = slc_new.reshape(1, DB, DS, 2, G, HEAD_DIM)
    win_kv_sample = jnp.concatenate([wbuf, win_new.reshape(DB, DS, 2, G, HEAD_DIM)], axis=1)[None, :, DS:]
    xp, xs = moe(0, xp, xs)

    w16, gain, flag, kinds = _dil_proj_plan(dil_w_in[0], dil_qk_gain[0])
    w_out16 = dil_w_out[0].astype(BF16)
    p16, dil_lines = norm_mm(xp, norm_mix[1], w16, gain, flag, kinds, tm=PROJ_TM, tn=PROJ_TN,
                             lines=((DIL_W, 2 * DIL_W),))
    dbias = dil_bias_tiles(dil_table)
    stats = [dil_prompt_pattern(p16.reshape(B, S, 3 * DIL_W), dbias, pat, B, S) for pat in range(N_DIL)]
    o = dil_merge([n for n, _ in stats], [ml for _, ml in stats], tm=512)
    xp = mm_res(o, w_out16, xp, tm=1024, tn=PROJ_TN)
    dn = min(DIL_MAX_WINDOW, S)
    dil_kv_prompt = dil_lines.reshape(1, B, S, 2, DIL_HEADS, HEAD_DIM)[:, :, S - dn:]
    _, q32 = norm_mm(xs, norm_mix[1], w16, gain, flag, kinds, tm=DB * DS, tn=PROJ_TN, f32_cols=(0, 3 * DIL_W))
    q32b = q32.reshape(DB, DS, 3 * DIL_W)
    dil_new = q32b[:, :, DIL_W:]
    dbuf = state_dil_kv[0]
    o_s = dil_decode(dil_table, _block_diag_rows(q32b[:, :, :DIL_W], DIL_HEADS).astype(BF16),
                     dbuf.reshape(-1, HEAD_DIM), _pad_rows(dil_new, PAGE_SIZE), DS)
    xs = mm_res(_rows_to_tokens(o_s, DIL_HEADS, DS).astype(BF16), w_out16, xs, tm=DB * DS, tn=PROJ_TN)
    dil_kv_sample = jnp.concatenate([dbuf, dil_new.reshape(DB, DS, 2, DIL_HEADS, HEAD_DIM)], axis=1)[None, :, DS:]
    xp, xs = moe(1, xp, xs)

    return (xp.reshape(B, S, D), xs.reshape(DB, DS, D), sb_kv_prompt, sb_kv_sample, cmp_kv_prompt, cmp_kv_sample,
            slc_kv_prompt, slc_kv_sample, win_kv_prompt, win_kv_sample, dil_kv_prompt, dil_kv_sample)
```

```python
import functools
import math

import jax
import jax.numpy as jnp
import numpy as np
from jax import lax
from jax.experimental import pallas as pl
from jax.experimental.pallas import tpu as pltpu

D_MODEL = 2048
HEAD_DIM = 128
SB_HEADS = 8
NSA_HEADS = 8
NSA_KV_HEADS = 2
NSA_REP = NSA_HEADS // NSA_KV_HEADS
CMP_BLOCK = 32
CMP_STRIDE = 16
SLC_BLOCK = 64
N_SELECT = 16
NSA_WINDOW = 512
DIL_HEADS = 16
DIL_PATTERNS = ((128, 1), (512, 4), (2048, 16))
DIL_MAX_WINDOW = 2048
QBLK = 128
N_BUCKETS = 32
BUCKET_MAX_DIST = 2048
N_GROUPS = 4
EXPERTS_PER_GROUP = 4
N_EXPERTS = N_GROUPS * EXPERTS_PER_GROUP
TOP_K_INNER = 2
D_EXPERT = 1024
PAGE_SIZE = 128
RMS_EPS = 1e-6
NEG_INF = -1e30
FORCE_SCORE = 1e9
SB_W = SB_HEADS * HEAD_DIM
NSA_QW = NSA_HEADS * HEAD_DIM
NSA_KVW = NSA_KV_HEADS * HEAD_DIM
SPARSE_IN = 3 * SB_W + NSA_QW + 6 * NSA_KVW + 3 * NSA_HEADS
DIL_W = DIL_HEADS * HEAD_DIM

LANES = 128
SUBLANES = 8
VMEM_LIMIT = 56 * 1024 * 1024

BF16 = jnp.bfloat16
F32 = jnp.float32


def _cparams(sem):
    return pltpu.CompilerParams(dimension_semantics=sem, vmem_limit_bytes=VMEM_LIMIT)


def _dot(a, b):
    return jnp.dot(a, b, preferred_element_type=F32)


def _dot_t(a, b):
    return lax.dot_general(a, b, (((1,), (1,)), ((), ())), preferred_element_type=F32)


def _split3(x):
    h1 = x.astype(BF16)
    r1 = x - h1.astype(F32)
    h2 = r1.astype(BF16)
    h3 = (r1 - h2.astype(F32)).astype(BF16)
    return h1, h2, h3


def _dot01(x, m01):
    h1, h2, h3 = _split3(x)
    return _dot(h1, m01) + _dot(h2, m01) + _dot(h3, m01)


EPI_PLAIN, EPI_NORM, EPI_SIGMOID = 0, 1, 2


MM_ROW_CHUNK = 256


def _norm_mm_kernel(cls_ref, x_ref, g_ref, w_ref, cg_ref, cf_ref, o16_ref, *rest, classes, has_f32, line_cfg):
    h_scr = rest[-1]
    o32_ref = rest[0] if has_f32 else None
    line_refs = rest[(1 if has_f32 else 0):-1]
    j = pl.program_id(1)
    tm, tn = o16_ref.shape
    rc = min(MM_ROW_CHUNK, tm)

    @pl.when(j == 0)
    def _():
        x = x_ref[...]
        y = x * lax.rsqrt(jnp.mean(x * x, axis=-1, keepdims=True) + RMS_EPS)
        h_scr[...] = (y * g_ref[...]).astype(BF16)

    def epilogue(kind, acc):
        if kind == EPI_PLAIN:
            return acc
        if kind == EPI_SIGMOID:
            return jax.nn.sigmoid(acc)
        parts = []
        for c in range(tn // HEAD_DIM):
            a = acc[:, c * HEAD_DIM:(c + 1) * HEAD_DIM]
            parts.append(a * lax.rsqrt(jnp.mean(a * a, axis=-1, keepdims=True) + RMS_EPS))
        return jnp.where(cf_ref[...] > 0.5, jnp.concatenate(parts, axis=1) * cg_ref[...], acc)

    for cid, (kind, in_f32, li) in enumerate(classes):
        @pl.when(cls_ref[j] == cid)
        def _(kind=kind, in_f32=in_f32, li=li):
            for r in range(tm // rc):
                rows = slice(r * rc, (r + 1) * rc)
                out = epilogue(kind, _dot(h_scr[rows, :], w_ref[...]))
                o16_ref[rows, :] = out.astype(BF16)
                if in_f32:
                    o32_ref[rows, :] = out
                if li is not None:
                    j0, slots = line_cfg[li]
                    for c in range(tn // LANES):
                        line_refs[li][pl.ds(r * rc * slots + (j - j0) * (tn // LANES) + c, rc, stride=slots), :] = (
                            out[:, c * LANES:(c + 1) * LANES])


def norm_mm(x, g, w16, col_gain, col_flag, tile_kind, *, tm, tn, f32_cols=None, lines=()):
    T, D = x.shape
    N = w16.shape[1]
    assert T % tm == 0 and N % tn == 0
    n_tiles = N // tn
    grid = (T // tm, n_tiles)
    tile_kind = np.asarray(tile_kind)
    out_specs = [pl.BlockSpec((tm, tn), lambda i, j, k: (i, j))]
    out_shape = [jax.ShapeDtypeStruct((T, N), BF16)]
    in_f32 = np.zeros((n_tiles,), bool)
    if f32_cols is not None:
        c0, width = f32_cols
        assert c0 % tn == 0 and width % tn == 0
        j0, nj = c0 // tn, width // tn
        in_f32[j0:j0 + nj] = True
        out_specs.append(pl.BlockSpec((tm, tn), lambda i, j, k, j0=j0, nj=nj: (i, jnp.clip(j - j0, 0, nj - 1))))
        out_shape.append(jax.ShapeDtypeStruct((T, width), F32))
    line_of = [None] * n_tiles
    line_cfg = []
    for li, (c0, width) in enumerate(lines):
        assert c0 % tn == 0 and width % tn == 0
        slots = width // LANES
        line_cfg.append((c0 // tn, slots))
        for j in range(c0 // tn, (c0 + width) // tn):
            line_of[j] = li
        out_specs.append(pl.BlockSpec((tm * slots, LANES), lambda i, j, k: (i, 0)))
        out_shape.append(jax.ShapeDtypeStruct((T * slots, LANES), F32))
    per_tile = [(int(tile_kind[j]), bool(in_f32[j]), line_of[j]) for j in range(n_tiles)]
    classes = tuple(dict.fromkeys(per_tile))
    tile_cls = jnp.asarray([classes.index(c) for c in per_tile], jnp.int32)
    kern = functools.partial(_norm_mm_kernel, classes=classes, has_f32=f32_cols is not None, line_cfg=tuple(line_cfg))
    return pl.pallas_call(
        kern,
        grid_spec=pltpu.PrefetchScalarGridSpec(
            num_scalar_prefetch=1,
            grid=grid,
            in_specs=[
                pl.BlockSpec((tm, D), lambda i, j, k: (i, 0)),
                pl.BlockSpec((1, D), lambda i, j, k: (0, 0)),
                pl.BlockSpec((D, tn), lambda i, j, k: (0, j)),
                pl.BlockSpec((1, tn), lambda i, j, k: (0, j)),
                pl.BlockSpec((1, tn), lambda i, j, k: (0, j)),
            ],
            out_specs=out_specs,
            scratch_shapes=[pltpu.VMEM((tm, D), BF16)],
        ),
        out_shape=out_shape,
        compiler_params=_cparams(("parallel", "arbitrary")),
        name="norm_mm",
    )(tile_cls, x, g.reshape(1, D), w16, col_gain.reshape(1, N), col_flag.reshape(1, N))


def _mm_res_kernel(a_ref, w_ref, r_ref, o_ref):
    tm = o_ref.shape[0]
    rc = min(MM_ROW_CHUNK, tm)
    for r in range(tm // rc):
        rows = slice(r * rc, (r + 1) * rc)
        o_ref[rows, :] = r_ref[rows, :] + _dot(a_ref[rows, :], w_ref[...])


def mm_res(a16, w16, res, *, tm, tn):
    T, K = a16.shape
    N = w16.shape[1]
    assert T % tm == 0 and N % tn == 0
    return pl.pallas_call(
        _mm_res_kernel,
        grid=(T // tm, N // tn),
        in_specs=[
            pl.BlockSpec((tm, K), lambda i, j: (i, 0)),
            pl.BlockSpec((K, tn), lambda i, j: (0, j)),
            pl.BlockSpec((tm, tn), lambda i, j: (i, j)),
        ],
        out_specs=pl.BlockSpec((tm, tn), lambda i, j: (i, j)),
        out_shape=jax.ShapeDtypeStruct((T, N), F32),
        compiler_params=_cparams(("parallel", "arbitrary")),
        name="mm_res",
    )(a16, w16, res)


LOG2_E = math.log2(math.e)


def _log2_keep(z2):
    nz = -z2
    return jnp.minimum(nz, 0.0) - jnp.log2(1.0 + jnp.exp2(jnp.minimum(z2, nz)))


def _upper01(n):
    r = lax.broadcasted_iota(jnp.int32, (2 * n, n), 0)
    c = lax.broadcasted_iota(jnp.int32, (2 * n, n), 1)
    return jnp.where(jnp.where(r >= n, r - n, r) > c, 1.0, 0.0).astype(BF16)


def _after_in_tile(lk, upper2):
    h1 = lk.astype(BF16)
    h2 = (lk - h1.astype(F32)).astype(BF16)
    return _dot(jnp.concatenate([h1, h2], axis=1), upper2)


def _sb_prompt_kernel(q_ref, k_ref, v_ref, o_ref, acc_scr, run_scr, *, bq, scale):
    bk = LANES
    i = pl.program_id(2)
    nd = bq // bk
    acc_scr[...] = jnp.zeros(acc_scr.shape, F32)
    run_scr[...] = jnp.zeros(run_scr.shape, F32)
    upper2 = _upper01(bk)

    def absorb(kb, r0, masked):
        ks = pl.multiple_of(kb * bk, bk)
        k = k_ref[0, pl.ds(ks, bk), :]
        v = v_ref[0, pl.ds(ks, bk), :]
        z2 = _dot_t(q_ref[0, r0:, :], k) * (scale * LOG2_E)
        lk = _log2_keep(z2)
        lb = z2 + lk
        if masked:
            qpos = i * bq + r0 + lax.broadcasted_iota(jnp.int32, (bq - r0, 1), 0)
            mask = ks + lax.broadcasted_iota(jnp.int32, (1, bk), 1) < qpos
            lk = jnp.where(mask, lk, 0.0)
        w = jnp.exp2(lb + _after_in_tile(lk, upper2) + run_scr[r0:, :])
        if masked:
            w = jnp.where(mask, w, 0.0)
        acc_scr[r0:, :] = acc_scr[r0:, :] + _dot(w.astype(BF16), v)
        run_scr[r0:, :] = run_scr[r0:, :] + jnp.sum(lk, axis=-1, keepdims=True)

    for d in range(nd - 1, -1, -1):
        absorb(i * nd + d, d * bk, True)

    assert nd % 2 == 0

    def body(t, carry):
        ks = pl.multiple_of((i * nd - 2 - 2 * t) * bk, 2 * bk)
        k = k_ref[0, pl.ds(ks, 2 * bk), :]
        v = v_ref[0, pl.ds(ks, 2 * bk), :]
        z2 = _dot_t(q_ref[0], k) * (scale * LOG2_E)
        lk = _log2_keep(z2)
        lb = z2 + lk
        lk_l, lk_r = lk[:, :bk], lk[:, bk:]
        sum_r = jnp.sum(lk_r, axis=-1, keepdims=True)
        run = run_scr[...]
        w_r = jnp.exp2(lb[:, bk:] + _after_in_tile(lk_r, upper2) + run)
        w_l = jnp.exp2(lb[:, :bk] + _after_in_tile(lk_l, upper2) + (run + sum_r))
        acc_scr[...] = acc_scr[...] + _dot(jnp.concatenate([w_l.astype(BF16), w_r.astype(BF16)], axis=1), v)
        run_scr[...] = run + (sum_r + jnp.sum(lk_l, axis=-1, keepdims=True))
        return carry

    lax.fori_loop(0, i * (nd // 2), body, 0)
    o_ref[0] = acc_scr[...].astype(o_ref.dtype)


def sb_prompt(p16, B, S, *, q_col, k_col, v_col, bq=512):
    assert S % bq == 0 and bq % LANES == 0
    kern = functools.partial(_sb_prompt_kernel, bq=bq, scale=HEAD_DIM ** -0.5)
    qb, kb_, vb = q_col // HEAD_DIM, k_col // HEAD_DIM, v_col // HEAD_DIM
    return pl.pallas_call(
        kern,
        grid=(B, SB_HEADS, S // bq),
        in_specs=[
            pl.BlockSpec((1, bq, HEAD_DIM), lambda b, h, i: (b, i, qb + h)),
            pl.BlockSpec((1, S, HEAD_DIM), lambda b, h, i: (b, 0, kb_ + h)),
            pl.BlockSpec((1, S, HEAD_DIM), lambda b, h, i: (b, 0, vb + h)),
        ],
        out_specs=pl.BlockSpec((1, bq, HEAD_DIM), lambda b, h, i: (b, i, h)),
        out_shape=jax.ShapeDtypeStruct((B, S, SB_W), BF16),
        scratch_shapes=[pltpu.VMEM((bq, HEAD_DIM), F32), pltpu.VMEM((bq, LANES), F32)],
        compiler_params=_cparams(("parallel", "parallel", "arbitrary")),
        name="sb_prompt",
    )(p16, p16, p16)


def _t5_bucket(dist):
    n = jnp.maximum(dist, 0)
    exact = N_BUCKETS // 2
    nf = jnp.maximum(n, exact).astype(F32)
    far = exact + (jnp.log(nf / exact) / math.log(BUCKET_MAX_DIST / exact) * (N_BUCKETS - exact)).astype(jnp.int32)
    return jnp.where(n < exact, n, jnp.minimum(far, N_BUCKETS - 1))


def _table_lookup(bucket, tab_ref, head):
    out = jnp.zeros(bucket.shape, F32)
    for k in range(N_BUCKETS):
        out = jnp.where(bucket == k, tab_ref[k, head], out)
    return out


N_BIAS_DIAG = -(-(BUCKET_MAX_DIST + QBLK) // QBLK) + 1


def _bias_tiles_kernel(tab_ref, o_ref):
    g = pl.program_id(0)
    d = pl.program_id(1)
    dist = d * QBLK + lax.broadcasted_iota(jnp.int32, (QBLK, QBLK), 0) - lax.broadcasted_iota(jnp.int32, (QBLK, QBLK), 1)
    bucket = _t5_bucket(dist)
    for r in range(NSA_REP):
        o_ref[0, 0, r * QBLK:(r + 1) * QBLK, :] = _table_lookup(bucket, tab_ref, g * NSA_REP + r)


def nsa_bias_tiles(table):
    return pl.pallas_call(
        _bias_tiles_kernel,
        grid=(NSA_KV_HEADS, N_BIAS_DIAG),
        in_specs=[pl.BlockSpec(memory_space=pltpu.SMEM)],
        out_specs=pl.BlockSpec((1, 1, NSA_REP * QBLK, QBLK), lambda g, d: (g, d, 0, 0)),
        out_shape=jax.ShapeDtypeStruct((NSA_KV_HEADS, N_BIAS_DIAG, NSA_REP * QBLK, QBLK), F32),
        compiler_params=_cparams(("parallel", "arbitrary")),
        name="nsa_bias_tiles",
    )(table)


CMP_HALF = CMP_BLOCK // CMP_STRIDE
CHUNK_W = CMP_STRIDE * 2 * NSA_KVW


def _compress_chunks(x, w_ref, pe_ref, gain):
    n = x.shape[0]
    outs = []
    for kv in range(2):
        for g in range(NSA_KV_HEADS):
            off = kv * NSA_KVW + g * HEAD_DIM
            xs = jnp.concatenate(
                [x[:, l * 2 * NSA_KVW + off:l * 2 * NSA_KVW + off + HEAD_DIM] for l in range(CMP_STRIDE)], axis=1)
            halves = []
            for j in range(CMP_HALF):
                w = w_ref[kv, j]
                pe = jnp.broadcast_to(pe_ref[kv, j], (SUBLANES, CMP_STRIDE * HEAD_DIM)).astype(BF16)
                halves.append(_dot(xs, w) + _dot(pe, w)[0:1, :])
            out = halves[0]
            for j in range(1, CMP_HALF):
                out = out + pltpu.roll(halves[j], n - j, axis=0)
            if kv == 0:
                out = out * lax.rsqrt(jnp.mean(out * out, axis=-1, keepdims=True) + RMS_EPS) * gain
            outs.append(out)
    kc = jnp.concatenate(outs[:NSA_KV_HEADS], axis=1)
    vc = jnp.concatenate(outs[NSA_KV_HEADS:], axis=1)
    return kc, vc


def _compress_prompt_kernel(x_ref, w_ref, pe_ref, g_ref, kc_ref, vc_ref):
    kc, vc = _compress_chunks(x_ref[0], w_ref, pe_ref, g_ref[...])
    kc_ref[0] = kc.astype(BF16)
    vc_ref[0] = vc.astype(BF16)


def _prep_cmp_weights(cmp_w, cmp_pe):
    w = cmp_w.reshape(2, CMP_HALF, CMP_STRIDE * HEAD_DIM, HEAD_DIM).astype(BF16)
    pe = cmp_pe.reshape(2, CMP_HALF, 1, CMP_STRIDE * HEAD_DIM)
    return w, pe


def compress_prompt(chunks16, w16, pe, ck_gain):
    B, n, _ = chunks16.shape
    return pl.pallas_call(
        _compress_prompt_kernel,
        grid=(B,),
        in_specs=[
            pl.BlockSpec((1, n, CHUNK_W), lambda b: (b, 0, 0)),
            pl.BlockSpec((2, CMP_HALF, CMP_STRIDE * HEAD_DIM, HEAD_DIM), lambda b: (0, 0, 0, 0)),
            pl.BlockSpec((2, CMP_HALF, 1, CMP_STRIDE * HEAD_DIM), lambda b: (0, 0, 0, 0)),
            pl.BlockSpec((1, HEAD_DIM), lambda b: (0, 0)),
        ],
        out_specs=[pl.BlockSpec((1, n, NSA_KVW), lambda b: (b, 0, 0))] * 2,
        out_shape=[jax.ShapeDtypeStruct((B, n, NSA_KVW), BF16)] * 2,
        compiler_params=_cparams(("parallel",)),
        name="nsa_compress_prompt",
    )(chunks16, w16, pe, ck_gain.reshape(1, HEAD_DIM))


def _flash_step(s, mask, v, m_scr, l_scr, acc_scr):
    n = s.shape[1] // LANES
    cols = [slice(c * LANES, (c + 1) * LANES) for c in range(n)]
    sm = [jnp.where(mask[:, c], s[:, c], NEG_INF) for c in cols]
    top = sm[0]
    for x in sm[1:]:
        top = jnp.maximum(top, x)
    m_prev = m_scr[...]
    m_new = jnp.maximum(m_prev, jnp.max(top, axis=-1, keepdims=True))
    ps = [jnp.where(mask[:, c], jnp.exp(x - m_new), 0.0) for c, x in zip(cols, sm)]
    tot = ps[0]
    for x in ps[1:]:
        tot = tot + x
    alpha = jnp.exp(m_prev - m_new)
    l_scr[...] = alpha * l_scr[...] + jnp.sum(tot, axis=-1, keepdims=True)
    acc_scr[...] = alpha * acc_scr[...] + _dot(jnp.concatenate([x.astype(BF16) for x in ps], axis=1), v)
    m_scr[...] = m_new


def _flash_reset(m_scr, l_scr, acc_scr):
    m_scr[...] = jnp.full(m_scr.shape, NEG_INF, F32)
    l_scr[...] = jnp.zeros(l_scr.shape, F32)
    acc_scr[...] = jnp.zeros(acc_scr.shape, F32)


def _select_blocks(score, q_blk, n_sel):
    Q, n_s = score.shape
    blk = lax.broadcasted_iota(jnp.int32, (1, n_s), 1)
    valid = blk <= q_blk
    forced = valid & ((blk == 0) | (blk == q_blk) | (blk == q_blk - 1))
    score = jnp.where(forced, FORCE_SCORE, jnp.where(valid, score, NEG_INF))
    rank = jnp.zeros((Q, n_s), F32)
    for c in range(n_s):
        col = score[:, c:c + 1]
        before = (col > score) | ((col == score) & (blk > c))
        rank = rank + jnp.where(before, 1.0, 0.0)
    return jnp.where(rank < n_sel, 1.0, 0.0)


def _select_blocks_t(score_t, q_blk, n_sel):
    n_s, Q = score_t.shape
    blk = lax.broadcasted_iota(jnp.int32, (n_s, 1), 0)
    valid = blk <= q_blk
    forced = valid & ((blk == 0) | (blk == q_blk) | (blk == q_blk - 1))
    score_t = jnp.where(forced, FORCE_SCORE, jnp.where(valid, score_t, NEG_INF))
    rank = jnp.zeros((n_s, Q), F32)
    for c in range(n_s):
        row = score_t[c:c + 1, :]
        before = (row > score_t) | ((row == score_t) & (blk > c))
        rank = rank + jnp.where(before, 1.0, 0.0)
    return jnp.where(rank < n_sel, 1.0, 0.0)


def _nsa_cmp_bias_kernel(tab_ref, o_ref):
    g = pl.program_id(0)
    i = pl.program_id(1)
    ncp = o_ref.shape[3]
    qpos1 = i * QBLK + lax.broadcasted_iota(jnp.int32, (QBLK, 1), 0)
    c_end = lax.broadcasted_iota(jnp.int32, (1, ncp), 1) * CMP_STRIDE + (CMP_BLOCK - 1)
    bucket = _t5_bucket(qpos1 - c_end)
    for r in range(NSA_REP):
        o_ref[0, 0, r * QBLK:(r + 1) * QBLK, :] = _table_lookup(bucket, tab_ref, g * NSA_REP + r)


def nsa_cmp_bias(table, S, ncp):
    return pl.pallas_call(
        _nsa_cmp_bias_kernel,
        grid=(NSA_KV_HEADS, S // QBLK),
        in_specs=[pl.BlockSpec(memory_space=pltpu.SMEM)],
        out_specs=pl.BlockSpec((1, 1, NSA_REP * QBLK, ncp), lambda g, i: (g, i, 0, 0)),
        out_shape=jax.ShapeDtypeStruct((NSA_KV_HEADS, S // QBLK, NSA_REP * QBLK, ncp), F32),
        compiler_params=_cparams(("parallel", "arbitrary")),
        name="nsa_cmp_bias",
    )(table)


NSA_KT = 2 * QBLK


def _nsa_prompt_kernel(q_ref, kc_ref, vc_ref, ks_ref, vs_ref, kw_ref, vw_ref, gt_ref, tb_ref, cb_ref,
                       o_ref, m_scr, l_scr, acc_scr, sel_scr, *, n_c, scale):
    g = pl.program_id(1)
    i = pl.program_id(2)
    R = NSA_REP
    qs = i * QBLK
    q4 = q_ref[0]
    q = jnp.concatenate([q4[:, r * HEAD_DIM:(r + 1) * HEAD_DIM] for r in range(R)], axis=0)
    qpos1 = qs + lax.broadcasted_iota(jnp.int32, (QBLK, 1), 0)
    qpos = jnp.concatenate([qpos1] * R, axis=0)

    ncp = kc_ref.shape[1]
    s = _dot_t(q, kc_ref[0]) * scale
    n_idx = lax.broadcasted_iota(jnp.int32, (1, ncp), 1)
    mask = (qpos - (n_idx * CMP_STRIDE + (CMP_BLOCK - 1)) >= 0) & (n_idx < n_c)
    s = jnp.where(mask, s + cb_ref[0, 0], NEG_INF)
    m = jnp.max(s, axis=-1, keepdims=True)
    p = jnp.where(mask, jnp.exp(s - m), 0.0)
    l = jnp.sum(p, axis=-1, keepdims=True)
    p = p / jnp.maximum(l, 1.0)
    o_c = _dot(p.astype(BF16), vc_ref[0])
    imp = p[0:QBLK]
    for r in range(1, R):
        imp = imp + p[r * QBLK:(r + 1) * QBLK]
    n_s = ks_ref.shape[1] // SLC_BLOCK
    ratio = SLC_BLOCK // CMP_STRIDE
    ti = lax.broadcasted_iota(jnp.int32, (n_s, ncp), 0)
    ci = lax.broadcasted_iota(jnp.int32, (n_s, ncp), 1)
    gather01 = jnp.where((ci >= ti * ratio - (CMP_HALF - 1)) & (ci < (ti + 1) * ratio), 1.0, 0.0).astype(BF16)
    i1, i2, i3 = _split3(imp)
    score_t = _dot_t(gather01, i1) + _dot_t(gather01, i2) + _dot_t(gather01, i3)
    q_blk = (qs + lax.broadcasted_iota(jnp.int32, (1, QBLK), 1)) // SLC_BLOCK
    sel_scr[...] = _select_blocks_t(score_t, q_blk, min(N_SELECT, n_s)).T.astype(BF16)

    kcol = lax.broadcasted_iota(jnp.int32, (1, NSA_KT), 1)

    def tile_bias(kt):
        parts = [tb_ref[0, jnp.clip(i - (kt * (NSA_KT // QBLK) + t), 0, N_BIAS_DIAG - 1)] for t in range(NSA_KT // QBLK)]
        return jnp.concatenate(parts, axis=1)

    _flash_reset(m_scr, l_scr, acc_scr)

    def slc_body(kt, carry):
        ks_ = pl.multiple_of(kt * NSA_KT, NSA_KT)
        k = ks_ref[0, pl.ds(ks_, NSA_KT), :]
        v = vs_ref[0, pl.ds(ks_, NSA_KT), :]
        dist = qpos - (ks_ + kcol)
        bi = lax.broadcasted_iota(jnp.int32, (n_s, NSA_KT), 0)
        ki = lax.broadcasted_iota(jnp.int32, (n_s, NSA_KT), 1)
        expand = jnp.where(bi == kt * (NSA_KT // SLC_BLOCK) + ki // SLC_BLOCK, 1.0, 0.0).astype(BF16)
        selk = _dot(sel_scr[...], expand)
        selk = jnp.concatenate([selk] * R, axis=0)
        msk = (dist >= 0) & (selk > 0.5)
        sc = _dot_t(q, k) * scale + tile_bias(kt)
        _flash_step(sc, msk, v, m_scr, l_scr, acc_scr)
        return carry

    lax.fori_loop(0, (qs + QBLK - 1) // NSA_KT + 1, slc_body, 0)
    o_s = acc_scr[...] / jnp.maximum(l_scr[...], 1.0)

    _flash_reset(m_scr, l_scr, acc_scr)

    def win_body(kt, carry):
        ks_ = pl.multiple_of(kt * NSA_KT, NSA_KT)
        k = kw_ref[0, pl.ds(ks_, NSA_KT), :]
        v = vw_ref[0, pl.ds(ks_, NSA_KT), :]
        dist = qpos - (ks_ + kcol)
        msk = (dist >= 0) & (dist < NSA_WINDOW)
        sc = _dot_t(q, k) * scale + tile_bias(kt)
        _flash_step(sc, msk, v, m_scr, l_scr, acc_scr)
        return carry

    lax.fori_loop(jnp.maximum(qs - (NSA_WINDOW - 1), 0) // NSA_KT, (qs + QBLK - 1) // NSA_KT + 1, win_body, 0)
    o_w = acc_scr[...] / jnp.maximum(l_scr[...], 1.0)

    gt = gt_ref[0]
    outs = []
    for r in range(R):
        c0 = (g * R + r) * 3
        rows = slice(r * QBLK, (r + 1) * QBLK)
        lane = lax.broadcasted_iota(jnp.int32, (1, LANES), 1)
        gates = [jnp.sum(jnp.where(lane == c0 + t, gt, 0.0), axis=-1, keepdims=True) for t in range(3)]
        outs.append(gates[0] * o_c[rows] + gates[1] * o_s[rows] + gates[2] * o_w[rows])
    o_ref[0] = jnp.concatenate(outs, axis=1).astype(o_ref.dtype)


def nsa_prompt(p16, p32, kc, vc, tb, cb, B, S, *, q_col, sk_col, sv_col, wk_col, wv_col, gate_col):
    G, R = NSA_KV_HEADS, NSA_REP
    assert S % NSA_KT == 0
    n_c = S // CMP_STRIDE - CMP_HALF + 1
    kern = functools.partial(_nsa_prompt_kernel, n_c=n_c, scale=HEAD_DIM ** -0.5)
    hd = HEAD_DIM
    return pl.pallas_call(
        kern,
        grid_spec=pltpu.PrefetchScalarGridSpec(
            num_scalar_prefetch=0,
            grid=(B, G, S // QBLK),
            in_specs=[
                pl.BlockSpec((1, QBLK, R * hd), lambda b, g, i: (b, i, q_col // (R * hd) + g)),
                pl.BlockSpec((1, kc.shape[1], hd), lambda b, g, i: (b, 0, g)),
                pl.BlockSpec((1, vc.shape[1], hd), lambda b, g, i: (b, 0, g)),
                pl.BlockSpec((1, S, hd), lambda b, g, i: (b, 0, sk_col // hd + g)),
                pl.BlockSpec((1, S, hd), lambda b, g, i: (b, 0, sv_col // hd + g)),
                pl.BlockSpec((1, S, hd), lambda b, g, i: (b, 0, wk_col // hd + g)),
                pl.BlockSpec((1, S, hd), lambda b, g, i: (b, 0, wv_col // hd + g)),
                pl.BlockSpec((1, QBLK, LANES), lambda b, g, i: (b, i, gate_col // LANES)),
                pl.BlockSpec((1, N_BIAS_DIAG, R * QBLK, QBLK), lambda b, g, i: (g, 0, 0, 0)),
                pl.BlockSpec((1, 1, R * QBLK, cb.shape[3]), lambda b, g, i: (g, i, 0, 0)),
            ],
            out_specs=pl.BlockSpec((1, QBLK, R * hd), lambda b, g, i: (b, i, g)),
            scratch_shapes=[
                pltpu.VMEM((R * QBLK, LANES), F32),
                pltpu.VMEM((R * QBLK, LANES), F32),
                pltpu.VMEM((R * QBLK, hd), F32),
                pltpu.VMEM((QBLK, S // SLC_BLOCK), BF16),
            ],
        ),
        out_shape=jax.ShapeDtypeStruct((B, S, NSA_QW), BF16),
        compiler_params=_cparams(("parallel", "parallel", "arbitrary")),
        name="nsa_prompt",
    )(p16, kc, vc, p16, p16, p16, p16, p32, tb, cb)


ROUTER_W = LANES
MOE_TM = 256


def _router_kernel(x_ref, g_ref, w_ref, b_ref, h_ref, info_ref):
    x = x_ref[...]
    hn = x * lax.rsqrt(jnp.mean(x * x, axis=-1, keepdims=True) + RMS_EPS) * g_ref[...]
    h_ref[...] = hn
    logits = _dot(hn.astype(BF16), w_ref[...]) + b_ref[...]
    lane = lax.broadcasted_iota(jnp.int32, (1, ROUTER_W), 1)
    big = ROUTER_W

    def first_lane(cond):
        return jnp.min(jnp.where(cond, lane, big), axis=-1, keepdims=True)

    gmask = lane < N_GROUPS
    gl = jnp.where(gmask, logits, NEG_INF)
    gm = jnp.max(gl, axis=-1, keepdims=True)
    g_idx = first_lane(gmask & (gl == gm))
    g_w = 1.0 / jnp.sum(jnp.where(gmask, jnp.exp(gl - gm), 0.0), axis=-1, keepdims=True)
    lo = N_GROUPS + g_idx * EXPERTS_PER_GROUP
    emask = (lane >= lo) & (lane < lo + EXPERTS_PER_GROUP)
    el = jnp.where(emask, logits, NEG_INF)
    em = jnp.max(el, axis=-1, keepdims=True)
    ee = jnp.where(emask, jnp.exp(el - em), 0.0)
    pe = jnp.where(emask, ee / jnp.sum(ee, axis=-1, keepdims=True), -1.0)
    v1 = jnp.max(pe, axis=-1, keepdims=True)
    i1 = first_lane(pe == v1)
    pe2 = jnp.where(lane == i1, -1.0, pe)
    v2 = jnp.max(pe2, axis=-1, keepdims=True)
    i2 = first_lane((pe2 == v2) & emask & (lane != i1))
    tot = v1 + v2
    info = jnp.where(lane == 0, (i1 - N_GROUPS).astype(F32), 0.0)
    info = jnp.where(lane == 1, (i2 - N_GROUPS).astype(F32), info)
    info = jnp.where(lane == 2, g_w * (v1 / tot), info)
    info = jnp.where(lane == 3, g_w * (v2 / tot), info)
    info_ref[...] = info


def moe_router(x, g, w_group, b_group, w_expert, b_expert, *, tm):
    T, D = x.shape
    pad = ROUTER_W - N_GROUPS - N_EXPERTS
    w = jnp.pad(jnp.concatenate([w_group, w_expert], axis=1), ((0, 0), (0, pad)))
    b = jnp.pad(jnp.concatenate([b_group, b_expert]), (0, pad)).reshape(1, ROUTER_W)
    return pl.pallas_call(
        _router_kernel,
        grid=(T // tm,),
        in_specs=[
            pl.BlockSpec((tm, D), lambda i: (i, 0)),
            pl.BlockSpec((1, D), lambda i: (0, 0)),
            pl.BlockSpec((D, ROUTER_W), lambda i: (0, 0)),
            pl.BlockSpec((1, ROUTER_W), lambda i: (0, 0)),
        ],
        out_specs=[pl.BlockSpec((tm, D), lambda i: (i, 0)), pl.BlockSpec((tm, ROUTER_W), lambda i: (i, 0))],
        out_shape=[jax.ShapeDtypeStruct((T, D), F32), jax.ShapeDtypeStruct((T, ROUTER_W), F32)],
        compiler_params=_cparams(("parallel",)),
        name="moe_router",
    )(x, g.reshape(1, D), w.astype(BF16), b)


MOE_ROW_CHUNK = 128


def _moe_ffn_kernel(te_ref, nu_ref, nx_ref, x_ref, gw_ref, wg_hbm, wu_hbm, wd_hbm, y_ref,
                    stage_g, stage_u, stage_d, wg16, wu16, wd16, sem, *, layer):
    i = pl.program_id(0)

    def weight_copies(e):
        return (pltpu.make_async_copy(wg_hbm.at[layer, e], stage_g, sem.at[0]),
                pltpu.make_async_copy(wu_hbm.at[layer, e], stage_u, sem.at[1]),
                pltpu.make_async_copy(wd_hbm.at[layer, e], stage_d, sem.at[2]))

    @pl.when(i == 0)
    def _():
        for c in weight_copies(te_ref[0]):
            c.start()

    @pl.when((i == 0) | (te_ref[i] != te_ref[jnp.maximum(i - 1, 0)]))
    def _():
        for c, stage, w16 in zip(weight_copies(te_ref[i]), (stage_g, stage_u, stage_d), (wg16, wu16, wd16)):
            c.wait()
            w16[...] = stage[...].astype(BF16)

        @pl.when(nx_ref[i] >= 0)
        def _():
            for c in weight_copies(nx_ref[i]):
                c.start()

    @pl.when(i < nu_ref[0])
    def _():
        for r in range(MOE_TM // MOE_ROW_CHUNK):
            rows = slice(r * MOE_ROW_CHUNK, (r + 1) * MOE_ROW_CHUNK)
            x = x_ref[rows, :].astype(BF16)
            h = _dot(x, wg16[...])
            u = _dot(x, wu16[...])
            y_ref[rows, :] = _dot(((h * jax.nn.sigmoid(h)) * u * gw_ref[rows, :]).astype(BF16), wd16[...])

    @pl.when(i >= nu_ref[0])
    def _():
        y_ref[...] = jnp.zeros(y_ref.shape, F32)


def moe_ffn(xs, gate_rows, tile_expert, n_used, next_expert, w_gate, w_up, w_down, layer):
    M, D = xs.shape
    F = w_gate.shape[3]
    n_tiles = M // MOE_TM
    hbm = pl.BlockSpec(memory_space=pl.ANY)
    return pl.pallas_call(
        functools.partial(_moe_ffn_kernel, layer=layer),
        grid_spec=pltpu.PrefetchScalarGridSpec(
            num_scalar_prefetch=3,
            grid=(n_tiles,),
            in_specs=[pl.BlockSpec((MOE_TM, D), lambda i, te, nu, nx: (i, 0)),
                      pl.BlockSpec((MOE_TM, 1), lambda i, te, nu, nx: (i, 0)), hbm, hbm, hbm],
            out_specs=pl.BlockSpec((MOE_TM, D), lambda i, te, nu, nx: (i, 0)),
            scratch_shapes=[pltpu.VMEM((D, F), F32), pltpu.VMEM((D, F), F32), pltpu.VMEM((F, D), F32),
                            pltpu.VMEM((D, F), BF16), pltpu.VMEM((D, F), BF16), pltpu.VMEM((F, D), BF16),
                            pltpu.SemaphoreType.DMA((3,))],
        ),
        out_shape=jax.ShapeDtypeStruct((M, D), F32),
        compiler_params=_cparams(("arbitrary",)),
        name="moe_ffn",
    )(tile_expert, n_used, next_expert, xs, gate_rows, w_gate, w_up, w_down)


def moe_plan(info):
    T = info.shape[0]
    eid = info[:, :TOP_K_INNER].astype(jnp.int32).reshape(-1)
    M = T * TOP_K_INNER
    n_tiles = -(-(M + N_EXPERTS * (MOE_TM - 1)) // MOE_TM)
    onehot = (eid[:, None] == jnp.arange(N_EXPERTS)[None, :]).astype(jnp.int32)
    rank = jnp.cumsum(onehot, axis=0) - onehot
    counts = jnp.sum(onehot, axis=0)
    tiles_per = (counts + MOE_TM - 1) // MOE_TM
    tile_end = jnp.cumsum(tiles_per)
    offs = (tile_end - tiles_per) * MOE_TM
    dest = jnp.sum(onehot * (offs[None, :] + rank), axis=1).astype(jnp.int32)
    n_used = tile_end[-1]
    tile_ids = jnp.arange(n_tiles)
    tile_expert = jnp.minimum(jnp.sum(tile_ids[:, None] >= tile_end[None, :], axis=1), N_EXPERTS - 1)
    last_e = jnp.max(jnp.where(counts > 0, jnp.arange(N_EXPERTS), 0))
    tile_expert = jnp.where(tile_ids < n_used, tile_expert, last_e).astype(jnp.int32)
    ids = jnp.arange(N_EXPERTS)
    later = (ids[None, :] > ids[:, None]) & (counts[None, :] > 0)
    next_of = jnp.where(jnp.any(later, axis=1), jnp.argmax(later, axis=1), -1)
    next_expert = next_of[tile_expert].astype(jnp.int32)
    return (dest.reshape(T, TOP_K_INNER), tile_expert, n_used.reshape(1).astype(jnp.int32), next_expert,
            n_tiles * MOE_TM)


def _row_copy(src_ref, s, dst_ref, d, sem):
    return pltpu.make_async_copy(src_ref.at[pl.ds(s, 1), :], dst_ref.at[pl.ds(d, 1), :], sem)


def _moe_scatter_kernel(dest_ref, h_ref, xs_in_ref, xs_ref, sem):
    del xs_in_ref
    tm = h_ref.shape[0]

    def issue(r, c):
        for slot in range(TOP_K_INNER):
            _row_copy(h_ref, r, xs_ref, dest_ref[0, 0, TOP_K_INNER * r + slot], sem).start()
        return c

    lax.fori_loop(0, tm, issue, 0)

    def drain(r, c):
        for slot in range(TOP_K_INNER):
            _row_copy(h_ref, 0, xs_ref, 0, sem).wait()
        return c

    lax.fori_loop(0, tm, drain, 0)


def moe_scatter(h, dest, xs, *, tm):
    T, D = h.shape
    assert T % tm == 0
    return pl.pallas_call(
        _moe_scatter_kernel,
        grid=(T // tm,),
        in_specs=[
            pl.BlockSpec((1, 1, TOP_K_INNER * tm), lambda i: (i, 0, 0), memory_space=pltpu.SMEM),
            pl.BlockSpec((tm, D), lambda i: (i, 0)),
            pl.BlockSpec(memory_space=pl.ANY),
        ],
        out_specs=pl.BlockSpec(memory_space=pl.ANY),
        out_shape=jax.ShapeDtypeStruct(xs.shape, xs.dtype),
        scratch_shapes=[pltpu.SemaphoreType.DMA(())],
        input_output_aliases={2: 0},
        compiler_params=_cparams(("arbitrary",)),
        name="moe_scatter",
    )(dest.reshape(T // tm, 1, TOP_K_INNER * tm), h, xs)


def _moe_combine_kernel(dest_ref, x_ref, y_ref, o_ref, buf, sem):
    tm = x_ref.shape[0]

    def issue(r, c):
        for slot in range(TOP_K_INNER):
            _row_copy(y_ref, dest_ref[0, 0, TOP_K_INNER * r + slot], buf.at[slot], r, sem).start()
        return c

    lax.fori_loop(0, tm, issue, 0)

    def drain(r, c):
        for slot in range(TOP_K_INNER):
            _row_copy(y_ref, 0, buf.at[slot], 0, sem).wait()
        return c

    lax.fori_loop(0, tm, drain, 0)
    out = x_ref[...]
    for slot in range(TOP_K_INNER):
        out = out + buf[slot]
    o_ref[...] = out


def moe_combine(x, dest, y, *, tm):
    T, D = x.shape
    assert T % tm == 0
    return pl.pallas_call(
        _moe_combine_kernel,
        grid=(T // tm,),
        in_specs=[
            pl.BlockSpec((1, 1, TOP_K_INNER * tm), lambda i: (i, 0, 0), memory_space=pltpu.SMEM),
            pl.BlockSpec((tm, D), lambda i: (i, 0)),
            pl.BlockSpec(memory_space=pl.ANY),
        ],
        out_specs=pl.BlockSpec((tm, D), lambda i: (i, 0)),
        out_shape=jax.ShapeDtypeStruct((T, D), F32),
        scratch_shapes=[pltpu.VMEM((TOP_K_INNER, tm, D), F32), pltpu.SemaphoreType.DMA(())],
        compiler_params=_cparams(("arbitrary",)),
        name="moe_combine",
    )(dest.reshape(T // tm, 1, TOP_K_INNER * tm), x, y)


MOE_ROW_TILE = 256


def moe_layer(xp, xs, g, w_group, b_group, w_expert, b_expert, w_gate, w_up, w_down, layer):
    Tp, Ts = xp.shape[0], xs.shape[0]
    hp, ip = moe_router(xp, g, w_group, b_group, w_expert, b_expert, tm=512)
    hs, is_ = moe_router(xs, g, w_group, b_group, w_expert, b_expert, tm=Ts)
    info = jnp.concatenate([ip, is_], axis=0)
    dest, tile_expert, n_used, next_expert, m_rows = moe_plan(info)
    gate_rows = jnp.zeros((m_rows,), F32).at[dest.reshape(-1)].set(
        info[:, TOP_K_INNER:2 * TOP_K_INNER].reshape(-1)).reshape(m_rows, 1)
    rows = jnp.zeros((m_rows, xp.shape[1]), F32)
    rows = moe_scatter(hp, dest[:Tp], rows, tm=MOE_ROW_TILE)
    rows = moe_scatter(hs, dest[Tp:], rows, tm=Ts)
    y = moe_ffn(rows, gate_rows, tile_expert, n_used, next_expert, w_gate, w_up, w_down, layer)
    return (moe_combine(xp, dest[:Tp], y, tm=MOE_ROW_TILE), moe_combine(xs, dest[Tp:], y, tm=Ts))


N_DIL = len(DIL_PATTERNS)
DIL_SPANS = tuple(w // d for w, d in DIL_PATTERNS)
DIL_STEPS = tuple(d for _, d in DIL_PATTERNS)
assert all(s <= QBLK for s in DIL_SPANS)
assert all(d & (d - 1) == 0 for d in DIL_STEPS)


def _dil_bias_kernel(dil_ref, tab_ref, o_ref):
    p = pl.program_id(0)
    dm = QBLK + lax.broadcasted_iota(jnp.int32, (QBLK, 2 * QBLK), 0) - lax.broadcasted_iota(jnp.int32, (QBLK, 2 * QBLK), 1)
    bucket = _t5_bucket(dm * dil_ref[p])
    for h in range(DIL_HEADS):
        o_ref[0, h] = _table_lookup(bucket, tab_ref, h)


def dil_bias_tiles(table):
    return pl.pallas_call(
        _dil_bias_kernel,
        grid=(N_DIL,),
        in_specs=[pl.BlockSpec(memory_space=pltpu.SMEM), pl.BlockSpec(memory_space=pltpu.SMEM)],
        out_specs=pl.BlockSpec((1, DIL_HEADS, QBLK, 2 * QBLK), lambda p: (p, 0, 0, 0)),
        out_shape=jax.ShapeDtypeStruct((N_DIL, DIL_HEADS, QBLK, 2 * QBLK), F32),
        compiler_params=_cparams(("arbitrary",)),
        name="dil_bias_tiles",
    )(jnp.array(DIL_STEPS, jnp.int32), table)


def _dil_prompt_kernel(q_ref, kp_ref, kc_ref, vp_ref, vc_ref, b_ref, num_ref, ml_ref, *, span, scale):
    mb = pl.program_id(2)
    qi = lax.broadcasted_iota(jnp.int32, (QBLK, 2 * QBLK), 0)
    kj = lax.broadcasted_iota(jnp.int32, (QBLK, 2 * QBLK), 1)
    dm = QBLK + qi - kj
    mask = (dm >= 0) & (dm <= span) & ((kj >= QBLK) | (mb > 0))
    lane = lax.broadcasted_iota(jnp.int32, (1, LANES), 1)
    m_t = jnp.zeros((QBLK, LANES), F32)
    l_t = jnp.zeros((QBLK, LANES), F32)
    for h in range(DIL_HEADS):
        cs = slice(h * HEAD_DIM, (h + 1) * HEAD_DIM)
        k = jnp.concatenate([kp_ref[0, :, cs], kc_ref[0, :, cs]], axis=0)
        v = jnp.concatenate([vp_ref[0, :, cs], vc_ref[0, :, cs]], axis=0)
        s = _dot_t(q_ref[0, :, cs], k) * scale + b_ref[0, h]
        s = jnp.where(mask, s, NEG_INF)
        m = jnp.max(s, axis=-1, keepdims=True)
        p = jnp.where(mask, jnp.exp(s - m), 0.0)
        l = jnp.sum(p, axis=-1, keepdims=True)
        num_ref[0, :, cs] = _dot(p.astype(BF16), v)
        m_t = jnp.where(lane == h, m, m_t)
        l_t = jnp.where(lane == h, l, l_t)
    ml_ref[0, :, 0:LANES] = m_t
    ml_ref[0, :, LANES:2 * LANES] = l_t


def dil_prompt_pattern(p16, bias, pat, B, S):
    span, dil = DIL_SPANS[pat], DIL_STEPS[pat]
    M = S // dil
    assert M % QBLK == 0
    pv = p16.reshape(B, M, dil * 3 * DIL_W)
    kern = functools.partial(_dil_prompt_kernel, span=span, scale=HEAD_DIM ** -0.5)
    blk = (1, QBLK, DIL_W)
    prev = lambda m: jnp.maximum(m - 1, 0)
    num, ml = pl.pallas_call(
        kern,
        grid=(B, dil, M // QBLK),
        in_specs=[
            pl.BlockSpec(blk, lambda b, r, m: (b, m, 3 * r)),
            pl.BlockSpec(blk, lambda b, r, m: (b, prev(m), 3 * r + 1)),
            pl.BlockSpec(blk, lambda b, r, m: (b, m, 3 * r + 1)),
            pl.BlockSpec(blk, lambda b, r, m: (b, prev(m), 3 * r + 2)),
            pl.BlockSpec(blk, lambda b, r, m: (b, m, 3 * r + 2)),
            pl.BlockSpec((1, DIL_HEADS, QBLK, 2 * QBLK), lambda b, r, m: (pat, 0, 0, 0)),
        ],
        out_specs=[
            pl.BlockSpec(blk, lambda b, r, m: (b, m, r)),
            pl.BlockSpec((1, QBLK, 2 * LANES), lambda b, r, m: (b, m, r)),
        ],
        out_shape=[jax.ShapeDtypeStruct((B, M, dil * DIL_W), F32), jax.ShapeDtypeStruct((B, M, dil * 2 * LANES), F32)],
        compiler_params=_cparams(("parallel", "parallel", "arbitrary")),
        name=f"dil_prompt_{dil}",
    )(pv, pv, pv, pv, pv, bias)
    return num.reshape(B * S, DIL_W), ml.reshape(B * S, 2 * LANES)


def _dil_merge_kernel(*refs):
    nums, mls, o_ref = refs[:N_DIL], refs[N_DIL:2 * N_DIL], refs[2 * N_DIL]
    m_all = [r[:, 0:LANES] for r in mls]
    l_all = [r[:, LANES:2 * LANES] for r in mls]
    m_max = m_all[0]
    for m in m_all[1:]:
        m_max = jnp.maximum(m_max, m)
    w_all = [jnp.exp(m - m_max) for m in m_all]
    den = w_all[0] * l_all[0]
    for w, l in zip(w_all[1:], l_all[1:]):
        den = den + w * l
    for h in range(DIL_HEADS):
        cs = slice(h * HEAD_DIM, (h + 1) * HEAD_DIM)
        acc = w_all[0][:, h:h + 1] * nums[0][:, cs]
        for w, n in zip(w_all[1:], nums[1:]):
            acc = acc + w[:, h:h + 1] * n[:, cs]
        o_ref[:, cs] = (acc / den[:, h:h + 1]).astype(o_ref.dtype)


def dil_merge(nums, mls, *, tm):
    T = nums[0].shape[0]
    return pl.pallas_call(
        _dil_merge_kernel,
        grid=(T // tm,),
        in_specs=[pl.BlockSpec((tm, DIL_W), lambda i: (i, 0))] * N_DIL + [pl.BlockSpec((tm, 2 * LANES), lambda i: (i, 0))] * N_DIL,
        out_specs=pl.BlockSpec((tm, DIL_W), lambda i: (i, 0)),
        out_shape=jax.ShapeDtypeStruct((T, DIL_W), BF16),
        compiler_params=_cparams(("parallel",)),
        name="dil_merge",
    )(*nums, *mls)


def _block_diag_rows(q, n_heads):
    DB, Q, W = q.shape
    dh = W // n_heads
    eye = jnp.repeat(jnp.eye(n_heads, dtype=q.dtype), dh, axis=1)
    return (q[:, None, :, :] * eye[None, :, None, :]).reshape(DB, n_heads * Q, W)


def _own_block(acc, n_heads, rows_per_head):
    rows = acc.shape[0]
    rh = lax.broadcasted_iota(jnp.int32, (rows, 1), 0) // rows_per_head
    out = jnp.zeros((rows, HEAD_DIM), F32)
    for h in range(n_heads):
        out = jnp.where(rh == h, acc[:, h * HEAD_DIM:(h + 1) * HEAD_DIM], out)
    return out


def _row_bias(bucket, tab_ref, head0, n_heads, rows_per_head):
    rows = bucket.shape[0]
    rh = lax.broadcasted_iota(jnp.int32, (rows, 1), 0) // rows_per_head
    out = jnp.zeros(bucket.shape, F32)
    for h in range(n_heads):
        out = jnp.where(rh == h, _table_lookup(bucket, tab_ref, head0 + h), out)
    return out


def _flash_step_cols(s, mask, v, m_scr, l_scr, acc_scr):
    s = jnp.where(mask, s, NEG_INF)
    m_prev = m_scr[:, 0:1]
    m_new = jnp.maximum(m_prev, jnp.max(s, axis=-1, keepdims=True))
    p = jnp.where(mask, jnp.exp(s - m_new), 0.0)
    alpha = jnp.exp(m_prev - m_new)
    l_new = alpha * l_scr[:, 0:1] + jnp.sum(p, axis=-1, keepdims=True)
    acc_scr[...] = alpha * acc_scr[...] + _dot(p.astype(BF16), v)
    m_scr[...] = jnp.broadcast_to(m_new, m_scr.shape)
    l_scr[...] = jnp.broadcast_to(l_new, l_scr.shape)


def _head_cols(ref, first, n_heads, n_rows, stride, row0=0):
    return jnp.concatenate(
        [ref[pl.ds(row0 * stride + first + h, n_rows, stride=stride), :] for h in range(n_heads)], axis=1)


SB_DEC_PAGES = 8


def _sb_decode_kernel(pt_ref, q_ref, new_ref, *rest, n_steps, n_q, scale):
    pages, (o_ref, acc_scr, run_scr) = rest[:SB_DEC_PAGES], rest[SB_DEC_PAGES:]
    s_ = pl.program_id(1)
    q = q_ref[0]
    rows = q.shape[0]
    upper2 = _upper01(PAGE_SIZE)

    def absorb(k, v, mask):
        z2 = _dot_t(q, k.astype(BF16)) * (scale * LOG2_E)
        lk = _log2_keep(z2)
        lb = z2 + lk
        if mask is not None:
            lk = jnp.where(mask, lk, 0.0)
        w = jnp.exp2(lb + _after_in_tile(lk, upper2) + run_scr[...])
        if mask is not None:
            w = jnp.where(mask, w, 0.0)
        acc_scr[...] = acc_scr[...] + _dot(w.astype(BF16), v.astype(BF16))
        run_scr[...] = run_scr[...] + jnp.sum(lk, axis=-1, keepdims=True)

    @pl.when(s_ == 0)
    def _():
        acc_scr[...] = jnp.zeros(acc_scr.shape, F32)
        run_scr[...] = jnp.zeros(run_scr.shape, F32)
        t = lax.broadcasted_iota(jnp.int32, (rows, PAGE_SIZE), 1)
        qi = lax.broadcasted_iota(jnp.int32, (rows, PAGE_SIZE), 0) % n_q
        absorb(new_ref[0, :, :SB_W], new_ref[0, :, SB_W:], (t < qi) & (t < n_q))

    for page in pages:
        absorb(_head_cols(page, 0, SB_HEADS, PAGE_SIZE, 2 * SB_HEADS),
               _head_cols(page, SB_HEADS, SB_HEADS, PAGE_SIZE, 2 * SB_HEADS), None)

    @pl.when(s_ == n_steps - 1)
    def _():
        o_ref[0] = _own_block(acc_scr[...], SB_HEADS, n_q)


def sb_decode(page_table, qbd16, new_kv, cache_lines):
    DB, n_pages = page_table.shape
    assert n_pages % SB_DEC_PAGES == 0
    n_steps = n_pages // SB_DEC_PAGES
    rows = qbd16.shape[1]
    n_q = rows // SB_HEADS
    lines = PAGE_SIZE * 2 * SB_HEADS
    kern = functools.partial(_sb_decode_kernel, n_steps=n_steps, n_q=n_q, scale=HEAD_DIM ** -0.5)
    page_spec = lambda j: pl.BlockSpec(
        (lines, HEAD_DIM), lambda b, s, pt: (pt[b, n_pages - 1 - (s * SB_DEC_PAGES + j)], 0))
    return pl.pallas_call(
        kern,
        grid_spec=pltpu.PrefetchScalarGridSpec(
            num_scalar_prefetch=1,
            grid=(DB, n_steps),
            in_specs=[
                pl.BlockSpec((1, rows, SB_W), lambda b, s, pt: (b, 0, 0)),
                pl.BlockSpec((1, PAGE_SIZE, 2 * SB_W), lambda b, s, pt: (b, 0, 0)),
            ] + [page_spec(j) for j in range(SB_DEC_PAGES)],
            out_specs=pl.BlockSpec((1, rows, HEAD_DIM), lambda b, s, pt: (b, 0, 0)),
            scratch_shapes=[pltpu.VMEM((rows, SB_W), F32), pltpu.VMEM((rows, LANES), F32)],
        ),
        out_shape=jax.ShapeDtypeStruct((DB, rows, HEAD_DIM), F32),
        compiler_params=_cparams(("parallel", "arbitrary")),
        name="sb_decode",
    )(page_table, qbd16, new_kv, *([cache_lines] * SB_DEC_PAGES))


CMP_ROWS_TILE = 256
CHUNKS_PER_PAGE = PAGE_SIZE // CMP_STRIDE


CMP_DEC_PAGES = 8


def _nsa_cmp_decode_kernel(pt_ref, tab_ref, q_ref, w_ref, pe_ref, g_ref, *rest, n_steps, n_q, past, scale):
    pages, (oc_ref, sel_ref, a_scr) = rest[:CMP_DEC_PAGES], rest[CMP_DEC_PAGES:]
    s_ = pl.program_id(1)
    G, R = NSA_KV_HEADS, NSA_REP
    slots = 2 * G
    n_stage = CMP_DEC_PAGES * CHUNKS_PER_PAGE
    rs = pl.ds(pl.multiple_of(s_ * n_stage, n_stage), n_stage)
    for kv in range(2):
        for g in range(G):
            kvg = kv * G + g
            off = kv * NSA_KVW + g * HEAD_DIM
            xs = jnp.concatenate(
                [jnp.concatenate([pg[pl.ds(l * slots + kvg, CHUNKS_PER_PAGE, stride=CMP_STRIDE * slots), :]
                                  for pg in pages], axis=0) for l in range(CMP_STRIDE)], axis=1)
            for j in range(CMP_HALF):
                a_scr[j, rs, off:off + HEAD_DIM] = _dot((xs + pe_ref[kv, j]).astype(BF16), w_ref[kv, j])

    @pl.when(s_ == n_steps - 1)
    def _():
        n_chunk = a_scr.shape[1]
        n_c = n_chunk - CMP_HALF + 1
        out = a_scr[0]
        for j in range(1, CMP_HALF):
            out = out + pltpu.roll(a_scr[j], n_chunk - j, axis=0)
        kparts = []
        for g in range(G):
            kg = out[:, g * HEAD_DIM:(g + 1) * HEAD_DIM]
            kparts.append(kg * lax.rsqrt(jnp.mean(kg * kg, axis=-1, keepdims=True) + RMS_EPS) * g_ref[...])
        kc = jnp.concatenate(kparts, axis=1).astype(BF16)
        vc = out[:, NSA_KVW:2 * NSA_KVW].astype(BF16)

        q = q_ref[0]
        nrows = q.shape[0]
        qi = lax.broadcasted_iota(jnp.int32, (nrows, 1), 0) % n_q
        n_idx = lax.broadcasted_iota(jnp.int32, (1, n_chunk), 1)
        dist = past + qi - (n_idx * CMP_STRIDE + (CMP_BLOCK - 1))
        mask = (dist >= 0) & (n_idx < n_c)
        s = _dot_t(q, kc) * scale + _row_bias(_t5_bucket(dist), tab_ref, 0, NSA_HEADS, n_q)
        s = jnp.where(mask, s, NEG_INF)
        m = jnp.max(s, axis=-1, keepdims=True)
        p = jnp.where(mask, jnp.exp(s - m), 0.0)
        p = p / jnp.maximum(jnp.sum(p, axis=-1, keepdims=True), 1.0)
        oc_ref[0] = _own_block(_dot(p.astype(BF16), vc), G, R * n_q)
        gq = G * n_q
        oi = lax.broadcasted_iota(jnp.int32, (gq, nrows), 0)
        ri = lax.broadcasted_iota(jnp.int32, (gq, nrows), 1)
        pick = jnp.where((ri // (R * n_q) == oi // n_q) & (ri % n_q == oi % n_q), 1.0, 0.0).astype(BF16)
        p1, p2, p3 = _split3(p)
        imp = _dot(pick, p1) + _dot(pick, p2) + _dot(pick, p3)
        n_sp = sel_ref.shape[2]
        ratio = SLC_BLOCK // CMP_STRIDE
        ci = lax.broadcasted_iota(jnp.int32, (n_chunk, n_sp), 0)
        ti = lax.broadcasted_iota(jnp.int32, (n_chunk, n_sp), 1)
        gather01 = jnp.where((ci >= ti * ratio - (CMP_HALF - 1)) & (ci < (ti + 1) * ratio), 1.0, 0.0).astype(BF16)
        score = _dot01(imp, gather01)
        q_blk = (past + lax.broadcasted_iota(jnp.int32, (gq, 1), 0) % n_q) // SLC_BLOCK
        sel_ref[0] = _select_blocks(score, q_blk, N_SELECT)


def nsa_cmp_decode(page_table, table, qbd16, cache_lines, w16, pe, ck_gain, n_q):
    DB, n_pages = page_table.shape
    assert CMP_DEC_PAGES % 2 == 0 and n_pages % CMP_DEC_PAGES == 0
    n_steps = n_pages // CMP_DEC_PAGES
    n_chunk = n_pages * CHUNKS_PER_PAGE
    assert n_chunk % CMP_ROWS_TILE == 0
    past = n_pages * PAGE_SIZE
    n_s = -(-(past + n_q) // SLC_BLOCK)
    n_sp = -(-n_s // LANES) * LANES
    rows = qbd16.shape[1]
    kern = functools.partial(_nsa_cmp_decode_kernel, n_steps=n_steps, n_q=n_q, past=past, scale=HEAD_DIM ** -0.5)
    pg = (PAGE_SIZE * 2 * NSA_KV_HEADS, HEAD_DIM)
    return pl.pallas_call(
        kern,
        grid_spec=pltpu.PrefetchScalarGridSpec(
            num_scalar_prefetch=1,
            grid=(DB, n_steps),
            in_specs=[
                pl.BlockSpec(memory_space=pltpu.SMEM),
                pl.BlockSpec((1, rows, NSA_KVW), lambda b, s, pt: (b, 0, 0)),
                pl.BlockSpec((2, CMP_HALF, CMP_STRIDE * HEAD_DIM, HEAD_DIM), lambda b, s, pt: (0, 0, 0, 0)),
                pl.BlockSpec((2, CMP_HALF, 1, CMP_STRIDE * HEAD_DIM), lambda b, s, pt: (0, 0, 0, 0)),
                pl.BlockSpec((1, HEAD_DIM), lambda b, s, pt: (0, 0)),
            ] + [pl.BlockSpec(pg, lambda b, s, pt, j=j: (pt[b, CMP_DEC_PAGES * s + j], 0)) for j in range(CMP_DEC_PAGES)],
            out_specs=[
                pl.BlockSpec((1, rows, HEAD_DIM), lambda b, s, pt: (b, 0, 0)),
                pl.BlockSpec((1, NSA_KV_HEADS * n_q, n_sp), lambda b, s, pt: (b, 0, 0)),
            ],
            scratch_shapes=[pltpu.VMEM((CMP_HALF, n_chunk, 2 * NSA_KVW), F32)],
        ),
        out_shape=[jax.ShapeDtypeStruct((DB, rows, HEAD_DIM), F32),
                   jax.ShapeDtypeStruct((DB, NSA_KV_HEADS * n_q, n_sp), F32)],
        compiler_params=_cparams(("parallel", "arbitrary")),
        name="nsa_cmp_decode",
    )(page_table, table, qbd16, w16, pe, ck_gain.reshape(1, HEAD_DIM), *([cache_lines] * CMP_DEC_PAGES))


SLC_DEC_PAGES = 4


def _nsa_slc_decode_kernel(pt_ref, plist_ref, pcnt_ref, tab_ref, q_ref, sel_ref, new_ref, wbuf_ref, wnew_ref, oc_ref,
                           gt_ref, *rest, n_steps, n_q, past, scale):
    pages, (o_ref, s_scr, v_scr) = rest[:SLC_DEC_PAGES], rest[SLC_DEC_PAGES:]
    b_ = pl.program_id(0)
    s_ = pl.program_id(1)
    q = q_ref[0]
    rows = q.shape[0]
    G, R = NSA_KV_HEADS, NSA_REP
    slots = 2 * G
    qi = lax.broadcasted_iota(jnp.int32, (rows, 1), 0) % n_q
    rh = lax.broadcasted_iota(jnp.int32, (rows, 1), 0) // n_q
    per_page = PAGE_SIZE // SLC_BLOCK
    n_sp = sel_ref.shape[2]
    col = lax.broadcasted_iota(jnp.int32, (1, PAGE_SIZE), 1)

    def sel_cols(first_block):
        bi = lax.broadcasted_iota(jnp.int32, (n_sp, PAGE_SIZE), 0)
        ki = lax.broadcasted_iota(jnp.int32, (n_sp, PAGE_SIZE), 1)
        expand = jnp.where(bi == first_block + ki // SLC_BLOCK, 1.0, 0.0).astype(BF16)
        return _dot(sel_ref[0], expand) > 0.5

    def near_bias(dist):
        return _row_bias(_t5_bucket(dist), tab_ref, 0, NSA_HEADS, n_q)

    def far_bias():
        out = jnp.zeros((rows, 1), F32)
        for h in range(NSA_HEADS):
            out = jnp.where(rh == h, tab_ref[N_BUCKETS - 1, h], out)
        return jnp.broadcast_to(out, (rows, PAGE_SIZE))

    def stash(slot, k, v, bias, mask):
        s = _dot_t(q, k.astype(BF16)) * scale + bias
        s_scr[slot] = jnp.where(mask, s, NEG_INF)
        v_scr[slot] = v.astype(BF16)

    def softmax_out(n_slots):
        m = lax.fori_loop(0, n_slots, lambda t, m: jnp.maximum(m, jnp.max(s_scr[t], axis=-1, keepdims=True)),
                          jnp.full((rows, 1), NEG_INF, F32))

        def probs(t):
            s = s_scr[t]
            return jnp.where(s > 0.5 * NEG_INF, jnp.exp(s - m), 0.0)

        l = lax.fori_loop(0, n_slots, lambda t, l: l + jnp.sum(probs(t), axis=-1, keepdims=True),
                          jnp.zeros((rows, 1), F32))
        den = jnp.maximum(l, 1.0)
        acc = lax.fori_loop(0, n_slots, lambda t, a: a + _dot((probs(t) / den).astype(BF16), v_scr[t]),
                            jnp.zeros((rows, NSA_KVW), F32))
        return _own_block(acc, G, R * n_q)

    for j, page in enumerate(pages):
        idx = s_ * SLC_DEC_PAGES + j

        @pl.when(idx < pcnt_ref[b_])
        def _(page=page, idx=idx):
            p = plist_ref[b_, idx]
            dist = past + qi - (p * PAGE_SIZE + col)
            bias = lax.cond(past - (p + 1) * PAGE_SIZE + 1 >= BUCKET_MAX_DIST, far_bias, lambda: near_bias(dist))
            stash(idx, _head_cols(page, 0, G, PAGE_SIZE, slots), _head_cols(page, G, G, PAGE_SIZE, slots), bias,
                  sel_cols(p * per_page) & (dist >= 0))

    @pl.when(s_ == n_steps - 1)
    def _():
        n_vis = pcnt_ref[b_]
        dist_n = qi - col
        stash(n_vis, new_ref[0, :, :NSA_KVW], new_ref[0, :, NSA_KVW:], near_bias(dist_n),
              sel_cols(past // SLC_BLOCK) & (dist_n >= 0) & (col < n_q))
        o_s = softmax_out(n_vis + 1)
        wb = wbuf_ref.shape[0] // slots
        n_wt = wb // PAGE_SIZE
        for t in range(n_wt):
            dist_w = wb + qi - (t * PAGE_SIZE + col)
            stash(t, _head_cols(wbuf_ref, 0, G, PAGE_SIZE, slots, row0=t * PAGE_SIZE),
                  _head_cols(wbuf_ref, G, G, PAGE_SIZE, slots, row0=t * PAGE_SIZE), near_bias(dist_w),
                  (dist_w >= 0) & (dist_w < NSA_WINDOW))
        stash(n_wt, wnew_ref[0, :, :NSA_KVW], wnew_ref[0, :, NSA_KVW:], near_bias(dist_n), (dist_n >= 0) & (col < n_q))
        o_w = softmax_out(n_wt + 1)
        gt = gt_ref[0]
        o_ref[0] = gt[:, 0:1] * oc_ref[0] + gt[:, 1:2] * o_s + gt[:, 2:3] * o_w


def nsa_slc_decode(page_table, table, qbd16, sel16, cache_lines, new_kv, win_lines, win_new, o_c, gates, n_q):
    DB, n_pages = page_table.shape
    assert n_pages % SLC_DEC_PAGES == 0
    n_steps = n_pages // SLC_DEC_PAGES
    rows = qbd16.shape[1]
    past = n_pages * PAGE_SIZE
    slots = 2 * NSA_KV_HEADS
    wb_lines = win_lines.shape[0] // DB
    assert (wb_lines // slots) % PAGE_SIZE == 0
    n_slots = max(n_pages, wb_lines // slots // PAGE_SIZE) + 1
    kern = functools.partial(_nsa_slc_decode_kernel, n_steps=n_steps, n_q=n_q, past=past, scale=HEAD_DIM ** -0.5)
    per_page = PAGE_SIZE // SLC_BLOCK
    need = jnp.any(sel16[:, :, :n_pages * per_page].reshape(DB, rows, n_pages, per_page) > 0, axis=(1, 3))
    plist = jnp.argsort(jnp.logical_not(need), axis=1, stable=True).astype(jnp.int32)
    pcnt = jnp.maximum(jnp.sum(need, axis=1), 1).astype(jnp.int32)
    full = lambda shape: pl.BlockSpec((1,) + shape, lambda b, s, pt, pls, pc: (b, 0, 0))

    def page_spec(j):
        def index(b, s, pt, pls, pc):
            return (pt[b, pls[b, jnp.minimum(s * SLC_DEC_PAGES + j, pc[b] - 1)]], 0)
        return pl.BlockSpec((PAGE_SIZE * slots, HEAD_DIM), index)

    return pl.pallas_call(
        kern,
        grid_spec=pltpu.PrefetchScalarGridSpec(
            num_scalar_prefetch=3,
            grid=(DB, n_steps),
            in_specs=[
                pl.BlockSpec(memory_space=pltpu.SMEM),
                full((rows, NSA_KVW)),
                full((rows, sel16.shape[2])),
                full((PAGE_SIZE, 2 * NSA_KVW)),
                pl.BlockSpec((wb_lines, HEAD_DIM), lambda b, s, pt, pls, pc: (b, 0)),
                full((PAGE_SIZE, 2 * NSA_KVW)),
                full((rows, HEAD_DIM)),
                full((rows, LANES)),
            ] + [page_spec(j) for j in range(SLC_DEC_PAGES)],
            out_specs=full((rows, HEAD_DIM)),
            scratch_shapes=[pltpu.VMEM((n_slots, rows, PAGE_SIZE), F32),
                            pltpu.VMEM((n_slots, PAGE_SIZE, NSA_KVW), BF16)],
        ),
        out_shape=jax.ShapeDtypeStruct((DB, rows, HEAD_DIM), F32),
        compiler_params=_cparams(("parallel", "arbitrary")),
        name="nsa_slc_decode",
    )(page_table, plist, pcnt, table, qbd16, sel16, new_kv, win_lines, win_new, o_c, gates,
      *([cache_lines] * SLC_DEC_PAGES))


DIL_DEC_TILE = 256


def _dil_decode_kernel(tab_ref, q_ref, buf_ref, new_ref, o_ref, s_scr, v_scr, *, n_tiles, n_q, wb, scale):
    kt = pl.program_id(1)
    q = q_ref[0]
    rows = q.shape[0]
    qi = lax.broadcasted_iota(jnp.int32, (rows, 1), 0) % n_q
    n = DIL_DEC_TILE
    slots = 2 * DIL_HEADS
    col = lax.broadcasted_iota(jnp.int32, (1, n), 1)

    def stash(slot, k, v, dist):
        s_scr[slot] = _dot_t(q, k.astype(BF16)) * scale + _row_bias(_t5_bucket(dist), tab_ref, 0, DIL_HEADS, n_q)
        v_scr[slot] = v.astype(BF16)

    stash(kt, _head_cols(buf_ref, 0, DIL_HEADS, n, slots), _head_cols(buf_ref, DIL_HEADS, DIL_HEADS, n, slots),
          wb + qi - (kt * n + col))

    @pl.when(kt == n_tiles - 1)
    def _():
        stash(n_tiles, new_ref[0, :, :DIL_W], new_ref[0, :, DIL_W:], qi - col)

        def tile_mask(t, span, dil):
            dist = (wb + qi - (t * n + col)) if t < n_tiles else (qi - col)
            ok = (dist >= 0) & (jnp.bitwise_and(dist, dil - 1) == 0) & (dist <= span * dil)
            return ok if t < n_tiles else ok & (col < n_q)

        nums, ms, ls = [], [], []
        for span, dil in zip(DIL_SPANS, DIL_STEPS):
            masks = [tile_mask(t, span, dil) for t in range(n_tiles + 1)]
            m = jnp.full((rows, 1), NEG_INF, F32)
            for t, mk in enumerate(masks):
                m = jnp.maximum(m, jnp.max(jnp.where(mk, s_scr[t], NEG_INF), axis=-1, keepdims=True))
            l = jnp.zeros((rows, 1), F32)
            acc = jnp.zeros((rows, DIL_W), F32)
            for t, mk in enumerate(masks):
                p = jnp.where(mk, jnp.exp(s_scr[t] - m), 0.0)
                l = l + jnp.sum(p, axis=-1, keepdims=True)
                acc = acc + _dot(p.astype(BF16), v_scr[t])
            nums.append(_own_block(acc, DIL_HEADS, n_q))
            ms.append(m)
            ls.append(l)
        m_max = ms[0]
        for m in ms[1:]:
            m_max = jnp.maximum(m_max, m)
        ws = [jnp.exp(m - m_max) for m in ms]
        num = ws[0] * nums[0]
        den = ws[0] * ls[0]
        for w, n_, l in zip(ws[1:], nums[1:], ls[1:]):
            num = num + w * n_
            den = den + w * l
        o_ref[0] = num / den


def dil_decode(table, qbd16, buf_lines, new_kv, n_q):
    assert new_kv.shape[1] == DIL_DEC_TILE
    DB = qbd16.shape[0]
    slots = 2 * DIL_HEADS
    wb = buf_lines.shape[0] // (DB * slots)
    rows = qbd16.shape[1]
    assert wb % DIL_DEC_TILE == 0
    n_tiles = wb // DIL_DEC_TILE
    kern = functools.partial(_dil_decode_kernel, n_tiles=n_tiles, n_q=n_q, wb=wb, scale=HEAD_DIM ** -0.5)
    return pl.pallas_call(
        kern,
        grid=(DB, n_tiles),
        in_specs=[
            pl.BlockSpec(memory_space=pltpu.SMEM),
            pl.BlockSpec((1, rows, DIL_W), lambda b, t: (b, 0, 0)),
            pl.BlockSpec((DIL_DEC_TILE * slots, HEAD_DIM), lambda b, t: (b * n_tiles + t, 0)),
            pl.BlockSpec((1, new_kv.shape[1], 2 * DIL_W), lambda b, t: (b, 0, 0)),
        ],
        out_specs=pl.BlockSpec((1, rows, HEAD_DIM), lambda b, t: (b, 0, 0)),
        out_shape=jax.ShapeDtypeStruct((DB, rows, HEAD_DIM), F32),
        scratch_shapes=[pltpu.VMEM((n_tiles + 1, rows, DIL_DEC_TILE), F32),
                        pltpu.VMEM((n_tiles + 1, DIL_DEC_TILE, DIL_W), BF16)],
        compiler_params=_cparams(("parallel", "arbitrary")),
        name="dil_decode",
    )(table, qbd16, buf_lines, new_kv)


PROJ_TN = 256
PROJ_TM = 512
_C_SBQ, _C_SBK, _C_SBV = 0, SB_W, 2 * SB_W
_C_NQ = 3 * SB_W
_C_CMP = _C_NQ + NSA_QW
_C_SLC = _C_CMP + 2 * NSA_KVW
_C_WIN = _C_SLC + 2 * NSA_KVW
_C_GATE = _C_WIN + 2 * NSA_KVW
SPARSE_NP = -(-SPARSE_IN // PROJ_TN) * PROJ_TN


def _sparse_proj_plan(w_in, qk_gain):
    w16 = jnp.pad(w_in, ((0, 0), (0, SPARSE_NP - SPARSE_IN))).astype(BF16)
    gain = jnp.ones((SPARSE_NP,), F32)
    flag = np.zeros((SPARSE_NP,), np.float32)
    kinds = np.full((SPARSE_NP // PROJ_TN,), EPI_PLAIN, np.int32)
    for col, width, gi in ((_C_NQ, NSA_QW, 0), (_C_SLC, NSA_KVW, 2), (_C_WIN, NSA_KVW, 3)):
        gain = gain.at[col:col + width].set(jnp.tile(qk_gain[gi], width // HEAD_DIM))
        flag[col:col + width] = 1.0
        assert col % PROJ_TN == 0 and width % PROJ_TN == 0
        kinds[col // PROJ_TN:(col + width) // PROJ_TN] = EPI_NORM
    assert _C_GATE % PROJ_TN == 0 and SPARSE_NP - _C_GATE == PROJ_TN
    kinds[_C_GATE // PROJ_TN] = EPI_SIGMOID
    return w16, gain, jnp.asarray(flag), kinds


def _dil_proj_plan(w_in, qk_gain):
    N = 3 * DIL_W
    gain = jnp.concatenate([jnp.tile(qk_gain[0], DIL_HEADS), jnp.tile(qk_gain[1], DIL_HEADS), jnp.ones((DIL_W,), F32)])
    flag = np.concatenate([np.ones((2 * DIL_W,), np.float32), np.zeros((DIL_W,), np.float32)])
    kinds = np.full((N // PROJ_TN,), EPI_PLAIN, np.int32)
    kinds[:2 * DIL_W // PROJ_TN] = EPI_NORM
    return w_in.astype(BF16), gain, jnp.asarray(flag), kinds


def _pad_rows(a, n):
    return jnp.pad(a, ((0, 0), (0, n - a.shape[1]), (0, 0)))


def _rows_to_tokens(o, n_heads, n_q):
    DB = o.shape[0]
    return o.reshape(DB, n_heads, n_q, HEAD_DIM).transpose(0, 2, 1, 3).reshape(DB * n_q, n_heads * HEAD_DIM)


def kernel(x_prompt, x_sample, cache_sb_kv, cache_nsa_cmp_kv, cache_nsa_slc_kv, state_nsa_win_kv, state_dil_kv,
           page_table, rel_bias_table, norm_mix, norm_ffn, sparse_w_in, sparse_w_out, nsa_qk_gain, nsa_cmp_w,
           nsa_cmp_pe, dil_w_in, dil_w_out, dil_qk_gain, moe_w_group, moe_b_group, moe_w_expert, moe_b_expert,
           moe_w_gate, moe_w_up, moe_w_down):
    B, S, D = x_prompt.shape
    DB, DS, _ = x_sample.shape
    n_pool = cache_sb_kv.shape[1]
    xp = x_prompt.reshape(B * S, D)
    xs = x_sample.reshape(DB * DS, D)
    nsa_table = rel_bias_table[:, :NSA_HEADS]
    dil_table = rel_bias_table[:, :DIL_HEADS]
    G, R = NSA_KV_HEADS, NSA_REP

    def moe(layer, xp, xs):
        return moe_layer(xp, xs, norm_ffn[layer], moe_w_group[layer], moe_b_group[layer], moe_w_expert[layer],
                         moe_b_expert[layer], moe_w_gate, moe_w_up, moe_w_down, layer)

    w16, gain, flag, kinds = _sparse_proj_plan(sparse_w_in[0], nsa_qk_gain[0])
    w_out16 = sparse_w_out[0].astype(BF16)
    cw16, cpe = _prep_cmp_weights(nsa_cmp_w[0], nsa_cmp_pe[0])
    ck_gain = nsa_qk_gain[0, 1]
    kv_cols = ((_C_SBK, 2 * SB_W), (_C_CMP, 2 * NSA_KVW), (_C_SLC, 2 * NSA_KVW), (_C_WIN, 2 * NSA_KVW))
    p16, gate32, sb_lines, cmp_lines, slc_lines, win_lines = norm_mm(
        xp, norm_mix[0], w16, gain, flag, kinds, tm=PROJ_TM, tn=PROJ_TN, f32_cols=(_C_GATE, PROJ_TN), lines=kv_cols)
    p16b = p16.reshape(B, S, SPARSE_NP)
    o_sb = sb_prompt(p16b, B, S, q_col=_C_SBQ, k_col=_C_SBK, v_col=_C_SBV)
    chunks = p16b[:, :, _C_CMP:_C_CMP + 2 * NSA_KVW].reshape(B, S // CMP_STRIDE, CHUNK_W)
    kc, vc = compress_prompt(chunks, cw16, cpe, ck_gain)
    tb = nsa_bias_tiles(nsa_table)
    cb = nsa_cmp_bias(nsa_table, S, kc.shape[1])
    o_nsa = nsa_prompt(p16b, gate32.reshape(B, S, PROJ_TN), kc, vc, tb, cb, B, S, q_col=_C_NQ, sk_col=_C_SLC,
                       sv_col=_C_SLC + NSA_KVW, wk_col=_C_WIN, wv_col=_C_WIN + NSA_KVW, gate_col=0)
    a = jnp.concatenate([o_sb, o_nsa], axis=-1).reshape(B * S, SB_W + NSA_QW)
    xp = mm_res(a, w_out16, xp, tm=1024, tn=PROJ_TN)
    sb_kv_prompt = sb_lines.reshape(1, B, S, 2, SB_HEADS, HEAD_DIM)
    cmp_kv_prompt = cmp_lines.reshape(1, B, S, 2, G, HEAD_DIM)
    slc_kv_prompt = slc_lines.reshape(1, B, S, 2, G, HEAD_DIM)
    wn = min(NSA_WINDOW, S)
    win_kv_prompt = win_lines.reshape(1, B, S, 2, G, HEAD_DIM)[:, :, S - wn:]
    _, q32 = norm_mm(xs, norm_mix[0], w16, gain, flag, kinds, tm=DB * DS, tn=PROJ_TN, f32_cols=(0, SPARSE_NP))
    q32b = q32.reshape(DB, DS, SPARSE_NP)
    sb_new = q32b[:, :, _C_SBK:_C_SBK + 2 * SB_W]
    cmp_new = q32b[:, :, _C_CMP:_C_CMP + 2 * NSA_KVW]
    slc_new = q32b[:, :, _C_SLC:_C_SLC + 2 * NSA_KVW]
    win_new = q32b[:, :, _C_WIN:_C_WIN + 2 * NSA_KVW]
    o_sb_s = sb_decode(page_table, _block_diag_rows(q32b[:, :, _C_SBQ:_C_SBQ + SB_W], SB_HEADS).astype(BF16),
                       _pad_rows(sb_new, PAGE_SIZE), cache_sb_kv[0].reshape(-1, HEAD_DIM))
    qn = q32b[:, :, _C_NQ:_C_NQ + NSA_QW].reshape(DB, DS, G, R, HEAD_DIM).transpose(0, 2, 3, 1, 4)
    qn = (qn[:, :, :, :, None, :] * jnp.eye(G, dtype=F32)[None, :, None, None, :, None]).reshape(DB, G * R * DS, NSA_KVW)
    qn16 = qn.astype(BF16)
    o_c, sel = nsa_cmp_decode(page_table, nsa_table, qn16, cache_nsa_cmp_kv[0].reshape(-1, HEAD_DIM), cw16, cpe, ck_gain, DS)
    sel16 = jnp.broadcast_to(sel.reshape(DB, G, 1, DS, -1), (DB, G, R, DS, sel.shape[-1])).reshape(DB, G * R * DS, -1).astype(BF16)
    gates = q32b[:, :, _C_GATE:_C_GATE + 3 * NSA_HEADS].reshape(DB, DS, G, R, 3).transpose(0, 2, 3, 1, 4).reshape(DB, G * R * DS, 3)
    gates = jnp.pad(gates, ((0, 0), (0, 0), (0, LANES - 3)))
    wbuf = state_nsa_win_kv[0]
    o_nsa_s = nsa_slc_decode(page_table, nsa_table, qn16, sel16, cache_nsa_slc_kv[0].reshape(-1, HEAD_DIM),
                             _pad_rows(slc_new, PAGE_SIZE), wbuf.reshape(-1, HEAD_DIM), _pad_rows(win_new, PAGE_SIZE),
                             o_c, gates, DS)
    a_s = jnp.concatenate([_rows_to_tokens(o_sb_s, SB_HEADS, DS), _rows_to_tokens(o_nsa_s, NSA_HEADS, DS)], axis=-1)
    xs = mm_res(a_s.astype(BF16), w_out16, xs, tm=DB * DS, tn=PROJ_TN)
    sb_kv_sample = sb_new.reshape(1, DB, DS, 2, SB_HEADS, HEAD_DIM)
    cmp_kv_sample = cmp_new.reshape(1, DB, DS, 2, G, HEAD_DIM)
    slc_kv_sample = slc_new.reshape(1, DB, DS, 2, G, HEAD_DIM)
    win_kv_sample = jnp.concatenate([wbuf, win_new.reshape(DB, DS, 2, G, HEAD_DIM)], axis=1)[None, :, DS:]
    xp, xs = moe(0, xp, xs)

    w16, gain, flag, kinds = _dil_proj_plan(dil_w_in[0], dil_qk_gain[0])
    w_out16 = dil_w_out[0].astype(BF16)
    p16, dil_lines = norm_mm(xp, norm_mix[1], w16, gain, flag, kinds, tm=PROJ_TM, tn=PROJ_TN,
                             lines=((DIL_W, 2 * DIL_W),))
    dbias = dil_bias_tiles(dil_table)
    stats = [dil_prompt_pattern(p16.reshape(B, S, 3 * DIL_W), dbias, pat, B, S) for pat in range(N_DIL)]
    o = dil_merge([n for n, _ in stats], [ml for _, ml in stats], tm=512)
    xp = mm_res(o, w_out16, xp, tm=1024, tn=PROJ_TN)
    dn = min(DIL_MAX_WINDOW, S)
    dil_kv_prompt = dil_lines.reshape(1, B, S, 2, DIL_HEADS, HEAD_DIM)[:, :, S - dn:]
    _, q32 = norm_mm(xs, norm_mix[1], w16, gain, flag, kinds, tm=DB * DS, tn=PROJ_TN, f32_cols=(0, 3 * DIL_W))
    q32b = q32.reshape(DB, DS, 3 * DIL_W)
    dil_new = q32b[:, :, DIL_W:]
    dbuf = state_dil_kv[0]
    o_s = dil_decode(dil_table, _block_diag_rows(q32b[:, :, :DIL_W], DIL_HEADS).astype(BF16),
                     dbuf.reshape(-1, HEAD_DIM), _pad_rows(dil_new, DIL_DEC_TILE), DS)
    xs = mm_res(_rows_to_tokens(o_s, DIL_HEADS, DS).astype(BF16), w_out16, xs, tm=DB * DS, tn=PROJ_TN)
    dil_kv_sample = jnp.concatenate([dbuf, dil_new.reshape(DB, DS, 2, DIL_HEADS, HEAD_DIM)], axis=1)[None, :, DS:]
    xp, xs = moe(1, xp, xs)

    return (xp.reshape(B, S, D), xs.reshape(DB, DS, D), sb_kv_prompt, sb_kv_sample, cmp_kv_prompt, cmp_kv_sample,
            slc_kv_prompt, slc_kv_sample, win_kv_prompt, win_kv_sample, dil_kv_prompt, dil_kv_sample)
```

```python
import functools
import math

import jax
import jax.numpy as jnp
import numpy as np
from jax import lax
from jax.experimental import pallas as pl
from jax.experimental.pallas import tpu as pltpu

D_MODEL = 2048
HEAD_DIM = 128
SB_HEADS = 8
NSA_HEADS = 8
NSA_KV_HEADS = 2
NSA_REP = NSA_HEADS // NSA_KV_HEADS
CMP_BLOCK = 32
CMP_STRIDE = 16
SLC_BLOCK = 64
N_SELECT = 16
NSA_WINDOW = 512
DIL_HEADS = 16
DIL_PATTERNS = ((128, 1), (512, 4), (2048, 16))
DIL_MAX_WINDOW = 2048
QBLK = 128
N_BUCKETS = 32
BUCKET_MAX_DIST = 2048
N_GROUPS = 4
EXPERTS_PER_GROUP = 4
N_EXPERTS = N_GROUPS * EXPERTS_PER_GROUP
TOP_K_INNER = 2
D_EXPERT = 1024
PAGE_SIZE = 128
RMS_EPS = 1e-6
NEG_INF = -1e30
FORCE_SCORE = 1e9
SB_W = SB_HEADS * HEAD_DIM
NSA_QW = NSA_HEADS * HEAD_DIM
NSA_KVW = NSA_KV_HEADS * HEAD_DIM
SPARSE_IN = 3 * SB_W + NSA_QW + 6 * NSA_KVW + 3 * NSA_HEADS
DIL_W = DIL_HEADS * HEAD_DIM

LANES = 128
SUBLANES = 8
VMEM_LIMIT = 56 * 1024 * 1024

BF16 = jnp.bfloat16
F32 = jnp.float32


def _cparams(sem):
    return pltpu.CompilerParams(dimension_semantics=sem, vmem_limit_bytes=VMEM_LIMIT)


def _dot(a, b):
    return jnp.dot(a, b, preferred_element_type=F32)


def _dot_t(a, b):
    return lax.dot_general(a, b, (((1,), (1,)), ((), ())), preferred_element_type=F32)


def _split3(x):
    h1 = x.astype(BF16)
    r1 = x - h1.astype(F32)
    h2 = r1.astype(BF16)
    h3 = (r1 - h2.astype(F32)).astype(BF16)
    return h1, h2, h3


def _dot01(x, m01):
    h1, h2, h3 = _split3(x)
    return _dot(h1, m01) + _dot(h2, m01) + _dot(h3, m01)


EPI_PLAIN, EPI_NORM, EPI_SIGMOID = 0, 1, 2


MM_ROW_CHUNK = 256


def _norm_mm_kernel(cls_ref, x_ref, g_ref, w_ref, cg_ref, cf_ref, o16_ref, *rest, classes, has_f32, line_cfg):
    h_scr = rest[-1]
    o32_ref = rest[0] if has_f32 else None
    line_refs = rest[(1 if has_f32 else 0):-1]
    j = pl.program_id(1)
    tm, tn = o16_ref.shape
    rc = min(MM_ROW_CHUNK, tm)

    @pl.when(j == 0)
    def _():
        x = x_ref[...]
        y = x * lax.rsqrt(jnp.mean(x * x, axis=-1, keepdims=True) + RMS_EPS)
        h_scr[...] = (y * g_ref[...]).astype(BF16)

    def epilogue(kind, acc):
        if kind == EPI_PLAIN:
            return acc
        if kind == EPI_SIGMOID:
            return jax.nn.sigmoid(acc)
        parts = []
        for c in range(tn // HEAD_DIM):
            a = acc[:, c * HEAD_DIM:(c + 1) * HEAD_DIM]
            parts.append(a * lax.rsqrt(jnp.mean(a * a, axis=-1, keepdims=True) + RMS_EPS))
        return jnp.where(cf_ref[...] > 0.5, jnp.concatenate(parts, axis=1) * cg_ref[...], acc)

    for cid, (kind, in_f32, li) in enumerate(classes):
        @pl.when(cls_ref[j] == cid)
        def _(kind=kind, in_f32=in_f32, li=li):
            for r in range(tm // rc):
                rows = slice(r * rc, (r + 1) * rc)
                out = epilogue(kind, _dot(h_scr[rows, :], w_ref[...]))
                o16_ref[rows, :] = out.astype(BF16)
                if in_f32:
                    o32_ref[rows, :] = out
                if li is not None:
                    j0, slots = line_cfg[li]
                    for c in range(tn // LANES):
                        line_refs[li][pl.ds(r * rc * slots + (j - j0) * (tn // LANES) + c, rc, stride=slots), :] = (
                            out[:, c * LANES:(c + 1) * LANES])


def norm_mm(x, g, w16, col_gain, col_flag, tile_kind, *, tm, tn, f32_cols=None, lines=()):
    T, D = x.shape
    N = w16.shape[1]
    assert T % tm == 0 and N % tn == 0
    n_tiles = N // tn
    grid = (T // tm, n_tiles)
    tile_kind = np.asarray(tile_kind)
    out_specs = [pl.BlockSpec((tm, tn), lambda i, j, k: (i, j))]
    out_shape = [jax.ShapeDtypeStruct((T, N), BF16)]
    in_f32 = np.zeros((n_tiles,), bool)
    if f32_cols is not None:
        c0, width = f32_cols
        assert c0 % tn == 0 and width % tn == 0
        j0, nj = c0 // tn, width // tn
        in_f32[j0:j0 + nj] = True
        out_specs.append(pl.BlockSpec((tm, tn), lambda i, j, k, j0=j0, nj=nj: (i, jnp.clip(j - j0, 0, nj - 1))))
        out_shape.append(jax.ShapeDtypeStruct((T, width), F32))
    line_of = [None] * n_tiles
    line_cfg = []
    for li, (c0, width) in enumerate(lines):
        assert c0 % tn == 0 and width % tn == 0
        slots = width // LANES
        line_cfg.append((c0 // tn, slots))
        for j in range(c0 // tn, (c0 + width) // tn):
            line_of[j] = li
        out_specs.append(pl.BlockSpec((tm * slots, LANES), lambda i, j, k: (i, 0)))
        out_shape.append(jax.ShapeDtypeStruct((T * slots, LANES), F32))
    per_tile = [(int(tile_kind[j]), bool(in_f32[j]), line_of[j]) for j in range(n_tiles)]
    classes = tuple(dict.fromkeys(per_tile))
    tile_cls = jnp.asarray([classes.index(c) for c in per_tile], jnp.int32)
    kern = functools.partial(_norm_mm_kernel, classes=classes, has_f32=f32_cols is not None, line_cfg=tuple(line_cfg))
    return pl.pallas_call(
        kern,
        grid_spec=pltpu.PrefetchScalarGridSpec(
            num_scalar_prefetch=1,
            grid=grid,
            in_specs=[
                pl.BlockSpec((tm, D), lambda i, j, k: (i, 0)),
                pl.BlockSpec((1, D), lambda i, j, k: (0, 0)),
                pl.BlockSpec((D, tn), lambda i, j, k: (0, j)),
                pl.BlockSpec((1, tn), lambda i, j, k: (0, j)),
                pl.BlockSpec((1, tn), lambda i, j, k: (0, j)),
            ],
            out_specs=out_specs,
            scratch_shapes=[pltpu.VMEM((tm, D), BF16)],
        ),
        out_shape=out_shape,
        compiler_params=_cparams(("parallel", "arbitrary")),
        name="norm_mm",
    )(tile_cls, x, g.reshape(1, D), w16, col_gain.reshape(1, N), col_flag.reshape(1, N))


def _mm_res_kernel(a_ref, w_ref, r_ref, o_ref):
    tm = o_ref.shape[0]
    rc = min(MM_ROW_CHUNK, tm)
    for r in range(tm // rc):
        rows = slice(r * rc, (r + 1) * rc)
        o_ref[rows, :] = r_ref[rows, :] + _dot(a_ref[rows, :], w_ref[...])


def mm_res(a16, w16, res, *, tm, tn):
    T, K = a16.shape
    N = w16.shape[1]
    assert T % tm == 0 and N % tn == 0
    return pl.pallas_call(
        _mm_res_kernel,
        grid=(T // tm, N // tn),
        in_specs=[
            pl.BlockSpec((tm, K), lambda i, j: (i, 0)),
            pl.BlockSpec((K, tn), lambda i, j: (0, j)),
            pl.BlockSpec((tm, tn), lambda i, j: (i, j)),
        ],
        out_specs=pl.BlockSpec((tm, tn), lambda i, j: (i, j)),
        out_shape=jax.ShapeDtypeStruct((T, N), F32),
        compiler_params=_cparams(("parallel", "arbitrary")),
        name="mm_res",
    )(a16, w16, res)


LOG2_E = math.log2(math.e)


def _log2_keep(z2):
    nz = -z2
    return jnp.minimum(nz, 0.0) - jnp.log2(1.0 + jnp.exp2(jnp.minimum(z2, nz)))


def _upper01(n):
    r = lax.broadcasted_iota(jnp.int32, (2 * n, n), 0)
    c = lax.broadcasted_iota(jnp.int32, (2 * n, n), 1)
    return jnp.where(jnp.where(r >= n, r - n, r) > c, 1.0, 0.0).astype(BF16)


def _after_in_tile(lk, upper2):
    h1 = lk.astype(BF16)
    h2 = (lk - h1.astype(F32)).astype(BF16)
    return _dot(jnp.concatenate([h1, h2], axis=1), upper2)


def _sb_prompt_kernel(q_ref, k_ref, v_ref, o_ref, acc_scr, run_scr, *, bq, scale):
    bk = LANES
    i = pl.program_id(2)
    nd = bq // bk
    acc_scr[...] = jnp.zeros(acc_scr.shape, F32)
    run_scr[...] = jnp.zeros(run_scr.shape, F32)
    upper2 = _upper01(bk)

    def absorb(kb, r0, masked):
        ks = pl.multiple_of(kb * bk, bk)
        k = k_ref[0, pl.ds(ks, bk), :]
        v = v_ref[0, pl.ds(ks, bk), :]
        z2 = _dot_t(q_ref[0, r0:, :], k) * (scale * LOG2_E)
        lk = _log2_keep(z2)
        lb = z2 + lk
        if masked:
            qpos = i * bq + r0 + lax.broadcasted_iota(jnp.int32, (bq - r0, 1), 0)
            mask = ks + lax.broadcasted_iota(jnp.int32, (1, bk), 1) < qpos
            lk = jnp.where(mask, lk, 0.0)
        w = jnp.exp2(lb + _after_in_tile(lk, upper2) + run_scr[r0:, :])
        if masked:
            w = jnp.where(mask, w, 0.0)
        acc_scr[r0:, :] = acc_scr[r0:, :] + _dot(w.astype(BF16), v)
        run_scr[r0:, :] = run_scr[r0:, :] + jnp.sum(lk, axis=-1, keepdims=True)

    for d in range(nd - 1, -1, -1):
        absorb(i * nd + d, d * bk, True)

    assert nd % 2 == 0

    def body(t, carry):
        ks = pl.multiple_of((i * nd - 2 - 2 * t) * bk, 2 * bk)
        k = k_ref[0, pl.ds(ks, 2 * bk), :]
        v = v_ref[0, pl.ds(ks, 2 * bk), :]
        z2 = _dot_t(q_ref[0], k) * (scale * LOG2_E)
        lk = _log2_keep(z2)
        lb = z2 + lk
        lk_l, lk_r = lk[:, :bk], lk[:, bk:]
        sum_r = jnp.sum(lk_r, axis=-1, keepdims=True)
        run = run_scr[...]
        w_r = jnp.exp2(lb[:, bk:] + _after_in_tile(lk_r, upper2) + run)
        w_l = jnp.exp2(lb[:, :bk] + _after_in_tile(lk_l, upper2) + (run + sum_r))
        acc_scr[...] = acc_scr[...] + _dot(jnp.concatenate([w_l.astype(BF16), w_r.astype(BF16)], axis=1), v)
        run_scr[...] = run + (sum_r + jnp.sum(lk_l, axis=-1, keepdims=True))
        return carry

    lax.fori_loop(0, i * (nd // 2), body, 0)
    o_ref[0] = acc_scr[...].astype(o_ref.dtype)


def sb_prompt(p16, B, S, *, q_col, k_col, v_col, bq=512):
    assert S % bq == 0 and bq % LANES == 0
    kern = functools.partial(_sb_prompt_kernel, bq=bq, scale=HEAD_DIM ** -0.5)
    qb, kb_, vb = q_col // HEAD_DIM, k_col // HEAD_DIM, v_col // HEAD_DIM
    return pl.pallas_call(
        kern,
        grid=(B, SB_HEADS, S // bq),
        in_specs=[
            pl.BlockSpec((1, bq, HEAD_DIM), lambda b, h, i: (b, i, qb + h)),
            pl.BlockSpec((1, S, HEAD_DIM), lambda b, h, i: (b, 0, kb_ + h)),
            pl.BlockSpec((1, S, HEAD_DIM), lambda b, h, i: (b, 0, vb + h)),
        ],
        out_specs=pl.BlockSpec((1, bq, HEAD_DIM), lambda b, h, i: (b, i, h)),
        out_shape=jax.ShapeDtypeStruct((B, S, SB_W), BF16),
        scratch_shapes=[pltpu.VMEM((bq, HEAD_DIM), F32), pltpu.VMEM((bq, LANES), F32)],
        compiler_params=_cparams(("parallel", "parallel", "arbitrary")),
        name="sb_prompt",
    )(p16, p16, p16)


def _t5_bucket(dist):
    n = jnp.maximum(dist, 0)
    exact = N_BUCKETS // 2
    nf = jnp.maximum(n, exact).astype(F32)
    far = exact + (jnp.log(nf / exact) / math.log(BUCKET_MAX_DIST / exact) * (N_BUCKETS - exact)).astype(jnp.int32)
    return jnp.where(n < exact, n, jnp.minimum(far, N_BUCKETS - 1))


def _table_lookup(bucket, tab_ref, head):
    out = jnp.zeros(bucket.shape, F32)
    for k in range(N_BUCKETS):
        out = jnp.where(bucket == k, tab_ref[k, head], out)
    return out


N_BIAS_DIAG = -(-(BUCKET_MAX_DIST + QBLK) // QBLK) + 1


def _bias_tiles_kernel(tab_ref, o_ref):
    g = pl.program_id(0)
    d = pl.program_id(1)
    dist = d * QBLK + lax.broadcasted_iota(jnp.int32, (QBLK, QBLK), 0) - lax.broadcasted_iota(jnp.int32, (QBLK, QBLK), 1)
    bucket = _t5_bucket(dist)
    for r in range(NSA_REP):
        o_ref[0, 0, r * QBLK:(r + 1) * QBLK, :] = _table_lookup(bucket, tab_ref, g * NSA_REP + r)


def nsa_bias_tiles(table):
    return pl.pallas_call(
        _bias_tiles_kernel,
        grid=(NSA_KV_HEADS, N_BIAS_DIAG),
        in_specs=[pl.BlockSpec(memory_space=pltpu.SMEM)],
        out_specs=pl.BlockSpec((1, 1, NSA_REP * QBLK, QBLK), lambda g, d: (g, d, 0, 0)),
        out_shape=jax.ShapeDtypeStruct((NSA_KV_HEADS, N_BIAS_DIAG, NSA_REP * QBLK, QBLK), F32),
        compiler_params=_cparams(("parallel", "arbitrary")),
        name="nsa_bias_tiles",
    )(table)


CMP_HALF = CMP_BLOCK // CMP_STRIDE
CHUNK_W = CMP_STRIDE * 2 * NSA_KVW


def _compress_chunks(x, w_ref, pe_ref, gain):
    n = x.shape[0]
    outs = []
    for kv in range(2):
        for g in range(NSA_KV_HEADS):
            off = kv * NSA_KVW + g * HEAD_DIM
            xs = jnp.concatenate(
                [x[:, l * 2 * NSA_KVW + off:l * 2 * NSA_KVW + off + HEAD_DIM] for l in range(CMP_STRIDE)], axis=1)
            halves = []
            for j in range(CMP_HALF):
                w = w_ref[kv, j]
                pe = jnp.broadcast_to(pe_ref[kv, j], (SUBLANES, CMP_STRIDE * HEAD_DIM)).astype(BF16)
                halves.append(_dot(xs, w) + _dot(pe, w)[0:1, :])
            out = halves[0]
            for j in range(1, CMP_HALF):
                out = out + pltpu.roll(halves[j], n - j, axis=0)
            if kv == 0:
                out = out * lax.rsqrt(jnp.mean(out * out, axis=-1, keepdims=True) + RMS_EPS) * gain
            outs.append(out)
    kc = jnp.concatenate(outs[:NSA_KV_HEADS], axis=1)
    vc = jnp.concatenate(outs[NSA_KV_HEADS:], axis=1)
    return kc, vc


def _compress_prompt_kernel(x_ref, w_ref, pe_ref, g_ref, kc_ref, vc_ref):
    kc, vc = _compress_chunks(x_ref[0], w_ref, pe_ref, g_ref[...])
    kc_ref[0] = kc.astype(BF16)
    vc_ref[0] = vc.astype(BF16)


def _prep_cmp_weights(cmp_w, cmp_pe):
    w = cmp_w.reshape(2, CMP_HALF, CMP_STRIDE * HEAD_DIM, HEAD_DIM).astype(BF16)
    pe = cmp_pe.reshape(2, CMP_HALF, 1, CMP_STRIDE * HEAD_DIM)
    return w, pe


def compress_prompt(chunks16, w16, pe, ck_gain):
    B, n, _ = chunks16.shape
    return pl.pallas_call(
        _compress_prompt_kernel,
        grid=(B,),
        in_specs=[
            pl.BlockSpec((1, n, CHUNK_W), lambda b: (b, 0, 0)),
            pl.BlockSpec((2, CMP_HALF, CMP_STRIDE * HEAD_DIM, HEAD_DIM), lambda b: (0, 0, 0, 0)),
            pl.BlockSpec((2, CMP_HALF, 1, CMP_STRIDE * HEAD_DIM), lambda b: (0, 0, 0, 0)),
            pl.BlockSpec((1, HEAD_DIM), lambda b: (0, 0)),
        ],
        out_specs=[pl.BlockSpec((1, n, NSA_KVW), lambda b: (b, 0, 0))] * 2,
        out_shape=[jax.ShapeDtypeStruct((B, n, NSA_KVW), BF16)] * 2,
        compiler_params=_cparams(("parallel",)),
        name="nsa_compress_prompt",
    )(chunks16, w16, pe, ck_gain.reshape(1, HEAD_DIM))


def _flash_step(s, mask, v, m_scr, l_scr, acc_scr):
    n = s.shape[1] // LANES
    cols = [slice(c * LANES, (c + 1) * LANES) for c in range(n)]
    sm = [jnp.where(mask[:, c], s[:, c], NEG_INF) for c in cols]
    top = sm[0]
    for x in sm[1:]:
        top = jnp.maximum(top, x)
    m_prev = m_scr[...]
    m_new = jnp.maximum(m_prev, jnp.max(top, axis=-1, keepdims=True))
    ps = [jnp.where(mask[:, c], jnp.exp(x - m_new), 0.0) for c, x in zip(cols, sm)]
    tot = ps[0]
    for x in ps[1:]:
        tot = tot + x
    alpha = jnp.exp(m_prev - m_new)
    l_scr[...] = alpha * l_scr[...] + jnp.sum(tot, axis=-1, keepdims=True)
    acc_scr[...] = alpha * acc_scr[...] + _dot(jnp.concatenate([x.astype(BF16) for x in ps], axis=1), v)
    m_scr[...] = m_new


def _flash_reset(m_scr, l_scr, acc_scr):
    m_scr[...] = jnp.full(m_scr.shape, NEG_INF, F32)
    l_scr[...] = jnp.zeros(l_scr.shape, F32)
    acc_scr[...] = jnp.zeros(acc_scr.shape, F32)


def _select_blocks(score, q_blk, n_sel):
    Q, n_s = score.shape
    blk = lax.broadcasted_iota(jnp.int32, (1, n_s), 1)
    valid = blk <= q_blk
    forced = valid & ((blk == 0) | (blk == q_blk) | (blk == q_blk - 1))
    score = jnp.where(forced, FORCE_SCORE, jnp.where(valid, score, NEG_INF))
    rank = jnp.zeros((Q, n_s), F32)
    for c in range(n_s):
        col = score[:, c:c + 1]
        before = (col > score) | ((col == score) & (blk > c))
        rank = rank + jnp.where(before, 1.0, 0.0)
    return jnp.where(rank < n_sel, 1.0, 0.0)


def _select_blocks_t(score_t, q_blk, n_sel):
    n_s, Q = score_t.shape
    blk = lax.broadcasted_iota(jnp.int32, (n_s, 1), 0)
    valid = blk <= q_blk
    forced = valid & ((blk == 0) | (blk == q_blk) | (blk == q_blk - 1))
    score_t = jnp.where(forced, FORCE_SCORE, jnp.where(valid, score_t, NEG_INF))
    rank = jnp.zeros((n_s, Q), F32)
    for c in range(n_s):
        row = score_t[c:c + 1, :]
        before = (row > score_t) | ((row == score_t) & (blk > c))
        rank = rank + jnp.where(before, 1.0, 0.0)
    return jnp.where(rank < n_sel, 1.0, 0.0)


def _nsa_cmp_bias_kernel(tab_ref, o_ref):
    g = pl.program_id(0)
    i = pl.program_id(1)
    ncp = o_ref.shape[3]
    qpos1 = i * QBLK + lax.broadcasted_iota(jnp.int32, (QBLK, 1), 0)
    c_end = lax.broadcasted_iota(jnp.int32, (1, ncp), 1) * CMP_STRIDE + (CMP_BLOCK - 1)
    bucket = _t5_bucket(qpos1 - c_end)
    for r in range(NSA_REP):
        o_ref[0, 0, r * QBLK:(r + 1) * QBLK, :] = _table_lookup(bucket, tab_ref, g * NSA_REP + r)


def nsa_cmp_bias(table, S, ncp):
    return pl.pallas_call(
        _nsa_cmp_bias_kernel,
        grid=(NSA_KV_HEADS, S // QBLK),
        in_specs=[pl.BlockSpec(memory_space=pltpu.SMEM)],
        out_specs=pl.BlockSpec((1, 1, NSA_REP * QBLK, ncp), lambda g, i: (g, i, 0, 0)),
        out_shape=jax.ShapeDtypeStruct((NSA_KV_HEADS, S // QBLK, NSA_REP * QBLK, ncp), F32),
        compiler_params=_cparams(("parallel", "arbitrary")),
        name="nsa_cmp_bias",
    )(table)


NSA_KT = 2 * QBLK


def _nsa_prompt_kernel(q_ref, kc_ref, vc_ref, ks_ref, vs_ref, kw_ref, vw_ref, gt_ref, tb_ref, cb_ref,
                       o_ref, m_scr, l_scr, acc_scr, sel_scr, *, n_c, scale):
    g = pl.program_id(1)
    i = pl.program_id(2)
    R = NSA_REP
    qs = i * QBLK
    q4 = q_ref[0]
    q = jnp.concatenate([q4[:, r * HEAD_DIM:(r + 1) * HEAD_DIM] for r in range(R)], axis=0)
    qpos1 = qs + lax.broadcasted_iota(jnp.int32, (QBLK, 1), 0)
    qpos = jnp.concatenate([qpos1] * R, axis=0)

    ncp = kc_ref.shape[1]
    s = _dot_t(q, kc_ref[0]) * scale
    n_idx = lax.broadcasted_iota(jnp.int32, (1, ncp), 1)
    mask = (qpos - (n_idx * CMP_STRIDE + (CMP_BLOCK - 1)) >= 0) & (n_idx < n_c)
    s = jnp.where(mask, s + cb_ref[0, 0], NEG_INF)
    m = jnp.max(s, axis=-1, keepdims=True)
    p = jnp.where(mask, jnp.exp(s - m), 0.0)
    l = jnp.sum(p, axis=-1, keepdims=True)
    p = p / jnp.maximum(l, 1.0)
    o_c = _dot(p.astype(BF16), vc_ref[0])
    imp = p[0:QBLK]
    for r in range(1, R):
        imp = imp + p[r * QBLK:(r + 1) * QBLK]
    n_s = ks_ref.shape[1] // SLC_BLOCK
    ratio = SLC_BLOCK // CMP_STRIDE
    ti = lax.broadcasted_iota(jnp.int32, (n_s, ncp), 0)
    ci = lax.broadcasted_iota(jnp.int32, (n_s, ncp), 1)
    gather01 = jnp.where((ci >= ti * ratio - (CMP_HALF - 1)) & (ci < (ti + 1) * ratio), 1.0, 0.0).astype(BF16)
    i1, i2, i3 = _split3(imp)
    score_t = _dot_t(gather01, i1) + _dot_t(gather01, i2) + _dot_t(gather01, i3)
    q_blk = (qs + lax.broadcasted_iota(jnp.int32, (1, QBLK), 1)) // SLC_BLOCK
    sel_scr[...] = _select_blocks_t(score_t, q_blk, min(N_SELECT, n_s)).T.astype(BF16)

    kcol = lax.broadcasted_iota(jnp.int32, (1, NSA_KT), 1)

    def tile_bias(kt):
        parts = [tb_ref[0, jnp.clip(i - (kt * (NSA_KT // QBLK) + t), 0, N_BIAS_DIAG - 1)] for t in range(NSA_KT // QBLK)]
        return jnp.concatenate(parts, axis=1)

    _flash_reset(m_scr, l_scr, acc_scr)

    def slc_tile(kt, causal):
        ks_ = pl.multiple_of(kt * NSA_KT, NSA_KT)
        k = ks_ref[0, pl.ds(ks_, NSA_KT), :]
        v = vs_ref[0, pl.ds(ks_, NSA_KT), :]
        bi = lax.broadcasted_iota(jnp.int32, (n_s, NSA_KT), 0)
        ki = lax.broadcasted_iota(jnp.int32, (n_s, NSA_KT), 1)
        expand = jnp.where(bi == kt * (NSA_KT // SLC_BLOCK) + ki // SLC_BLOCK, 1.0, 0.0).astype(BF16)
        selk = _dot(sel_scr[...], expand)
        msk = jnp.concatenate([selk] * R, axis=0) > 0.5
        if causal:
            msk = msk & (qpos - (ks_ + kcol) >= 0)
        sc = _dot_t(q, k) * scale + tile_bias(kt)
        _flash_step(sc, msk, v, m_scr, l_scr, acc_scr)

    assert NSA_KT % QBLK == 0
    n_full = qs // NSA_KT

    def slc_body(kt, carry):
        slc_tile(kt, False)
        return carry

    lax.fori_loop(0, n_full, slc_body, 0)
    slc_tile(n_full, True)
    o_s = acc_scr[...] / jnp.maximum(l_scr[...], 1.0)

    _flash_reset(m_scr, l_scr, acc_scr)

    def win_body(kt, carry):
        ks_ = pl.multiple_of(kt * NSA_KT, NSA_KT)
        k = kw_ref[0, pl.ds(ks_, NSA_KT), :]
        v = vw_ref[0, pl.ds(ks_, NSA_KT), :]
        dist = qpos - (ks_ + kcol)
        msk = (dist >= 0) & (dist < NSA_WINDOW)
        sc = _dot_t(q, k) * scale + tile_bias(kt)
        _flash_step(sc, msk, v, m_scr, l_scr, acc_scr)
        return carry

    lax.fori_loop(jnp.maximum(qs - (NSA_WINDOW - 1), 0) // NSA_KT, (qs + QBLK - 1) // NSA_KT + 1, win_body, 0)
    o_w = acc_scr[...] / jnp.maximum(l_scr[...], 1.0)

    gt = gt_ref[0]
    outs = []
    for r in range(R):
        c0 = (g * R + r) * 3
        rows = slice(r * QBLK, (r + 1) * QBLK)
        lane = lax.broadcasted_iota(jnp.int32, (1, LANES), 1)
        gates = [jnp.sum(jnp.where(lane == c0 + t, gt, 0.0), axis=-1, keepdims=True) for t in range(3)]
        outs.append(gates[0] * o_c[rows] + gates[1] * o_s[rows] + gates[2] * o_w[rows])
    o_ref[0] = jnp.concatenate(outs, axis=1).astype(o_ref.dtype)


def nsa_prompt(p16, p32, kc, vc, tb, cb, B, S, *, q_col, sk_col, sv_col, wk_col, wv_col, gate_col):
    G, R = NSA_KV_HEADS, NSA_REP
    assert S % NSA_KT == 0
    n_c = S // CMP_STRIDE - CMP_HALF + 1
    kern = functools.partial(_nsa_prompt_kernel, n_c=n_c, scale=HEAD_DIM ** -0.5)
    hd = HEAD_DIM
    return pl.pallas_call(
        kern,
        grid_spec=pltpu.PrefetchScalarGridSpec(
            num_scalar_prefetch=0,
            grid=(B, G, S // QBLK),
            in_specs=[
                pl.BlockSpec((1, QBLK, R * hd), lambda b, g, i: (b, i, q_col // (R * hd) + g)),
                pl.BlockSpec((1, kc.shape[1], hd), lambda b, g, i: (b, 0, g)),
                pl.BlockSpec((1, vc.shape[1], hd), lambda b, g, i: (b, 0, g)),
                pl.BlockSpec((1, S, hd), lambda b, g, i: (b, 0, sk_col // hd + g)),
                pl.BlockSpec((1, S, hd), lambda b, g, i: (b, 0, sv_col // hd + g)),
                pl.BlockSpec((1, S, hd), lambda b, g, i: (b, 0, wk_col // hd + g)),
                pl.BlockSpec((1, S, hd), lambda b, g, i: (b, 0, wv_col // hd + g)),
                pl.BlockSpec((1, QBLK, LANES), lambda b, g, i: (b, i, gate_col // LANES)),
                pl.BlockSpec((1, N_BIAS_DIAG, R * QBLK, QBLK), lambda b, g, i: (g, 0, 0, 0)),
                pl.BlockSpec((1, 1, R * QBLK, cb.shape[3]), lambda b, g, i: (g, i, 0, 0)),
            ],
            out_specs=pl.BlockSpec((1, QBLK, R * hd), lambda b, g, i: (b, i, g)),
            scratch_shapes=[
                pltpu.VMEM((R * QBLK, LANES), F32),
                pltpu.VMEM((R * QBLK, LANES), F32),
                pltpu.VMEM((R * QBLK, hd), F32),
                pltpu.VMEM((QBLK, S // SLC_BLOCK), BF16),
            ],
        ),
        out_shape=jax.ShapeDtypeStruct((B, S, NSA_QW), BF16),
        compiler_params=_cparams(("parallel", "parallel", "arbitrary")),
        name="nsa_prompt",
    )(p16, kc, vc, p16, p16, p16, p16, p32, tb, cb)


ROUTER_W = LANES
MOE_TM = 256


def _router_kernel(x_ref, g_ref, w_ref, b_ref, h_ref, info_ref):
    x = x_ref[...]
    hn = x * lax.rsqrt(jnp.mean(x * x, axis=-1, keepdims=True) + RMS_EPS) * g_ref[...]
    h_ref[...] = hn
    logits = _dot(hn.astype(BF16), w_ref[...]) + b_ref[...]
    lane = lax.broadcasted_iota(jnp.int32, (1, ROUTER_W), 1)
    big = ROUTER_W

    def first_lane(cond):
        return jnp.min(jnp.where(cond, lane, big), axis=-1, keepdims=True)

    gmask = lane < N_GROUPS
    gl = jnp.where(gmask, logits, NEG_INF)
    gm = jnp.max(gl, axis=-1, keepdims=True)
    g_idx = first_lane(gmask & (gl == gm))
    g_w = 1.0 / jnp.sum(jnp.where(gmask, jnp.exp(gl - gm), 0.0), axis=-1, keepdims=True)
    lo = N_GROUPS + g_idx * EXPERTS_PER_GROUP
    emask = (lane >= lo) & (lane < lo + EXPERTS_PER_GROUP)
    el = jnp.where(emask, logits, NEG_INF)
    em = jnp.max(el, axis=-1, keepdims=True)
    ee = jnp.where(emask, jnp.exp(el - em), 0.0)
    pe = jnp.where(emask, ee / jnp.sum(ee, axis=-1, keepdims=True), -1.0)
    v1 = jnp.max(pe, axis=-1, keepdims=True)
    i1 = first_lane(pe == v1)
    pe2 = jnp.where(lane == i1, -1.0, pe)
    v2 = jnp.max(pe2, axis=-1, keepdims=True)
    i2 = first_lane((pe2 == v2) & emask & (lane != i1))
    tot = v1 + v2
    info = jnp.where(lane == 0, (i1 - N_GROUPS).astype(F32), 0.0)
    info = jnp.where(lane == 1, (i2 - N_GROUPS).astype(F32), info)
    info = jnp.where(lane == 2, g_w * (v1 / tot), info)
    info = jnp.where(lane == 3, g_w * (v2 / tot), info)
    info_ref[...] = info


def moe_router(x, g, w_group, b_group, w_expert, b_expert, *, tm):
    T, D = x.shape
    pad = ROUTER_W - N_GROUPS - N_EXPERTS
    w = jnp.pad(jnp.concatenate([w_group, w_expert], axis=1), ((0, 0), (0, pad)))
    b = jnp.pad(jnp.concatenate([b_group, b_expert]), (0, pad)).reshape(1, ROUTER_W)
    return pl.pallas_call(
        _router_kernel,
        grid=(T // tm,),
        in_specs=[
            pl.BlockSpec((tm, D), lambda i: (i, 0)),
            pl.BlockSpec((1, D), lambda i: (0, 0)),
            pl.BlockSpec((D, ROUTER_W), lambda i: (0, 0)),
            pl.BlockSpec((1, ROUTER_W), lambda i: (0, 0)),
        ],
        out_specs=[pl.BlockSpec((tm, D), lambda i: (i, 0)), pl.BlockSpec((tm, ROUTER_W), lambda i: (i, 0))],
        out_shape=[jax.ShapeDtypeStruct((T, D), F32), jax.ShapeDtypeStruct((T, ROUTER_W), F32)],
        compiler_params=_cparams(("parallel",)),
        name="moe_router",
    )(x, g.reshape(1, D), w.astype(BF16), b)


MOE_ROW_CHUNK = 128


def _moe_ffn_kernel(te_ref, nu_ref, nx_ref, x_ref, gw_ref, wg_hbm, wu_hbm, wd_hbm, y_ref,
                    stage_g, stage_u, stage_d, wg16, wu16, wd16, sem, *, layer):
    i = pl.program_id(0)

    def weight_copies(e):
        return (pltpu.make_async_copy(wg_hbm.at[layer, e], stage_g, sem.at[0]),
                pltpu.make_async_copy(wu_hbm.at[layer, e], stage_u, sem.at[1]),
                pltpu.make_async_copy(wd_hbm.at[layer, e], stage_d, sem.at[2]))

    @pl.when(i == 0)
    def _():
        for c in weight_copies(te_ref[0]):
            c.start()

    @pl.when((i == 0) | (te_ref[i] != te_ref[jnp.maximum(i - 1, 0)]))
    def _():
        for c, stage, w16 in zip(weight_copies(te_ref[i]), (stage_g, stage_u, stage_d), (wg16, wu16, wd16)):
            c.wait()
            w16[...] = stage[...].astype(BF16)

        @pl.when(nx_ref[i] >= 0)
        def _():
            for c in weight_copies(nx_ref[i]):
                c.start()

    @pl.when(i < nu_ref[0])
    def _():
        for r in range(MOE_TM // MOE_ROW_CHUNK):
            rows = slice(r * MOE_ROW_CHUNK, (r + 1) * MOE_ROW_CHUNK)
            x = x_ref[rows, :].astype(BF16)
            h = _dot(x, wg16[...])
            u = _dot(x, wu16[...])
            y_ref[rows, :] = _dot(((h * jax.nn.sigmoid(h)) * u * gw_ref[rows, :]).astype(BF16), wd16[...])

    @pl.when(i >= nu_ref[0])
    def _():
        y_ref[...] = jnp.zeros(y_ref.shape, F32)


def moe_ffn(xs, gate_rows, tile_expert, n_used, next_expert, w_gate, w_up, w_down, layer):
    M, D = xs.shape
    F = w_gate.shape[3]
    n_tiles = M // MOE_TM
    hbm = pl.BlockSpec(memory_space=pl.ANY)
    return pl.pallas_call(
        functools.partial(_moe_ffn_kernel, layer=layer),
        grid_spec=pltpu.PrefetchScalarGridSpec(
            num_scalar_prefetch=3,
            grid=(n_tiles,),
            in_specs=[pl.BlockSpec((MOE_TM, D), lambda i, te, nu, nx: (i, 0)),
                      pl.BlockSpec((MOE_TM, 1), lambda i, te, nu, nx: (i, 0)), hbm, hbm, hbm],
            out_specs=pl.BlockSpec((MOE_TM, D), lambda i, te, nu, nx: (i, 0)),
            scratch_shapes=[pltpu.VMEM((D, F), F32), pltpu.VMEM((D, F), F32), pltpu.VMEM((F, D), F32),
                            pltpu.VMEM((D, F), BF16), pltpu.VMEM((D, F), BF16), pltpu.VMEM((F, D), BF16),
                            pltpu.SemaphoreType.DMA((3,))],
        ),
        out_shape=jax.ShapeDtypeStruct((M, D), F32),
        compiler_params=_cparams(("arbitrary",)),
        name="moe_ffn",
    )(tile_expert, n_used, next_expert, xs, gate_rows, w_gate, w_up, w_down)


def moe_plan(info):
    T = info.shape[0]
    eid = info[:, :TOP_K_INNER].astype(jnp.int32).reshape(-1)
    M = T * TOP_K_INNER
    n_tiles = -(-(M + N_EXPERTS * (MOE_TM - 1)) // MOE_TM)
    onehot = (eid[:, None] == jnp.arange(N_EXPERTS)[None, :]).astype(jnp.int32)
    rank = jnp.cumsum(onehot, axis=0) - onehot
    counts = jnp.sum(onehot, axis=0)
    tiles_per = (counts + MOE_TM - 1) // MOE_TM
    tile_end = jnp.cumsum(tiles_per)
    offs = (tile_end - tiles_per) * MOE_TM
    dest = jnp.sum(onehot * (offs[None, :] + rank), axis=1).astype(jnp.int32)
    n_used = tile_end[-1]
    tile_ids = jnp.arange(n_tiles)
    tile_expert = jnp.minimum(jnp.sum(tile_ids[:, None] >= tile_end[None, :], axis=1), N_EXPERTS - 1)
    last_e = jnp.max(jnp.where(counts > 0, jnp.arange(N_EXPERTS), 0))
    tile_expert = jnp.where(tile_ids < n_used, tile_expert, last_e).astype(jnp.int32)
    ids = jnp.arange(N_EXPERTS)
    later = (ids[None, :] > ids[:, None]) & (counts[None, :] > 0)
    next_of = jnp.where(jnp.any(later, axis=1), jnp.argmax(later, axis=1), -1)
    next_expert = next_of[tile_expert].astype(jnp.int32)
    return (dest.reshape(T, TOP_K_INNER), tile_expert, n_used.reshape(1).astype(jnp.int32), next_expert,
            n_tiles * MOE_TM)


MOE_DMA_UNROLL = 8


def _row_copy(src_ref, s, dst_ref, d, sem):
    return pltpu.make_async_copy(src_ref.at[pl.ds(s, 1), :], dst_ref.at[pl.ds(d, 1), :], sem)


def _moe_scatter_kernel(dest_ref, h_ref, xs_in_ref, xs_ref, sem):
    del xs_in_ref
    tm = h_ref.shape[0]

    def issue(r, c):
        for slot in range(TOP_K_INNER):
            _row_copy(h_ref, r, xs_ref, dest_ref[0, 0, TOP_K_INNER * r + slot], sem).start()
        return c

    lax.fori_loop(0, tm, issue, 0, unroll=MOE_DMA_UNROLL)

    def drain(r, c):
        for slot in range(TOP_K_INNER):
            _row_copy(h_ref, 0, xs_ref, 0, sem).wait()
        return c

    lax.fori_loop(0, tm, drain, 0, unroll=MOE_DMA_UNROLL)


def moe_scatter(h, dest, xs, *, tm):
    T, D = h.shape
    assert T % tm == 0
    return pl.pallas_call(
        _moe_scatter_kernel,
        grid=(T // tm,),
        in_specs=[
            pl.BlockSpec((1, 1, TOP_K_INNER * tm), lambda i: (i, 0, 0), memory_space=pltpu.SMEM),
            pl.BlockSpec((tm, D), lambda i: (i, 0)),
            pl.BlockSpec(memory_space=pl.ANY),
        ],
        out_specs=pl.BlockSpec(memory_space=pl.ANY),
        out_shape=jax.ShapeDtypeStruct(xs.shape, xs.dtype),
        scratch_shapes=[pltpu.SemaphoreType.DMA(())],
        input_output_aliases={2: 0},
        compiler_params=_cparams(("arbitrary",)),
        name="moe_scatter",
    )(dest.reshape(T // tm, 1, TOP_K_INNER * tm), h, xs)


def _moe_combine_kernel(dest_ref, x_ref, y_ref, o_ref, buf, sem):
    tm = x_ref.shape[0]

    def issue(r, c):
        for slot in range(TOP_K_INNER):
            _row_copy(y_ref, dest_ref[0, 0, TOP_K_INNER * r + slot], buf.at[slot], r, sem).start()
        return c

    lax.fori_loop(0, tm, issue, 0, unroll=MOE_DMA_UNROLL)

    def drain(r, c):
        for slot in range(TOP_K_INNER):
            _row_copy(y_ref, 0, buf.at[slot], 0, sem).wait()
        return c

    lax.fori_loop(0, tm, drain, 0, unroll=MOE_DMA_UNROLL)
    out = x_ref[...]
    for slot in range(TOP_K_INNER):
        out = out + buf[slot]
    o_ref[...] = out


def moe_combine(x, dest, y, *, tm):
    T, D = x.shape
    assert T % tm == 0
    return pl.pallas_call(
        _moe_combine_kernel,
        grid=(T // tm,),
        in_specs=[
            pl.BlockSpec((1, 1, TOP_K_INNER * tm), lambda i: (i, 0, 0), memory_space=pltpu.SMEM),
            pl.BlockSpec((tm, D), lambda i: (i, 0)),
            pl.BlockSpec(memory_space=pl.ANY),
        ],
        out_specs=pl.BlockSpec((tm, D), lambda i: (i, 0)),
        out_shape=jax.ShapeDtypeStruct((T, D), F32),
        scratch_shapes=[pltpu.VMEM((TOP_K_INNER, tm, D), F32), pltpu.SemaphoreType.DMA(())],
        compiler_params=_cparams(("arbitrary",)),
        name="moe_combine",
    )(dest.reshape(T // tm, 1, TOP_K_INNER * tm), x, y)


MOE_ROW_TILE = 256


def moe_layer(xp, xs, g, w_group, b_group, w_expert, b_expert, w_gate, w_up, w_down, layer, spare=None):
    Tp, Ts = xp.shape[0], xs.shape[0]
    hp, ip = moe_router(xp, g, w_group, b_group, w_expert, b_expert, tm=512)
    hs, is_ = moe_router(xs, g, w_group, b_group, w_expert, b_expert, tm=Ts)
    info = jnp.concatenate([ip, is_], axis=0)
    dest, tile_expert, n_used, next_expert, m_rows = moe_plan(info)
    gate_rows = jnp.zeros((m_rows,), F32).at[dest.reshape(-1)].set(
        info[:, TOP_K_INNER:2 * TOP_K_INNER].reshape(-1)).reshape(m_rows, 1)
    rows = jnp.zeros((m_rows, xp.shape[1]), F32) if spare is None else spare
    rows = moe_scatter(hp, dest[:Tp], rows, tm=MOE_ROW_TILE)
    rows = moe_scatter(hs, dest[Tp:], rows, tm=Ts)
    y = moe_ffn(rows, gate_rows, tile_expert, n_used, next_expert, w_gate, w_up, w_down, layer)
    return moe_combine(xp, dest[:Tp], y, tm=MOE_ROW_TILE), moe_combine(xs, dest[Tp:], y, tm=Ts), y


N_DIL = len(DIL_PATTERNS)
DIL_SPANS = tuple(w // d for w, d in DIL_PATTERNS)
DIL_STEPS = tuple(d for _, d in DIL_PATTERNS)
assert all(s <= QBLK for s in DIL_SPANS)
assert all(d & (d - 1) == 0 for d in DIL_STEPS)


def _dil_bias_kernel(dil_ref, tab_ref, o_ref):
    p = pl.program_id(0)
    dm = QBLK + lax.broadcasted_iota(jnp.int32, (QBLK, 2 * QBLK), 0) - lax.broadcasted_iota(jnp.int32, (QBLK, 2 * QBLK), 1)
    bucket = _t5_bucket(dm * dil_ref[p])
    for h in range(DIL_HEADS):
        o_ref[0, h] = _table_lookup(bucket, tab_ref, h)


def dil_bias_tiles(table):
    return pl.pallas_call(
        _dil_bias_kernel,
        grid=(N_DIL,),
        in_specs=[pl.BlockSpec(memory_space=pltpu.SMEM), pl.BlockSpec(memory_space=pltpu.SMEM)],
        out_specs=pl.BlockSpec((1, DIL_HEADS, QBLK, 2 * QBLK), lambda p: (p, 0, 0, 0)),
        out_shape=jax.ShapeDtypeStruct((N_DIL, DIL_HEADS, QBLK, 2 * QBLK), F32),
        compiler_params=_cparams(("arbitrary",)),
        name="dil_bias_tiles",
    )(jnp.array(DIL_STEPS, jnp.int32), table)


def _dil_prompt_kernel(q_ref, kp_ref, kc_ref, vp_ref, vc_ref, b_ref, num_ref, ml_ref, *, span, scale):
    mb = pl.program_id(2)
    qi = lax.broadcasted_iota(jnp.int32, (QBLK, 2 * QBLK), 0)
    kj = lax.broadcasted_iota(jnp.int32, (QBLK, 2 * QBLK), 1)
    dm = QBLK + qi - kj
    mask = (dm >= 0) & (dm <= span) & ((kj >= QBLK) | (mb > 0))
    lane = lax.broadcasted_iota(jnp.int32, (1, LANES), 1)
    m_t = jnp.zeros((QBLK, LANES), F32)
    l_t = jnp.zeros((QBLK, LANES), F32)
    for h in range(DIL_HEADS):
        cs = slice(h * HEAD_DIM, (h + 1) * HEAD_DIM)
        k = jnp.concatenate([kp_ref[0, :, cs], kc_ref[0, :, cs]], axis=0)
        v = jnp.concatenate([vp_ref[0, :, cs], vc_ref[0, :, cs]], axis=0)
        s = _dot_t(q_ref[0, :, cs], k) * scale + b_ref[0, h]
        s = jnp.where(mask, s, NEG_INF)
        m = jnp.max(s, axis=-1, keepdims=True)
        p = jnp.where(mask, jnp.exp(s - m), 0.0)
        l = jnp.sum(p, axis=-1, keepdims=True)
        num_ref[0, :, cs] = _dot(p.astype(BF16), v)
        m_t = jnp.where(lane == h, m, m_t)
        l_t = jnp.where(lane == h, l, l_t)
    ml_ref[0, :, 0:LANES] = m_t
    ml_ref[0, :, LANES:2 * LANES] = l_t


def dil_prompt_pattern(p16, bias, pat, B, S):
    span, dil = DIL_SPANS[pat], DIL_STEPS[pat]
    M = S // dil
    assert M % QBLK == 0
    pv = p16.reshape(B, M, dil * 3 * DIL_W)
    kern = functools.partial(_dil_prompt_kernel, span=span, scale=HEAD_DIM ** -0.5)
    blk = (1, QBLK, DIL_W)
    prev = lambda m: jnp.maximum(m - 1, 0)
    num, ml = pl.pallas_call(
        kern,
        grid=(B, dil, M // QBLK),
        in_specs=[
            pl.BlockSpec(blk, lambda b, r, m: (b, m, 3 * r)),
            pl.BlockSpec(blk, lambda b, r, m: (b, prev(m), 3 * r + 1)),
            pl.BlockSpec(blk, lambda b, r, m: (b, m, 3 * r + 1)),
            pl.BlockSpec(blk, lambda b, r, m: (b, prev(m), 3 * r + 2)),
            pl.BlockSpec(blk, lambda b, r, m: (b, m, 3 * r + 2)),
            pl.BlockSpec((1, DIL_HEADS, QBLK, 2 * QBLK), lambda b, r, m: (pat, 0, 0, 0)),
        ],
        out_specs=[
            pl.BlockSpec(blk, lambda b, r, m: (b, m, r)),
            pl.BlockSpec((1, QBLK, 2 * LANES), lambda b, r, m: (b, m, r)),
        ],
        out_shape=[jax.ShapeDtypeStruct((B, M, dil * DIL_W), F32), jax.ShapeDtypeStruct((B, M, dil * 2 * LANES), F32)],
        compiler_params=_cparams(("parallel", "parallel", "arbitrary")),
        name=f"dil_prompt_{dil}",
    )(pv, pv, pv, pv, pv, bias)
    return num.reshape(B * S, DIL_W), ml.reshape(B * S, 2 * LANES)


def _dil_merge_kernel(*refs):
    nums, mls, o_ref = refs[:N_DIL], refs[N_DIL:2 * N_DIL], refs[2 * N_DIL]
    m_all = [r[:, 0:LANES] for r in mls]
    l_all = [r[:, LANES:2 * LANES] for r in mls]
    m_max = m_all[0]
    for m in m_all[1:]:
        m_max = jnp.maximum(m_max, m)
    w_all = [jnp.exp(m - m_max) for m in m_all]
    den = w_all[0] * l_all[0]
    for w, l in zip(w_all[1:], l_all[1:]):
        den = den + w * l
    for h in range(DIL_HEADS):
        cs = slice(h * HEAD_DIM, (h + 1) * HEAD_DIM)
        acc = w_all[0][:, h:h + 1] * nums[0][:, cs]
        for w, n in zip(w_all[1:], nums[1:]):
            acc = acc + w[:, h:h + 1] * n[:, cs]
        o_ref[:, cs] = (acc / den[:, h:h + 1]).astype(o_ref.dtype)


def dil_merge(nums, mls, *, tm):
    T = nums[0].shape[0]
    return pl.pallas_call(
        _dil_merge_kernel,
        grid=(T // tm,),
        in_specs=[pl.BlockSpec((tm, DIL_W), lambda i: (i, 0))] * N_DIL + [pl.BlockSpec((tm, 2 * LANES), lambda i: (i, 0))] * N_DIL,
        out_specs=pl.BlockSpec((tm, DIL_W), lambda i: (i, 0)),
        out_shape=jax.ShapeDtypeStruct((T, DIL_W), BF16),
        compiler_params=_cparams(("parallel",)),
        name="dil_merge",
    )(*nums, *mls)


def _block_diag_rows(q, n_heads):
    DB, Q, W = q.shape
    dh = W // n_heads
    eye = jnp.repeat(jnp.eye(n_heads, dtype=q.dtype), dh, axis=1)
    return (q[:, None, :, :] * eye[None, :, None, :]).reshape(DB, n_heads * Q, W)


def _own_block(acc, n_heads, rows_per_head):
    rows = acc.shape[0]
    rh = lax.broadcasted_iota(jnp.int32, (rows, 1), 0) // rows_per_head
    out = jnp.zeros((rows, HEAD_DIM), F32)
    for h in range(n_heads):
        out = jnp.where(rh == h, acc[:, h * HEAD_DIM:(h + 1) * HEAD_DIM], out)
    return out


def _row_bias(bucket, tab_ref, head0, n_heads, rows_per_head):
    rows = bucket.shape[0]
    rh = lax.broadcasted_iota(jnp.int32, (rows, 1), 0) // rows_per_head
    out = jnp.zeros(bucket.shape, F32)
    for h in range(n_heads):
        out = jnp.where(rh == h, _table_lookup(bucket, tab_ref, head0 + h), out)
    return out


def _flash_step_cols(s, mask, v, m_scr, l_scr, acc_scr):
    s = jnp.where(mask, s, NEG_INF)
    m_prev = m_scr[:, 0:1]
    m_new = jnp.maximum(m_prev, jnp.max(s, axis=-1, keepdims=True))
    p = jnp.where(mask, jnp.exp(s - m_new), 0.0)
    alpha = jnp.exp(m_prev - m_new)
    l_new = alpha * l_scr[:, 0:1] + jnp.sum(p, axis=-1, keepdims=True)
    acc_scr[...] = alpha * acc_scr[...] + _dot(p.astype(BF16), v)
    m_scr[...] = jnp.broadcast_to(m_new, m_scr.shape)
    l_scr[...] = jnp.broadcast_to(l_new, l_scr.shape)


def _head_cols(ref, first, n_heads, n_rows, stride, row0=0):
    return jnp.concatenate(
        [ref[pl.ds(row0 * stride + first + h, n_rows, stride=stride), :] for h in range(n_heads)], axis=1)


SB_DEC_PAGES = 8


def _sb_decode_kernel(pt_ref, q_ref, new_ref, *rest, n_steps, n_q, scale):
    pages, (o_ref, acc_scr, run_scr) = rest[:SB_DEC_PAGES], rest[SB_DEC_PAGES:]
    s_ = pl.program_id(1)
    q = q_ref[0]
    rows = q.shape[0]
    upper2 = _upper01(PAGE_SIZE)

    def absorb(k, v, mask):
        z2 = _dot_t(q, k.astype(BF16)) * (scale * LOG2_E)
        lk = _log2_keep(z2)
        lb = z2 + lk
        if mask is not None:
            lk = jnp.where(mask, lk, 0.0)
        w = jnp.exp2(lb + _after_in_tile(lk, upper2) + run_scr[...])
        if mask is not None:
            w = jnp.where(mask, w, 0.0)
        acc_scr[...] = acc_scr[...] + _dot(w.astype(BF16), v.astype(BF16))
        run_scr[...] = run_scr[...] + jnp.sum(lk, axis=-1, keepdims=True)

    @pl.when(s_ == 0)
    def _():
        acc_scr[...] = jnp.zeros(acc_scr.shape, F32)
        run_scr[...] = jnp.zeros(run_scr.shape, F32)
        t = lax.broadcasted_iota(jnp.int32, (rows, PAGE_SIZE), 1)
        qi = lax.broadcasted_iota(jnp.int32, (rows, PAGE_SIZE), 0) % n_q
        absorb(new_ref[0, :, :SB_W], new_ref[0, :, SB_W:], (t < qi) & (t < n_q))

    for page in pages:
        absorb(_head_cols(page, 0, SB_HEADS, PAGE_SIZE, 2 * SB_HEADS),
               _head_cols(page, SB_HEADS, SB_HEADS, PAGE_SIZE, 2 * SB_HEADS), None)

    @pl.when(s_ == n_steps - 1)
    def _():
        o_ref[0] = _own_block(acc_scr[...], SB_HEADS, n_q)


def sb_decode(page_table, qbd16, new_kv, cache_lines):
    DB, n_pages = page_table.shape
    assert n_pages % SB_DEC_PAGES == 0
    n_steps = n_pages // SB_DEC_PAGES
    rows = qbd16.shape[1]
    n_q = rows // SB_HEADS
    lines = PAGE_SIZE * 2 * SB_HEADS
    kern = functools.partial(_sb_decode_kernel, n_steps=n_steps, n_q=n_q, scale=HEAD_DIM ** -0.5)
    page_spec = lambda j: pl.BlockSpec(
        (lines, HEAD_DIM), lambda b, s, pt: (pt[b, n_pages - 1 - (s * SB_DEC_PAGES + j)], 0))
    return pl.pallas_call(
        kern,
        grid_spec=pltpu.PrefetchScalarGridSpec(
            num_scalar_prefetch=1,
            grid=(DB, n_steps),
            in_specs=[
                pl.BlockSpec((1, rows, SB_W), lambda b, s, pt: (b, 0, 0)),
                pl.BlockSpec((1, PAGE_SIZE, 2 * SB_W), lambda b, s, pt: (b, 0, 0)),
            ] + [page_spec(j) for j in range(SB_DEC_PAGES)],
            out_specs=pl.BlockSpec((1, rows, HEAD_DIM), lambda b, s, pt: (b, 0, 0)),
            scratch_shapes=[pltpu.VMEM((rows, SB_W), F32), pltpu.VMEM((rows, LANES), F32)],
        ),
        out_shape=jax.ShapeDtypeStruct((DB, rows, HEAD_DIM), F32),
        compiler_params=_cparams(("parallel", "arbitrary")),
        name="sb_decode",
    )(page_table, qbd16, new_kv, *([cache_lines] * SB_DEC_PAGES))


CMP_ROWS_TILE = 256
CHUNKS_PER_PAGE = PAGE_SIZE // CMP_STRIDE


CMP_DEC_PAGES = 8


def _nsa_cmp_decode_kernel(pt_ref, tab_ref, q_ref, w_ref, pe_ref, g_ref, *rest, n_steps, n_q, past, scale):
    pages, (oc_ref, sel_ref, a_scr) = rest[:CMP_DEC_PAGES], rest[CMP_DEC_PAGES:]
    s_ = pl.program_id(1)
    G, R = NSA_KV_HEADS, NSA_REP
    slots = 2 * G
    n_stage = CMP_DEC_PAGES * CHUNKS_PER_PAGE
    rs = pl.ds(pl.multiple_of(s_ * n_stage, n_stage), n_stage)
    for kv in range(2):
        for g in range(G):
            kvg = kv * G + g
            off = kv * NSA_KVW + g * HEAD_DIM
            xs = jnp.concatenate(
                [jnp.concatenate([pg[pl.ds(l * slots + kvg, CHUNKS_PER_PAGE, stride=CMP_STRIDE * slots), :]
                                  for pg in pages], axis=0) for l in range(CMP_STRIDE)], axis=1)
            for j in range(CMP_HALF):
                a_scr[j, rs, off:off + HEAD_DIM] = _dot((xs + pe_ref[kv, j]).astype(BF16), w_ref[kv, j])

    @pl.when(s_ == n_steps - 1)
    def _():
        n_chunk = a_scr.shape[1]
        n_c = n_chunk - CMP_HALF + 1
        out = a_scr[0]
        for j in range(1, CMP_HALF):
            out = out + pltpu.roll(a_scr[j], n_chunk - j, axis=0)
        kparts = []
        for g in range(G):
            kg = out[:, g * HEAD_DIM:(g + 1) * HEAD_DIM]
            kparts.append(kg * lax.rsqrt(jnp.mean(kg * kg, axis=-1, keepdims=True) + RMS_EPS) * g_ref[...])
        kc = jnp.concatenate(kparts, axis=1).astype(BF16)
        vc = out[:, NSA_KVW:2 * NSA_KVW].astype(BF16)

        q = q_ref[0]
        nrows = q.shape[0]
        qi = lax.broadcasted_iota(jnp.int32, (nrows, 1), 0) % n_q
        n_idx = lax.broadcasted_iota(jnp.int32, (1, n_chunk), 1)
        dist = past + qi - (n_idx * CMP_STRIDE + (CMP_BLOCK - 1))
        mask = (dist >= 0) & (n_idx < n_c)
        s = _dot_t(q, kc) * scale + _row_bias(_t5_bucket(dist), tab_ref, 0, NSA_HEADS, n_q)
        s = jnp.where(mask, s, NEG_INF)
        m = jnp.max(s, axis=-1, keepdims=True)
        p = jnp.where(mask, jnp.exp(s - m), 0.0)
        p = p / jnp.maximum(jnp.sum(p, axis=-1, keepdims=True), 1.0)
        oc_ref[0] = _own_block(_dot(p.astype(BF16), vc), G, R * n_q)
        gq = G * n_q
        oi = lax.broadcasted_iota(jnp.int32, (gq, nrows), 0)
        ri = lax.broadcasted_iota(jnp.int32, (gq, nrows), 1)
        pick = jnp.where((ri // (R * n_q) == oi // n_q) & (ri % n_q == oi % n_q), 1.0, 0.0).astype(BF16)
        p1, p2, p3 = _split3(p)
        imp = _dot(pick, p1) + _dot(pick, p2) + _dot(pick, p3)
        n_sp = sel_ref.shape[2]
        ratio = SLC_BLOCK // CMP_STRIDE
        ci = lax.broadcasted_iota(jnp.int32, (n_chunk, n_sp), 0)
        ti = lax.broadcasted_iota(jnp.int32, (n_chunk, n_sp), 1)
        gather01 = jnp.where((ci >= ti * ratio - (CMP_HALF - 1)) & (ci < (ti + 1) * ratio), 1.0, 0.0).astype(BF16)
        score = _dot01(imp, gather01)
        q_blk = (past + lax.broadcasted_iota(jnp.int32, (gq, 1), 0) % n_q) // SLC_BLOCK
        sel_ref[0] = _select_blocks(score, q_blk, N_SELECT)


def nsa_cmp_decode(page_table, table, qbd16, cache_lines, w16, pe, ck_gain, n_q):
    DB, n_pages = page_table.shape
    assert CMP_DEC_PAGES % 2 == 0 and n_pages % CMP_DEC_PAGES == 0
    n_steps = n_pages // CMP_DEC_PAGES
    n_chunk = n_pages * CHUNKS_PER_PAGE
    assert n_chunk % CMP_ROWS_TILE == 0
    past = n_pages * PAGE_SIZE
    n_s = -(-(past + n_q) // SLC_BLOCK)
    n_sp = -(-n_s // LANES) * LANES
    rows = qbd16.shape[1]
    kern = functools.partial(_nsa_cmp_decode_kernel, n_steps=n_steps, n_q=n_q, past=past, scale=HEAD_DIM ** -0.5)
    pg = (PAGE_SIZE * 2 * NSA_KV_HEADS, HEAD_DIM)
    return pl.pallas_call(
        kern,
        grid_spec=pltpu.PrefetchScalarGridSpec(
            num_scalar_prefetch=1,
            grid=(DB, n_steps),
            in_specs=[
                pl.BlockSpec(memory_space=pltpu.SMEM),
                pl.BlockSpec((1, rows, NSA_KVW), lambda b, s, pt: (b, 0, 0)),
                pl.BlockSpec((2, CMP_HALF, CMP_STRIDE * HEAD_DIM, HEAD_DIM), lambda b, s, pt: (0, 0, 0, 0)),
                pl.BlockSpec((2, CMP_HALF, 1, CMP_STRIDE * HEAD_DIM), lambda b, s, pt: (0, 0, 0, 0)),
                pl.BlockSpec((1, HEAD_DIM), lambda b, s, pt: (0, 0)),
            ] + [pl.BlockSpec(pg, lambda b, s, pt, j=j: (pt[b, CMP_DEC_PAGES * s + j], 0)) for j in range(CMP_DEC_PAGES)],
            out_specs=[
                pl.BlockSpec((1, rows, HEAD_DIM), lambda b, s, pt: (b, 0, 0)),
                pl.BlockSpec((1, NSA_KV_HEADS * n_q, n_sp), lambda b, s, pt: (b, 0, 0)),
            ],
            scratch_shapes=[pltpu.VMEM((CMP_HALF, n_chunk, 2 * NSA_KVW), F32)],
        ),
        out_shape=[jax.ShapeDtypeStruct((DB, rows, HEAD_DIM), F32),
                   jax.ShapeDtypeStruct((DB, NSA_KV_HEADS * n_q, n_sp), F32)],
        compiler_params=_cparams(("parallel", "arbitrary")),
        name="nsa_cmp_decode",
    )(page_table, table, qbd16, w16, pe, ck_gain.reshape(1, HEAD_DIM), *([cache_lines] * CMP_DEC_PAGES))


SLC_DEC_PAGES = 8


def _nsa_slc_decode_kernel(pt_ref, plist_ref, pcnt_ref, tab_ref, q_ref, sel_ref, new_ref, wbuf_ref, wnew_ref, oc_ref,
                           gt_ref, *rest, n_steps, n_q, past, scale):
    pages, (o_ref, s_scr, v_scr) = rest[:SLC_DEC_PAGES], rest[SLC_DEC_PAGES:]
    b_ = pl.program_id(0)
    s_ = pl.program_id(1)
    q = q_ref[0]
    rows = q.shape[0]
    G, R = NSA_KV_HEADS, NSA_REP
    slots = 2 * G
    qi = lax.broadcasted_iota(jnp.int32, (rows, 1), 0) % n_q
    rh = lax.broadcasted_iota(jnp.int32, (rows, 1), 0) // n_q
    per_page = PAGE_SIZE // SLC_BLOCK
    n_sp = sel_ref.shape[2]
    col = lax.broadcasted_iota(jnp.int32, (1, PAGE_SIZE), 1)

    def sel_cols(first_block):
        bi = lax.broadcasted_iota(jnp.int32, (n_sp, PAGE_SIZE), 0)
        ki = lax.broadcasted_iota(jnp.int32, (n_sp, PAGE_SIZE), 1)
        expand = jnp.where(bi == first_block + ki // SLC_BLOCK, 1.0, 0.0).astype(BF16)
        return _dot(sel_ref[0], expand) > 0.5

    def near_bias(dist):
        return _row_bias(_t5_bucket(dist), tab_ref, 0, NSA_HEADS, n_q)

    def far_bias():
        out = jnp.zeros((rows, 1), F32)
        for h in range(NSA_HEADS):
            out = jnp.where(rh == h, tab_ref[N_BUCKETS - 1, h], out)
        return jnp.broadcast_to(out, (rows, PAGE_SIZE))

    def stash(slot, k, v, bias, mask):
        s = _dot_t(q, k.astype(BF16)) * scale + bias
        s_scr[slot] = jnp.where(mask, s, NEG_INF)
        v_scr[slot] = v.astype(BF16)

    def softmax_out(n_slots):
        m = lax.fori_loop(0, n_slots, lambda t, m: jnp.maximum(m, jnp.max(s_scr[t], axis=-1, keepdims=True)),
                          jnp.full((rows, 1), NEG_INF, F32))

        def probs(t):
            s = s_scr[t]
            return jnp.where(s > 0.5 * NEG_INF, jnp.exp(s - m), 0.0)

        l = lax.fori_loop(0, n_slots, lambda t, l: l + jnp.sum(probs(t), axis=-1, keepdims=True),
                          jnp.zeros((rows, 1), F32))
        den = jnp.maximum(l, 1.0)
        acc = lax.fori_loop(0, n_slots, lambda t, a: a + _dot((probs(t) / den).astype(BF16), v_scr[t]),
                            jnp.zeros((rows, NSA_KVW), F32))
        return _own_block(acc, G, R * n_q)

    for j, page in enumerate(pages):
        idx = s_ * SLC_DEC_PAGES + j

        @pl.when(idx < pcnt_ref[b_])
        def _(page=page, idx=idx):
            p = plist_ref[b_, idx]
            dist = past + qi - (p * PAGE_SIZE + col)
            bias = lax.cond(past - (p + 1) * PAGE_SIZE + 1 >= BUCKET_MAX_DIST, far_bias, lambda: near_bias(dist))
            stash(idx, _head_cols(page, 0, G, PAGE_SIZE, slots), _head_cols(page, G, G, PAGE_SIZE, slots), bias,
                  sel_cols(p * per_page) & (dist >= 0))

    @pl.when(s_ == n_steps - 1)
    def _():
        n_vis = pcnt_ref[b_]
        dist_n = qi - col
        stash(n_vis, new_ref[0, :, :NSA_KVW], new_ref[0, :, NSA_KVW:], near_bias(dist_n),
              sel_cols(past // SLC_BLOCK) & (dist_n >= 0) & (col < n_q))
        o_s = softmax_out(n_vis + 1)
        wb = wbuf_ref.shape[0] // slots
        n_wt = wb // PAGE_SIZE
        for t in range(n_wt):
            dist_w = wb + qi - (t * PAGE_SIZE + col)
            stash(t, _head_cols(wbuf_ref, 0, G, PAGE_SIZE, slots, row0=t * PAGE_SIZE),
                  _head_cols(wbuf_ref, G, G, PAGE_SIZE, slots, row0=t * PAGE_SIZE), near_bias(dist_w),
                  (dist_w >= 0) & (dist_w < NSA_WINDOW))
        stash(n_wt, wnew_ref[0, :, :NSA_KVW], wnew_ref[0, :, NSA_KVW:], near_bias(dist_n), (dist_n >= 0) & (col < n_q))
        o_w = softmax_out(n_wt + 1)
        gt = gt_ref[0]
        o_ref[0] = gt[:, 0:1] * oc_ref[0] + gt[:, 1:2] * o_s + gt[:, 2:3] * o_w


def nsa_slc_decode(page_table, table, qbd16, sel16, cache_lines, new_kv, win_lines, win_new, o_c, gates, n_q):
    DB, n_pages = page_table.shape
    assert n_pages % SLC_DEC_PAGES == 0
    n_steps = n_pages // SLC_DEC_PAGES
    rows = qbd16.shape[1]
    past = n_pages * PAGE_SIZE
    slots = 2 * NSA_KV_HEADS
    wb_lines = win_lines.shape[0] // DB
    assert (wb_lines // slots) % PAGE_SIZE == 0
    n_slots = max(n_pages, wb_lines // slots // PAGE_SIZE) + 1
    kern = functools.partial(_nsa_slc_decode_kernel, n_steps=n_steps, n_q=n_q, past=past, scale=HEAD_DIM ** -0.5)
    per_page = PAGE_SIZE // SLC_BLOCK
    need = jnp.any(sel16[:, :, :n_pages * per_page].reshape(DB, rows, n_pages, per_page) > 0, axis=(1, 3))
    plist = jnp.argsort(jnp.logical_not(need), axis=1, stable=True).astype(jnp.int32)
    pcnt = jnp.maximum(jnp.sum(need, axis=1), 1).astype(jnp.int32)
    full = lambda shape: pl.BlockSpec((1,) + shape, lambda b, s, pt, pls, pc: (b, 0, 0))

    def page_spec(j):
        def index(b, s, pt, pls, pc):
            return (pt[b, pls[b, jnp.minimum(s * SLC_DEC_PAGES + j, pc[b] - 1)]], 0)
        return pl.BlockSpec((PAGE_SIZE * slots, HEAD_DIM), index)

    return pl.pallas_call(
        kern,
        grid_spec=pltpu.PrefetchScalarGridSpec(
            num_scalar_prefetch=3,
            grid=(DB, n_steps),
            in_specs=[
                pl.BlockSpec(memory_space=pltpu.SMEM),
                full((rows, NSA_KVW)),
                full((rows, sel16.shape[2])),
                full((PAGE_SIZE, 2 * NSA_KVW)),
                pl.BlockSpec((wb_lines, HEAD_DIM), lambda b, s, pt, pls, pc: (b, 0)),
                full((PAGE_SIZE, 2 * NSA_KVW)),
                full((rows, HEAD_DIM)),
                full((rows, LANES)),
            ] + [page_spec(j) for j in range(SLC_DEC_PAGES)],
            out_specs=full((rows, HEAD_DIM)),
            scratch_shapes=[pltpu.VMEM((n_slots, rows, PAGE_SIZE), F32),
                            pltpu.VMEM((n_slots, PAGE_SIZE, NSA_KVW), BF16)],
        ),
        out_shape=jax.ShapeDtypeStruct((DB, rows, HEAD_DIM), F32),
        compiler_params=_cparams(("parallel", "arbitrary")),
        name="nsa_slc_decode",
    )(page_table, plist, pcnt, table, qbd16, sel16, new_kv, win_lines, win_new, o_c, gates,
      *([cache_lines] * SLC_DEC_PAGES))


DIL_DEC_TILE = 256


def _dil_decode_kernel(tab_ref, q_ref, buf_ref, new_ref, o_ref, s_scr, v_scr, *, n_tiles, n_q, wb, scale):
    kt = pl.program_id(1)
    q = q_ref[0]
    rows = q.shape[0]
    qi = lax.broadcasted_iota(jnp.int32, (rows, 1), 0) % n_q
    n = DIL_DEC_TILE
    slots = 2 * DIL_HEADS
    col = lax.broadcasted_iota(jnp.int32, (1, n), 1)

    def stash(slot, k, v, dist):
        s_scr[slot] = _dot_t(q, k.astype(BF16)) * scale + _row_bias(_t5_bucket(dist), tab_ref, 0, DIL_HEADS, n_q)
        v_scr[slot] = v.astype(BF16)

    stash(kt, _head_cols(buf_ref, 0, DIL_HEADS, n, slots), _head_cols(buf_ref, DIL_HEADS, DIL_HEADS, n, slots),
          wb + qi - (kt * n + col))

    @pl.when(kt == n_tiles - 1)
    def _():
        stash(n_tiles, new_ref[0, :, :DIL_W], new_ref[0, :, DIL_W:], qi - col)

        def tile_mask(t, span, dil):
            dist = (wb + qi - (t * n + col)) if t < n_tiles else (qi - col)
            ok = (dist >= 0) & (jnp.bitwise_and(dist, dil - 1) == 0) & (dist <= span * dil)
            return ok if t < n_tiles else ok & (col < n_q)

        nums, ms, ls = [], [], []
        for span, dil in zip(DIL_SPANS, DIL_STEPS):
            masks = [tile_mask(t, span, dil) for t in range(n_tiles + 1)]
            m = jnp.full((rows, 1), NEG_INF, F32)
            for t, mk in enumerate(masks):
                m = jnp.maximum(m, jnp.max(jnp.where(mk, s_scr[t], NEG_INF), axis=-1, keepdims=True))
            l = jnp.zeros((rows, 1), F32)
            acc = jnp.zeros((rows, DIL_W), F32)
            for t, mk in enumerate(masks):
                p = jnp.where(mk, jnp.exp(s_scr[t] - m), 0.0)
                l = l + jnp.sum(p, axis=-1, keepdims=True)
                acc = acc + _dot(p.astype(BF16), v_scr[t])
            nums.append(_own_block(acc, DIL_HEADS, n_q))
            ms.append(m)
            ls.append(l)
        m_max = ms[0]
        for m in ms[1:]:
            m_max = jnp.maximum(m_max, m)
        ws = [jnp.exp(m - m_max) for m in ms]
        num = ws[0] * nums[0]
        den = ws[0] * ls[0]
        for w, n_, l in zip(ws[1:], nums[1:], ls[1:]):
            num = num + w * n_
            den = den + w * l
        o_ref[0] = num / den


def dil_decode(table, qbd16, buf_lines, new_kv, n_q):
    assert new_kv.shape[1] == DIL_DEC_TILE
    DB = qbd16.shape[0]
    slots = 2 * DIL_HEADS
    wb = buf_lines.shape[0] // (DB * slots)
    rows = qbd16.shape[1]
    assert wb % DIL_DEC_TILE == 0
    n_tiles = wb // DIL_DEC_TILE
    kern = functools.partial(_dil_decode_kernel, n_tiles=n_tiles, n_q=n_q, wb=wb, scale=HEAD_DIM ** -0.5)
    return pl.pallas_call(
        kern,
        grid=(DB, n_tiles),
        in_specs=[
            pl.BlockSpec(memory_space=pltpu.SMEM),
            pl.BlockSpec((1, rows, DIL_W), lambda b, t: (b, 0, 0)),
            pl.BlockSpec((DIL_DEC_TILE * slots, HEAD_DIM), lambda b, t: (b * n_tiles + t, 0)),
            pl.BlockSpec((1, new_kv.shape[1], 2 * DIL_W), lambda b, t: (b, 0, 0)),
        ],
        out_specs=pl.BlockSpec((1, rows, HEAD_DIM), lambda b, t: (b, 0, 0)),
        out_shape=jax.ShapeDtypeStruct((DB, rows, HEAD_DIM), F32),
        scratch_shapes=[pltpu.VMEM((n_tiles + 1, rows, DIL_DEC_TILE), F32),
                        pltpu.VMEM((n_tiles + 1, DIL_DEC_TILE, DIL_W), BF16)],
        compiler_params=_cparams(("parallel", "arbitrary")),
        name="dil_decode",
    )(table, qbd16, buf_lines, new_kv)


PROJ_TN = 256
PROJ_TM = 512
_C_SBQ, _C_SBK, _C_SBV = 0, SB_W, 2 * SB_W
_C_NQ = 3 * SB_W
_C_CMP = _C_NQ + NSA_QW
_C_SLC = _C_CMP + 2 * NSA_KVW
_C_WIN = _C_SLC + 2 * NSA_KVW
_C_GATE = _C_WIN + 2 * NSA_KVW
SPARSE_NP = -(-SPARSE_IN // PROJ_TN) * PROJ_TN


def _sparse_proj_plan(w_in, qk_gain):
    w16 = jnp.pad(w_in, ((0, 0), (0, SPARSE_NP - SPARSE_IN))).astype(BF16)
    gain = jnp.ones((SPARSE_NP,), F32)
    flag = np.zeros((SPARSE_NP,), np.float32)
    kinds = np.full((SPARSE_NP // PROJ_TN,), EPI_PLAIN, np.int32)
    for col, width, gi in ((_C_NQ, NSA_QW, 0), (_C_SLC, NSA_KVW, 2), (_C_WIN, NSA_KVW, 3)):
        gain = gain.at[col:col + width].set(jnp.tile(qk_gain[gi], width // HEAD_DIM))
        flag[col:col + width] = 1.0
        assert col % PROJ_TN == 0 and width % PROJ_TN == 0
        kinds[col // PROJ_TN:(col + width) // PROJ_TN] = EPI_NORM
    assert _C_GATE % PROJ_TN == 0 and SPARSE_NP - _C_GATE == PROJ_TN
    kinds[_C_GATE // PROJ_TN] = EPI_SIGMOID
    return w16, gain, jnp.asarray(flag), kinds


def _dil_proj_plan(w_in, qk_gain):
    N = 3 * DIL_W
    gain = jnp.concatenate([jnp.tile(qk_gain[0], DIL_HEADS), jnp.tile(qk_gain[1], DIL_HEADS), jnp.ones((DIL_W,), F32)])
    flag = np.concatenate([np.ones((2 * DIL_W,), np.float32), np.zeros((DIL_W,), np.float32)])
    kinds = np.full((N // PROJ_TN,), EPI_PLAIN, np.int32)
    kinds[:2 * DIL_W // PROJ_TN] = EPI_NORM
    return w_in.astype(BF16), gain, jnp.asarray(flag), kinds


def _pad_rows(a, n):
    return jnp.pad(a, ((0, 0), (0, n - a.shape[1]), (0, 0)))


def _rows_to_tokens(o, n_heads, n_q):
    DB = o.shape[0]
    return o.reshape(DB, n_heads, n_q, HEAD_DIM).transpose(0, 2, 1, 3).reshape(DB * n_q, n_heads * HEAD_DIM)


def kernel(x_prompt, x_sample, cache_sb_kv, cache_nsa_cmp_kv, cache_nsa_slc_kv, state_nsa_win_kv, state_dil_kv,
           page_table, rel_bias_table, norm_mix, norm_ffn, sparse_w_in, sparse_w_out, nsa_qk_gain, nsa_cmp_w,
           nsa_cmp_pe, dil_w_in, dil_w_out, dil_qk_gain, moe_w_group, moe_b_group, moe_w_expert, moe_b_expert,
           moe_w_gate, moe_w_up, moe_w_down):
    B, S, D = x_prompt.shape
    DB, DS, _ = x_sample.shape
    n_pool = cache_sb_kv.shape[1]
    xp = x_prompt.reshape(B * S, D)
    xs = x_sample.reshape(DB * DS, D)
    nsa_table = rel_bias_table[:, :NSA_HEADS]
    dil_table = rel_bias_table[:, :DIL_HEADS]
    G, R = NSA_KV_HEADS, NSA_REP

    def moe(layer, xp, xs, spare=None):
        return moe_layer(xp, xs, norm_ffn[layer], moe_w_group[layer], moe_b_group[layer], moe_w_expert[layer],
                         moe_b_expert[layer], moe_w_gate, moe_w_up, moe_w_down, layer, spare)

    w16, gain, flag, kinds = _sparse_proj_plan(sparse_w_in[0], nsa_qk_gain[0])
    w_out16 = sparse_w_out[0].astype(BF16)
    cw16, cpe = _prep_cmp_weights(nsa_cmp_w[0], nsa_cmp_pe[0])
    ck_gain = nsa_qk_gain[0, 1]
    kv_cols = ((_C_SBK, 2 * SB_W), (_C_CMP, 2 * NSA_KVW), (_C_SLC, 2 * NSA_KVW), (_C_WIN, 2 * NSA_KVW))
    p16, gate32, sb_lines, cmp_lines, slc_lines, win_lines = norm_mm(
        xp, norm_mix[0], w16, gain, flag, kinds, tm=PROJ_TM, tn=PROJ_TN, f32_cols=(_C_GATE, PROJ_TN), lines=kv_cols)
    p16b = p16.reshape(B, S, SPARSE_NP)
    o_sb = sb_prompt(p16b, B, S, q_col=_C_SBQ, k_col=_C_SBK, v_col=_C_SBV)
    chunks = p16b[:, :, _C_CMP:_C_CMP + 2 * NSA_KVW].reshape(B, S // CMP_STRIDE, CHUNK_W)
    kc, vc = compress_prompt(chunks, cw16, cpe, ck_gain)
    tb = nsa_bias_tiles(nsa_table)
    cb = nsa_cmp_bias(nsa_table, S, kc.shape[1])
    o_nsa = nsa_prompt(p16b, gate32.reshape(B, S, PROJ_TN), kc, vc, tb, cb, B, S, q_col=_C_NQ, sk_col=_C_SLC,
                       sv_col=_C_SLC + NSA_KVW, wk_col=_C_WIN, wv_col=_C_WIN + NSA_KVW, gate_col=0)
    a = jnp.concatenate([o_sb, o_nsa], axis=-1).reshape(B * S, SB_W + NSA_QW)
    xp = mm_res(a, w_out16, xp, tm=1024, tn=PROJ_TN)
    sb_kv_prompt = sb_lines.reshape(1, B, S, 2, SB_HEADS, HEAD_DIM)
    cmp_kv_prompt = cmp_lines.reshape(1, B, S, 2, G, HEAD_DIM)
    slc_kv_prompt = slc_lines.reshape(1, B, S, 2, G, HEAD_DIM)
    wn = min(NSA_WINDOW, S)
    win_kv_prompt = win_lines.reshape(1, B, S, 2, G, HEAD_DIM)[:, :, S - wn:]
    _, q32 = norm_mm(xs, norm_mix[0], w16, gain, flag, kinds, tm=DB * DS, tn=PROJ_TN, f32_cols=(0, SPARSE_NP))
    q32b = q32.reshape(DB, DS, SPARSE_NP)
    sb_new = q32b[:, :, _C_SBK:_C_SBK + 2 * SB_W]
    cmp_new = q32b[:, :, _C_CMP:_C_CMP + 2 * NSA_KVW]
    slc_new = q32b[:, :, _C_SLC:_C_SLC + 2 * NSA_KVW]
    win_new = q32b[:, :, _C_WIN:_C_WIN + 2 * NSA_KVW]
    o_sb_s = sb_decode(page_table, _block_diag_rows(q32b[:, :, _C_SBQ:_C_SBQ + SB_W], SB_HEADS).astype(BF16),
                       _pad_rows(sb_new, PAGE_SIZE), cache_sb_kv[0].reshape(-1, HEAD_DIM))
    qn = q32b[:, :, _C_NQ:_C_NQ + NSA_QW].reshape(DB, DS, G, R, HEAD_DIM).transpose(0, 2, 3, 1, 4)
    qn = (qn[:, :, :, :, None, :] * jnp.eye(G, dtype=F32)[None, :, None, None, :, None]).reshape(DB, G * R * DS, NSA_KVW)
    qn16 = qn.astype(BF16)
    o_c, sel = nsa_cmp_decode(page_table, nsa_table, qn16, cache_nsa_cmp_kv[0].reshape(-1, HEAD_DIM), cw16, cpe, ck_gain, DS)
    sel16 = jnp.broadcast_to(sel.reshape(DB, G, 1, DS, -1), (DB, G, R, DS, sel.shape[-1])).reshape(DB, G * R * DS, -1).astype(BF16)
    gates = q32b[:, :, _C_GATE:_C_GATE + 3 * NSA_HEADS].reshape(DB, DS, G, R, 3).transpose(0, 2, 3, 1, 4).reshape(DB, G * R * DS, 3)
    gates = jnp.pad(gates, ((0, 0), (0, 0), (0, LANES - 3)))
    wbuf = state_nsa_win_kv[0]
    o_nsa_s = nsa_slc_decode(page_table, nsa_table, qn16, sel16, cache_nsa_slc_kv[0].reshape(-1, HEAD_DIM),
                             _pad_rows(slc_new, PAGE_SIZE), wbuf.reshape(-1, HEAD_DIM), _pad_rows(win_new, PAGE_SIZE),
                             o_c, gates, DS)
    a_s = jnp.concatenate([_rows_to_tokens(o_sb_s, SB_HEADS, DS), _rows_to_tokens(o_nsa_s, NSA_HEADS, DS)], axis=-1)
    xs = mm_res(a_s.astype(BF16), w_out16, xs, tm=DB * DS, tn=PROJ_TN)
    sb_kv_sample = sb_new.reshape(1, DB, DS, 2, SB_HEADS, HEAD_DIM)
    cmp_kv_sample = cmp_new.reshape(1, DB, DS, 2, G, HEAD_DIM)
    slc_kv_sample = slc_new.reshape(1, DB, DS, 2, G, HEAD_DIM)
    win_kv_sample = jnp.concatenate([wbuf, win_new.reshape(DB, DS, 2, G, HEAD_DIM)], axis=1)[None, :, DS:]
    xp, xs, moe_rows = moe(0, xp, xs)

    w16, gain, flag, kinds = _dil_proj_plan(dil_w_in[0], dil_qk_gain[0])
    w_out16 = dil_w_out[0].astype(BF16)
    p16, dil_lines = norm_mm(xp, norm_mix[1], w16, gain, flag, kinds, tm=PROJ_TM, tn=PROJ_TN,
                             lines=((DIL_W, 2 * DIL_W),))
    dbias = dil_bias_tiles(dil_table)
    stats = [dil_prompt_pattern(p16.reshape(B, S, 3 * DIL_W), dbias, pat, B, S) for pat in range(N_DIL)]
    o = dil_merge([n for n, _ in stats], [ml for _, ml in stats], tm=512)
    xp = mm_res(o, w_out16, xp, tm=1024, tn=PROJ_TN)
    dn = min(DIL_MAX_WINDOW, S)
    dil_kv_prompt = dil_lines.reshape(1, B, S, 2, DIL_HEADS, HEAD_DIM)[:, :, S - dn:]
    _, q32 = norm_mm(xs, norm_mix[1], w16, gain, flag, kinds, tm=DB * DS, tn=PROJ_TN, f32_cols=(0, 3 * DIL_W))
    q32b = q32.reshape(DB, DS, 3 * DIL_W)
    dil_new = q32b[:, :, DIL_W:]
    dbuf = state_dil_kv[0]
    o_s = dil_decode(dil_table, _block_diag_rows(q32b[:, :, :DIL_W], DIL_HEADS).astype(BF16),
                     dbuf.reshape(-1, HEAD_DIM), _pad_rows(dil_new, DIL_DEC_TILE), DS)
    xs = mm_res(_rows_to_tokens(o_s, DIL_HEADS, DS).astype(BF16), w_out16, xs, tm=DB * DS, tn=PROJ_TN)
    dil_kv_sample = jnp.concatenate([dbuf, dil_new.reshape(DB, DS, 2, DIL_HEADS, HEAD_DIM)], axis=1)[None, :, DS:]
    xp, xs, _ = moe(1, xp, xs, moe_rows)

    return (xp.reshape(B, S, D), xs.reshape(DB, DS, D), sb_kv_prompt, sb_kv_sample, cmp_kv_prompt, cmp_kv_sample,
            slc_kv_prompt, slc_kv_sample, win_kv_prompt, win_kv_sample, dil_kv_prompt, dil_kv_sample)
```

```python
import functools
import math

import jax
import jax.numpy as jnp
import numpy as np
from jax import lax
from jax.experimental import pallas as pl
from jax.experimental.pallas import tpu as pltpu

D_MODEL = 2048
HEAD_DIM = 128
SB_HEADS = 8
NSA_HEADS = 8
NSA_KV_HEADS = 2
NSA_REP = NSA_HEADS // NSA_KV_HEADS
CMP_BLOCK = 32
CMP_STRIDE = 16
SLC_BLOCK = 64
N_SELECT = 16
NSA_WINDOW = 512
DIL_HEADS = 16
DIL_PATTERNS = ((128, 1), (512, 4), (2048, 16))
DIL_MAX_WINDOW = 2048
QBLK = 128
N_BUCKETS = 32
BUCKET_MAX_DIST = 2048
N_GROUPS = 4
EXPERTS_PER_GROUP = 4
N_EXPERTS = N_GROUPS * EXPERTS_PER_GROUP
TOP_K_INNER = 2
D_EXPERT = 1024
PAGE_SIZE = 128
RMS_EPS = 1e-6
NEG_INF = -1e30
FORCE_SCORE = 1e9
SB_W = SB_HEADS * HEAD_DIM
NSA_QW = NSA_HEADS * HEAD_DIM
NSA_KVW = NSA_KV_HEADS * HEAD_DIM
SPARSE_IN = 3 * SB_W + NSA_QW + 6 * NSA_KVW + 3 * NSA_HEADS
DIL_W = DIL_HEADS * HEAD_DIM

LANES = 128
SUBLANES = 8
VMEM_LIMIT = 56 * 1024 * 1024

BF16 = jnp.bfloat16
F32 = jnp.float32


def _cparams(sem):
    return pltpu.CompilerParams(dimension_semantics=sem, vmem_limit_bytes=VMEM_LIMIT)


def _dot(a, b):
    return jnp.dot(a, b, preferred_element_type=F32)


def _dot_t(a, b):
    return lax.dot_general(a, b, (((1,), (1,)), ((), ())), preferred_element_type=F32)


def _split3(x):
    h1 = x.astype(BF16)
    r1 = x - h1.astype(F32)
    h2 = r1.astype(BF16)
    h3 = (r1 - h2.astype(F32)).astype(BF16)
    return h1, h2, h3


def _dot01(x, m01):
    h1, h2, h3 = _split3(x)
    return _dot(h1, m01) + _dot(h2, m01) + _dot(h3, m01)


EPI_PLAIN, EPI_NORM, EPI_SIGMOID = 0, 1, 2


MM_ROW_CHUNK = 256


def _norm_mm_kernel(cls_ref, x_ref, g_ref, w_ref, cg_ref, cf_ref, o16_ref, *rest, classes, has_f32, line_cfg):
    h_scr = rest[-1]
    o32_ref = rest[0] if has_f32 else None
    line_refs = rest[(1 if has_f32 else 0):-1]
    j = pl.program_id(1)
    tm, tn = o16_ref.shape
    rc = min(MM_ROW_CHUNK, tm)

    @pl.when(j == 0)
    def _():
        x = x_ref[...]
        y = x * lax.rsqrt(jnp.mean(x * x, axis=-1, keepdims=True) + RMS_EPS)
        h_scr[...] = (y * g_ref[...]).astype(BF16)

    def epilogue(kind, acc):
        if kind == EPI_PLAIN:
            return acc
        if kind == EPI_SIGMOID:
            return jax.nn.sigmoid(acc)
        parts = []
        for c in range(tn // HEAD_DIM):
            a = acc[:, c * HEAD_DIM:(c + 1) * HEAD_DIM]
            parts.append(a * lax.rsqrt(jnp.mean(a * a, axis=-1, keepdims=True) + RMS_EPS))
        return jnp.where(cf_ref[...] > 0.5, jnp.concatenate(parts, axis=1) * cg_ref[...], acc)

    for cid, (kind, in_f32, li) in enumerate(classes):
        @pl.when(cls_ref[j] == cid)
        def _(kind=kind, in_f32=in_f32, li=li):
            for r in range(tm // rc):
                rows = slice(r * rc, (r + 1) * rc)
                out = epilogue(kind, _dot(h_scr[rows, :], w_ref[...]))
                o16_ref[rows, :] = out.astype(BF16)
                if in_f32:
                    o32_ref[rows, :] = out
                if li is not None:
                    j0, slots = line_cfg[li]
                    for c in range(tn // LANES):
                        line_refs[li][pl.ds(r * rc * slots + (j - j0) * (tn // LANES) + c, rc, stride=slots), :] = (
                            out[:, c * LANES:(c + 1) * LANES])


def norm_mm(x, g, w16, col_gain, col_flag, tile_kind, *, tm, tn, f32_cols=None, lines=()):
    T, D = x.shape
    N = w16.shape[1]
    assert T % tm == 0 and N % tn == 0
    n_tiles = N // tn
    grid = (T // tm, n_tiles)
    tile_kind = np.asarray(tile_kind)
    out_specs = [pl.BlockSpec((tm, tn), lambda i, j, k: (i, j))]
    out_shape = [jax.ShapeDtypeStruct((T, N), BF16)]
    in_f32 = np.zeros((n_tiles,), bool)
    if f32_cols is not None:
        c0, width = f32_cols
        assert c0 % tn == 0 and width % tn == 0
        j0, nj = c0 // tn, width // tn
        in_f32[j0:j0 + nj] = True
        out_specs.append(pl.BlockSpec((tm, tn), lambda i, j, k, j0=j0, nj=nj: (i, jnp.clip(j - j0, 0, nj - 1))))
        out_shape.append(jax.ShapeDtypeStruct((T, width), F32))
    line_of = [None] * n_tiles
    line_cfg = []
    for li, (c0, width) in enumerate(lines):
        assert c0 % tn == 0 and width % tn == 0
        slots = width // LANES
        line_cfg.append((c0 // tn, slots))
        for j in range(c0 // tn, (c0 + width) // tn):
            line_of[j] = li
        out_specs.append(pl.BlockSpec((tm * slots, LANES), lambda i, j, k: (i, 0)))
        out_shape.append(jax.ShapeDtypeStruct((T * slots, LANES), F32))
    per_tile = [(int(tile_kind[j]), bool(in_f32[j]), line_of[j]) for j in range(n_tiles)]
    classes = tuple(dict.fromkeys(per_tile))
    tile_cls = jnp.asarray([classes.index(c) for c in per_tile], jnp.int32)
    kern = functools.partial(_norm_mm_kernel, classes=classes, has_f32=f32_cols is not None, line_cfg=tuple(line_cfg))
    return pl.pallas_call(
        kern,
        grid_spec=pltpu.PrefetchScalarGridSpec(
            num_scalar_prefetch=1,
            grid=grid,
            in_specs=[
                pl.BlockSpec((tm, D), lambda i, j, k: (i, 0)),
                pl.BlockSpec((1, D), lambda i, j, k: (0, 0)),
                pl.BlockSpec((D, tn), lambda i, j, k: (0, j)),
                pl.BlockSpec((1, tn), lambda i, j, k: (0, j)),
                pl.BlockSpec((1, tn), lambda i, j, k: (0, j)),
            ],
            out_specs=out_specs,
            scratch_shapes=[pltpu.VMEM((tm, D), BF16)],
        ),
        out_shape=out_shape,
        compiler_params=_cparams(("parallel", "arbitrary")),
        name="norm_mm",
    )(tile_cls, x, g.reshape(1, D), w16, col_gain.reshape(1, N), col_flag.reshape(1, N))


def _mm_res_kernel(a_ref, w_ref, r_ref, o_ref):
    tm = o_ref.shape[0]
    rc = min(MM_ROW_CHUNK, tm)
    for r in range(tm // rc):
        rows = slice(r * rc, (r + 1) * rc)
        o_ref[rows, :] = r_ref[rows, :] + _dot(a_ref[rows, :], w_ref[...])


def mm_res(a16, w16, res, *, tm, tn):
    T, K = a16.shape
    N = w16.shape[1]
    assert T % tm == 0 and N % tn == 0
    return pl.pallas_call(
        _mm_res_kernel,
        grid=(T // tm, N // tn),
        in_specs=[
            pl.BlockSpec((tm, K), lambda i, j: (i, 0)),
            pl.BlockSpec((K, tn), lambda i, j: (0, j)),
            pl.BlockSpec((tm, tn), lambda i, j: (i, j)),
        ],
        out_specs=pl.BlockSpec((tm, tn), lambda i, j: (i, j)),
        out_shape=jax.ShapeDtypeStruct((T, N), F32),
        compiler_params=_cparams(("parallel", "arbitrary")),
        name="mm_res",
    )(a16, w16, res)


LOG2_E = math.log2(math.e)


def _log2_keep(z2):
    nz = -z2
    return jnp.minimum(nz, 0.0) - jnp.log2(1.0 + jnp.exp2(jnp.minimum(z2, nz)))


def _upper01(n):
    r = lax.broadcasted_iota(jnp.int32, (2 * n, n), 0)
    c = lax.broadcasted_iota(jnp.int32, (2 * n, n), 1)
    return jnp.where(jnp.where(r >= n, r - n, r) > c, 1.0, 0.0).astype(BF16)


def _after_in_tile(lk, upper2):
    h1 = lk.astype(BF16)
    h2 = (lk - h1.astype(F32)).astype(BF16)
    return _dot(jnp.concatenate([h1, h2], axis=1), upper2)


def _sb_prompt_kernel(q_ref, k_ref, v_ref, o_ref, acc_scr, run_scr, *, bq, scale):
    bk = LANES
    i = pl.program_id(2)
    nd = bq // bk
    acc_scr[...] = jnp.zeros(acc_scr.shape, F32)
    run_scr[...] = jnp.zeros(run_scr.shape, F32)
    upper2 = _upper01(bk)

    def absorb_pair_masked(kbl, r0):
        ks = pl.multiple_of(kbl * bk, 2 * bk)
        k = k_ref[0, pl.ds(ks, 2 * bk), :]
        v = v_ref[0, pl.ds(ks, 2 * bk), :]
        z2 = _dot_t(q_ref[0, r0:, :], k) * (scale * LOG2_E)
        lk = _log2_keep(z2)
        lb = z2 + lk
        qpos = i * bq + r0 + lax.broadcasted_iota(jnp.int32, (bq - r0, 1), 0)
        mask = ks + lax.broadcasted_iota(jnp.int32, (1, 2 * bk), 1) < qpos
        lk = jnp.where(mask, lk, 0.0)
        lk_l, lk_r = lk[:, :bk], lk[:, bk:]
        sum_r = jnp.sum(lk_r, axis=-1, keepdims=True)
        run = run_scr[r0:, :]
        w_r = jnp.exp2(lb[:, bk:] + _after_in_tile(lk_r, upper2) + run)
        w_l = jnp.exp2(lb[:, :bk] + _after_in_tile(lk_l, upper2) + (run + sum_r))
        w = jnp.where(mask, jnp.concatenate([w_l, w_r], axis=1), 0.0)
        acc_scr[r0:, :] = acc_scr[r0:, :] + _dot(w.astype(BF16), v)
        run_scr[r0:, :] = run + (sum_r + jnp.sum(lk_l, axis=-1, keepdims=True))

    for p in range(nd // 2 - 1, -1, -1):
        absorb_pair_masked(i * nd + 2 * p, 2 * p * bk)

    assert nd % 2 == 0

    def body(t, carry):
        ks = pl.multiple_of((i * nd - 2 - 2 * t) * bk, 2 * bk)
        k = k_ref[0, pl.ds(ks, 2 * bk), :]
        v = v_ref[0, pl.ds(ks, 2 * bk), :]
        z2 = _dot_t(q_ref[0], k) * (scale * LOG2_E)
        lk = _log2_keep(z2)
        lb = z2 + lk
        lk_l, lk_r = lk[:, :bk], lk[:, bk:]
        sum_r = jnp.sum(lk_r, axis=-1, keepdims=True)
        run = run_scr[...]
        w_r = jnp.exp2(lb[:, bk:] + _after_in_tile(lk_r, upper2) + run)
        w_l = jnp.exp2(lb[:, :bk] + _after_in_tile(lk_l, upper2) + (run + sum_r))
        acc_scr[...] = acc_scr[...] + _dot(jnp.concatenate([w_l.astype(BF16), w_r.astype(BF16)], axis=1), v)
        run_scr[...] = run + (sum_r + jnp.sum(lk_l, axis=-1, keepdims=True))
        return carry

    lax.fori_loop(0, i * (nd // 2), body, 0)
    o_ref[0] = acc_scr[...].astype(o_ref.dtype)


def sb_prompt(p16, B, S, *, q_col, k_col, v_col, bq=512):
    assert S % bq == 0 and bq % LANES == 0
    kern = functools.partial(_sb_prompt_kernel, bq=bq, scale=HEAD_DIM ** -0.5)
    qb, kb_, vb = q_col // HEAD_DIM, k_col // HEAD_DIM, v_col // HEAD_DIM
    return pl.pallas_call(
        kern,
        grid=(B, SB_HEADS, S // bq),
        in_specs=[
            pl.BlockSpec((1, bq, HEAD_DIM), lambda b, h, i: (b, i, qb + h)),
            pl.BlockSpec((1, S, HEAD_DIM), lambda b, h, i: (b, 0, kb_ + h)),
            pl.BlockSpec((1, S, HEAD_DIM), lambda b, h, i: (b, 0, vb + h)),
        ],
        out_specs=pl.BlockSpec((1, bq, HEAD_DIM), lambda b, h, i: (b, i, h)),
        out_shape=jax.ShapeDtypeStruct((B, S, SB_W), BF16),
        scratch_shapes=[pltpu.VMEM((bq, HEAD_DIM), F32), pltpu.VMEM((bq, LANES), F32)],
        compiler_params=_cparams(("parallel", "parallel", "arbitrary")),
        name="sb_prompt",
    )(p16, p16, p16)


def _t5_bucket(dist):
    n = jnp.maximum(dist, 0)
    exact = N_BUCKETS // 2
    nf = jnp.maximum(n, exact).astype(F32)
    far = exact + (jnp.log(nf / exact) / math.log(BUCKET_MAX_DIST / exact) * (N_BUCKETS - exact)).astype(jnp.int32)
    return jnp.where(n < exact, n, jnp.minimum(far, N_BUCKETS - 1))


def _table_lookup(bucket, tab_ref, head):
    out = jnp.zeros(bucket.shape, F32)
    for k in range(N_BUCKETS):
        out = jnp.where(bucket == k, tab_ref[k, head], out)
    return out


N_BIAS_DIAG = -(-(BUCKET_MAX_DIST + QBLK) // QBLK) + 1


def _bias_tiles_kernel(tab_ref, o_ref):
    g = pl.program_id(0)
    d = pl.program_id(1)
    dist = d * QBLK + lax.broadcasted_iota(jnp.int32, (QBLK, QBLK), 0) - lax.broadcasted_iota(jnp.int32, (QBLK, QBLK), 1)
    bucket = _t5_bucket(dist)
    for r in range(NSA_REP):
        o_ref[0, 0, r * QBLK:(r + 1) * QBLK, :] = _table_lookup(bucket, tab_ref, g * NSA_REP + r)


def nsa_bias_tiles(table):
    return pl.pallas_call(
        _bias_tiles_kernel,
        grid=(NSA_KV_HEADS, N_BIAS_DIAG),
        in_specs=[pl.BlockSpec(memory_space=pltpu.SMEM)],
        out_specs=pl.BlockSpec((1, 1, NSA_REP * QBLK, QBLK), lambda g, d: (g, d, 0, 0)),
        out_shape=jax.ShapeDtypeStruct((NSA_KV_HEADS, N_BIAS_DIAG, NSA_REP * QBLK, QBLK), F32),
        compiler_params=_cparams(("parallel", "arbitrary")),
        name="nsa_bias_tiles",
    )(table)


CMP_HALF = CMP_BLOCK // CMP_STRIDE
CHUNK_W = CMP_STRIDE * 2 * NSA_KVW


def _compress_chunks(x, w_ref, pe_ref, gain):
    n = x.shape[0]
    outs = []
    for kv in range(2):
        for g in range(NSA_KV_HEADS):
            off = kv * NSA_KVW + g * HEAD_DIM
            xs = jnp.concatenate(
                [x[:, l * 2 * NSA_KVW + off:l * 2 * NSA_KVW + off + HEAD_DIM] for l in range(CMP_STRIDE)], axis=1)
            halves = []
            for j in range(CMP_HALF):
                w = w_ref[kv, j]
                pe = jnp.broadcast_to(pe_ref[kv, j], (SUBLANES, CMP_STRIDE * HEAD_DIM)).astype(BF16)
                halves.append(_dot(xs, w) + _dot(pe, w)[0:1, :])
            out = halves[0]
            for j in range(1, CMP_HALF):
                out = out + pltpu.roll(halves[j], n - j, axis=0)
            if kv == 0:
                out = out * lax.rsqrt(jnp.mean(out * out, axis=-1, keepdims=True) + RMS_EPS) * gain
            outs.append(out)
    kc = jnp.concatenate(outs[:NSA_KV_HEADS], axis=1)
    vc = jnp.concatenate(outs[NSA_KV_HEADS:], axis=1)
    return kc, vc


def _compress_prompt_kernel(x_ref, w_ref, pe_ref, g_ref, kc_ref, vc_ref):
    kc, vc = _compress_chunks(x_ref[0], w_ref, pe_ref, g_ref[...])
    kc_ref[0] = kc.astype(BF16)
    vc_ref[0] = vc.astype(BF16)


def _prep_cmp_weights(cmp_w, cmp_pe):
    w = cmp_w.reshape(2, CMP_HALF, CMP_STRIDE * HEAD_DIM, HEAD_DIM).astype(BF16)
    pe = cmp_pe.reshape(2, CMP_HALF, 1, CMP_STRIDE * HEAD_DIM)
    return w, pe


def compress_prompt(chunks16, w16, pe, ck_gain):
    B, n, _ = chunks16.shape
    return pl.pallas_call(
        _compress_prompt_kernel,
        grid=(B,),
        in_specs=[
            pl.BlockSpec((1, n, CHUNK_W), lambda b: (b, 0, 0)),
            pl.BlockSpec((2, CMP_HALF, CMP_STRIDE * HEAD_DIM, HEAD_DIM), lambda b: (0, 0, 0, 0)),
            pl.BlockSpec((2, CMP_HALF, 1, CMP_STRIDE * HEAD_DIM), lambda b: (0, 0, 0, 0)),
            pl.BlockSpec((1, HEAD_DIM), lambda b: (0, 0)),
        ],
        out_specs=[pl.BlockSpec((1, n, NSA_KVW), lambda b: (b, 0, 0))] * 2,
        out_shape=[jax.ShapeDtypeStruct((B, n, NSA_KVW), BF16)] * 2,
        compiler_params=_cparams(("parallel",)),
        name="nsa_compress_prompt",
    )(chunks16, w16, pe, ck_gain.reshape(1, HEAD_DIM))


def _flash_step(s, mask, v, m_scr, l_scr, acc_scr):
    n = s.shape[1] // LANES
    cols = [slice(c * LANES, (c + 1) * LANES) for c in range(n)]
    sm = [jnp.where(mask[:, c], s[:, c], NEG_INF) for c in cols]
    top = sm[0]
    for x in sm[1:]:
        top = jnp.maximum(top, x)
    m_prev = m_scr[...]
    m_new = jnp.maximum(m_prev, jnp.max(top, axis=-1, keepdims=True))
    ps = [jnp.where(mask[:, c], jnp.exp(x - m_new), 0.0) for c, x in zip(cols, sm)]
    tot = ps[0]
    for x in ps[1:]:
        tot = tot + x
    alpha = jnp.exp(m_prev - m_new)
    l_scr[...] = alpha * l_scr[...] + jnp.sum(tot, axis=-1, keepdims=True)
    acc_scr[...] = alpha * acc_scr[...] + _dot(jnp.concatenate([x.astype(BF16) for x in ps], axis=1), v)
    m_scr[...] = m_new


def _flash_reset(m_scr, l_scr, acc_scr):
    m_scr[...] = jnp.full(m_scr.shape, NEG_INF, F32)
    l_scr[...] = jnp.zeros(l_scr.shape, F32)
    acc_scr[...] = jnp.zeros(acc_scr.shape, F32)


def _select_blocks(score, q_blk, n_sel):
    Q, n_s = score.shape
    blk = lax.broadcasted_iota(jnp.int32, (1, n_s), 1)
    valid = blk <= q_blk
    forced = valid & ((blk == 0) | (blk == q_blk) | (blk == q_blk - 1))
    score = jnp.where(forced, FORCE_SCORE, jnp.where(valid, score, NEG_INF))
    rank = jnp.zeros((Q, n_s), F32)
    for c in range(n_s):
        col = score[:, c:c + 1]
        before = (col > score) | ((col == score) & (blk > c))
        rank = rank + jnp.where(before, 1.0, 0.0)
    return jnp.where(rank < n_sel, 1.0, 0.0)


def _select_blocks_t(score_t, q_blk, n_sel):
    n_s, Q = score_t.shape
    blk = lax.broadcasted_iota(jnp.int32, (n_s, 1), 0)
    valid = blk <= q_blk
    forced = valid & ((blk == 0) | (blk == q_blk) | (blk == q_blk - 1))
    score_t = jnp.where(forced, FORCE_SCORE, jnp.where(valid, score_t, NEG_INF))
    rank = jnp.zeros((n_s, Q), F32)
    for c in range(n_s):
        row = score_t[c:c + 1, :]
        before = (row > score_t) | ((row == score_t) & (blk > c))
        rank = rank + jnp.where(before, 1.0, 0.0)
    return jnp.where(rank < n_sel, 1.0, 0.0)


def _nsa_cmp_bias_kernel(tab_ref, o_ref):
    g = pl.program_id(0)
    i = pl.program_id(1)
    ncp = o_ref.shape[3]
    qpos1 = i * QBLK + lax.broadcasted_iota(jnp.int32, (QBLK, 1), 0)
    c_end = lax.broadcasted_iota(jnp.int32, (1, ncp), 1) * CMP_STRIDE + (CMP_BLOCK - 1)
    bucket = _t5_bucket(qpos1 - c_end)
    for r in range(NSA_REP):
        o_ref[0, 0, r * QBLK:(r + 1) * QBLK, :] = _table_lookup(bucket, tab_ref, g * NSA_REP + r)


def nsa_cmp_bias(table, S, ncp):
    return pl.pallas_call(
        _nsa_cmp_bias_kernel,
        grid=(NSA_KV_HEADS, S // QBLK),
        in_specs=[pl.BlockSpec(memory_space=pltpu.SMEM)],
        out_specs=pl.BlockSpec((1, 1, NSA_REP * QBLK, ncp), lambda g, i: (g, i, 0, 0)),
        out_shape=jax.ShapeDtypeStruct((NSA_KV_HEADS, S // QBLK, NSA_REP * QBLK, ncp), F32),
        compiler_params=_cparams(("parallel", "arbitrary")),
        name="nsa_cmp_bias",
    )(table)


NSA_KT = 2 * QBLK


def _nsa_prompt_kernel(q_ref, kc_ref, vc_ref, ks_ref, vs_ref, kw_ref, vw_ref, gt_ref, tb_ref, cb_ref,
                       o_ref, m_scr, l_scr, acc_scr, sel_scr, *, n_c, scale):
    g = pl.program_id(1)
    i = pl.program_id(2)
    R = NSA_REP
    qs = i * QBLK
    q4 = q_ref[0]
    q = jnp.concatenate([q4[:, r * HEAD_DIM:(r + 1) * HEAD_DIM] for r in range(R)], axis=0)
    qpos1 = qs + lax.broadcasted_iota(jnp.int32, (QBLK, 1), 0)
    qpos = jnp.concatenate([qpos1] * R, axis=0)

    ncp = kc_ref.shape[1]
    s = _dot_t(q, kc_ref[0]) * scale
    n_idx = lax.broadcasted_iota(jnp.int32, (1, ncp), 1)
    mask = (qpos - (n_idx * CMP_STRIDE + (CMP_BLOCK - 1)) >= 0) & (n_idx < n_c)
    s = jnp.where(mask, s + cb_ref[0, 0], NEG_INF)
    m = jnp.max(s, axis=-1, keepdims=True)
    p = jnp.where(mask, jnp.exp(s - m), 0.0)
    l = jnp.sum(p, axis=-1, keepdims=True)
    p = p / jnp.maximum(l, 1.0)
    o_c = _dot(p.astype(BF16), vc_ref[0])
    imp = p[0:QBLK]
    for r in range(1, R):
        imp = imp + p[r * QBLK:(r + 1) * QBLK]
    n_s = ks_ref.shape[1] // SLC_BLOCK
    ratio = SLC_BLOCK // CMP_STRIDE
    ti = lax.broadcasted_iota(jnp.int32, (n_s, ncp), 0)
    ci = lax.broadcasted_iota(jnp.int32, (n_s, ncp), 1)
    gather01 = jnp.where((ci >= ti * ratio - (CMP_HALF - 1)) & (ci < (ti + 1) * ratio), 1.0, 0.0).astype(BF16)
    i1, i2, i3 = _split3(imp)
    score_t = _dot_t(gather01, i1) + _dot_t(gather01, i2) + _dot_t(gather01, i3)
    q_blk = (qs + lax.broadcasted_iota(jnp.int32, (1, QBLK), 1)) // SLC_BLOCK
    sel_scr[...] = _select_blocks_t(score_t, q_blk, min(N_SELECT, n_s)).T.astype(BF16)

    kcol = lax.broadcasted_iota(jnp.int32, (1, NSA_KT), 1)

    def tile_bias(kt):
        parts = [tb_ref[0, jnp.clip(i - (kt * (NSA_KT // QBLK) + t), 0, N_BIAS_DIAG - 1)] for t in range(NSA_KT // QBLK)]
        return jnp.concatenate(parts, axis=1)

    _flash_reset(m_scr, l_scr, acc_scr)

    def slc_tile(kt, causal):
        ks_ = pl.multiple_of(kt * NSA_KT, NSA_KT)
        k = ks_ref[0, pl.ds(ks_, NSA_KT), :]
        v = vs_ref[0, pl.ds(ks_, NSA_KT), :]
        bi = lax.broadcasted_iota(jnp.int32, (n_s, NSA_KT), 0)
        ki = lax.broadcasted_iota(jnp.int32, (n_s, NSA_KT), 1)
        expand = jnp.where(bi == kt * (NSA_KT // SLC_BLOCK) + ki // SLC_BLOCK, 1.0, 0.0).astype(BF16)
        selk = _dot(sel_scr[...], expand)
        msk = jnp.concatenate([selk] * R, axis=0) > 0.5
        if causal:
            msk = msk & (qpos - (ks_ + kcol) >= 0)
        sc = _dot_t(q, k) * scale + tile_bias(kt)
        _flash_step(sc, msk, v, m_scr, l_scr, acc_scr)

    assert NSA_KT % QBLK == 0
    n_full = qs // NSA_KT

    def slc_body(kt, carry):
        slc_tile(kt, False)
        return carry

    lax.fori_loop(0, n_full, slc_body, 0)
    slc_tile(n_full, True)
    o_s = acc_scr[...] / jnp.maximum(l_scr[...], 1.0)

    _flash_reset(m_scr, l_scr, acc_scr)

    def win_body(kt, carry):
        ks_ = pl.multiple_of(kt * NSA_KT, NSA_KT)
        k = kw_ref[0, pl.ds(ks_, NSA_KT), :]
        v = vw_ref[0, pl.ds(ks_, NSA_KT), :]
        dist = qpos - (ks_ + kcol)
        msk = (dist >= 0) & (dist < NSA_WINDOW)
        sc = _dot_t(q, k) * scale + tile_bias(kt)
        _flash_step(sc, msk, v, m_scr, l_scr, acc_scr)
        return carry

    lax.fori_loop(jnp.maximum(qs - (NSA_WINDOW - 1), 0) // NSA_KT, (qs + QBLK - 1) // NSA_KT + 1, win_body, 0)
    o_w = acc_scr[...] / jnp.maximum(l_scr[...], 1.0)

    gt = gt_ref[0]
    outs = []
    for r in range(R):
        c0 = (g * R + r) * 3
        rows = slice(r * QBLK, (r + 1) * QBLK)
        lane = lax.broadcasted_iota(jnp.int32, (1, LANES), 1)
        gates = [jnp.sum(jnp.where(lane == c0 + t, gt, 0.0), axis=-1, keepdims=True) for t in range(3)]
        outs.append(gates[0] * o_c[rows] + gates[1] * o_s[rows] + gates[2] * o_w[rows])
    o_ref[0] = jnp.concatenate(outs, axis=1).astype(o_ref.dtype)


def nsa_prompt(p16, p32, kc, vc, tb, cb, B, S, *, q_col, sk_col, sv_col, wk_col, wv_col, gate_col):
    G, R = NSA_KV_HEADS, NSA_REP
    assert S % NSA_KT == 0
    n_c = S // CMP_STRIDE - CMP_HALF + 1
    kern = functools.partial(_nsa_prompt_kernel, n_c=n_c, scale=HEAD_DIM ** -0.5)
    hd = HEAD_DIM
    return pl.pallas_call(
        kern,
        grid_spec=pltpu.PrefetchScalarGridSpec(
            num_scalar_prefetch=0,
            grid=(B, G, S // QBLK),
            in_specs=[
                pl.BlockSpec((1, QBLK, R * hd), lambda b, g, i: (b, i, q_col // (R * hd) + g)),
                pl.BlockSpec((1, kc.shape[1], hd), lambda b, g, i: (b, 0, g)),
                pl.BlockSpec((1, vc.shape[1], hd), lambda b, g, i: (b, 0, g)),
                pl.BlockSpec((1, S, hd), lambda b, g, i: (b, 0, sk_col // hd + g)),
                pl.BlockSpec((1, S, hd), lambda b, g, i: (b, 0, sv_col // hd + g)),
                pl.BlockSpec((1, S, hd), lambda b, g, i: (b, 0, wk_col // hd + g)),
                pl.BlockSpec((1, S, hd), lambda b, g, i: (b, 0, wv_col // hd + g)),
                pl.BlockSpec((1, QBLK, LANES), lambda b, g, i: (b, i, gate_col // LANES)),
                pl.BlockSpec((1, N_BIAS_DIAG, R * QBLK, QBLK), lambda b, g, i: (g, 0, 0, 0)),
                pl.BlockSpec((1, 1, R * QBLK, cb.shape[3]), lambda b, g, i: (g, i, 0, 0)),
            ],
            out_specs=pl.BlockSpec((1, QBLK, R * hd), lambda b, g, i: (b, i, g)),
            scratch_shapes=[
                pltpu.VMEM((R * QBLK, LANES), F32),
                pltpu.VMEM((R * QBLK, LANES), F32),
                pltpu.VMEM((R * QBLK, hd), F32),
                pltpu.VMEM((QBLK, S // SLC_BLOCK), BF16),
            ],
        ),
        out_shape=jax.ShapeDtypeStruct((B, S, NSA_QW), BF16),
        compiler_params=_cparams(("parallel", "parallel", "arbitrary")),
        name="nsa_prompt",
    )(p16, kc, vc, p16, p16, p16, p16, p32, tb, cb)


ROUTER_W = LANES
MOE_TM = 256


def _router_kernel(x_ref, g_ref, w_ref, b_ref, h_ref, info_ref):
    x = x_ref[...]
    hn = x * lax.rsqrt(jnp.mean(x * x, axis=-1, keepdims=True) + RMS_EPS) * g_ref[...]
    h_ref[...] = hn
    logits = _dot(hn.astype(BF16), w_ref[...]) + b_ref[...]
    lane = lax.broadcasted_iota(jnp.int32, (1, ROUTER_W), 1)
    big = ROUTER_W

    def first_lane(cond):
        return jnp.min(jnp.where(cond, lane, big), axis=-1, keepdims=True)

    gmask = lane < N_GROUPS
    gl = jnp.where(gmask, logits, NEG_INF)
    gm = jnp.max(gl, axis=-1, keepdims=True)
    g_idx = first_lane(gmask & (gl == gm))
    g_w = 1.0 / jnp.sum(jnp.where(gmask, jnp.exp(gl - gm), 0.0), axis=-1, keepdims=True)
    lo = N_GROUPS + g_idx * EXPERTS_PER_GROUP
    emask = (lane >= lo) & (lane < lo + EXPERTS_PER_GROUP)
    el = jnp.where(emask, logits, NEG_INF)
    em = jnp.max(el, axis=-1, keepdims=True)
    ee = jnp.where(emask, jnp.exp(el - em), 0.0)
    pe = jnp.where(emask, ee / jnp.sum(ee, axis=-1, keepdims=True), -1.0)
    v1 = jnp.max(pe, axis=-1, keepdims=True)
    i1 = first_lane(pe == v1)
    pe2 = jnp.where(lane == i1, -1.0, pe)
    v2 = jnp.max(pe2, axis=-1, keepdims=True)
    i2 = first_lane((pe2 == v2) & emask & (lane != i1))
    tot = v1 + v2
    info = jnp.where(lane == 0, (i1 - N_GROUPS).astype(F32), 0.0)
    info = jnp.where(lane == 1, (i2 - N_GROUPS).astype(F32), info)
    info = jnp.where(lane == 2, g_w * (v1 / tot), info)
    info = jnp.where(lane == 3, g_w * (v2 / tot), info)
    info_ref[...] = info


def moe_router(x, g, w_group, b_group, w_expert, b_expert, *, tm):
    T, D = x.shape
    pad = ROUTER_W - N_GROUPS - N_EXPERTS
    w = jnp.pad(jnp.concatenate([w_group, w_expert], axis=1), ((0, 0), (0, pad)))
    b = jnp.pad(jnp.concatenate([b_group, b_expert]), (0, pad)).reshape(1, ROUTER_W)
    return pl.pallas_call(
        _router_kernel,
        grid=(T // tm,),
        in_specs=[
            pl.BlockSpec((tm, D), lambda i: (i, 0)),
            pl.BlockSpec((1, D), lambda i: (0, 0)),
            pl.BlockSpec((D, ROUTER_W), lambda i: (0, 0)),
            pl.BlockSpec((1, ROUTER_W), lambda i: (0, 0)),
        ],
        out_specs=[pl.BlockSpec((tm, D), lambda i: (i, 0)), pl.BlockSpec((tm, ROUTER_W), lambda i: (i, 0))],
        out_shape=[jax.ShapeDtypeStruct((T, D), F32), jax.ShapeDtypeStruct((T, ROUTER_W), F32)],
        compiler_params=_cparams(("parallel",)),
        name="moe_router",
    )(x, g.reshape(1, D), w.astype(BF16), b)


MOE_ROW_CHUNK = 128


def _moe_ffn_kernel(te_ref, nu_ref, nx_ref, x_ref, gw_ref, wg_hbm, wu_hbm, wd_hbm, y_ref,
                    stage_g, stage_u, stage_d, wg16, wu16, wd16, sem, *, layer):
    i = pl.program_id(0)

    def weight_copies(e):
        return (pltpu.make_async_copy(wg_hbm.at[layer, e], stage_g, sem.at[0]),
                pltpu.make_async_copy(wu_hbm.at[layer, e], stage_u, sem.at[1]),
                pltpu.make_async_copy(wd_hbm.at[layer, e], stage_d, sem.at[2]))

    @pl.when(i == 0)
    def _():
        for c in weight_copies(te_ref[0]):
            c.start()

    @pl.when((i == 0) | (te_ref[i] != te_ref[jnp.maximum(i - 1, 0)]))
    def _():
        for c, stage, w16 in zip(weight_copies(te_ref[i]), (stage_g, stage_u, stage_d), (wg16, wu16, wd16)):
            c.wait()
            w16[...] = stage[...].astype(BF16)

        @pl.when(nx_ref[i] >= 0)
        def _():
            for c in weight_copies(nx_ref[i]):
                c.start()

    @pl.when(i < nu_ref[0])
    def _():
        for r in range(MOE_TM // MOE_ROW_CHUNK):
            rows = slice(r * MOE_ROW_CHUNK, (r + 1) * MOE_ROW_CHUNK)
            x = x_ref[rows, :].astype(BF16)
            h = _dot(x, wg16[...])
            u = _dot(x, wu16[...])
            y_ref[rows, :] = _dot(((h * jax.nn.sigmoid(h)) * u * gw_ref[rows, :]).astype(BF16), wd16[...])

    @pl.when(i >= nu_ref[0])
    def _():
        y_ref[...] = jnp.zeros(y_ref.shape, F32)


def moe_ffn(xs, gate_rows, tile_expert, n_used, next_expert, w_gate, w_up, w_down, layer):
    M, D = xs.shape
    F = w_gate.shape[3]
    n_tiles = M // MOE_TM
    hbm = pl.BlockSpec(memory_space=pl.ANY)
    return pl.pallas_call(
        functools.partial(_moe_ffn_kernel, layer=layer),
        grid_spec=pltpu.PrefetchScalarGridSpec(
            num_scalar_prefetch=3,
            grid=(n_tiles,),
            in_specs=[pl.BlockSpec((MOE_TM, D), lambda i, te, nu, nx: (i, 0)),
                      pl.BlockSpec((MOE_TM, 1), lambda i, te, nu, nx: (i, 0)), hbm, hbm, hbm],
            out_specs=pl.BlockSpec((MOE_TM, D), lambda i, te, nu, nx: (i, 0)),
            scratch_shapes=[pltpu.VMEM((D, F), F32), pltpu.VMEM((D, F), F32), pltpu.VMEM((F, D), F32),
                            pltpu.VMEM((D, F), BF16), pltpu.VMEM((D, F), BF16), pltpu.VMEM((F, D), BF16),
                            pltpu.SemaphoreType.DMA((3,))],
        ),
        out_shape=jax.ShapeDtypeStruct((M, D), F32),
        compiler_params=_cparams(("arbitrary",)),
        name="moe_ffn",
    )(tile_expert, n_used, next_expert, xs, gate_rows, w_gate, w_up, w_down)


def moe_plan(info):
    T = info.shape[0]
    eid = info[:, :TOP_K_INNER].astype(jnp.int32).reshape(-1)
    M = T * TOP_K_INNER
    n_tiles = -(-(M + N_EXPERTS * (MOE_TM - 1)) // MOE_TM)
    onehot = (eid[:, None] == jnp.arange(N_EXPERTS)[None, :]).astype(jnp.int32)
    rank = jnp.cumsum(onehot, axis=0) - onehot
    counts = jnp.sum(onehot, axis=0)
    tiles_per = (counts + MOE_TM - 1) // MOE_TM
    tile_end = jnp.cumsum(tiles_per)
    offs = (tile_end - tiles_per) * MOE_TM
    dest = jnp.sum(onehot * (offs[None, :] + rank), axis=1).astype(jnp.int32)
    n_used = tile_end[-1]
    tile_ids = jnp.arange(n_tiles)
    tile_expert = jnp.minimum(jnp.sum(tile_ids[:, None] >= tile_end[None, :], axis=1), N_EXPERTS - 1)
    last_e = jnp.max(jnp.where(counts > 0, jnp.arange(N_EXPERTS), 0))
    tile_expert = jnp.where(tile_ids < n_used, tile_expert, last_e).astype(jnp.int32)
    ids = jnp.arange(N_EXPERTS)
    later = (ids[None, :] > ids[:, None]) & (counts[None, :] > 0)
    next_of = jnp.where(jnp.any(later, axis=1), jnp.argmax(later, axis=1), -1)
    next_expert = next_of[tile_expert].astype(jnp.int32)
    return (dest.reshape(T, TOP_K_INNER), tile_expert, n_used.reshape(1).astype(jnp.int32), next_expert,
            n_tiles * MOE_TM)


MOE_DMA_UNROLL = 8


def _row_copy(src_ref, s, dst_ref, d, sem):
    return pltpu.make_async_copy(src_ref.at[pl.ds(s, 1), :], dst_ref.at[pl.ds(d, 1), :], sem)


def _moe_scatter_kernel(dest_ref, h_ref, xs_in_ref, xs_ref, sem):
    del xs_in_ref
    tm = h_ref.shape[0]

    def issue(r, c):
        for slot in range(TOP_K_INNER):
            _row_copy(h_ref, r, xs_ref, dest_ref[0, 0, TOP_K_INNER * r + slot], sem).start()
        return c

    lax.fori_loop(0, tm, issue, 0, unroll=MOE_DMA_UNROLL)

    def drain(r, c):
        for slot in range(TOP_K_INNER):
            _row_copy(h_ref, 0, xs_ref, 0, sem).wait()
        return c

    lax.fori_loop(0, tm, drain, 0, unroll=MOE_DMA_UNROLL)


def moe_scatter(h, dest, xs, *, tm):
    T, D = h.shape
    assert T % tm == 0
    return pl.pallas_call(
        _moe_scatter_kernel,
        grid=(T // tm,),
        in_specs=[
            pl.BlockSpec((1, 1, TOP_K_INNER * tm), lambda i: (i, 0, 0), memory_space=pltpu.SMEM),
            pl.BlockSpec((tm, D), lambda i: (i, 0)),
            pl.BlockSpec(memory_space=pl.ANY),
        ],
        out_specs=pl.BlockSpec(memory_space=pl.ANY),
        out_shape=jax.ShapeDtypeStruct(xs.shape, xs.dtype),
        scratch_shapes=[pltpu.SemaphoreType.DMA(())],
        input_output_aliases={2: 0},
        compiler_params=_cparams(("arbitrary",)),
        name="moe_scatter",
    )(dest.reshape(T // tm, 1, TOP_K_INNER * tm), h, xs)


def _moe_combine_kernel(dest_ref, x_ref, y_ref, o_ref, buf, sem):
    tm = x_ref.shape[0]

    def issue(r, c):
        for slot in range(TOP_K_INNER):
            _row_copy(y_ref, dest_ref[0, 0, TOP_K_INNER * r + slot], buf.at[slot], r, sem).start()
        return c

    lax.fori_loop(0, tm, issue, 0, unroll=MOE_DMA_UNROLL)

    def drain(r, c):
        for slot in range(TOP_K_INNER):
            _row_copy(y_ref, 0, buf.at[slot], 0, sem).wait()
        return c

    lax.fori_loop(0, tm, drain, 0, unroll=MOE_DMA_UNROLL)
    out = x_ref[...]
    for slot in range(TOP_K_INNER):
        out = out + buf[slot]
    o_ref[...] = out


def moe_combine(x, dest, y, *, tm):
    T, D = x.shape
    assert T % tm == 0
    return pl.pallas_call(
        _moe_combine_kernel,
        grid=(T // tm,),
        in_specs=[
            pl.BlockSpec((1, 1, TOP_K_INNER * tm), lambda i: (i, 0, 0), memory_space=pltpu.SMEM),
            pl.BlockSpec((tm, D), lambda i: (i, 0)),
            pl.BlockSpec(memory_space=pl.ANY),
        ],
        out_specs=pl.BlockSpec((tm, D), lambda i: (i, 0)),
        out_shape=jax.ShapeDtypeStruct((T, D), F32),
        scratch_shapes=[pltpu.VMEM((TOP_K_INNER, tm, D), F32), pltpu.SemaphoreType.DMA(())],
        compiler_params=_cparams(("arbitrary",)),
        name="moe_combine",
    )(dest.reshape(T // tm, 1, TOP_K_INNER * tm), x, y)


MOE_ROW_TILE = 256


def moe_layer(xp, xs, g, w_group, b_group, w_expert, b_expert, w_gate, w_up, w_down, layer, spare=None):
    Tp, Ts = xp.shape[0], xs.shape[0]
    hp, ip = moe_router(xp, g, w_group, b_group, w_expert, b_expert, tm=512)
    hs, is_ = moe_router(xs, g, w_group, b_group, w_expert, b_expert, tm=Ts)
    info = jnp.concatenate([ip, is_], axis=0)
    dest, tile_expert, n_used, next_expert, m_rows = moe_plan(info)
    gate_rows = jnp.zeros((m_rows,), F32).at[dest.reshape(-1)].set(
        info[:, TOP_K_INNER:2 * TOP_K_INNER].reshape(-1)).reshape(m_rows, 1)
    rows = jnp.zeros((m_rows, xp.shape[1]), F32) if spare is None else spare
    rows = moe_scatter(hp, dest[:Tp], rows, tm=MOE_ROW_TILE)
    rows = moe_scatter(hs, dest[Tp:], rows, tm=Ts)
    y = moe_ffn(rows, gate_rows, tile_expert, n_used, next_expert, w_gate, w_up, w_down, layer)
    return moe_combine(xp, dest[:Tp], y, tm=MOE_ROW_TILE), moe_combine(xs, dest[Tp:], y, tm=Ts), y


N_DIL = len(DIL_PATTERNS)
DIL_SPANS = tuple(w // d for w, d in DIL_PATTERNS)
DIL_STEPS = tuple(d for _, d in DIL_PATTERNS)
assert all(s <= QBLK for s in DIL_SPANS)
assert all(d & (d - 1) == 0 for d in DIL_STEPS)


def _dil_bias_kernel(dil_ref, tab_ref, o_ref):
    p = pl.program_id(0)
    dm = QBLK + lax.broadcasted_iota(jnp.int32, (QBLK, 2 * QBLK), 0) - lax.broadcasted_iota(jnp.int32, (QBLK, 2 * QBLK), 1)
    bucket = _t5_bucket(dm * dil_ref[p])
    for h in range(DIL_HEADS):
        o_ref[0, h] = _table_lookup(bucket, tab_ref, h)


def dil_bias_tiles(table):
    return pl.pallas_call(
        _dil_bias_kernel,
        grid=(N_DIL,),
        in_specs=[pl.BlockSpec(memory_space=pltpu.SMEM), pl.BlockSpec(memory_space=pltpu.SMEM)],
        out_specs=pl.BlockSpec((1, DIL_HEADS, QBLK, 2 * QBLK), lambda p: (p, 0, 0, 0)),
        out_shape=jax.ShapeDtypeStruct((N_DIL, DIL_HEADS, QBLK, 2 * QBLK), F32),
        compiler_params=_cparams(("arbitrary",)),
        name="dil_bias_tiles",
    )(jnp.array(DIL_STEPS, jnp.int32), table)


def _dil_prompt_kernel(q_ref, kp_ref, kc_ref, vp_ref, vc_ref, b_ref, num_ref, ml_ref, *, span, scale):
    mb = pl.program_id(2)
    qi = lax.broadcasted_iota(jnp.int32, (QBLK, 2 * QBLK), 0)
    kj = lax.broadcasted_iota(jnp.int32, (QBLK, 2 * QBLK), 1)
    dm = QBLK + qi - kj
    mask = (dm >= 0) & (dm <= span) & ((kj >= QBLK) | (mb > 0))
    lane = lax.broadcasted_iota(jnp.int32, (1, LANES), 1)
    m_t = jnp.zeros((QBLK, LANES), F32)
    l_t = jnp.zeros((QBLK, LANES), F32)
    for h in range(DIL_HEADS):
        cs = slice(h * HEAD_DIM, (h + 1) * HEAD_DIM)
        k = jnp.concatenate([kp_ref[0, :, cs], kc_ref[0, :, cs]], axis=0)
        v = jnp.concatenate([vp_ref[0, :, cs], vc_ref[0, :, cs]], axis=0)
        s = _dot_t(q_ref[0, :, cs], k) * scale + b_ref[0, h]
        s = jnp.where(mask, s, NEG_INF)
        m = jnp.max(s, axis=-1, keepdims=True)
        p = jnp.where(mask, jnp.exp(s - m), 0.0)
        l = jnp.sum(p, axis=-1, keepdims=True)
        num_ref[0, :, cs] = _dot(p.astype(BF16), v)
        m_t = jnp.where(lane == h, m, m_t)
        l_t = jnp.where(lane == h, l, l_t)
    ml_ref[0, :, 0:LANES] = m_t
    ml_ref[0, :, LANES:2 * LANES] = l_t


def dil_prompt_pattern(p16, bias, pat, B, S):
    span, dil = DIL_SPANS[pat], DIL_STEPS[pat]
    M = S // dil
    assert M % QBLK == 0
    pv = p16.reshape(B, M, dil * 3 * DIL_W)
    kern = functools.partial(_dil_prompt_kernel, span=span, scale=HEAD_DIM ** -0.5)
    blk = (1, QBLK, DIL_W)
    prev = lambda m: jnp.maximum(m - 1, 0)
    num, ml = pl.pallas_call(
        kern,
        grid=(B, dil, M // QBLK),
        in_specs=[
            pl.BlockSpec(blk, lambda b, r, m: (b, m, 3 * r)),
            pl.BlockSpec(blk, lambda b, r, m: (b, prev(m), 3 * r + 1)),
            pl.BlockSpec(blk, lambda b, r, m: (b, m, 3 * r + 1)),
            pl.BlockSpec(blk, lambda b, r, m: (b, prev(m), 3 * r + 2)),
            pl.BlockSpec(blk, lambda b, r, m: (b, m, 3 * r + 2)),
            pl.BlockSpec((1, DIL_HEADS, QBLK, 2 * QBLK), lambda b, r, m: (pat, 0, 0, 0)),
        ],
        out_specs=[
            pl.BlockSpec(blk, lambda b, r, m: (b, m, r)),
            pl.BlockSpec((1, QBLK, 2 * LANES), lambda b, r, m: (b, m, r)),
        ],
        out_shape=[jax.ShapeDtypeStruct((B, M, dil * DIL_W), F32), jax.ShapeDtypeStruct((B, M, dil * 2 * LANES), F32)],
        compiler_params=_cparams(("parallel", "parallel", "arbitrary")),
        name=f"dil_prompt_{dil}",
    )(pv, pv, pv, pv, pv, bias)
    return num.reshape(B * S, DIL_W), ml.reshape(B * S, 2 * LANES)


def _dil_merge_kernel(*refs):
    nums, mls, o_ref = refs[:N_DIL], refs[N_DIL:2 * N_DIL], refs[2 * N_DIL]
    m_all = [r[:, 0:LANES] for r in mls]
    l_all = [r[:, LANES:2 * LANES] for r in mls]
    m_max = m_all[0]
    for m in m_all[1:]:
        m_max = jnp.maximum(m_max, m)
    w_all = [jnp.exp(m - m_max) for m in m_all]
    den = w_all[0] * l_all[0]
    for w, l in zip(w_all[1:], l_all[1:]):
        den = den + w * l
    for h in range(DIL_HEADS):
        cs = slice(h * HEAD_DIM, (h + 1) * HEAD_DIM)
        acc = w_all[0][:, h:h + 1] * nums[0][:, cs]
        for w, n in zip(w_all[1:], nums[1:]):
            acc = acc + w[:, h:h + 1] * n[:, cs]
        o_ref[:, cs] = (acc / den[:, h:h + 1]).astype(o_ref.dtype)


def dil_merge(nums, mls, *, tm):
    T = nums[0].shape[0]
    return pl.pallas_call(
        _dil_merge_kernel,
        grid=(T // tm,),
        in_specs=[pl.BlockSpec((tm, DIL_W), lambda i: (i, 0))] * N_DIL + [pl.BlockSpec((tm, 2 * LANES), lambda i: (i, 0))] * N_DIL,
        out_specs=pl.BlockSpec((tm, DIL_W), lambda i: (i, 0)),
        out_shape=jax.ShapeDtypeStruct((T, DIL_W), BF16),
        compiler_params=_cparams(("parallel",)),
        name="dil_merge",
    )(*nums, *mls)


def _block_diag_rows(q, n_heads):
    DB, Q, W = q.shape
    dh = W // n_heads
    eye = jnp.repeat(jnp.eye(n_heads, dtype=q.dtype), dh, axis=1)
    return (q[:, None, :, :] * eye[None, :, None, :]).reshape(DB, n_heads * Q, W)


def _own_block(acc, n_heads, rows_per_head):
    rows = acc.shape[0]
    rh = lax.broadcasted_iota(jnp.int32, (rows, 1), 0) // rows_per_head
    out = jnp.zeros((rows, HEAD_DIM), F32)
    for h in range(n_heads):
        out = jnp.where(rh == h, acc[:, h * HEAD_DIM:(h + 1) * HEAD_DIM], out)
    return out


def _row_bias(bucket, tab_ref, head0, n_heads, rows_per_head):
    rows = bucket.shape[0]
    rh = lax.broadcasted_iota(jnp.int32, (rows, 1), 0) // rows_per_head
    out = jnp.zeros(bucket.shape, F32)
    for h in range(n_heads):
        out = jnp.where(rh == h, _table_lookup(bucket, tab_ref, head0 + h), out)
    return out


def _flash_step_cols(s, mask, v, m_scr, l_scr, acc_scr):
    s = jnp.where(mask, s, NEG_INF)
    m_prev = m_scr[:, 0:1]
    m_new = jnp.maximum(m_prev, jnp.max(s, axis=-1, keepdims=True))
    p = jnp.where(mask, jnp.exp(s - m_new), 0.0)
    alpha = jnp.exp(m_prev - m_new)
    l_new = alpha * l_scr[:, 0:1] + jnp.sum(p, axis=-1, keepdims=True)
    acc_scr[...] = alpha * acc_scr[...] + _dot(p.astype(BF16), v)
    m_scr[...] = jnp.broadcast_to(m_new, m_scr.shape)
    l_scr[...] = jnp.broadcast_to(l_new, l_scr.shape)


def _head_cols(ref, first, n_heads, n_rows, stride, row0=0):
    return jnp.concatenate(
        [ref[pl.ds(row0 * stride + first + h, n_rows, stride=stride), :] for h in range(n_heads)], axis=1)


SB_DEC_PAGES = 8


def _sb_decode_kernel(pt_ref, q_ref, new_ref, *rest, n_steps, n_q, scale):
    pages, (o_ref, acc_scr, run_scr) = rest[:SB_DEC_PAGES], rest[SB_DEC_PAGES:]
    s_ = pl.program_id(1)
    q = q_ref[0]
    rows = q.shape[0]
    upper2 = _upper01(PAGE_SIZE)

    def absorb(k, v, mask):
        z2 = _dot_t(q, k.astype(BF16)) * (scale * LOG2_E)
        lk = _log2_keep(z2)
        lb = z2 + lk
        if mask is not None:
            lk = jnp.where(mask, lk, 0.0)
        w = jnp.exp2(lb + _after_in_tile(lk, upper2) + run_scr[...])
        if mask is not None:
            w = jnp.where(mask, w, 0.0)
        acc_scr[...] = acc_scr[...] + _dot(w.astype(BF16), v.astype(BF16))
        run_scr[...] = run_scr[...] + jnp.sum(lk, axis=-1, keepdims=True)

    @pl.when(s_ == 0)
    def _():
        acc_scr[...] = jnp.zeros(acc_scr.shape, F32)
        run_scr[...] = jnp.zeros(run_scr.shape, F32)
        t = lax.broadcasted_iota(jnp.int32, (rows, PAGE_SIZE), 1)
        qi = lax.broadcasted_iota(jnp.int32, (rows, PAGE_SIZE), 0) % n_q
        absorb(new_ref[0, :, :SB_W], new_ref[0, :, SB_W:], (t < qi) & (t < n_q))

    for page in pages:
        absorb(_head_cols(page, 0, SB_HEADS, PAGE_SIZE, 2 * SB_HEADS),
               _head_cols(page, SB_HEADS, SB_HEADS, PAGE_SIZE, 2 * SB_HEADS), None)

    @pl.when(s_ == n_steps - 1)
    def _():
        o_ref[0] = _own_block(acc_scr[...], SB_HEADS, n_q)


def sb_decode(page_table, qbd16, new_kv, cache_lines):
    DB, n_pages = page_table.shape
    assert n_pages % SB_DEC_PAGES == 0
    n_steps = n_pages // SB_DEC_PAGES
    rows = qbd16.shape[1]
    n_q = rows // SB_HEADS
    lines = PAGE_SIZE * 2 * SB_HEADS
    kern = functools.partial(_sb_decode_kernel, n_steps=n_steps, n_q=n_q, scale=HEAD_DIM ** -0.5)
    page_spec = lambda j: pl.BlockSpec(
        (lines, HEAD_DIM), lambda b, s, pt: (pt[b, n_pages - 1 - (s * SB_DEC_PAGES + j)], 0))
    return pl.pallas_call(
        kern,
        grid_spec=pltpu.PrefetchScalarGridSpec(
            num_scalar_prefetch=1,
            grid=(DB, n_steps),
            in_specs=[
                pl.BlockSpec((1, rows, SB_W), lambda b, s, pt: (b, 0, 0)),
                pl.BlockSpec((1, PAGE_SIZE, 2 * SB_W), lambda b, s, pt: (b, 0, 0)),
            ] + [page_spec(j) for j in range(SB_DEC_PAGES)],
            out_specs=pl.BlockSpec((1, rows, HEAD_DIM), lambda b, s, pt: (b, 0, 0)),
            scratch_shapes=[pltpu.VMEM((rows, SB_W), F32), pltpu.VMEM((rows, LANES), F32)],
        ),
        out_shape=jax.ShapeDtypeStruct((DB, rows, HEAD_DIM), F32),
        compiler_params=_cparams(("parallel", "arbitrary")),
        name="sb_decode",
    )(page_table, qbd16, new_kv, *([cache_lines] * SB_DEC_PAGES))


CMP_ROWS_TILE = 256
CHUNKS_PER_PAGE = PAGE_SIZE // CMP_STRIDE


CMP_DEC_PAGES = 8


def _nsa_cmp_decode_kernel(pt_ref, tab_ref, q_ref, w_ref, pe_ref, g_ref, *rest, n_steps, n_q, past, scale):
    pages, (oc_ref, sel_ref, a_scr) = rest[:CMP_DEC_PAGES], rest[CMP_DEC_PAGES:]
    s_ = pl.program_id(1)
    G, R = NSA_KV_HEADS, NSA_REP
    slots = 2 * G
    n_stage = CMP_DEC_PAGES * CHUNKS_PER_PAGE
    rs = pl.ds(pl.multiple_of(s_ * n_stage, n_stage), n_stage)
    for kv in range(2):
        for g in range(G):
            kvg = kv * G + g
            off = kv * NSA_KVW + g * HEAD_DIM
            xs = jnp.concatenate(
                [jnp.concatenate([pg[pl.ds(l * slots + kvg, CHUNKS_PER_PAGE, stride=CMP_STRIDE * slots), :]
                                  for pg in pages], axis=0) for l in range(CMP_STRIDE)], axis=1)
            for j in range(CMP_HALF):
                a_scr[j, rs, off:off + HEAD_DIM] = _dot((xs + pe_ref[kv, j]).astype(BF16), w_ref[kv, j])

    @pl.when(s_ == n_steps - 1)
    def _():
        n_chunk = a_scr.shape[1]
        n_c = n_chunk - CMP_HALF + 1
        out = a_scr[0]
        for j in range(1, CMP_HALF):
            out = out + pltpu.roll(a_scr[j], n_chunk - j, axis=0)
        kparts = []
        for g in range(G):
            kg = out[:, g * HEAD_DIM:(g + 1) * HEAD_DIM]
            kparts.append(kg * lax.rsqrt(jnp.mean(kg * kg, axis=-1, keepdims=True) + RMS_EPS) * g_ref[...])
        kc = jnp.concatenate(kparts, axis=1).astype(BF16)
        vc = out[:, NSA_KVW:2 * NSA_KVW].astype(BF16)

        q = q_ref[0]
        nrows = q.shape[0]
        qi = lax.broadcasted_iota(jnp.int32, (nrows, 1), 0) % n_q
        n_idx = lax.broadcasted_iota(jnp.int32, (1, n_chunk), 1)
        dist = past + qi - (n_idx * CMP_STRIDE + (CMP_BLOCK - 1))
        mask = (dist >= 0) & (n_idx < n_c)
        s = _dot_t(q, kc) * scale + _row_bias(_t5_bucket(dist), tab_ref, 0, NSA_HEADS, n_q)
        s = jnp.where(mask, s, NEG_INF)
        m = jnp.max(s, axis=-1, keepdims=True)
        p = jnp.where(mask, jnp.exp(s - m), 0.0)
        p = p / jnp.maximum(jnp.sum(p, axis=-1, keepdims=True), 1.0)
        oc_ref[0] = _own_block(_dot(p.astype(BF16), vc), G, R * n_q)
        gq = G * n_q
        oi = lax.broadcasted_iota(jnp.int32, (gq, nrows), 0)
        ri = lax.broadcasted_iota(jnp.int32, (gq, nrows), 1)
        pick = jnp.where((ri // (R * n_q) == oi // n_q) & (ri % n_q == oi % n_q), 1.0, 0.0).astype(BF16)
        p1, p2, p3 = _split3(p)
        imp = _dot(pick, p1) + _dot(pick, p2) + _dot(pick, p3)
        n_sp = sel_ref.shape[2]
        ratio = SLC_BLOCK // CMP_STRIDE
        ci = lax.broadcasted_iota(jnp.int32, (n_chunk, n_sp), 0)
        ti = lax.broadcasted_iota(jnp.int32, (n_chunk, n_sp), 1)
        gather01 = jnp.where((ci >= ti * ratio - (CMP_HALF - 1)) & (ci < (ti + 1) * ratio), 1.0, 0.0).astype(BF16)
        score = _dot01(imp, gather01)
        q_blk = (past + lax.broadcasted_iota(jnp.int32, (gq, 1), 0) % n_q) // SLC_BLOCK
        sel_ref[0] = _select_blocks(score, q_blk, N_SELECT)


def nsa_cmp_decode(page_table, table, qbd16, cache_lines, w16, pe, ck_gain, n_q):
    DB, n_pages = page_table.shape
    assert CMP_DEC_PAGES % 2 == 0 and n_pages % CMP_DEC_PAGES == 0
    n_steps = n_pages // CMP_DEC_PAGES
    n_chunk = n_pages * CHUNKS_PER_PAGE
    assert n_chunk % CMP_ROWS_TILE == 0
    past = n_pages * PAGE_SIZE
    n_s = -(-(past + n_q) // SLC_BLOCK)
    n_sp = -(-n_s // LANES) * LANES
    rows = qbd16.shape[1]
    kern = functools.partial(_nsa_cmp_decode_kernel, n_steps=n_steps, n_q=n_q, past=past, scale=HEAD_DIM ** -0.5)
    pg = (PAGE_SIZE * 2 * NSA_KV_HEADS, HEAD_DIM)
    return pl.pallas_call(
        kern,
        grid_spec=pltpu.PrefetchScalarGridSpec(
            num_scalar_prefetch=1,
            grid=(DB, n_steps),
            in_specs=[
                pl.BlockSpec(memory_space=pltpu.SMEM),
                pl.BlockSpec((1, rows, NSA_KVW), lambda b, s, pt: (b, 0, 0)),
                pl.BlockSpec((2, CMP_HALF, CMP_STRIDE * HEAD_DIM, HEAD_DIM), lambda b, s, pt: (0, 0, 0, 0)),
                pl.BlockSpec((2, CMP_HALF, 1, CMP_STRIDE * HEAD_DIM), lambda b, s, pt: (0, 0, 0, 0)),
                pl.BlockSpec((1, HEAD_DIM), lambda b, s, pt: (0, 0)),
            ] + [pl.BlockSpec(pg, lambda b, s, pt, j=j: (pt[b, CMP_DEC_PAGES * s + j], 0)) for j in range(CMP_DEC_PAGES)],
            out_specs=[
                pl.BlockSpec((1, rows, HEAD_DIM), lambda b, s, pt: (b, 0, 0)),
                pl.BlockSpec((1, NSA_KV_HEADS * n_q, n_sp), lambda b, s, pt: (b, 0, 0)),
            ],
            scratch_shapes=[pltpu.VMEM((CMP_HALF, n_chunk, 2 * NSA_KVW), F32)],
        ),
        out_shape=[jax.ShapeDtypeStruct((DB, rows, HEAD_DIM), F32),
                   jax.ShapeDtypeStruct((DB, NSA_KV_HEADS * n_q, n_sp), F32)],
        compiler_params=_cparams(("parallel", "arbitrary")),
        name="nsa_cmp_decode",
    )(page_table, table, qbd16, w16, pe, ck_gain.reshape(1, HEAD_DIM), *([cache_lines] * CMP_DEC_PAGES))


SLC_DEC_PAGES = 8


def _nsa_slc_decode_kernel(pt_ref, plist_ref, pcnt_ref, tab_ref, q_ref, sel_ref, new_ref, wbuf_ref, wnew_ref, oc_ref,
                           gt_ref, *rest, n_steps, n_q, past, scale):
    pages, (o_ref, s_scr, v_scr) = rest[:SLC_DEC_PAGES], rest[SLC_DEC_PAGES:]
    b_ = pl.program_id(0)
    s_ = pl.program_id(1)
    q = q_ref[0]
    rows = q.shape[0]
    G, R = NSA_KV_HEADS, NSA_REP
    slots = 2 * G
    qi = lax.broadcasted_iota(jnp.int32, (rows, 1), 0) % n_q
    rh = lax.broadcasted_iota(jnp.int32, (rows, 1), 0) // n_q
    per_page = PAGE_SIZE // SLC_BLOCK
    n_sp = sel_ref.shape[2]
    col = lax.broadcasted_iota(jnp.int32, (1, PAGE_SIZE), 1)

    def sel_cols(first_block):
        bi = lax.broadcasted_iota(jnp.int32, (n_sp, PAGE_SIZE), 0)
        ki = lax.broadcasted_iota(jnp.int32, (n_sp, PAGE_SIZE), 1)
        expand = jnp.where(bi == first_block + ki // SLC_BLOCK, 1.0, 0.0).astype(BF16)
        return _dot(sel_ref[0], expand) > 0.5

    def near_bias(dist):
        return _row_bias(_t5_bucket(dist), tab_ref, 0, NSA_HEADS, n_q)

    def far_bias():
        out = jnp.zeros((rows, 1), F32)
        for h in range(NSA_HEADS):
            out = jnp.where(rh == h, tab_ref[N_BUCKETS - 1, h], out)
        return jnp.broadcast_to(out, (rows, PAGE_SIZE))

    def stash(slot, k, v, bias, mask):
        s = _dot_t(q, k.astype(BF16)) * scale + bias
        s_scr[slot] = jnp.where(mask, s, NEG_INF)
        v_scr[slot] = v.astype(BF16)

    def softmax_out(n_slots):
        m = lax.fori_loop(0, n_slots, lambda t, m: jnp.maximum(m, jnp.max(s_scr[t], axis=-1, keepdims=True)),
                          jnp.full((rows, 1), NEG_INF, F32))

        def probs(t):
            s = s_scr[t]
            return jnp.where(s > 0.5 * NEG_INF, jnp.exp(s - m), 0.0)

        l = lax.fori_loop(0, n_slots, lambda t, l: l + jnp.sum(probs(t), axis=-1, keepdims=True),
                          jnp.zeros((rows, 1), F32))
        den = jnp.maximum(l, 1.0)
        acc = lax.fori_loop(0, n_slots, lambda t, a: a + _dot((probs(t) / den).astype(BF16), v_scr[t]),
                            jnp.zeros((rows, NSA_KVW), F32))
        return _own_block(acc, G, R * n_q)

    for j, page in enumerate(pages):
        idx = s_ * SLC_DEC_PAGES + j

        @pl.when(idx < pcnt_ref[b_])
        def _(page=page, idx=idx):
            p = plist_ref[b_, idx]
            dist = past + qi - (p * PAGE_SIZE + col)
            bias = lax.cond(past - (p + 1) * PAGE_SIZE + 1 >= BUCKET_MAX_DIST, far_bias, lambda: near_bias(dist))
            stash(idx, _head_cols(page, 0, G, PAGE_SIZE, slots), _head_cols(page, G, G, PAGE_SIZE, slots), bias,
                  sel_cols(p * per_page) & (dist >= 0))

    @pl.when(s_ == n_steps - 1)
    def _():
        n_vis = pcnt_ref[b_]
        dist_n = qi - col
        stash(n_vis, new_ref[0, :, :NSA_KVW], new_ref[0, :, NSA_KVW:], near_bias(dist_n),
              sel_cols(past // SLC_BLOCK) & (dist_n >= 0) & (col < n_q))
        o_s = softmax_out(n_vis + 1)
        wb = wbuf_ref.shape[0] // slots
        n_wt = wb // PAGE_SIZE
        for t in range(n_wt):
            dist_w = wb + qi - (t * PAGE_SIZE + col)
            stash(t, _head_cols(wbuf_ref, 0, G, PAGE_SIZE, slots, row0=t * PAGE_SIZE),
                  _head_cols(wbuf_ref, G, G, PAGE_SIZE, slots, row0=t * PAGE_SIZE), near_bias(dist_w),
                  (dist_w >= 0) & (dist_w < NSA_WINDOW))
        stash(n_wt, wnew_ref[0, :, :NSA_KVW], wnew_ref[0, :, NSA_KVW:], near_bias(dist_n), (dist_n >= 0) & (col < n_q))
        o_w = softmax_out(n_wt + 1)
        gt = gt_ref[0]
        o_ref[0] = gt[:, 0:1] * oc_ref[0] + gt[:, 1:2] * o_s + gt[:, 2:3] * o_w


def nsa_slc_decode(page_table, table, qbd16, sel16, cache_lines, new_kv, win_lines, win_new, o_c, gates, n_q):
    DB, n_pages = page_table.shape
    assert n_pages % SLC_DEC_PAGES == 0
    n_steps = n_pages // SLC_DEC_PAGES
    rows = qbd16.shape[1]
    past = n_pages * PAGE_SIZE
    slots = 2 * NSA_KV_HEADS
    wb_lines = win_lines.shape[0] // DB
    assert (wb_lines // slots) % PAGE_SIZE == 0
    n_slots = max(n_pages, wb_lines // slots // PAGE_SIZE) + 1
    kern = functools.partial(_nsa_slc_decode_kernel, n_steps=n_steps, n_q=n_q, past=past, scale=HEAD_DIM ** -0.5)
    per_page = PAGE_SIZE // SLC_BLOCK
    need = jnp.any(sel16[:, :, :n_pages * per_page].reshape(DB, rows, n_pages, per_page) > 0, axis=(1, 3))
    plist = jnp.argsort(jnp.logical_not(need), axis=1, stable=True).astype(jnp.int32)
    pcnt = jnp.maximum(jnp.sum(need, axis=1), 1).astype(jnp.int32)
    full = lambda shape: pl.BlockSpec((1,) + shape, lambda b, s, pt, pls, pc: (b, 0, 0))

    def page_spec(j):
        def index(b, s, pt, pls, pc):
            return (pt[b, pls[b, jnp.minimum(s * SLC_DEC_PAGES + j, pc[b] - 1)]], 0)
        return pl.BlockSpec((PAGE_SIZE * slots, HEAD_DIM), index)

    return pl.pallas_call(
        kern,
        grid_spec=pltpu.PrefetchScalarGridSpec(
            num_scalar_prefetch=3,
            grid=(DB, n_steps),
            in_specs=[
                pl.BlockSpec(memory_space=pltpu.SMEM),
                full((rows, NSA_KVW)),
                full((rows, sel16.shape[2])),
                full((PAGE_SIZE, 2 * NSA_KVW)),
                pl.BlockSpec((wb_lines, HEAD_DIM), lambda b, s, pt, pls, pc: (b, 0)),
                full((PAGE_SIZE, 2 * NSA_KVW)),
                full((rows, HEAD_DIM)),
                full((rows, LANES)),
            ] + [page_spec(j) for j in range(SLC_DEC_PAGES)],
            out_specs=full((rows, HEAD_DIM)),
            scratch_shapes=[pltpu.VMEM((n_slots, rows, PAGE_SIZE), F32),
                            pltpu.VMEM((n_slots, PAGE_SIZE, NSA_KVW), BF16)],
        ),
        out_shape=jax.ShapeDtypeStruct((DB, rows, HEAD_DIM), F32),
        compiler_params=_cparams(("parallel", "arbitrary")),
        name="nsa_slc_decode",
    )(page_table, plist, pcnt, table, qbd16, sel16, new_kv, win_lines, win_new, o_c, gates,
      *([cache_lines] * SLC_DEC_PAGES))


DIL_DEC_TILE = 256


def _dil_decode_kernel(tab_ref, q_ref, buf_ref, new_ref, o_ref, s_scr, v_scr, *, n_tiles, n_q, wb, scale):
    kt = pl.program_id(1)
    q = q_ref[0]
    rows = q.shape[0]
    qi = lax.broadcasted_iota(jnp.int32, (rows, 1), 0) % n_q
    n = DIL_DEC_TILE
    slots = 2 * DIL_HEADS
    col = lax.broadcasted_iota(jnp.int32, (1, n), 1)

    def stash(slot, k, v, dist):
        s_scr[slot] = _dot_t(q, k.astype(BF16)) * scale + _row_bias(_t5_bucket(dist), tab_ref, 0, DIL_HEADS, n_q)
        v_scr[slot] = v.astype(BF16)

    stash(kt, _head_cols(buf_ref, 0, DIL_HEADS, n, slots), _head_cols(buf_ref, DIL_HEADS, DIL_HEADS, n, slots),
          wb + qi - (kt * n + col))

    @pl.when(kt == n_tiles - 1)
    def _():
        stash(n_tiles, new_ref[0, :, :DIL_W], new_ref[0, :, DIL_W:], qi - col)

        def tile_mask(t, span, dil):
            dist = (wb + qi - (t * n + col)) if t < n_tiles else (qi - col)
            ok = (dist >= 0) & (jnp.bitwise_and(dist, dil - 1) == 0) & (dist <= span * dil)
            return ok if t < n_tiles else ok & (col < n_q)

        nums, ms, ls = [], [], []
        for span, dil in zip(DIL_SPANS, DIL_STEPS):
            masks = [tile_mask(t, span, dil) for t in range(n_tiles + 1)]
            m = jnp.full((rows, 1), NEG_INF, F32)
            for t, mk in enumerate(masks):
                m = jnp.maximum(m, jnp.max(jnp.where(mk, s_scr[t], NEG_INF), axis=-1, keepdims=True))
            l = jnp.zeros((rows, 1), F32)
            acc = jnp.zeros((rows, DIL_W), F32)
            for t, mk in enumerate(masks):
                p = jnp.where(mk, jnp.exp(s_scr[t] - m), 0.0)
                l = l + jnp.sum(p, axis=-1, keepdims=True)
                acc = acc + _dot(p.astype(BF16), v_scr[t])
            nums.append(_own_block(acc, DIL_HEADS, n_q))
            ms.append(m)
            ls.append(l)
        m_max = ms[0]
        for m in ms[1:]:
            m_max = jnp.maximum(m_max, m)
        ws = [jnp.exp(m - m_max) for m in ms]
        num = ws[0] * nums[0]
        den = ws[0] * ls[0]
        for w, n_, l in zip(ws[1:], nums[1:], ls[1:]):
            num = num + w * n_
            den = den + w * l
        o_ref[0] = num / den


def dil_decode(table, qbd16, buf_lines, new_kv, n_q):
    assert new_kv.shape[1] == DIL_DEC_TILE
    DB = qbd16.shape[0]
    slots = 2 * DIL_HEADS
    wb = buf_lines.shape[0] // (DB * slots)
    rows = qbd16.shape[1]
    assert wb % DIL_DEC_TILE == 0
    n_tiles = wb // DIL_DEC_TILE
    kern = functools.partial(_dil_decode_kernel, n_tiles=n_tiles, n_q=n_q, wb=wb, scale=HEAD_DIM ** -0.5)
    return pl.pallas_call(
        kern,
        grid=(DB, n_tiles),
        in_specs=[
            pl.BlockSpec(memory_space=pltpu.SMEM),
            pl.BlockSpec((1, rows, DIL_W), lambda b, t: (b, 0, 0)),
            pl.BlockSpec((DIL_DEC_TILE * slots, HEAD_DIM), lambda b, t: (b * n_tiles + t, 0)),
            pl.BlockSpec((1, new_kv.shape[1], 2 * DIL_W), lambda b, t: (b, 0, 0)),
        ],
        out_specs=pl.BlockSpec((1, rows, HEAD_DIM), lambda b, t: (b, 0, 0)),
        out_shape=jax.ShapeDtypeStruct((DB, rows, HEAD_DIM), F32),
        scratch_shapes=[pltpu.VMEM((n_tiles + 1, rows, DIL_DEC_TILE), F32),
                        pltpu.VMEM((n_tiles + 1, DIL_DEC_TILE, DIL_W), BF16)],
        compiler_params=_cparams(("parallel", "arbitrary")),
        name="dil_decode",
    )(table, qbd16, buf_lines, new_kv)


PROJ_TN = 256
PROJ_TM = 512
_C_SBQ, _C_SBK, _C_SBV = 0, SB_W, 2 * SB_W
_C_NQ = 3 * SB_W
_C_CMP = _C_NQ + NSA_QW
_C_SLC = _C_CMP + 2 * NSA_KVW
_C_WIN = _C_SLC + 2 * NSA_KVW
_C_GATE = _C_WIN + 2 * NSA_KVW
SPARSE_NP = -(-SPARSE_IN // PROJ_TN) * PROJ_TN


def _sparse_proj_plan(w_in, qk_gain):
    w16 = jnp.pad(w_in, ((0, 0), (0, SPARSE_NP - SPARSE_IN))).astype(BF16)
    gain = jnp.ones((SPARSE_NP,), F32)
    flag = np.zeros((SPARSE_NP,), np.float32)
    kinds = np.full((SPARSE_NP // PROJ_TN,), EPI_PLAIN, np.int32)
    for col, width, gi in ((_C_NQ, NSA_QW, 0), (_C_SLC, NSA_KVW, 2), (_C_WIN, NSA_KVW, 3)):
        gain = gain.at[col:col + width].set(jnp.tile(qk_gain[gi], width // HEAD_DIM))
        flag[col:col + width] = 1.0
        assert col % PROJ_TN == 0 and width % PROJ_TN == 0
        kinds[col // PROJ_TN:(col + width) // PROJ_TN] = EPI_NORM
    assert _C_GATE % PROJ_TN == 0 and SPARSE_NP - _C_GATE == PROJ_TN
    kinds[_C_GATE // PROJ_TN] = EPI_SIGMOID
    return w16, gain, jnp.asarray(flag), kinds


def _dil_proj_plan(w_in, qk_gain):
    N = 3 * DIL_W
    gain = jnp.concatenate([jnp.tile(qk_gain[0], DIL_HEADS), jnp.tile(qk_gain[1], DIL_HEADS), jnp.ones((DIL_W,), F32)])
    flag = np.concatenate([np.ones((2 * DIL_W,), np.float32), np.zeros((DIL_W,), np.float32)])
    kinds = np.full((N // PROJ_TN,), EPI_PLAIN, np.int32)
    kinds[:2 * DIL_W // PROJ_TN] = EPI_NORM
    return w_in.astype(BF16), gain, jnp.asarray(flag), kinds


def _pad_rows(a, n):
    return jnp.pad(a, ((0, 0), (0, n - a.shape[1]), (0, 0)))


def _rows_to_tokens(o, n_heads, n_q):
    DB = o.shape[0]
    return o.reshape(DB, n_heads, n_q, HEAD_DIM).transpose(0, 2, 1, 3).reshape(DB * n_q, n_heads * HEAD_DIM)


def kernel(x_prompt, x_sample, cache_sb_kv, cache_nsa_cmp_kv, cache_nsa_slc_kv, state_nsa_win_kv, state_dil_kv,
           page_table, rel_bias_table, norm_mix, norm_ffn, sparse_w_in, sparse_w_out, nsa_qk_gain, nsa_cmp_w,
           nsa_cmp_pe, dil_w_in, dil_w_out, dil_qk_gain, moe_w_group, moe_b_group, moe_w_expert, moe_b_expert,
           moe_w_gate, moe_w_up, moe_w_down):
    B, S, D = x_prompt.shape
    DB, DS, _ = x_sample.shape
    n_pool = cache_sb_kv.shape[1]
    xp = x_prompt.reshape(B * S, D)
    xs = x_sample.reshape(DB * DS, D)
    nsa_table = rel_bias_table[:, :NSA_HEADS]
    dil_table = rel_bias_table[:, :DIL_HEADS]
    G, R = NSA_KV_HEADS, NSA_REP

    def moe(layer, xp, xs, spare=None):
        return moe_layer(xp, xs, norm_ffn[layer], moe_w_group[layer], moe_b_group[layer], moe_w_expert[layer],
                         moe_b_expert[layer], moe_w_gate, moe_w_up, moe_w_down, layer, spare)

    w16, gain, flag, kinds = _sparse_proj_plan(sparse_w_in[0], nsa_qk_gain[0])
    w_out16 = sparse_w_out[0].astype(BF16)
    cw16, cpe = _prep_cmp_weights(nsa_cmp_w[0], nsa_cmp_pe[0])
    ck_gain = nsa_qk_gain[0, 1]
    kv_cols = ((_C_SBK, 2 * SB_W), (_C_CMP, 2 * NSA_KVW), (_C_SLC, 2 * NSA_KVW), (_C_WIN, 2 * NSA_KVW))
    p16, gate32, sb_lines, cmp_lines, slc_lines, win_lines = norm_mm(
        xp, norm_mix[0], w16, gain, flag, kinds, tm=PROJ_TM, tn=PROJ_TN, f32_cols=(_C_GATE, PROJ_TN), lines=kv_cols)
    p16b = p16.reshape(B, S, SPARSE_NP)
    o_sb = sb_prompt(p16b, B, S, q_col=_C_SBQ, k_col=_C_SBK, v_col=_C_SBV)
    chunks = p16b[:, :, _C_CMP:_C_CMP + 2 * NSA_KVW].reshape(B, S // CMP_STRIDE, CHUNK_W)
    kc, vc = compress_prompt(chunks, cw16, cpe, ck_gain)
    tb = nsa_bias_tiles(nsa_table)
    cb = nsa_cmp_bias(nsa_table, S, kc.shape[1])
    o_nsa = nsa_prompt(p16b, gate32.reshape(B, S, PROJ_TN), kc, vc, tb, cb, B, S, q_col=_C_NQ, sk_col=_C_SLC,
                       sv_col=_C_SLC + NSA_KVW, wk_col=_C_WIN, wv_col=_C_WIN + NSA_KVW, gate_col=0)
    a = jnp.concatenate([o_sb, o_nsa], axis=-1).reshape(B * S, SB_W + NSA_QW)
    xp = mm_res(a, w_out16, xp, tm=1024, tn=PROJ_TN)
    sb_kv_prompt = sb_lines.reshape(1, B, S, 2, SB_HEADS, HEAD_DIM)
    cmp_kv_prompt = cmp_lines.reshape(1, B, S, 2, G, HEAD_DIM)
    slc_kv_prompt = slc_lines.reshape(1, B, S, 2, G, HEAD_DIM)
    wn = min(NSA_WINDOW, S)
    win_kv_prompt = win_lines.reshape(1, B, S, 2, G, HEAD_DIM)[:, :, S - wn:]
    _, q32 = norm_mm(xs, norm_mix[0], w16, gain, flag, kinds, tm=DB * DS, tn=PROJ_TN, f32_cols=(0, SPARSE_NP))
    q32b = q32.reshape(DB, DS, SPARSE_NP)
    sb_new = q32b[:, :, _C_SBK:_C_SBK + 2 * SB_W]
    cmp_new = q32b[:, :, _C_CMP:_C_CMP + 2 * NSA_KVW]
    slc_new = q32b[:, :, _C_SLC:_C_SLC + 2 * NSA_KVW]
    win_new = q32b[:, :, _C_WIN:_C_WIN + 2 * NSA_KVW]
    o_sb_s = sb_decode(page_table, _block_diag_rows(q32b[:, :, _C_SBQ:_C_SBQ + SB_W], SB_HEADS).astype(BF16),
                       _pad_rows(sb_new, PAGE_SIZE), cache_sb_kv[0].reshape(-1, HEAD_DIM))
    qn = q32b[:, :, _C_NQ:_C_NQ + NSA_QW].reshape(DB, DS, G, R, HEAD_DIM).transpose(0, 2, 3, 1, 4)
    qn = (qn[:, :, :, :, None, :] * jnp.eye(G, dtype=F32)[None, :, None, None, :, None]).reshape(DB, G * R * DS, NSA_KVW)
    qn16 = qn.astype(BF16)
    o_c, sel = nsa_cmp_decode(page_table, nsa_table, qn16, cache_nsa_cmp_kv[0].reshape(-1, HEAD_DIM), cw16, cpe, ck_gain, DS)
    sel16 = jnp.broadcast_to(sel.reshape(DB, G, 1, DS, -1), (DB, G, R, DS, sel.shape[-1])).reshape(DB, G * R * DS, -1).astype(BF16)
    gates = q32b[:, :, _C_GATE:_C_GATE + 3 * NSA_HEADS].reshape(DB, DS, G, R, 3).transpose(0, 2, 3, 1, 4).reshape(DB, G * R * DS, 3)
    gates = jnp.pad(gates, ((0, 0), (0, 0), (0, LANES - 3)))
    wbuf = state_nsa_win_kv[0]
    o_nsa_s = nsa_slc_decode(page_table, nsa_table, qn16, sel16, cache_nsa_slc_kv[0].reshape(-1, HEAD_DIM),
                             _pad_rows(slc_new, PAGE_SIZE), wbuf.reshape(-1, HEAD_DIM), _pad_rows(win_new, PAGE_SIZE),
                             o_c, gates, DS)
    a_s = jnp.concatenate([_rows_to_tokens(o_sb_s, SB_HEADS, DS), _rows_to_tokens(o_nsa_s, NSA_HEADS, DS)], axis=-1)
    xs = mm_res(a_s.astype(BF16), w_out16, xs, tm=DB * DS, tn=PROJ_TN)
    sb_kv_sample = sb_new.reshape(1, DB, DS, 2, SB_HEADS, HEAD_DIM)
    cmp_kv_sample = cmp_new.reshape(1, DB, DS, 2, G, HEAD_DIM)
    slc_kv_sample = slc_new.reshape(1, DB, DS, 2, G, HEAD_DIM)
    win_kv_sample = jnp.concatenate([wbuf, win_new.reshape(DB, DS, 2, G, HEAD_DIM)], axis=1)[None, :, DS:]
    xp, xs, moe_rows = moe(0, xp, xs)

    w16, gain, flag, kinds = _dil_proj_plan(dil_w_in[0], dil_qk_gain[0])
    w_out16 = dil_w_out[0].astype(BF16)
    p16, dil_lines = norm_mm(xp, norm_mix[1], w16, gain, flag, kinds, tm=PROJ_TM, tn=PROJ_TN,
                             lines=((DIL_W, 2 * DIL_W),))
    dbias = dil_bias_tiles(dil_table)
    stats = [dil_prompt_pattern(p16.reshape(B, S, 3 * DIL_W), dbias, pat, B, S) for pat in range(N_DIL)]
    o = dil_merge([n for n, _ in stats], [ml for _, ml in stats], tm=512)
    xp = mm_res(o, w_out16, xp, tm=1024, tn=PROJ_TN)
    dn = min(DIL_MAX_WINDOW, S)
    dil_kv_prompt = dil_lines.reshape(1, B, S, 2, DIL_HEADS, HEAD_DIM)[:, :, S - dn:]
    _, q32 = norm_mm(xs, norm_mix[1], w16, gain, flag, kinds, tm=DB * DS, tn=PROJ_TN, f32_cols=(0, 3 * DIL_W))
    q32b = q32.reshape(DB, DS, 3 * DIL_W)
    dil_new = q32b[:, :, DIL_W:]
    dbuf = state_dil_kv[0]
    o_s = dil_decode(dil_table, _block_diag_rows(q32b[:, :, :DIL_W], DIL_HEADS).astype(BF16),
                     dbuf.reshape(-1, HEAD_DIM), _pad_rows(dil_new, DIL_DEC_TILE), DS)
    xs = mm_res(_rows_to_tokens(o_s, DIL_HEADS, DS).astype(BF16), w_out16, xs, tm=DB * DS, tn=PROJ_TN)
    dil_kv_sample = jnp.concatenate([dbuf, dil_new.reshape(DB, DS, 2, DIL_HEADS, HEAD_DIM)], axis=1)[None, :, DS:]
    xp, xs, _ = moe(1, xp, xs, moe_rows)

    return (xp.reshape(B, S, D), xs.reshape(DB, DS, D), sb_kv_prompt, sb_kv_sample, cmp_kv_prompt, cmp_kv_sample,
            slc_kv_prompt, slc_kv_sample, win_kv_prompt, win_kv_sample, dil_kv_prompt, dil_kv_sample)
```
